```python
import math
import jax, jax.numpy as jnp
from jax import lax
import numpy as np

D_MODEL = 1024
BATCH = 8
SEQ = 2048
DEPTH = 2

CHUNK = 64
D_MIX = D_MODEL
GROUP_W = D_MIX // 4
EPS = 1e-6

SB_HEADS = 4
SB_HEAD_DIM = GROUP_W // SB_HEADS
SB_BLOCK = 128
RG_WIDTH = GROUP_W
RG_BLOCKS = 4
RG_BLOCK_DIM = RG_WIDTH // RG_BLOCKS
RG_CONV = 4
RG_C = 8.0
HG_HEADS = 4
HG_DK = GROUP_W // HG_HEADS
HG_DV = GROUP_W // HG_HEADS
M2_HEADS = 4
M2_HEAD_DIM = GROUP_W // M2_HEADS
M2_GROUPS = 2
M2_STATE = 128
M2_CONV = 4
M2_CONV_DIM = GROUP_W + 2 * M2_GROUPS * M2_STATE
N_EXPERTS = 32
TOP_K = 4
D_FF = D_MODEL
SWIGLU_LIMIT = 7.0
SWIGLU_ALPHA = 1.702
MOE_BLOCK = 128

IN_WIDTHS = ([GROUP_W] * 3
             + [GROUP_W] * 2
             + [GROUP_W] * 4
             + [GROUP_W, M2_CONV_DIM, M2_HEADS])
D_IN = sum(IN_WIDTHS)

kernel_name = "hybrid_sb_rglru_hgrn2_ssd_moe"


def rms_norm(x, g):
    xf = x.astype(jnp.float32)
    y = xf * lax.rsqrt(jnp.mean(xf * xf, axis=-1, keepdims=True) + EPS)
    return (y * g.astype(jnp.float32)).astype(x.dtype)


def group_rms(x, n_groups):
    shp = x.shape
    xg = x.reshape(shp[:-1] + (n_groups, shp[-1] // n_groups))
    xg = xg * lax.rsqrt(jnp.mean(xg * xg, axis=-1, keepdims=True) + EPS)
    return xg.reshape(shp)


def causal_depthwise_conv(x, w, b):
    k = w.shape[0]
    y = lax.conv_general_dilated(x, w[:, None, :], window_strides=(1,), padding=[(k - 1, 0)],
                                 dimension_numbers=("NWC", "WIO", "NWC"),
                                 feature_group_count=x.shape[-1])
    return y + b


def stick_breaking_attention(q, k, v):
    L = q.shape[2]
    scale = q.shape[-1] ** -0.5
    outs = []
    for blk in range(L // SB_BLOCK):
        q0 = blk * SB_BLOCK
        lk = q0 + SB_BLOCK
        z = jnp.einsum("bhtd,bhsd->bhts", q[:, :, q0:lk], k[:, :, :lk]).astype(jnp.float32) * scale
        t_idx = q0 + jnp.arange(SB_BLOCK)[:, None]
        s_idx = jnp.arange(lk)[None, :]
        mask = s_idx < t_idx
        log_keep = jnp.where(mask, jax.nn.log_sigmoid(-z), 0.0)
        log_remain = lax.cumsum(log_keep, axis=3, reverse=True) - log_keep
        log_a = jnp.where(mask, jax.nn.log_sigmoid(z) + log_remain, -jnp.inf)
        a = jnp.exp(log_a)
        outs.append(jnp.einsum("bhts,bhsd->bhtd", a.astype(v.dtype), v[:, :, :lk]))
    return jnp.concatenate(outs, axis=2)


def rg_lru_branch(x_in, gate_in, conv_w, conv_b, wa, ba, wx, bx, lam):
    B_, L, W = x_in.shape
    xc = causal_depthwise_conv(x_in, conv_w, conv_b)
    xb = xc.reshape(B_, L, RG_BLOCKS, RG_BLOCK_DIM)
    r = jax.nn.sigmoid((jnp.einsum("blni,nij->blnj", xb, wa).reshape(B_, L, W) + ba).astype(jnp.float32))
    ig = jax.nn.sigmoid((jnp.einsum("blni,nij->blnj", xb, wx).reshape(B_, L, W) + bx).astype(jnp.float32))
    log_a = -RG_C * r * jax.nn.softplus(-lam.astype(jnp.float32))
    a = jnp.exp(log_a)
    mult = jnp.sqrt(-jnp.expm1(2.0 * log_a))
    mult = jnp.where((jnp.arange(L) == 0)[None, :, None], 1.0, mult)
    u = mult * ig * xc.astype(jnp.float32)

    def combine(c1, c2):
        a1, b1 = c1
        a2, b2 = c2
        return a1 * a2, a2 * b1 + b2

    _, hs = lax.associative_scan(combine, (a, u), axis=1)
    return (hs * jax.nn.gelu(gate_in.astype(jnp.float32))).astype(x_in.dtype)


def hgrn2_branch(q_in, f_in, i_in, g_in, lb, norm_g):
    B_, L, W = q_in.shape
    nc = L // CHUNK
    forget = lb + (1.0 - lb) * jax.nn.sigmoid(f_in.astype(jnp.float32))
    log_f = jnp.log(forget)
    key = 1.0 - forget
    q = jax.nn.silu(q_in.astype(jnp.float32))
    v = i_in.astype(jnp.float32)

    def to_chunks(t, d):
        return t.reshape(B_, nc, CHUNK, HG_HEADS, d).transpose(1, 0, 3, 2, 4)

    qc, kc, lfc = to_chunks(q, HG_DK), to_chunks(key, HG_DK), to_chunks(log_f, HG_DK)
    vc = to_chunks(v, HG_DV)
    causal = jnp.tril(jnp.ones((CHUNK, CHUNK), bool))

    def step(state, inp):
        qb, kb, vb, lfb = inp
        bcum = jnp.cumsum(lfb, axis=2)
        diff = bcum[:, :, :, None, :] - bcum[:, :, None, :, :]
        decay = jnp.exp(jnp.where(causal[:, :, None], diff, -jnp.inf))
        scores = jnp.einsum("bhtc,bhsc,bhtsc->bhts", qb, kb, decay)
        o = jnp.einsum("bhts,bhsv->bhtv", scores, vb) + jnp.einsum("bhtc,bhcv->bhtv", qb * jnp.exp(bcum), state)
        b_last = bcum[:, :, -1:, :]
        new_state = (jnp.exp(b_last[:, :, 0, :, None]) * state
                     + jnp.einsum("bhsc,bhsv->bhcv", kb * jnp.exp(b_last - bcum), vb))
        return new_state, o

    s0 = jnp.zeros((B_, HG_HEADS, HG_DK, HG_DV), jnp.float32)
    _, o = lax.scan(step, s0, (qc, kc, vc, lfc))
    o = o.transpose(1, 0, 3, 2, 4).reshape(B_, L, W)
    o = group_rms(o, HG_HEADS) * norm_g.astype(jnp.float32) * jax.nn.silu(g_in.astype(jnp.float32))
    return o.astype(q_in.dtype)


def segsum(x):
    q = x.shape[-1]
    cs = jnp.cumsum(x, axis=-1)
    diff = cs[..., :, None] - cs[..., None, :]
    return jnp.where(jnp.tril(jnp.ones((q, q), bool)), diff, -jnp.inf)


def mamba2_branch(z, xbc, dt_raw, conv_w, conv_b, dt_bias, a_log, d_skip, norm_g):
    B_, L, _ = z.shape
    nc = L // CHUNK
    H, P, G, N = M2_HEADS, M2_HEAD_DIM, M2_GROUPS, M2_STATE
    xbc = jax.nn.silu(causal_depthwise_conv(xbc, conv_w, conv_b))
    xs = xbc[..., :GROUP_W].reshape(B_, L, H, P).astype(jnp.float32)
    bm = jnp.repeat(xbc[..., GROUP_W:GROUP_W + G * N].reshape(B_, L, G, N), H // G, axis=2).astype(jnp.float32)
    cm = jnp.repeat(xbc[..., GROUP_W + G * N:].reshape(B_, L, G, N), H // G, axis=2).astype(jnp.float32)
    dt = jax.nn.softplus(dt_raw.astype(jnp.float32) + dt_bias.astype(jnp.float32))
    da = dt * (-jnp.exp(a_log.astype(jnp.float32)))
    xd = xs * dt[..., None]
    xd_c = xd.reshape(B_, nc, CHUNK, H, P)
    b_c = bm.reshape(B_, nc, CHUNK, H, N)
    c_c = cm.reshape(B_, nc, CHUNK, H, N)
    da_c = da.reshape(B_, nc, CHUNK, H).transpose(0, 3, 1, 2)
    a_cs = jnp.cumsum(da_c, axis=-1)
    lmat = jnp.exp(segsum(da_c))
    y_diag = jnp.einsum("bclhn,bcshn,bhcls,bcshp->bclhp", c_c, b_c, lmat, xd_c)
    decay_states = jnp.exp(a_cs[..., -1:] - a_cs)
    states = jnp.einsum("bcshn,bhcs,bcshp->bchpn", b_c, decay_states, xd_c)
    chunk_decay = jnp.exp(a_cs[..., -1])

    def step(hstate, inp):
        st, dec = inp
        return dec[..., None, None] * hstate + st, hstate

    h0 = jnp.zeros((B_, H, P, N), jnp.float32)
    _, prev = lax.scan(step, h0, (states.transpose(1, 0, 2, 3, 4), chunk_decay.transpose(2, 0, 1)))
    prev = prev.transpose(1, 0, 2, 3, 4)
    y_off = jnp.einsum("bclhn,bchpn,bhcl->bclhp", c_c, prev, jnp.exp(a_cs))
    y = (y_diag + y_off).reshape(B_, L, H, P) + xs * d_skip.astype(jnp.float32)[:, None]
    y = y.reshape(B_, L, GROUP_W) * jax.nn.silu(z.astype(jnp.float32))
    y = group_rms(y, M2_GROUPS) * norm_g.astype(jnp.float32)
    return y.astype(z.dtype)


def moe_ffn(x, router_w, router_b, w_gu, b_gu, w_down, b_down):
    B_, L, D = x.shape
    n_tok = B_ * L
    xt = x.reshape(n_tok, D)
    logits = (xt @ router_w + router_b).astype(jnp.float32)
    top_logits, top_idx = lax.top_k(logits, TOP_K)
    gates = jax.nn.softmax(top_logits, axis=-1)
    n_assign = n_tok * TOP_K
    flat_e = top_idx.reshape(-1)
    order = jnp.argsort(flat_e)
    sorted_e = flat_e[order]
    sorted_tok = (order // TOP_K).astype(jnp.int32)
    counts = jnp.bincount(flat_e, length=N_EXPERTS)
    padded = (counts + MOE_BLOCK - 1) // MOE_BLOCK * MOE_BLOCK
    pad_end = jnp.cumsum(padded)
    pad_start = pad_end - padded
    start = jnp.cumsum(counts) - counts
    dest = (pad_start[sorted_e] + jnp.arange(n_assign) - start[sorted_e]).astype(jnp.int32)
    n_blocks = -(-n_assign // MOE_BLOCK) + N_EXPERTS
    slot_tok = jnp.zeros((n_blocks * MOE_BLOCK,), jnp.int32).at[dest].set(sorted_tok)
    block_expert = jnp.minimum(jnp.searchsorted(pad_end, jnp.arange(n_blocks) * MOE_BLOCK, side="right"),
                               N_EXPERTS - 1)

    def run_block(args):
        tok, e = args
        hmid = xt[tok] @ w_gu[e] + b_gu[e]
        glu = jnp.minimum(hmid[:, 0::2], SWIGLU_LIMIT)
        lin = jnp.clip(hmid[:, 1::2], -SWIGLU_LIMIT, SWIGLU_LIMIT)
        act = glu * jax.nn.sigmoid(SWIGLU_ALPHA * glu) * (lin + 1.0)
        return act @ w_down[e] + b_down[e]

    y_slots = lax.map(run_block, (slot_tok.reshape(n_blocks, MOE_BLOCK), block_expert)).reshape(-1, D)
    slot_of = jnp.zeros((n_assign,), jnp.int32).at[order].set(dest)
    y = jnp.einsum("tk,tkd->td", gates.astype(x.dtype), y_slots[slot_of].reshape(n_tok, TOP_K, D))
    return y.reshape(B_, L, D)


def setup_inputs(seed: int = 0) -> dict:
    key = jax.random.key(seed)
    ks = jax.random.split(key, 32)

    def nrm(k, shape, scale):
        return jax.random.normal(k, shape, jnp.float32) * scale

    def gain(k, shape):
        return 1.0 + 0.02 * jax.random.normal(k, shape, jnp.float32)

    x = nrm(ks[0], (BATCH, SEQ, D_MODEL), 1.0)
    norm_mix_g = gain(ks[1], (DEPTH, D_MODEL))
    w_in = nrm(ks[2], (DEPTH, D_MODEL, D_IN), D_MODEL ** -0.5)
    sb_norm_g = gain(ks[3], (DEPTH, GROUP_W))
    rg_conv_w = nrm(ks[4], (DEPTH, RG_CONV, RG_WIDTH), RG_CONV ** -0.5)
    rg_conv_b = nrm(ks[5], (DEPTH, RG_WIDTH), 0.01)
    rg_wa = nrm(ks[6], (DEPTH, RG_BLOCKS, RG_BLOCK_DIM, RG_BLOCK_DIM), RG_BLOCK_DIM ** -0.5)
    rg_ba = nrm(ks[7], (DEPTH, RG_WIDTH), 0.01)
    rg_wx = nrm(ks[8], (DEPTH, RG_BLOCKS, RG_BLOCK_DIM, RG_BLOCK_DIM), RG_BLOCK_DIM ** -0.5)
    rg_bx = nrm(ks[9], (DEPTH, RG_WIDTH), 0.01)
    a_pow = jax.random.uniform(ks[10], (DEPTH, RG_WIDTH), jnp.float32, 0.9, 0.999)
    s = a_pow ** (1.0 / RG_C)
    rg_lambda = jnp.log(s) - jnp.log1p(-s)
    rg_norm_g = gain(ks[11], (DEPTH, GROUP_W))
    hg_lower_bounds = nrm(ks[12], (DEPTH, GROUP_W), 0.1)
    hg_norm_g = gain(ks[13], (DEPTH, GROUP_W))
    m2_conv_w = nrm(ks[14], (DEPTH, M2_CONV, M2_CONV_DIM), M2_CONV ** -0.5)
    m2_conv_b = nrm(ks[15], (DEPTH, M2_CONV_DIM), 0.01)
    dt0 = jnp.exp(jax.random.uniform(ks[16], (DEPTH, M2_HEADS), jnp.float32, math.log(1e-3), math.log(1e-1)))
    m2_dt_bias = dt0 + jnp.log(-jnp.expm1(-dt0))
    m2_a_log = jnp.log(jax.random.uniform(ks[17], (DEPTH, M2_HEADS), jnp.float32, 1.0, 16.0))
    m2_d = gain(ks[18], (DEPTH, M2_HEADS))
    m2_norm_g = gain(ks[19], (DEPTH, GROUP_W))
    w_out = nrm(ks[20], (DEPTH, D_MIX, D_MODEL), 0.5 * D_MIX ** -0.5)
    norm_ffn_g = gain(ks[21], (DEPTH, D_MODEL))
    router_w = nrm(ks[22], (DEPTH, D_MODEL, N_EXPERTS), D_MODEL ** -0.5)
    router_b = nrm(ks[23], (DEPTH, N_EXPERTS), 0.01)
    moe_w_gu = nrm(ks[24], (DEPTH, N_EXPERTS, D_MODEL, 2 * D_FF), D_MODEL ** -0.5)
    moe_b_gu = nrm(ks[25], (DEPTH, N_EXPERTS, 2 * D_FF), 0.01)
    moe_w_down = nrm(ks[26], (DEPTH, N_EXPERTS, D_FF, D_MODEL), 0.5 * D_FF ** -0.5)
    moe_b_down = nrm(ks[27], (DEPTH, N_EXPERTS, D_MODEL), 0.01)
    final_norm_g = gain(ks[28], (D_MODEL,))
    return {"x": x, "norm_mix_g": norm_mix_g, "w_in": w_in, "sb_norm_g": sb_norm_g,
            "rg_conv_w": rg_conv_w, "rg_conv_b": rg_conv_b, "rg_wa": rg_wa, "rg_ba": rg_ba,
            "rg_wx": rg_wx, "rg_bx": rg_bx, "rg_lambda": rg_lambda, "rg_norm_g": rg_norm_g,
            "hg_lower_bounds": hg_lower_bounds, "hg_norm_g": hg_norm_g,
            "m2_conv_w": m2_conv_w, "m2_conv_b": m2_conv_b, "m2_dt_bias": m2_dt_bias,
            "m2_a_log": m2_a_log, "m2_d": m2_d, "m2_norm_g": m2_norm_g, "w_out": w_out,
            "norm_ffn_g": norm_ffn_g, "router_w": router_w, "router_b": router_b,
            "moe_w_gu": moe_w_gu, "moe_b_gu": moe_b_gu, "moe_w_down": moe_w_down,
            "moe_b_down": moe_b_down, "final_norm_g": final_norm_g}


def reference(x, norm_mix_g, w_in, sb_norm_g, rg_conv_w, rg_conv_b, rg_wa, rg_ba, rg_wx, rg_bx,
              rg_lambda, rg_norm_g, hg_lower_bounds, hg_norm_g, m2_conv_w, m2_conv_b, m2_dt_bias,
              m2_a_log, m2_d, m2_norm_g, w_out, norm_ffn_g, router_w, router_b, moe_w_gu, moe_b_gu,
              moe_w_down, moe_b_down, final_norm_g):
    B_, L, _ = x.shape
    lbs = jnp.cumsum(jax.nn.softmax(hg_lower_bounds.astype(jnp.float32), axis=0), axis=0)
    lbs = lbs - lbs[0]
    split_at = np.cumsum(IN_WIDTHS)[:-1].tolist()

    def to_heads(t, n):
        return t.reshape(B_, L, n, -1).transpose(0, 2, 1, 3)

    h = x
    for l in range(DEPTH):
        u = rms_norm(h, norm_mix_g[l])
        proj = u @ w_in[l]
        (sb_q, sb_k, sb_v, rg_x, rg_g, hg_q, hg_f, hg_i, hg_g,
         m2_z, m2_xbc, m2_dt) = jnp.split(proj, split_at, axis=-1)
        o_a = stick_breaking_attention(to_heads(sb_q, SB_HEADS), to_heads(sb_k, SB_HEADS), to_heads(sb_v, SB_HEADS))
        o_a = rms_norm(o_a.transpose(0, 2, 1, 3).reshape(B_, L, GROUP_W), sb_norm_g[l])
        o_b = rms_norm(rg_lru_branch(rg_x, rg_g, rg_conv_w[l], rg_conv_b[l], rg_wa[l], rg_ba[l],
                                     rg_wx[l], rg_bx[l], rg_lambda[l]), rg_norm_g[l])
        o_c = hgrn2_branch(hg_q, hg_f, hg_i, hg_g, lbs[l], hg_norm_g[l])
        o_d = mamba2_branch(m2_z, m2_xbc, m2_dt, m2_conv_w[l], m2_conv_b[l], m2_dt_bias[l],
                            m2_a_log[l], m2_d[l], m2_norm_g[l])
        mix = jnp.concatenate([o_a, o_b, o_c, o_d], axis=-1)
        h = h + mix @ w_out[l]
        h = h + moe_ffn(rms_norm(h, norm_ffn_g[l]), router_w[l], router_b[l], moe_w_gu[l],
                        moe_b_gu[l], moe_w_down[l], moe_b_down[l])
    return rms_norm(h, final_norm_g)
```

```python
import functools
import math

import jax
import jax.numpy as jnp
from jax import lax
from jax.experimental import pallas as pl
from jax.experimental.pallas import tpu as pltpu

F32 = jnp.float32
BF16 = jnp.bfloat16
I32 = jnp.int32
HIGHEST = lax.Precision.HIGHEST
EPS = 1e-6

LANES = 128
GROUP_W = 256
HEAD_DIM = 64
N_HEADS = 4
SSD_STATE = 128
RG_C = 8.0
N_EXPERTS = 32
TOP_K = 4
SWIGLU_LIMIT = 7.0
SWIGLU_ALPHA = 1.702
PROJ_BLOCKS = 14
VMEM_LIMIT = 56 * 1024 * 1024

ROW_TILE = 256
ATTN_BLOCK = 256
SCAN_TILE = 256
HG_CHUNK = 16
SSD_CHUNK = 128
MOE_BLOCK = 256
COMBINE_TILE = 128

PLAN_ACTIVE, PLAN_COUNT, PLAN_START, PLAN_PADDED, PLAN_EXPERTS = 0, 1, 2, 3, 4
PLAN_EXPERT_ROWS = 8 - PLAN_EXPERTS


def _params(*sem):
    return pltpu.CompilerParams(dimension_semantics=sem, vmem_limit_bytes=VMEM_LIMIT)


def _dot(a, b):
    return jnp.dot(a, b, preferred_element_type=F32)


def _dot_hi(a, b):
    return jnp.dot(a, b, preferred_element_type=F32, precision=HIGHEST)


def _dot_nt(a, b, precision=None):
    return lax.dot_general(a, b, (((1,), (1,)), ((), ())), preferred_element_type=F32,
                           precision=precision)


def _dot_tn(a, b):
    return lax.dot_general(a, b, (((0,), (0,)), ((), ())), preferred_element_type=F32)


def _sigmoid(x):
    return 1.0 / (1.0 + jnp.exp(-x))


def _silu(x):
    return x * _sigmoid(x)


def _softplus(x):
    return jnp.maximum(x, 0.0) + jnp.log(1.0 + jnp.exp(-jnp.abs(x)))


def _iota(shape, dim):
    return lax.broadcasted_iota(I32, shape, dim)


def _div_pow2(x, n):
    assert n & (n - 1) == 0
    return jnp.right_shift(x, int(math.log2(n)))


def _shift_rows(x, d, prev8):
    r = pltpu.roll(x, d, 0)
    p = pltpu.roll(prev8, d, 0)
    head = jnp.where(_iota(prev8.shape, 0) < d, p, r[:8])
    return jnp.concatenate([head, r[8:]], axis=0)


def _causal_conv(x, prev8, w, b):
    k = w.shape[0]
    y = x * w[k - 1:k] + b
    for d in range(1, k):
        y = y + _shift_rows(x, d, prev8) * w[k - 1 - d:k - d]
    return y


def _norm_inproj_kernel(x_ref, g_ref, w_ref, o_ref):
    x = x_ref[...]
    ms = jnp.mean(x * x, axis=-1, keepdims=True)
    xn = (x * lax.rsqrt(ms + EPS) * g_ref[...]).astype(BF16)
    o_ref[...] = _dot(xn, w_ref[...])


def _norm_inproj(h, g, w):
    t, d = h.shape
    n = w.shape[1]
    return pl.pallas_call(
        _norm_inproj_kernel,
        grid=(t // ROW_TILE,),
        in_specs=[pl.BlockSpec((ROW_TILE, d), lambda i: (i, 0)),
                  pl.BlockSpec((1, d), lambda i: (0, 0)),
                  pl.BlockSpec((d, n), lambda i: (0, 0))],
        out_specs=pl.BlockSpec((ROW_TILE, n), lambda i: (i, 0)),
        out_shape=jax.ShapeDtypeStruct((t, n), F32),
        compiler_params=_params("parallel"),
        name="norm_inproj",
    )(h, g.reshape(1, d), w)


def _sb_attn_kernel(q_ref, k_ref, v_ref, g_ref, o_ref, acc_ref, run_ref):
    i = pl.program_id(1)
    blk = ATTN_BLOCK
    scale = HEAD_DIM ** -0.5
    row = _iota((blk, blk), 0)
    col = _iota((blk, blk), 1)
    below = col < row
    later = (row > col).astype(BF16)
    acc_ref[...] = jnp.zeros_like(acc_ref)
    run_ref[...] = jnp.zeros_like(run_ref)
    qs = [(q_ref[0, :, h * HEAD_DIM:(h + 1) * HEAD_DIM] * scale).astype(BF16) for h in range(N_HEADS)]

    def body(jj, carry):
        j = i - jj
        ks = pl.multiple_of(j * blk, blk)
        valid = jnp.logical_or(jj > 0, below)
        for h in range(N_HEADS):
            sl = slice(h * HEAD_DIM, (h + 1) * HEAD_DIM)
            kh = k_ref[0, pl.ds(ks, blk), sl].astype(BF16)
            vh = v_ref[0, pl.ds(ks, blk), sl].astype(BF16)
            z = _dot_nt(qs[h], kh)
            ls = jnp.minimum(z, 0.0) - jnp.log(1.0 + jnp.exp(-jnp.abs(z)))
            lk = jnp.where(valid, ls - z, 0.0)
            lk_hi = lk.astype(BF16)
            lk_lo = (lk - lk_hi.astype(F32)).astype(BF16)
            run = run_ref[:, h:h + 1]
            rem = _dot(lk_hi, later) + _dot(lk_lo, later) + run
            a = jnp.where(valid, jnp.exp(ls + rem), 0.0)
            acc_ref[:, sl] += _dot(a.astype(BF16), vh)
            run_ref[:, h:h + 1] = run + jnp.sum(lk, axis=-1, keepdims=True)
        return carry

    lax.fori_loop(0, i + 1, body, 0)
    o = acc_ref[...]
    ms = jnp.mean(o * o, axis=-1, keepdims=True)
    o_ref[0] = o * lax.rsqrt(ms + EPS) * g_ref[...]


def _sb_attn(proj, g):
    b, l, _ = proj.shape
    blk = ATTN_BLOCK
    return pl.pallas_call(
        _sb_attn_kernel,
        grid=(b, l // blk),
        in_specs=[pl.BlockSpec((1, blk, GROUP_W), lambda bi, i: (bi, i, 0)),
                  pl.BlockSpec((1, l, GROUP_W), lambda bi, i: (bi, 0, 1)),
                  pl.BlockSpec((1, l, GROUP_W), lambda bi, i: (bi, 0, 2)),
                  pl.BlockSpec((1, GROUP_W), lambda bi, i: (0, 0))],
        out_specs=pl.BlockSpec((1, blk, GROUP_W), lambda bi, i: (bi, i, 0)),
        out_shape=jax.ShapeDtypeStruct((b, l, GROUP_W), F32),
        scratch_shapes=[pltpu.VMEM((blk, GROUP_W), F32), pltpu.VMEM((blk, LANES), F32)],
        compiler_params=_params("parallel", "arbitrary"),
        name="sb_attn",
    )(proj, proj, proj, g.reshape(1, GROUP_W))


def _rglru_kernel(x_ref, gate_ref, cw_ref, cb_ref, wa_ref, ba_ref, wx_ref, bx_ref, lam_ref,
                  ng_ref, o_ref):
    l = x_ref.shape[1]
    w = x_ref.shape[2]
    tc = SCAN_TILE
    rows = _iota((tc, w), 0)
    cw = cw_ref[...]
    cb = cb_ref[...]
    neg_c_sp = -RG_C * _softplus(-lam_ref[...])

    def chunk(c, carry):
        h_prev, tail = carry
        s = pl.multiple_of(c * tc, tc)
        x = x_ref[0, pl.ds(s, tc), :]
        xc = _causal_conv(x, tail, cw, cb)
        xcb = xc.astype(BF16)
        r = _sigmoid(_dot(xcb, wa_ref[...]) + ba_ref[...])
        ig = _sigmoid(_dot(xcb, wx_ref[...]) + bx_ref[...])
        log_a = r * neg_c_sp
        a = jnp.exp(log_a)
        th = jnp.tanh(log_a)
        mult = jnp.sqrt(-2.0 * th / (1.0 - th))
        mult = jnp.where(rows + s == 0, 1.0, mult)
        u = mult * ig * xc
        d = 1
        while d < tc:
            keep = rows >= d
            a_s = jnp.where(keep, pltpu.roll(a, d, 0), 1.0)
            u_s = jnp.where(keep, pltpu.roll(u, d, 0), 0.0)
            u = a * u_s + u
            a = a * a_s
            d *= 2
        hs = u + a * h_prev
        gate = gate_ref[0, pl.ds(s, tc), :]
        gelu = 0.5 * gate * (1.0 + jnp.tanh(math.sqrt(2.0 / math.pi) * (gate + 0.044715 * gate * gate * gate)))
        o = hs * gelu
        ms = jnp.mean(o * o, axis=-1, keepdims=True)
        o_ref[0, pl.ds(s, tc), :] = o * lax.rsqrt(ms + EPS) * ng_ref[...]
        return hs[tc - 1:tc], x[tc - 8:tc]

    lax.fori_loop(0, l // tc, chunk, (jnp.zeros((1, w), F32), jnp.zeros((8, w), F32)))


def _block_diag(wb):
    n, d, _ = wb.shape
    out = jnp.zeros((n * d, n * d), wb.dtype)
    for i in range(n):
        out = lax.dynamic_update_slice(out, wb[i], (i * d, i * d))
    return out


def _rglru(proj, conv_w, conv_b, wa, ba, wx, bx, lam, ng):
    b, l, _ = proj.shape
    w = GROUP_W
    row = lambda v: v.reshape(1, w)
    full = lambda shape: pl.BlockSpec(shape, lambda bi: (0,) * len(shape))
    return pl.pallas_call(
        _rglru_kernel,
        grid=(b,),
        in_specs=[pl.BlockSpec((1, l, w), lambda bi: (bi, 0, 3)),
                  pl.BlockSpec((1, l, w), lambda bi: (bi, 0, 4)),
                  full((conv_w.shape[0], w)), full((1, w)), full((w, w)), full((1, w)),
                  full((w, w)), full((1, w)), full((1, w)), full((1, w))],
        out_specs=pl.BlockSpec((1, l, w), lambda bi: (bi, 0, 0)),
        out_shape=jax.ShapeDtypeStruct((b, l, w), F32),
        compiler_params=_params("parallel"),
        name="rglru",
    )(proj, proj, conv_w, row(conv_b), _block_diag(wa).astype(BF16), row(ba),
      _block_diag(wx).astype(BF16), row(bx), row(lam), row(ng))


def _hgrn2_kernel(q_ref, f_ref, i_ref, g_ref, lb_ref, ng_ref, o_ref,
                  qe_s, ke_s, kl_s, dl_s, o_s):
    l = q_ref.shape[1]
    w = q_ref.shape[2]
    tile = SCAN_TILE
    c = HG_CHUNK
    r_t = _iota((tile, tile), 0)
    c_t = _iota((tile, tile), 1)
    same = _div_pow2(r_t, c) == _div_pow2(c_t, c)
    cum_m = jnp.where(jnp.logical_and(same, c_t <= r_t), 1.0, 0.0)
    tot_m = jnp.where(same, 1.0, 0.0)
    lb = lb_ref[...]

    def prep(t, carry):
        s = pl.multiple_of(t * tile, tile)
        forget = lb + (1.0 - lb) * _sigmoid(f_ref[0, pl.ds(s, tile), :])
        log_f = jnp.log(forget)
        key = 1.0 - forget
        bcum = _dot_hi(cum_m, log_f)
        blast = _dot_hi(tot_m, log_f)
        q = _silu(q_ref[0, pl.ds(s, tile), :])
        qe_s[pl.ds(s, tile), :] = (q * jnp.exp(bcum)).astype(BF16)
        ke_s[pl.ds(s, tile), :] = (key * jnp.exp(-bcum)).astype(BF16)
        kl_s[pl.ds(s, tile), :] = key * jnp.exp(blast - bcum)
        dl_s[pl.ds(s, tile), :] = jnp.exp(blast)
        return carry

    lax.fori_loop(0, l // tile, prep, 0)

    tril = _iota((c, c), 1) <= _iota((c, c), 0)

    def step(n, state):
        s = pl.multiple_of(n * c, c)
        qe = qe_s[pl.ds(s, c), :]
        ke = ke_s[pl.ds(s, c), :]
        kl = kl_s[pl.ds(s, c), :]
        v = i_ref[0, pl.ds(s, c), :]
        dl = dl_s[pl.ds(s, 8), :][0:1]
        vb = v.astype(BF16)
        outs = []
        new_state = []
        for h in range(N_HEADS):
            sl = slice(h * HEAD_DIM, (h + 1) * HEAD_DIM)
            sc = jnp.where(tril, _dot_nt(qe[:, sl], ke[:, sl]), 0.0)
            st = state[h]
            o = _dot(sc.astype(BF16), vb[:, sl]) + _dot_nt(qe[:, sl], st.astype(BF16))
            new_state.append(dl[:, sl] * st + _dot_tn(v[:, sl], kl[:, sl]))
            outs.append(o)
        o_s[pl.ds(s, c), :] = jnp.concatenate(outs, axis=1)
        return tuple(new_state)

    zero = jnp.zeros((HEAD_DIM, HEAD_DIM), F32)
    lax.fori_loop(0, l // c, step, (zero,) * N_HEADS)

    head_avg = jnp.where(_div_pow2(_iota((w, w), 0), HEAD_DIM) == _div_pow2(_iota((w, w), 1), HEAD_DIM),
                         1.0 / HEAD_DIM, 0.0)

    def finish(t, carry):
        s = pl.multiple_of(t * tile, tile)
        o = o_s[pl.ds(s, tile), :]
        ms = _dot_hi(o * o, head_avg)
        o_ref[0, pl.ds(s, tile), :] = (o * lax.rsqrt(ms + EPS) * ng_ref[...]
                                       * _silu(g_ref[0, pl.ds(s, tile), :]))
        return carry

    lax.fori_loop(0, l // tile, finish, 0)


def _hgrn2(proj, lb, ng):
    b, l, _ = proj.shape
    w = GROUP_W
    full = lambda shape: pl.BlockSpec(shape, lambda bi: (0,) * len(shape))
    col = lambda j: pl.BlockSpec((1, l, w), lambda bi: (bi, 0, j))
    return pl.pallas_call(
        _hgrn2_kernel,
        grid=(b,),
        in_specs=[col(5), col(6), col(7), col(8), full((1, w)), full((1, w))],
        out_specs=pl.BlockSpec((1, l, w), lambda bi: (bi, 0, 0)),
        out_shape=jax.ShapeDtypeStruct((b, l, w), F32),
        scratch_shapes=[pltpu.VMEM((l, w), BF16), pltpu.VMEM((l, w), BF16), pltpu.VMEM((l, w), F32),
                        pltpu.VMEM((l, w), F32), pltpu.VMEM((l, w), F32)],
        compiler_params=_params("parallel"),
        name="hgrn2",
    )(proj, proj, proj, proj, lb.reshape(1, w), ng.reshape(1, w))


def _ssd_kernel(z_ref, x_ref, bm_ref, cm_ref, dt_ref, cw_ref, cb_ref, dtb_ref, aneg_ref,
                dskip_ref, ng_ref, o_ref):
    l = z_ref.shape[1]
    w = GROUP_W
    q = SSD_CHUNK
    p = HEAD_DIM
    r_q = _iota((q, q), 0)
    c_q = _iota((q, q), 1)
    tril = c_q <= r_q
    cum_m = jnp.where(tril, 1.0, 0.0)
    expand = jnp.where(jnp.logical_and(_iota((w, w), 0) == _div_pow2(_iota((w, w), 1), p),
                                       _iota((w, w), 0) < N_HEADS), 1.0, 0.0)
    pick = jnp.where(_iota((8, w), 1) == _iota((8, w), 0) * p, 1.0, 0.0)
    cw = cw_ref[...]
    cb = cb_ref[...]

    def chunk(n, carry):
        state, tails = carry
        s = pl.multiple_of(n * q, q)
        parts = []
        new_tails = []
        for j, ref in enumerate((x_ref, bm_ref, cm_ref)):
            raw = ref[0, pl.ds(s, q), :]
            sl = slice(j * w, (j + 1) * w)
            parts.append(_silu(_causal_conv(raw, tails[j], cw[:, sl], cb[:, sl])))
            new_tails.append(raw[q - 8:q])
        xs, bm, cm = parts
        dt = _softplus(dt_ref[0, pl.ds(s, q), :] + dtb_ref[...])
        dt_rep = _dot_hi(dt, expand)
        da_rep = _dot_hi(dt * aneg_ref[...], expand)
        a_cs = _dot_hi(cum_m, da_rep)
        a_row = _dot_nt(pick, a_cs, precision=HIGHEST)
        xd = xs * dt_rep
        ys = []
        new_state = []
        for h in range(N_HEADS):
            grp = h // (N_HEADS // 2)
            gs = slice(grp * SSD_STATE, (grp + 1) * SSD_STATE)
            hs = slice(h * p, (h + 1) * p)
            cg = cm[:, gs].astype(BF16)
            bg = bm[:, gs].astype(BF16)
            cb_qq = _dot_nt(cg, bg)
            a_col = a_cs[:, h * p:h * p + 1]
            decay = jnp.exp(jnp.where(tril, a_col - a_row[h:h + 1, :], -jnp.inf))
            y = _dot((cb_qq * decay).astype(BF16), xd[:, hs].astype(BF16))
            st = state[h]
            y = y + jnp.exp(a_cs[:, hs]) * _dot_nt(cg, st.astype(BF16))
            a_last = a_cs[q - 1:q, hs]
            to_end = jnp.exp(a_last - a_cs[:, hs])
            upd = _dot_tn(xd[:, hs] * to_end, bm[:, gs])
            chunk_decay = jnp.exp(a_row[h:h + 1, q - 1:q])
            new_state.append(chunk_decay * st + upd)
            ys.append(y)
        y = jnp.concatenate(ys, axis=1) + xs * dskip_ref[...]
        y = y * _silu(z_ref[0, pl.ds(s, q), :])
        halves = []
        for gi in range(2):
            yg = y[:, gi * LANES:(gi + 1) * LANES]
            ms = jnp.mean(yg * yg, axis=-1, keepdims=True)
            halves.append(yg * lax.rsqrt(ms + EPS))
        o_ref[0, pl.ds(s, q), :] = jnp.concatenate(halves, axis=1) * ng_ref[...]
        return tuple(new_state), tuple(new_tails)

    zero = jnp.zeros((p, SSD_STATE), F32)
    tail0 = jnp.zeros((8, w), F32)
    lax.fori_loop(0, l // q, chunk, ((zero,) * N_HEADS, (tail0,) * 3))


def _ssd(proj, conv_w, conv_b, dt_bias, a_log, d_skip, ng):
    b, l, _ = proj.shape
    w = GROUP_W
    full = lambda shape: pl.BlockSpec(shape, lambda bi: (0,) * len(shape))
    col = lambda j: pl.BlockSpec((1, l, w), lambda bi: (bi, 0, j))
    pad_heads = lambda v: jnp.zeros((1, w), F32).at[0, :N_HEADS].set(v)
    return pl.pallas_call(
        _ssd_kernel,
        grid=(b,),
        in_specs=[col(9), col(10), col(11), col(12), col(13),
                  full((conv_w.shape[0], 3 * w)), full((1, 3 * w)), full((1, w)), full((1, w)),
                  full((1, w)), full((1, w))],
        out_specs=pl.BlockSpec((1, l, w), lambda bi: (bi, 0, 0)),
        out_shape=jax.ShapeDtypeStruct((b, l, w), F32),
        compiler_params=_params("parallel"),
        name="ssd",
    )(proj, proj, proj, proj, proj, conv_w, conv_b.reshape(1, 3 * w), pad_heads(dt_bias),
      pad_heads(-jnp.exp(a_log)), jnp.repeat(d_skip, HEAD_DIM).reshape(1, w), ng.reshape(1, w))


def _outproj_router_kernel(h_ref, oa_ref, ob_ref, oc_ref, od_ref, wo_ref, g_ref, rw_ref, rb_ref,
                           h_out, xn_out, sel_out, gate_out, rank_out, cnt_out, cnt_s):
    i = pl.program_id(0)
    tm = h_ref.shape[0]

    @pl.when(i == 0)
    def _():
        cnt_s[...] = jnp.zeros_like(cnt_s)

    acc = h_ref[...]
    for j, ref in enumerate((oa_ref, ob_ref, oc_ref, od_ref)):
        acc = acc + _dot(ref[...].astype(BF16), wo_ref[j * GROUP_W:(j + 1) * GROUP_W, :])
    h_out[...] = acc
    ms = jnp.mean(acc * acc, axis=-1, keepdims=True)
    xn = acc * lax.rsqrt(ms + EPS) * g_ref[...]
    xn_out[...] = xn
    lane = _iota((tm, LANES), 1)
    lane_f = lane.astype(F32)
    logits = jnp.where(lane < N_EXPERTS, _dot_hi(xn, rw_ref[...]) + rb_ref[...], -jnp.inf)
    sel = jnp.zeros((tm, LANES), F32)
    tops = []
    for k in range(TOP_K):
        m = jnp.max(logits, axis=-1, keepdims=True)
        idx = jnp.min(jnp.where(logits == m, lane_f, float(LANES)), axis=-1, keepdims=True)
        hit = lane_f == idx
        sel = jnp.where(hit, float(k + 1), sel)
        logits = jnp.where(hit, -jnp.inf, logits)
        tops.append(m)
    exps = [jnp.exp(m - tops[0]) for m in tops]
    denom = exps[0] + exps[1] + exps[2] + exps[3]
    gate = jnp.zeros((tm, LANES), F32)
    for k in range(TOP_K):
        gate = jnp.where(sel == float(k + 1), exps[k] / denom, gate)
    sel_out[...] = sel
    gate_out[...] = gate
    member = jnp.where(sel > 0.0, 1.0, 0.0)
    before = jnp.where(_iota((tm, tm), 1) < _iota((tm, tm), 0), 1.0, 0.0).astype(BF16)
    carry = cnt_s[0:1, :]
    rank_out[...] = _dot(before, member.astype(BF16)) + carry
    total = carry + jnp.sum(member, axis=0, keepdims=True)
    cnt_s[...] = jnp.broadcast_to(total, cnt_s.shape)
    cnt_out[...] = jnp.broadcast_to(total, cnt_out.shape)


def _outproj_router(h, outs, w_out, g, router_w, router_b):
    t, d = h.shape
    tm = ROW_TILE
    rw = jnp.zeros((d, LANES), F32).at[:, :N_EXPERTS].set(router_w)
    rb = jnp.zeros((1, LANES), F32).at[0, :N_EXPERTS].set(router_b)
    tile = lambda width: pl.BlockSpec((tm, width), lambda i: (i, 0))
    full = lambda shape: pl.BlockSpec(shape, lambda i: (0,) * len(shape))
    flat = [o.reshape(t, GROUP_W) for o in outs]
    return pl.pallas_call(
        _outproj_router_kernel,
        grid=(t // tm,),
        in_specs=[tile(d)] + [tile(GROUP_W)] * 4 + [full((d, d)), full((1, d)), full((d, LANES)),
                                                   full((1, LANES))],
        out_specs=[tile(d), tile(d), tile(LANES), tile(LANES), tile(LANES), full((8, LANES))],
        out_shape=[jax.ShapeDtypeStruct((t, d), F32), jax.ShapeDtypeStruct((t, d), F32),
                   jax.ShapeDtypeStruct((t, LANES), F32), jax.ShapeDtypeStruct((t, LANES), F32),
                   jax.ShapeDtypeStruct((t, LANES), F32), jax.ShapeDtypeStruct((8, LANES), F32)],
        scratch_shapes=[pltpu.VMEM((8, LANES), F32)],
        compiler_params=_params("arbitrary"),
        name="outproj_router",
    )(h, *flat, w_out.astype(BF16), g.reshape(1, d), rw, rb)


def _plan_kernel(cnt_ref, start_out, plan_out):
    cnt = cnt_ref[...]
    padded = jnp.ceil(cnt * (1.0 / MOE_BLOCK)) * MOE_BLOCK
    upto = jnp.where(_iota((LANES, LANES), 0) <= _iota((LANES, LANES), 1), 1.0, 0.0)
    pad_end = _dot_hi(padded, upto)
    pad_start = pad_end - padded
    start_out[...] = pad_start
    ends = jnp.transpose(jnp.broadcast_to(pad_end[0:1, :], (LANES, LANES)))
    expert_ok = _iota((LANES, LANES), 0) < N_EXPERTS
    n_active = pad_end[0:1, N_EXPERTS - 1:N_EXPERTS] * (1.0 / MOE_BLOCK)
    rows = [jnp.broadcast_to(n_active, (1, LANES)), cnt[0:1, :], pad_start[0:1, :], padded[0:1, :]]
    for r in range(PLAN_EXPERT_ROWS):
        blk_start = ((_iota((LANES, LANES), 1) + r * LANES) * MOE_BLOCK).astype(F32)
        done = jnp.where(jnp.logical_and(expert_ok, ends <= blk_start), 1.0, 0.0)
        rows.append(jnp.minimum(jnp.sum(done, axis=0, keepdims=True), N_EXPERTS - 1.0))
    plan_out[...] = jnp.concatenate(rows, axis=0).astype(I32)


def _plan(counts, n_blocks):
    assert n_blocks <= PLAN_EXPERT_ROWS * LANES
    return pl.pallas_call(
        _plan_kernel,
        out_shape=[jax.ShapeDtypeStruct((8, LANES), F32), jax.ShapeDtypeStruct((8, LANES), I32)],
        name="moe_plan",
    )(counts)


def _dest_kernel(sel_ref, gate_ref, rank_ref, start_ref, dest_out, gk_out):
    tm = sel_ref.shape[0]
    sel = sel_ref[...]
    gate = gate_ref[...]
    slot = rank_ref[...] + start_ref[0:1, :]
    lane = _iota((tm, LANES), 1)
    pieces = []
    gk = jnp.zeros((tm, LANES), F32)
    for k in range(TOP_K):
        hit = sel == float(k + 1)
        pieces.append(jnp.where(hit, slot, 0.0))
        gk = jnp.where(lane == k, jnp.sum(jnp.where(hit, gate, 0.0), axis=-1, keepdims=True), gk)
    stacked = jnp.concatenate(pieces, axis=1)
    picker = jnp.where(_div_pow2(_iota((8, TOP_K * LANES), 1), LANES) == _iota((8, TOP_K * LANES), 0),
                       1.0, 0.0)
    dest_out[...] = _dot_nt(picker, stacked, precision=HIGHEST).astype(I32)
    gk_out[...] = gk


def _dest(sel, gate, rank, pad_start):
    t = sel.shape[0]
    tm = ROW_TILE
    tile = pl.BlockSpec((tm, LANES), lambda i: (i, 0))
    return pl.pallas_call(
        _dest_kernel,
        grid=(t // tm,),
        in_specs=[tile, tile, tile, pl.BlockSpec((8, LANES), lambda i: (0, 0))],
        out_specs=[pl.BlockSpec((8, tm), lambda i: (0, i)), tile],
        out_shape=[jax.ShapeDtypeStruct((8, t), I32), jax.ShapeDtypeStruct((t, LANES), F32)],
        compiler_params=_params("parallel"),
        name="moe_dest",
    )(sel, gate, rank, pad_start)


def _dispatch_kernel(plan_ref, dest_ref, xn_hbm, xs_hbm, sem):
    i = pl.program_id(0)
    tm = dest_ref.shape[1]

    def row_copy(src_row, dst_row):
        return pltpu.make_async_copy(xn_hbm.at[pl.ds(src_row, 1)], xs_hbm.at[pl.ds(dst_row, 1)], sem)

    def issue(t, carry):
        for k in range(TOP_K):
            row_copy(i * tm + t, dest_ref[k, t]).start()
        return carry

    lax.fori_loop(0, tm, issue, 0)

    def drain(t, carry):
        for k in range(TOP_K):
            row_copy(0, 0).wait()
        return carry

    lax.fori_loop(0, tm, drain, 0)

    @pl.when(i == 0)
    def _():
        def per_expert(e, carry):
            cnt = plan_ref[PLAN_COUNT, e]
            first = plan_ref[PLAN_START, e] + cnt
            n_pad = plan_ref[PLAN_PADDED, e] - cnt

            def start(r, c):
                row_copy(0, first + r).start()
                return c

            def wait(r, c):
                row_copy(0, 0).wait()
                return c

            lax.fori_loop(0, n_pad, start, 0)
            lax.fori_loop(0, n_pad, wait, 0)
            return carry

        lax.fori_loop(0, N_EXPERTS, per_expert, 0)


def _dispatch(plan, dest, xn, n_slots):
    t, d = xn.shape
    tm = ROW_TILE
    return pl.pallas_call(
        _dispatch_kernel,
        grid=(t // tm,),
        in_specs=[pl.BlockSpec(memory_space=pltpu.SMEM),
                  pl.BlockSpec((8, tm), lambda i: (0, i), memory_space=pltpu.SMEM),
                  pl.BlockSpec(memory_space=pl.ANY)],
        out_specs=pl.BlockSpec(memory_space=pl.ANY),
        out_shape=jax.ShapeDtypeStruct((n_slots, d), F32),
        scratch_shapes=[pltpu.SemaphoreType.DMA(())],
        compiler_params=pltpu.CompilerParams(dimension_semantics=("arbitrary",),
                                             has_side_effects=True),
        name="moe_dispatch",
    )(plan, dest, xn)


def _ffn_kernel(be_ref, na_ref, x_ref, wg_ref, wu_ref, bg_ref, bu_ref, wd_ref, bd_ref, o_ref):
    j = pl.program_id(0)

    @pl.when(j < na_ref[0])
    def _():
        x = x_ref[...].astype(BF16)
        glu = jnp.minimum(_dot(x, wg_ref[0]) + bg_ref[0], SWIGLU_LIMIT)
        lin = jnp.clip(_dot(x, wu_ref[0]) + bu_ref[0], -SWIGLU_LIMIT, SWIGLU_LIMIT)
        act = glu * _sigmoid(SWIGLU_ALPHA * glu) * (lin + 1.0)
        o_ref[...] = _dot(act.astype(BF16), wd_ref[0]) + bd_ref[0]


def _ffn(block_expert, n_active, xs, wg, wu, bg, bu, wd, bd):
    n_slots, d = xs.shape
    dff = wg.shape[2]
    nb = n_slots // MOE_BLOCK
    rows = lambda j, be, na: (jnp.minimum(j, na[0] - 1), 0)
    per_expert = lambda j, be, na: (be[j], 0, 0)
    grid_spec = pltpu.PrefetchScalarGridSpec(
        num_scalar_prefetch=2,
        grid=(nb,),
        in_specs=[pl.BlockSpec((MOE_BLOCK, d), rows),
                  pl.BlockSpec((1, d, dff), per_expert), pl.BlockSpec((1, d, dff), per_expert),
                  pl.BlockSpec((1, 1, dff), per_expert), pl.BlockSpec((1, 1, dff), per_expert),
                  pl.BlockSpec((1, dff, d), per_expert), pl.BlockSpec((1, 1, d), per_expert)],
        out_specs=pl.BlockSpec((MOE_BLOCK, d), rows),
    )
    return pl.pallas_call(
        _ffn_kernel,
        grid_spec=grid_spec,
        out_shape=jax.ShapeDtypeStruct((n_slots, d), F32),
        compiler_params=_params("arbitrary"),
        name="moe_ffn",
    )(block_expert, n_active, xs, wg, wu, bg, bu, wd, bd)


def _combine_kernel(dest_ref, gk_ref, h_ref, g_ref, ys_hbm, o_ref, buf, sem, *, final_norm):
    tm = h_ref.shape[0]

    def row_copy(src_row, k, t):
        return pltpu.make_async_copy(ys_hbm.at[pl.ds(src_row, 1)], buf.at[k, pl.ds(t, 1)], sem)

    def issue(t, carry):
        for k in range(TOP_K):
            row_copy(dest_ref[k, t], k, t).start()
        return carry

    lax.fori_loop(0, tm, issue, 0)

    def drain(t, carry):
        for k in range(TOP_K):
            row_copy(0, 0, 0).wait()
        return carry

    lax.fori_loop(0, tm, drain, 0)
    acc = h_ref[...]
    gk = gk_ref[...]
    for k in range(TOP_K):
        acc = acc + gk[:, k:k + 1] * buf[k]
    if final_norm:
        ms = jnp.mean(acc * acc, axis=-1, keepdims=True)
        acc = acc * lax.rsqrt(ms + EPS) * g_ref[...]
    o_ref[...] = acc


def _combine(dest, gk, h, ys, g, final_norm):
    t, d = h.shape
    tm = COMBINE_TILE
    return pl.pallas_call(
        functools.partial(_combine_kernel, final_norm=final_norm),
        grid=(t // tm,),
        in_specs=[pl.BlockSpec((8, tm), lambda i: (0, i), memory_space=pltpu.SMEM),
                  pl.BlockSpec((tm, LANES), lambda i: (i, 0)),
                  pl.BlockSpec((tm, d), lambda i: (i, 0)),
                  pl.BlockSpec((1, d), lambda i: (0, 0)),
                  pl.BlockSpec(memory_space=pl.ANY)],
        out_specs=pl.BlockSpec((tm, d), lambda i: (i, 0)),
        out_shape=jax.ShapeDtypeStruct((t, d), F32),
        scratch_shapes=[pltpu.VMEM((TOP_K, tm, d), F32), pltpu.SemaphoreType.DMA(())],
        compiler_params=_params("arbitrary"),
        name="moe_combine",
    )(dest, gk, h, g.reshape(1, d), ys)


def _moe(h, xn, sel, gate, rank, counts, w_gu, b_gu, w_down, b_down, g_final, final_norm):
    t, d = h.shape
    n_slots = t * TOP_K + N_EXPERTS * MOE_BLOCK
    n_blocks = n_slots // MOE_BLOCK
    pad_start, plan = _plan(counts, n_blocks)
    dest, gk = _dest(sel, gate, rank, pad_start)
    xs = _dispatch(plan, dest, xn, n_slots)
    block_expert = plan[PLAN_EXPERTS:].reshape(-1)[:n_blocks]
    n_active = plan[PLAN_ACTIVE, :1]
    dff = w_down.shape[1]
    wg = w_gu[:, :, 0::2].astype(BF16)
    wu = w_gu[:, :, 1::2].astype(BF16)
    bg = b_gu[:, 0::2].reshape(N_EXPERTS, 1, dff)
    bu = b_gu[:, 1::2].reshape(N_EXPERTS, 1, dff)
    ys = _ffn(block_expert, n_active, xs, wg, wu, bg, bu, w_down.astype(BF16),
              b_down.reshape(N_EXPERTS, 1, d))
    return _combine(dest, gk, h, ys, g_final, final_norm)


def kernel(x, norm_mix_g, w_in, sb_norm_g, rg_conv_w, rg_conv_b, rg_wa, rg_ba, rg_wx, rg_bx,
           rg_lambda, rg_norm_g, hg_lower_bounds, hg_norm_g, m2_conv_w, m2_conv_b, m2_dt_bias,
           m2_a_log, m2_d, m2_norm_g, w_out, norm_ffn_g, router_w, router_b, moe_w_gu, moe_b_gu,
           moe_w_down, moe_b_down, final_norm_g):
    b, l, d = x.shape
    depth = w_in.shape[0]
    t = b * l
    lbs = jnp.cumsum(jax.nn.softmax(hg_lower_bounds.astype(F32), axis=0), axis=0)
    lbs = lbs - lbs[0]
    d_in = w_in.shape[2]
    w_in_p = jnp.zeros((depth, d, PROJ_BLOCKS * GROUP_W), BF16).at[:, :, :d_in].set(w_in.astype(BF16))
    h = x.reshape(t, d)
    for layer in range(depth):
        proj = _norm_inproj(h, norm_mix_g[layer], w_in_p[layer]).reshape(b, l, PROJ_BLOCKS * GROUP_W)
        o_a = _sb_attn(proj, sb_norm_g[layer])
        o_b = _rglru(proj, rg_conv_w[layer], rg_conv_b[layer], rg_wa[layer], rg_ba[layer],
                     rg_wx[layer], rg_bx[layer], rg_lambda[layer], rg_norm_g[layer])
        o_c = _hgrn2(proj, lbs[layer], hg_norm_g[layer])
        o_d = _ssd(proj, m2_conv_w[layer], m2_conv_b[layer], m2_dt_bias[layer], m2_a_log[layer],
                   m2_d[layer], m2_norm_g[layer])
        h, xn, sel, gate, rank, counts = _outproj_router(
            h, (o_a, o_b, o_c, o_d), w_out[layer], norm_ffn_g[layer], router_w[layer],
            router_b[layer])
        h = _moe(h, xn, sel, gate, rank, counts, moe_w_gu[layer], moe_b_gu[layer],
                 moe_w_down[layer], moe_b_down[layer], final_norm_g, layer == depth - 1)
    return h.reshape(b, l, d)
```

```python
import functools
import math

import jax
import jax.numpy as jnp
from jax import lax
from jax.experimental import pallas as pl
from jax.experimental.pallas import tpu as pltpu

F32 = jnp.float32
BF16 = jnp.bfloat16
I32 = jnp.int32
HIGHEST = lax.Precision.HIGHEST
EPS = 1e-6

LANES = 128
GROUP_W = 256
HEAD_DIM = 64
N_HEADS = 4
SSD_STATE = 128
RG_C = 8.0
N_EXPERTS = 32
TOP_K = 4
SWIGLU_LIMIT = 7.0
SWIGLU_ALPHA = 1.702
PROJ_BLOCKS = 14
VMEM_LIMIT = 56 * 1024 * 1024

ROW_TILE = 256
ATTN_BLOCK = 256
SCAN_TILE = 256
HG_CHUNK = 16
SSD_CHUNK = 128
MOE_BLOCK = 256
COMBINE_TILE = 128

PLAN_ACTIVE, PLAN_COUNT, PLAN_START, PLAN_PADDED, PLAN_EXPERTS = 0, 1, 2, 3, 4
PLAN_EXPERT_ROWS = 8 - PLAN_EXPERTS


def _params(*sem):
    return pltpu.CompilerParams(dimension_semantics=sem, vmem_limit_bytes=VMEM_LIMIT)


def _dot(a, b):
    return jnp.dot(a, b, preferred_element_type=F32)


def _dot_hi(a, b):
    return jnp.dot(a, b, preferred_element_type=F32, precision=HIGHEST)


def _dot_nt(a, b, precision=None):
    return lax.dot_general(a, b, (((1,), (1,)), ((), ())), preferred_element_type=F32,
                           precision=precision)


def _dot_tn(a, b):
    return lax.dot_general(a, b, (((0,), (0,)), ((), ())), preferred_element_type=F32)


def _sigmoid(x):
    return 1.0 / (1.0 + jnp.exp(-x))


def _silu(x):
    return x * _sigmoid(x)


def _softplus(x):
    return jnp.maximum(x, 0.0) + jnp.log(1.0 + jnp.exp(-jnp.abs(x)))


def _iota(shape, dim):
    return lax.broadcasted_iota(I32, shape, dim)


def _div_pow2(x, n):
    assert n & (n - 1) == 0
    return jnp.right_shift(x, int(math.log2(n)))


def _shift_rows(x, d, prev8):
    r = pltpu.roll(x, d, 0)
    p = pltpu.roll(prev8, d, 0)
    head = jnp.where(_iota(prev8.shape, 0) < d, p, r[:8])
    return jnp.concatenate([head, r[8:]], axis=0)


def _causal_conv(x, prev8, w, b):
    k = w.shape[0]
    y = x * w[k - 1:k] + b
    for d in range(1, k):
        y = y + _shift_rows(x, d, prev8) * w[k - 1 - d:k - d]
    return y


def _norm_inproj_kernel(x_ref, g_ref, w_ref, o_ref):
    x = x_ref[...]
    ms = jnp.mean(x * x, axis=-1, keepdims=True)
    xn = (x * lax.rsqrt(ms + EPS) * g_ref[...]).astype(BF16)
    o_ref[...] = _dot(xn, w_ref[...])


def _norm_inproj(h, g, w):
    t, d = h.shape
    n = w.shape[1]
    return pl.pallas_call(
        _norm_inproj_kernel,
        grid=(t // ROW_TILE,),
        in_specs=[pl.BlockSpec((ROW_TILE, d), lambda i: (i, 0)),
                  pl.BlockSpec((1, d), lambda i: (0, 0)),
                  pl.BlockSpec((d, n), lambda i: (0, 0))],
        out_specs=pl.BlockSpec((ROW_TILE, n), lambda i: (i, 0)),
        out_shape=jax.ShapeDtypeStruct((t, n), F32),
        compiler_params=_params("parallel"),
        name="norm_inproj",
    )(h, g.reshape(1, d), w)


def _sb_attn_kernel(q_ref, k_ref, v_ref, g_ref, o_ref, acc_ref, run_ref):
    i = pl.program_id(1)
    blk = ATTN_BLOCK
    scale = HEAD_DIM ** -0.5
    row = _iota((blk, blk), 0)
    col = _iota((blk, blk), 1)
    below = col < row
    later = (row > col).astype(BF16)
    acc_ref[...] = jnp.zeros_like(acc_ref)
    run_ref[...] = jnp.zeros_like(run_ref)
    qs = [(q_ref[0, :, h * HEAD_DIM:(h + 1) * HEAD_DIM] * scale).astype(BF16) for h in range(N_HEADS)]

    def body(jj, carry):
        j = i - jj
        ks = pl.multiple_of(j * blk, blk)
        valid = jnp.logical_or(jj > 0, below)
        for h in range(N_HEADS):
            sl = slice(h * HEAD_DIM, (h + 1) * HEAD_DIM)
            kh = k_ref[0, pl.ds(ks, blk), sl].astype(BF16)
            vh = v_ref[0, pl.ds(ks, blk), sl].astype(BF16)
            z = _dot_nt(qs[h], kh)
            ls = jnp.minimum(z, 0.0) - jnp.log(1.0 + jnp.exp(-jnp.abs(z)))
            lk = jnp.where(valid, ls - z, 0.0)
            lk_hi = lk.astype(BF16)
            lk_lo = (lk - lk_hi.astype(F32)).astype(BF16)
            run = run_ref[:, h:h + 1]
            rem = _dot(lk_hi, later) + _dot(lk_lo, later) + run
            a = jnp.where(valid, jnp.exp(ls + rem), 0.0)
            acc_ref[:, sl] += _dot(a.astype(BF16), vh)
            run_ref[:, h:h + 1] = run + jnp.sum(lk, axis=-1, keepdims=True)
        return carry

    lax.fori_loop(0, i + 1, body, 0)
    o = acc_ref[...]
    ms = jnp.mean(o * o, axis=-1, keepdims=True)
    o_ref[0] = o * lax.rsqrt(ms + EPS) * g_ref[...]


def _sb_attn(proj, g):
    b, l, _ = proj.shape
    blk = ATTN_BLOCK
    return pl.pallas_call(
        _sb_attn_kernel,
        grid=(b, l // blk),
        in_specs=[pl.BlockSpec((1, blk, GROUP_W), lambda bi, i: (bi, i, 0)),
                  pl.BlockSpec((1, l, GROUP_W), lambda bi, i: (bi, 0, 1)),
                  pl.BlockSpec((1, l, GROUP_W), lambda bi, i: (bi, 0, 2)),
                  pl.BlockSpec((1, GROUP_W), lambda bi, i: (0, 0))],
        out_specs=pl.BlockSpec((1, blk, GROUP_W), lambda bi, i: (bi, i, 0)),
        out_shape=jax.ShapeDtypeStruct((b, l, GROUP_W), F32),
        scratch_shapes=[pltpu.VMEM((blk, GROUP_W), F32), pltpu.VMEM((blk, LANES), F32)],
        compiler_params=_params("parallel", "arbitrary"),
        name="sb_attn",
    )(proj, proj, proj, g.reshape(1, GROUP_W))


def _rglru_kernel(x_ref, gate_ref, cw_ref, cb_ref, wa_ref, ba_ref, wx_ref, bx_ref, lam_ref,
                  ng_ref, o_ref):
    l = x_ref.shape[1]
    w = x_ref.shape[2]
    tc = SCAN_TILE
    rows = _iota((tc, w), 0)
    cw = cw_ref[...]
    cb = cb_ref[...]
    neg_c_sp = -RG_C * _softplus(-lam_ref[...])

    def chunk(c, carry):
        h_prev, tail = carry
        s = pl.multiple_of(c * tc, tc)
        x = x_ref[0, pl.ds(s, tc), :]
        xc = _causal_conv(x, tail, cw, cb)
        xcb = xc.astype(BF16)
        r = _sigmoid(_dot(xcb, wa_ref[...]) + ba_ref[...])
        ig = _sigmoid(_dot(xcb, wx_ref[...]) + bx_ref[...])
        log_a = r * neg_c_sp
        a = jnp.exp(log_a)
        th = jnp.tanh(log_a)
        mult = jnp.sqrt(-2.0 * th / (1.0 - th))
        mult = jnp.where(rows + s == 0, 1.0, mult)
        u = mult * ig * xc
        d = 1
        while d < tc:
            keep = rows >= d
            a_s = jnp.where(keep, pltpu.roll(a, d, 0), 1.0)
            u_s = jnp.where(keep, pltpu.roll(u, d, 0), 0.0)
            u = a * u_s + u
            a = a * a_s
            d *= 2
        hs = u + a * h_prev
        gate = gate_ref[0, pl.ds(s, tc), :]
        gelu = 0.5 * gate * (1.0 + jnp.tanh(math.sqrt(2.0 / math.pi) * (gate + 0.044715 * gate * gate * gate)))
        o = hs * gelu
        ms = jnp.mean(o * o, axis=-1, keepdims=True)
        o_ref[0, pl.ds(s, tc), :] = o * lax.rsqrt(ms + EPS) * ng_ref[...]
        return hs[tc - 1:tc], x[tc - 8:tc]

    lax.fori_loop(0, l // tc, chunk, (jnp.zeros((1, w), F32), jnp.zeros((8, w), F32)))


def _block_diag(wb):
    n, d, _ = wb.shape
    out = jnp.zeros((n * d, n * d), wb.dtype)
    for i in range(n):
        out = lax.dynamic_update_slice(out, wb[i], (i * d, i * d))
    return out


def _rglru(proj, conv_w, conv_b, wa, ba, wx, bx, lam, ng):
    b, l, _ = proj.shape
    w = GROUP_W
    row = lambda v: v.reshape(1, w)
    full = lambda shape: pl.BlockSpec(shape, lambda bi: (0,) * len(shape))
    return pl.pallas_call(
        _rglru_kernel,
        grid=(b,),
        in_specs=[pl.BlockSpec((1, l, w), lambda bi: (bi, 0, 3)),
                  pl.BlockSpec((1, l, w), lambda bi: (bi, 0, 4)),
                  full((conv_w.shape[0], w)), full((1, w)), full((w, w)), full((1, w)),
                  full((w, w)), full((1, w)), full((1, w)), full((1, w))],
        out_specs=pl.BlockSpec((1, l, w), lambda bi: (bi, 0, 0)),
        out_shape=jax.ShapeDtypeStruct((b, l, w), F32),
        compiler_params=_params("parallel"),
        name="rglru",
    )(proj, proj, conv_w, row(conv_b), _block_diag(wa).astype(BF16), row(ba),
      _block_diag(wx).astype(BF16), row(bx), row(lam), row(ng))


def _hgrn2_kernel(q_ref, f_ref, i_ref, g_ref, lb_ref, ng_ref, o_ref,
                  qe_s, ke_s, kl_s, dl_s, o_s):
    l = q_ref.shape[1]
    w = q_ref.shape[2]
    tile = SCAN_TILE
    c = HG_CHUNK
    r_t = _iota((tile, tile), 0)
    c_t = _iota((tile, tile), 1)
    same = _div_pow2(r_t, c) == _div_pow2(c_t, c)
    cum_m = jnp.where(jnp.logical_and(same, c_t <= r_t), 1.0, 0.0)
    tot_m = jnp.where(same, 1.0, 0.0)
    lb = lb_ref[...]

    def prep(t, carry):
        s = pl.multiple_of(t * tile, tile)
        forget = lb + (1.0 - lb) * _sigmoid(f_ref[0, pl.ds(s, tile), :])
        log_f = jnp.log(forget)
        key = 1.0 - forget
        bcum = _dot_hi(cum_m, log_f)
        blast = _dot_hi(tot_m, log_f)
        q = _silu(q_ref[0, pl.ds(s, tile), :])
        qe_s[pl.ds(s, tile), :] = (q * jnp.exp(bcum)).astype(BF16)
        ke_s[pl.ds(s, tile), :] = (key * jnp.exp(-bcum)).astype(BF16)
        kl_s[pl.ds(s, tile), :] = key * jnp.exp(blast - bcum)
        dl_s[pl.ds(s, tile), :] = jnp.exp(blast)
        return carry

    lax.fori_loop(0, l // tile, prep, 0)

    tril = _iota((c, c), 1) <= _iota((c, c), 0)

    def step(n, state):
        s = pl.multiple_of(n * c, c)
        qe = qe_s[pl.ds(s, c), :]
        ke = ke_s[pl.ds(s, c), :]
        kl = kl_s[pl.ds(s, c), :]
        v = i_ref[0, pl.ds(s, c), :]
        dl = dl_s[pl.ds(s, 8), :][0:1]
        vb = v.astype(BF16)
        outs = []
        new_state = []
        for h in range(N_HEADS):
            sl = slice(h * HEAD_DIM, (h + 1) * HEAD_DIM)
            sc = jnp.where(tril, _dot_nt(qe[:, sl], ke[:, sl]), 0.0)
            st = state[h]
            o = _dot(sc.astype(BF16), vb[:, sl]) + _dot_nt(qe[:, sl], st.astype(BF16))
            new_state.append(dl[:, sl] * st + _dot_tn(v[:, sl], kl[:, sl]))
            outs.append(o)
        o_s[pl.ds(s, c), :] = jnp.concatenate(outs, axis=1)
        return tuple(new_state)

    zero = jnp.zeros((HEAD_DIM, HEAD_DIM), F32)
    lax.fori_loop(0, l // c, step, (zero,) * N_HEADS)

    head_avg = jnp.where(_div_pow2(_iota((w, w), 0), HEAD_DIM) == _div_pow2(_iota((w, w), 1), HEAD_DIM),
                         1.0 / HEAD_DIM, 0.0)

    def finish(t, carry):
        s = pl.multiple_of(t * tile, tile)
        o = o_s[pl.ds(s, tile), :]
        ms = _dot_hi(o * o, head_avg)
        o_ref[0, pl.ds(s, tile), :] = (o * lax.rsqrt(ms + EPS) * ng_ref[...]
                                       * _silu(g_ref[0, pl.ds(s, tile), :]))
        return carry

    lax.fori_loop(0, l // tile, finish, 0)


def _hgrn2(proj, lb, ng):
    b, l, _ = proj.shape
    w = GROUP_W
    full = lambda shape: pl.BlockSpec(shape, lambda bi: (0,) * len(shape))
    col = lambda j: pl.BlockSpec((1, l, w), lambda bi: (bi, 0, j))
    return pl.pallas_call(
        _hgrn2_kernel,
        grid=(b,),
        in_specs=[col(5), col(6), col(7), col(8), full((1, w)), full((1, w))],
        out_specs=pl.BlockSpec((1, l, w), lambda bi: (bi, 0, 0)),
        out_shape=jax.ShapeDtypeStruct((b, l, w), F32),
        scratch_shapes=[pltpu.VMEM((l, w), BF16), pltpu.VMEM((l, w), BF16), pltpu.VMEM((l, w), F32),
                        pltpu.VMEM((l, w), F32), pltpu.VMEM((l, w), F32)],
        compiler_params=_params("parallel"),
        name="hgrn2",
    )(proj, proj, proj, proj, lb.reshape(1, w), ng.reshape(1, w))


def _ssd_kernel(z_ref, x_ref, bm_ref, cm_ref, dt_ref, cw_ref, cb_ref, dtb_ref, aneg_ref,
                dskip_ref, ng_ref, o_ref):
    l = z_ref.shape[1]
    w = GROUP_W
    q = SSD_CHUNK
    p = HEAD_DIM
    r_q = _iota((q, q), 0)
    c_q = _iota((q, q), 1)
    tril = c_q <= r_q
    cum_m = jnp.where(tril, 1.0, 0.0)
    expand = jnp.where(jnp.logical_and(_iota((w, w), 0) == _div_pow2(_iota((w, w), 1), p),
                                       _iota((w, w), 0) < N_HEADS), 1.0, 0.0)
    pick = jnp.where(_iota((8, w), 1) == _iota((8, w), 0) * p, 1.0, 0.0)
    cw = cw_ref[...]
    cb = cb_ref[...]

    def chunk(n, carry):
        state, tails = carry
        s = pl.multiple_of(n * q, q)
        parts = []
        new_tails = []
        for j, ref in enumerate((x_ref, bm_ref, cm_ref)):
            raw = ref[0, pl.ds(s, q), :]
            sl = slice(j * w, (j + 1) * w)
            parts.append(_silu(_causal_conv(raw, tails[j], cw[:, sl], cb[:, sl])))
            new_tails.append(raw[q - 8:q])
        xs, bm, cm = parts
        dt = _softplus(dt_ref[0, pl.ds(s, q), :] + dtb_ref[...])
        dt_rep = _dot_hi(dt, expand)
        da_rep = _dot_hi(dt * aneg_ref[...], expand)
        a_cs = _dot_hi(cum_m, da_rep)
        a_row = _dot_nt(pick, a_cs, precision=HIGHEST)
        xd = xs * dt_rep
        ys = []
        new_state = []
        for h in range(N_HEADS):
            grp = h // (N_HEADS // 2)
            gs = slice(grp * SSD_STATE, (grp + 1) * SSD_STATE)
            hs = slice(h * p, (h + 1) * p)
            cg = cm[:, gs].astype(BF16)
            bg = bm[:, gs].astype(BF16)
            cb_qq = _dot_nt(cg, bg)
            a_col = a_cs[:, h * p:h * p + 1]
            decay = jnp.exp(jnp.where(tril, a_col - a_row[h:h + 1, :], -jnp.inf))
            y = _dot((cb_qq * decay).astype(BF16), xd[:, hs].astype(BF16))
            st = state[h]
            y = y + jnp.exp(a_cs[:, hs]) * _dot_nt(cg, st.astype(BF16))
            a_last = a_cs[q - 1:q, hs]
            to_end = jnp.exp(a_last - a_cs[:, hs])
            upd = _dot_tn(xd[:, hs] * to_end, bm[:, gs])
            chunk_decay = jnp.exp(a_row[h:h + 1, q - 1:q])
            new_state.append(chunk_decay * st + upd)
            ys.append(y)
        y = jnp.concatenate(ys, axis=1) + xs * dskip_ref[...]
        y = y * _silu(z_ref[0, pl.ds(s, q), :])
        halves = []
        for gi in range(2):
            yg = y[:, gi * LANES:(gi + 1) * LANES]
            ms = jnp.mean(yg * yg, axis=-1, keepdims=True)
            halves.append(yg * lax.rsqrt(ms + EPS))
        o_ref[0, pl.ds(s, q), :] = jnp.concatenate(halves, axis=1) * ng_ref[...]
        return tuple(new_state), tuple(new_tails)

    zero = jnp.zeros((p, SSD_STATE), F32)
    tail0 = jnp.zeros((8, w), F32)
    lax.fori_loop(0, l // q, chunk, ((zero,) * N_HEADS, (tail0,) * 3))


def _ssd(proj, conv_w, conv_b, dt_bias, a_log, d_skip, ng):
    b, l, _ = proj.shape
    w = GROUP_W
    full = lambda shape: pl.BlockSpec(shape, lambda bi: (0,) * len(shape))
    col = lambda j: pl.BlockSpec((1, l, w), lambda bi: (bi, 0, j))
    pad_heads = lambda v: jnp.zeros((1, w), F32).at[0, :N_HEADS].set(v)
    return pl.pallas_call(
        _ssd_kernel,
        grid=(b,),
        in_specs=[col(9), col(10), col(11), col(12), col(13),
                  full((conv_w.shape[0], 3 * w)), full((1, 3 * w)), full((1, w)), full((1, w)),
                  full((1, w)), full((1, w))],
        out_specs=pl.BlockSpec((1, l, w), lambda bi: (bi, 0, 0)),
        out_shape=jax.ShapeDtypeStruct((b, l, w), F32),
        compiler_params=_params("parallel"),
        name="ssd",
    )(proj, proj, proj, proj, proj, conv_w, conv_b.reshape(1, 3 * w), pad_heads(dt_bias),
      pad_heads(-jnp.exp(a_log)), jnp.repeat(d_skip, HEAD_DIM).reshape(1, w), ng.reshape(1, w))


def _outproj_router_kernel(h_ref, oa_ref, ob_ref, oc_ref, od_ref, wo_ref, g_ref, rw_ref, rb_ref,
                           h_out, xn_out, sel_out, gate_out, rank_out, cnt_out, cnt_s):
    i = pl.program_id(0)
    tm = h_ref.shape[0]

    @pl.when(i == 0)
    def _():
        cnt_s[...] = jnp.zeros_like(cnt_s)

    acc = h_ref[...]
    for j, ref in enumerate((oa_ref, ob_ref, oc_ref, od_ref)):
        acc = acc + _dot(ref[...].astype(BF16), wo_ref[j * GROUP_W:(j + 1) * GROUP_W, :])
    h_out[...] = acc
    ms = jnp.mean(acc * acc, axis=-1, keepdims=True)
    xn = acc * lax.rsqrt(ms + EPS) * g_ref[...]
    xn_out[...] = xn
    lane = _iota((tm, LANES), 1)
    lane_f = lane.astype(F32)
    logits = jnp.where(lane < N_EXPERTS, _dot_hi(xn, rw_ref[...]) + rb_ref[...], -jnp.inf)
    sel = jnp.zeros((tm, LANES), F32)
    tops = []
    for k in range(TOP_K):
        m = jnp.max(logits, axis=-1, keepdims=True)
        idx = jnp.min(jnp.where(logits == m, lane_f, float(LANES)), axis=-1, keepdims=True)
        hit = lane_f == idx
        sel = jnp.where(hit, float(k + 1), sel)
        logits = jnp.where(hit, -jnp.inf, logits)
        tops.append(m)
    exps = [jnp.exp(m - tops[0]) for m in tops]
    denom = exps[0] + exps[1] + exps[2] + exps[3]
    gate = jnp.zeros((tm, LANES), F32)
    for k in range(TOP_K):
        gate = jnp.where(sel == float(k + 1), exps[k] / denom, gate)
    sel_out[...] = sel
    gate_out[...] = gate
    member = jnp.where(sel > 0.0, 1.0, 0.0)
    before = jnp.where(_iota((tm, tm), 1) < _iota((tm, tm), 0), 1.0, 0.0).astype(BF16)
    carry = cnt_s[0:1, :]
    rank_out[...] = _dot(before, member.astype(BF16)) + carry
    total = carry + jnp.sum(member, axis=0, keepdims=True)
    cnt_s[...] = jnp.broadcast_to(total, cnt_s.shape)
    cnt_out[...] = jnp.broadcast_to(total, cnt_out.shape)


def _outproj_router(h, outs, w_out, g, router_w, router_b):
    t, d = h.shape
    tm = ROW_TILE
    rw = jnp.zeros((d, LANES), F32).at[:, :N_EXPERTS].set(router_w)
    rb = jnp.zeros((1, LANES), F32).at[0, :N_EXPERTS].set(router_b)
    tile = lambda width: pl.BlockSpec((tm, width), lambda i: (i, 0))
    full = lambda shape: pl.BlockSpec(shape, lambda i: (0,) * len(shape))
    flat = [o.reshape(t, GROUP_W) for o in outs]
    return pl.pallas_call(
        _outproj_router_kernel,
        grid=(t // tm,),
        in_specs=[tile(d)] + [tile(GROUP_W)] * 4 + [full((d, d)), full((1, d)), full((d, LANES)),
                                                   full((1, LANES))],
        out_specs=[tile(d), tile(d), tile(LANES), tile(LANES), tile(LANES), full((8, LANES))],
        out_shape=[jax.ShapeDtypeStruct((t, d), F32), jax.ShapeDtypeStruct((t, d), F32),
                   jax.ShapeDtypeStruct((t, LANES), F32), jax.ShapeDtypeStruct((t, LANES), F32),
                   jax.ShapeDtypeStruct((t, LANES), F32), jax.ShapeDtypeStruct((8, LANES), F32)],
        scratch_shapes=[pltpu.VMEM((8, LANES), F32)],
        compiler_params=_params("arbitrary"),
        name="outproj_router",
    )(h, *flat, w_out.astype(BF16), g.reshape(1, d), rw, rb)


def _plan_kernel(cnt_ref, start_out, plan_out):
    cnt = cnt_ref[...]
    padded = jnp.ceil(cnt * (1.0 / MOE_BLOCK)) * MOE_BLOCK
    upto = jnp.where(_iota((LANES, LANES), 0) <= _iota((LANES, LANES), 1), 1.0, 0.0)
    pad_end = _dot_hi(padded, upto)
    pad_start = pad_end - padded
    start_out[...] = pad_start
    ends = jnp.transpose(jnp.broadcast_to(pad_end[0:1, :], (LANES, LANES)))
    expert_ok = _iota((LANES, LANES), 0) < N_EXPERTS
    n_active = pad_end[0:1, N_EXPERTS - 1:N_EXPERTS] * (1.0 / MOE_BLOCK)
    rows = [jnp.broadcast_to(n_active, (1, LANES)), cnt[0:1, :], pad_start[0:1, :], padded[0:1, :]]
    for r in range(PLAN_EXPERT_ROWS):
        blk_start = ((_iota((LANES, LANES), 1) + r * LANES) * MOE_BLOCK).astype(F32)
        done = jnp.where(jnp.logical_and(expert_ok, ends <= blk_start), 1.0, 0.0)
        rows.append(jnp.minimum(jnp.sum(done, axis=0, keepdims=True), N_EXPERTS - 1.0))
    plan_out[...] = jnp.concatenate(rows, axis=0).astype(I32)


def _plan(counts, n_blocks):
    assert n_blocks <= PLAN_EXPERT_ROWS * LANES
    return pl.pallas_call(
        _plan_kernel,
        out_shape=[jax.ShapeDtypeStruct((8, LANES), F32), jax.ShapeDtypeStruct((8, LANES), I32)],
        name="moe_plan",
    )(counts)


def _dest_kernel(sel_ref, gate_ref, rank_ref, start_ref, dest_out, gk_out):
    tm = sel_ref.shape[0]
    sel = sel_ref[...]
    gate = gate_ref[...]
    slot = rank_ref[...] + start_ref[0:1, :]
    lane = _iota((tm, LANES), 1)
    pieces = []
    gk = jnp.zeros((tm, LANES), F32)
    for k in range(TOP_K):
        hit = sel == float(k + 1)
        pieces.append(jnp.where(hit, slot, 0.0))
        gk = jnp.where(lane == k, jnp.sum(jnp.where(hit, gate, 0.0), axis=-1, keepdims=True), gk)
    stacked = jnp.concatenate(pieces, axis=1)
    picker = jnp.where(_div_pow2(_iota((8, TOP_K * LANES), 1), LANES) == _iota((8, TOP_K * LANES), 0),
                       1.0, 0.0)
    dest_out[...] = _dot_nt(picker, stacked, precision=HIGHEST).astype(I32)
    gk_out[...] = gk


def _dest(sel, gate, rank, pad_start):
    t = sel.shape[0]
    tm = ROW_TILE
    tile = pl.BlockSpec((tm, LANES), lambda i: (i, 0))
    return pl.pallas_call(
        _dest_kernel,
        grid=(t // tm,),
        in_specs=[tile, tile, tile, pl.BlockSpec((8, LANES), lambda i: (0, 0))],
        out_specs=[pl.BlockSpec((8, tm), lambda i: (0, i)), tile],
        out_shape=[jax.ShapeDtypeStruct((8, t), I32), jax.ShapeDtypeStruct((t, LANES), F32)],
        compiler_params=_params("parallel"),
        name="moe_dest",
    )(sel, gate, rank, pad_start)


def _dispatch_kernel(plan_ref, dest_ref, xn_ref, xs_hbm, sem):
    i = pl.program_id(0)
    tm = dest_ref.shape[1]

    def row_copy(src_row, dst_row):
        return pltpu.make_async_copy(xn_ref.at[pl.ds(src_row, 1)], xs_hbm.at[pl.ds(dst_row, 1)], sem)

    def issue(t, carry):
        for k in range(TOP_K):
            row_copy(t, dest_ref[k, t]).start()
        return carry

    lax.fori_loop(0, tm, issue, 0)

    def drain(t, carry):
        for k in range(TOP_K):
            row_copy(0, 0).wait()
        return carry

    lax.fori_loop(0, tm, drain, 0)

    @pl.when(i == 0)
    def _():
        def per_expert(e, carry):
            cnt = plan_ref[PLAN_COUNT, e]
            first = plan_ref[PLAN_START, e] + cnt
            n_pad = plan_ref[PLAN_PADDED, e] - cnt

            def start(r, c):
                row_copy(0, first + r).start()
                return c

            def wait(r, c):
                row_copy(0, 0).wait()
                return c

            lax.fori_loop(0, n_pad, start, 0)
            lax.fori_loop(0, n_pad, wait, 0)
            return carry

        lax.fori_loop(0, N_EXPERTS, per_expert, 0)


def _dispatch(plan, dest, xn, n_slots):
    t, d = xn.shape
    tm = ROW_TILE
    return pl.pallas_call(
        _dispatch_kernel,
        grid=(t // tm,),
        in_specs=[pl.BlockSpec(memory_space=pltpu.SMEM),
                  pl.BlockSpec((8, tm), lambda i: (0, i), memory_space=pltpu.SMEM),
                  pl.BlockSpec((tm, d), lambda i: (i, 0))],
        out_specs=pl.BlockSpec(memory_space=pl.ANY),
        out_shape=jax.ShapeDtypeStruct((n_slots, d), F32),
        scratch_shapes=[pltpu.SemaphoreType.DMA(())],
        compiler_params=_params("arbitrary"),
        name="moe_dispatch",
    )(plan, dest, xn)


def _ffn_kernel(be_ref, na_ref, x_ref, wgu_ref, bgu_ref, wd_ref, bd_ref, o_ref):
    j = pl.program_id(0)

    @pl.when(j < na_ref[0])
    def _():
        x = x_ref[...].astype(BF16)
        hmid = _dot(x, wgu_ref[0]) + bgu_ref[0]
        m = hmid.shape[0]
        even = (_iota((m, LANES), 1) & 1) == 0
        acts = []
        for k in range(hmid.shape[1] // (2 * LANES)):
            h0 = hmid[:, (2 * k) * LANES:(2 * k + 1) * LANES]
            h1 = hmid[:, (2 * k + 1) * LANES:(2 * k + 2) * LANES]
            glu = jnp.where(even, h0, pltpu.roll(h1, 1, 1))
            lin = jnp.where(even, pltpu.roll(h0, LANES - 1, 1), h1)
            glu = jnp.minimum(glu, SWIGLU_LIMIT)
            lin = jnp.clip(lin, -SWIGLU_LIMIT, SWIGLU_LIMIT)
            acts.append((glu * _sigmoid(SWIGLU_ALPHA * glu) * (lin + 1.0)).astype(BF16))
        act = jnp.concatenate(acts, axis=1)
        o_ref[...] = _dot(act, wd_ref[0]) + bd_ref[0]


def _pair_rows(w_down):
    e, dff, d = w_down.shape
    half = LANES // 2
    return w_down.reshape(e, dff // LANES, 2, half, d).transpose(0, 1, 3, 2, 4).reshape(e, dff, d)


def _ffn(block_expert, n_active, xs, wgu, bgu, wd, bd):
    n_slots, d = xs.shape
    dff2 = wgu.shape[2]
    nb = n_slots // MOE_BLOCK
    rows = lambda j, be, na: (jnp.minimum(j, na[0] - 1), 0)
    per_expert = lambda j, be, na: (be[j], 0, 0)
    grid_spec = pltpu.PrefetchScalarGridSpec(
        num_scalar_prefetch=2,
        grid=(nb,),
        in_specs=[pl.BlockSpec((MOE_BLOCK, d), rows),
                  pl.BlockSpec((1, d, dff2), per_expert), pl.BlockSpec((1, 1, dff2), per_expert),
                  pl.BlockSpec((1, dff2 // 2, d), per_expert), pl.BlockSpec((1, 1, d), per_expert)],
        out_specs=pl.BlockSpec((MOE_BLOCK, d), rows),
    )
    return pl.pallas_call(
        _ffn_kernel,
        grid_spec=grid_spec,
        out_shape=jax.ShapeDtypeStruct((n_slots, d), F32),
        compiler_params=_params("arbitrary"),
        name="moe_ffn",
    )(block_expert, n_active, xs, wgu, bgu, wd, bd)


def _combine_kernel(dest_ref, gk_ref, h_ref, g_ref, ys_hbm, o_ref, buf, sem, *, final_norm):
    tm = h_ref.shape[0]

    def row_copy(src_row, k, t):
        return pltpu.make_async_copy(ys_hbm.at[pl.ds(src_row, 1)], buf.at[k, pl.ds(t, 1)], sem)

    def issue(t, carry):
        for k in range(TOP_K):
            row_copy(dest_ref[k, t], k, t).start()
        return carry

    lax.fori_loop(0, tm, issue, 0)

    def drain(t, carry):
        for k in range(TOP_K):
            row_copy(0, 0, 0).wait()
        return carry

    lax.fori_loop(0, tm, drain, 0)
    acc = h_ref[...]
    gk = gk_ref[...]
    for k in range(TOP_K):
        acc = acc + gk[:, k:k + 1] * buf[k]
    if final_norm:
        ms = jnp.mean(acc * acc, axis=-1, keepdims=True)
        acc = acc * lax.rsqrt(ms + EPS) * g_ref[...]
    o_ref[...] = acc


def _combine(dest, gk, h, ys, g, final_norm):
    t, d = h.shape
    tm = COMBINE_TILE
    return pl.pallas_call(
        functools.partial(_combine_kernel, final_norm=final_norm),
        grid=(t // tm,),
        in_specs=[pl.BlockSpec((8, tm), lambda i: (0, i), memory_space=pltpu.SMEM),
                  pl.BlockSpec((tm, LANES), lambda i: (i, 0)),
                  pl.BlockSpec((tm, d), lambda i: (i, 0)),
                  pl.BlockSpec((1, d), lambda i: (0, 0)),
                  pl.BlockSpec(memory_space=pl.ANY)],
        out_specs=pl.BlockSpec((tm, d), lambda i: (i, 0)),
        out_shape=jax.ShapeDtypeStruct((t, d), F32),
        scratch_shapes=[pltpu.VMEM((TOP_K, tm, d), F32), pltpu.SemaphoreType.DMA(())],
        compiler_params=_params("arbitrary"),
        name="moe_combine",
    )(dest, gk, h, g.reshape(1, d), ys)


def _moe(h, xn, sel, gate, rank, counts, w_gu, b_gu, w_down, b_down, g_final, final_norm):
    t, d = h.shape
    n_slots = t * TOP_K + N_EXPERTS * MOE_BLOCK
    n_blocks = n_slots // MOE_BLOCK
    pad_start, plan = _plan(counts, n_blocks)
    dest, gk = _dest(sel, gate, rank, pad_start)
    xs = _dispatch(plan, dest, xn, n_slots)
    block_expert = plan[PLAN_EXPERTS:].reshape(-1)[:n_blocks]
    n_active = plan[PLAN_ACTIVE, :1]
    ys = _ffn(block_expert, n_active, xs, w_gu.astype(BF16), b_gu.reshape(N_EXPERTS, 1, -1),
              _pair_rows(w_down).astype(BF16), b_down.reshape(N_EXPERTS, 1, d))
    return _combine(dest, gk, h, ys, g_final, final_norm)


def kernel(x, norm_mix_g, w_in, sb_norm_g, rg_conv_w, rg_conv_b, rg_wa, rg_ba, rg_wx, rg_bx,
           rg_lambda, rg_norm_g, hg_lower_bounds, hg_norm_g, m2_conv_w, m2_conv_b, m2_dt_bias,
           m2_a_log, m2_d, m2_norm_g, w_out, norm_ffn_g, router_w, router_b, moe_w_gu, moe_b_gu,
           moe_w_down, moe_b_down, final_norm_g):
    b, l, d = x.shape
    depth = w_in.shape[0]
    t = b * l
    lbs = jnp.cumsum(jax.nn.softmax(hg_lower_bounds.astype(F32), axis=0), axis=0)
    lbs = lbs - lbs[0]
    d_in = w_in.shape[2]
    w_in_p = jnp.zeros((depth, d, PROJ_BLOCKS * GROUP_W), BF16).at[:, :, :d_in].set(w_in.astype(BF16))
    h = x.reshape(t, d)
    for layer in range(depth):
        proj = _norm_inproj(h, norm_mix_g[layer], w_in_p[layer]).reshape(b, l, PROJ_BLOCKS * GROUP_W)
        o_a = _sb_attn(proj, sb_norm_g[layer])
        o_b = _rglru(proj, rg_conv_w[layer], rg_conv_b[layer], rg_wa[layer], rg_ba[layer],
                     rg_wx[layer], rg_bx[layer], rg_lambda[layer], rg_norm_g[layer])
        o_c = _hgrn2(proj, lbs[layer], hg_norm_g[layer])
        o_d = _ssd(proj, m2_conv_w[layer], m2_conv_b[layer], m2_dt_bias[layer], m2_a_log[layer],
                   m2_d[layer], m2_norm_g[layer])
        h, xn, sel, gate, rank, counts = _outproj_router(
            h, (o_a, o_b, o_c, o_d), w_out[layer], norm_ffn_g[layer], router_w[layer],
            router_b[layer])
        h = _moe(h, xn, sel, gate, rank, counts, moe_w_gu[layer], moe_b_gu[layer],
                 moe_w_down[layer], moe_b_down[layer], final_norm_g, layer == depth - 1)
    return h.reshape(b, l, d)
```

```python
import functools
import math

import jax
import jax.numpy as jnp
from jax import lax
from jax.experimental import pallas as pl
from jax.experimental.pallas import tpu as pltpu

F32 = jnp.float32
BF16 = jnp.bfloat16
I32 = jnp.int32
HIGHEST = lax.Precision.HIGHEST
EPS = 1e-6

LANES = 128
GROUP_W = 256
HEAD_DIM = 64
N_HEADS = 4
SSD_STATE = 128
RG_C = 8.0
N_EXPERTS = 32
TOP_K = 4
SWIGLU_LIMIT = 7.0
SWIGLU_ALPHA = 1.702
PROJ_BLOCKS = 14
VMEM_LIMIT = 56 * 1024 * 1024

ROW_TILE = 256
ATTN_BLOCK = 256
SCAN_TILE = 256
HG_CHUNK = 16
SSD_CHUNK = 128
MOE_BLOCK = 256
COMBINE_TILE = 128
DMA_UNROLL = 8
HG_UNROLL = 8

PLAN_ACTIVE, PLAN_COUNT, PLAN_START, PLAN_PADDED, PLAN_EXPERTS = 0, 1, 2, 3, 4
PLAN_EXPERT_ROWS = 8 - PLAN_EXPERTS


def _params(*sem):
    return pltpu.CompilerParams(dimension_semantics=sem, vmem_limit_bytes=VMEM_LIMIT)


def _dot(a, b):
    return jnp.dot(a, b, preferred_element_type=F32)


def _dot_hi(a, b):
    return jnp.dot(a, b, preferred_element_type=F32, precision=HIGHEST)


def _dot_nt(a, b, precision=None):
    return lax.dot_general(a, b, (((1,), (1,)), ((), ())), preferred_element_type=F32,
                           precision=precision)


def _dot_tn(a, b):
    return lax.dot_general(a, b, (((0,), (0,)), ((), ())), preferred_element_type=F32)


def _sigmoid(x):
    return 1.0 / (1.0 + jnp.exp(-x))


def _silu(x):
    return x * _sigmoid(x)


def _softplus(x):
    return jnp.maximum(x, 0.0) + jnp.log(1.0 + jnp.exp(-jnp.abs(x)))


def _iota(shape, dim):
    return lax.broadcasted_iota(I32, shape, dim)


def _div_pow2(x, n):
    assert n & (n - 1) == 0
    return jnp.right_shift(x, int(math.log2(n)))


def _shift_rows(x, d, prev8):
    r = pltpu.roll(x, d, 0)
    p = pltpu.roll(prev8, d, 0)
    head = jnp.where(_iota(prev8.shape, 0) < d, p, r[:8])
    return jnp.concatenate([head, r[8:]], axis=0)


def _causal_conv(x, prev8, w, b):
    k = w.shape[0]
    y = x * w[k - 1:k] + b
    for d in range(1, k):
        y = y + _shift_rows(x, d, prev8) * w[k - 1 - d:k - d]
    return y


def _norm_inproj_kernel(x_ref, g_ref, w_ref, o_ref):
    x = x_ref[...]
    ms = jnp.mean(x * x, axis=-1, keepdims=True)
    xn = (x * lax.rsqrt(ms + EPS) * g_ref[...]).astype(BF16)
    o_ref[...] = _dot(xn, w_ref[...])


def _norm_inproj(h, g, w):
    t, d = h.shape
    n = w.shape[1]
    return pl.pallas_call(
        _norm_inproj_kernel,
        grid=(t // ROW_TILE,),
        in_specs=[pl.BlockSpec((ROW_TILE, d), lambda i: (i, 0)),
                  pl.BlockSpec((1, d), lambda i: (0, 0)),
                  pl.BlockSpec((d, n), lambda i: (0, 0))],
        out_specs=pl.BlockSpec((ROW_TILE, n), lambda i: (i, 0)),
        out_shape=jax.ShapeDtypeStruct((t, n), F32),
        compiler_params=_params("parallel"),
        name="norm_inproj",
    )(h, g.reshape(1, d), w)


def _sb_attn_kernel(q_ref, k_ref, v_ref, g_ref, o_ref, acc_ref, run_ref):
    i = pl.program_id(1)
    blk = ATTN_BLOCK
    scale = HEAD_DIM ** -0.5
    row = _iota((blk, blk), 0)
    col = _iota((blk, blk), 1)
    below = col < row
    r_w = _iota((blk, blk + LANES), 0)
    c_w = _iota((blk, blk + LANES), 1)
    later_sum = jnp.where(jnp.logical_or(c_w >= blk, r_w > c_w), 1.0, 0.0).astype(BF16)
    acc_ref[...] = jnp.zeros_like(acc_ref)
    run_ref[...] = jnp.zeros_like(run_ref)
    qs = [(q_ref[0, :, h * HEAD_DIM:(h + 1) * HEAD_DIM] * scale).astype(BF16) for h in range(N_HEADS)]

    def key_block(j, diagonal):
        ks = pl.multiple_of(j * blk, blk)
        for h in range(N_HEADS):
            sl = slice(h * HEAD_DIM, (h + 1) * HEAD_DIM)
            kh = k_ref[0, pl.ds(ks, blk), sl].astype(BF16)
            vh = v_ref[0, pl.ds(ks, blk), sl].astype(BF16)
            z = _dot_nt(qs[h], kh)
            ls = jnp.minimum(z, 0.0) - jnp.log(1.0 + jnp.exp(-jnp.abs(z)))
            lk = ls - z
            if diagonal:
                lk = jnp.where(below, lk, 0.0)
            lk_hi = lk.astype(BF16)
            lk_lo = (lk - lk_hi.astype(F32)).astype(BF16)
            sums = _dot(lk_hi, later_sum) + _dot(lk_lo, later_sum)
            run = run_ref[h]
            rem = sums[:, :blk] + jnp.concatenate([run] * (blk // LANES), axis=1)
            a = jnp.exp(ls + rem)
            if diagonal:
                a = jnp.where(below, a, 0.0)
            acc_ref[:, sl] += _dot(a.astype(BF16), vh)
            run_ref[h] = run + sums[:, blk:]

    key_block(i, True)

    def body(jj, carry):
        key_block(i - jj, False)
        return carry

    lax.fori_loop(1, i + 1, body, 0)
    o = acc_ref[...]
    ms = jnp.mean(o * o, axis=-1, keepdims=True)
    o_ref[0] = o * lax.rsqrt(ms + EPS) * g_ref[...]


def _sb_attn(proj, g):
    b, l, _ = proj.shape
    blk = ATTN_BLOCK
    return pl.pallas_call(
        _sb_attn_kernel,
        grid=(b, l // blk),
        in_specs=[pl.BlockSpec((1, blk, GROUP_W), lambda bi, i: (bi, i, 0)),
                  pl.BlockSpec((1, l, GROUP_W), lambda bi, i: (bi, 0, 1)),
                  pl.BlockSpec((1, l, GROUP_W), lambda bi, i: (bi, 0, 2)),
                  pl.BlockSpec((1, GROUP_W), lambda bi, i: (0, 0))],
        out_specs=pl.BlockSpec((1, blk, GROUP_W), lambda bi, i: (bi, i, 0)),
        out_shape=jax.ShapeDtypeStruct((b, l, GROUP_W), F32),
        scratch_shapes=[pltpu.VMEM((blk, GROUP_W), F32), pltpu.VMEM((N_HEADS, blk, LANES), F32)],
        compiler_params=_params("parallel", "arbitrary"),
        name="sb_attn",
    )(proj, proj, proj, g.reshape(1, GROUP_W))


def _rglru_kernel(x_ref, gate_ref, cw_ref, cb_ref, wa_ref, ba_ref, wx_ref, bx_ref, lam_ref,
                  ng_ref, o_ref):
    l = x_ref.shape[1]
    w = x_ref.shape[2]
    tc = SCAN_TILE
    rows = _iota((tc, w), 0)
    cw = cw_ref[...]
    cb = cb_ref[...]
    neg_c_sp = -RG_C * _softplus(-lam_ref[...])

    def chunk(c, carry):
        h_prev, tail = carry
        s = pl.multiple_of(c * tc, tc)
        x = x_ref[0, pl.ds(s, tc), :]
        xc = _causal_conv(x, tail, cw, cb)
        xcb = xc.astype(BF16)
        r = _sigmoid(_dot(xcb, wa_ref[...]) + ba_ref[...])
        ig = _sigmoid(_dot(xcb, wx_ref[...]) + bx_ref[...])
        log_a = r * neg_c_sp
        a = jnp.exp(log_a)
        th = jnp.tanh(log_a)
        mult = jnp.sqrt(-2.0 * th / (1.0 - th))
        mult = jnp.where(rows + s == 0, 1.0, mult)
        u = mult * ig * xc
        d = 1
        while d < tc:
            keep = rows >= d
            a_s = jnp.where(keep, pltpu.roll(a, d, 0), 1.0)
            u_s = jnp.where(keep, pltpu.roll(u, d, 0), 0.0)
            u = a * u_s + u
            a = a * a_s
            d *= 2
        hs = u + a * h_prev
        gate = gate_ref[0, pl.ds(s, tc), :]
        gelu = 0.5 * gate * (1.0 + jnp.tanh(math.sqrt(2.0 / math.pi) * (gate + 0.044715 * gate * gate * gate)))
        o = hs * gelu
        ms = jnp.mean(o * o, axis=-1, keepdims=True)
        o_ref[0, pl.ds(s, tc), :] = o * lax.rsqrt(ms + EPS) * ng_ref[...]
        return hs[tc - 1:tc], x[tc - 8:tc]

    lax.fori_loop(0, l // tc, chunk, (jnp.zeros((1, w), F32), jnp.zeros((8, w), F32)))


def _block_diag(wb):
    n, d, _ = wb.shape
    out = jnp.zeros((n * d, n * d), wb.dtype)
    for i in range(n):
        out = lax.dynamic_update_slice(out, wb[i], (i * d, i * d))
    return out


def _rglru(proj, conv_w, conv_b, wa, ba, wx, bx, lam, ng):
    b, l, _ = proj.shape
    w = GROUP_W
    row = lambda v: v.reshape(1, w)
    full = lambda shape: pl.BlockSpec(shape, lambda bi: (0,) * len(shape))
    return pl.pallas_call(
        _rglru_kernel,
        grid=(b,),
        in_specs=[pl.BlockSpec((1, l, w), lambda bi: (bi, 0, 3)),
                  pl.BlockSpec((1, l, w), lambda bi: (bi, 0, 4)),
                  full((conv_w.shape[0], w)), full((1, w)), full((w, w)), full((1, w)),
                  full((w, w)), full((1, w)), full((1, w)), full((1, w))],
        out_specs=pl.BlockSpec((1, l, w), lambda bi: (bi, 0, 0)),
        out_shape=jax.ShapeDtypeStruct((b, l, w), F32),
        compiler_params=_params("parallel"),
        name="rglru",
    )(proj, proj, conv_w, row(conv_b), _block_diag(wa).astype(BF16), row(ba),
      _block_diag(wx).astype(BF16), row(bx), row(lam), row(ng))


def _hgrn2_kernel(q_ref, f_ref, i_ref, g_ref, lb_ref, ng_ref, o_ref,
                  qe_s, ke_s, kl_s, dl_s, o_s):
    l = q_ref.shape[1]
    w = q_ref.shape[2]
    tile = SCAN_TILE
    c = HG_CHUNK
    r_t = _iota((tile, tile), 0)
    c_t = _iota((tile, tile), 1)
    same = _div_pow2(r_t, c) == _div_pow2(c_t, c)
    cum_m = jnp.where(jnp.logical_and(same, c_t <= r_t), 1.0, 0.0)
    tot_m = jnp.where(same, 1.0, 0.0)
    lb = lb_ref[...]

    def prep(t, carry):
        s = pl.multiple_of(t * tile, tile)
        forget = lb + (1.0 - lb) * _sigmoid(f_ref[0, pl.ds(s, tile), :])
        log_f = jnp.log(forget)
        key = 1.0 - forget
        bcum = _dot_hi(cum_m, log_f)
        blast = _dot_hi(tot_m, log_f)
        q = _silu(q_ref[0, pl.ds(s, tile), :])
        qe_s[pl.ds(s, tile), :] = (q * jnp.exp(bcum)).astype(BF16)
        ke_s[pl.ds(s, tile), :] = (key * jnp.exp(-bcum)).astype(BF16)
        kl_s[pl.ds(s, tile), :] = key * jnp.exp(blast - bcum)
        dl_s[pl.ds(s, tile), :] = jnp.exp(blast)
        return carry

    lax.fori_loop(0, l // tile, prep, 0)

    tril = _iota((c, c), 1) <= _iota((c, c), 0)

    def step(n, state):
        s = pl.multiple_of(n * c, c)
        qe = qe_s[pl.ds(s, c), :]
        ke = ke_s[pl.ds(s, c), :]
        kl = kl_s[pl.ds(s, c), :]
        v = i_ref[0, pl.ds(s, c), :]
        dl = dl_s[pl.ds(s, 8), :][0:1]
        vb = v.astype(BF16)
        outs = []
        new_state = []
        for h in range(N_HEADS):
            sl = slice(h * HEAD_DIM, (h + 1) * HEAD_DIM)
            sc = jnp.where(tril, _dot_nt(qe[:, sl], ke[:, sl]), 0.0)
            st = state[h]
            o = _dot(sc.astype(BF16), vb[:, sl]) + _dot_nt(qe[:, sl], st.astype(BF16))
            new_state.append(dl[:, sl] * st + _dot_tn(v[:, sl], kl[:, sl]))
            outs.append(o)
        o_s[pl.ds(s, c), :] = jnp.concatenate(outs, axis=1)
        return tuple(new_state)

    zero = jnp.zeros((HEAD_DIM, HEAD_DIM), F32)
    lax.fori_loop(0, l // c, step, (zero,) * N_HEADS, unroll=HG_UNROLL)

    head_avg = jnp.where(_div_pow2(_iota((w, w), 0), HEAD_DIM) == _div_pow2(_iota((w, w), 1), HEAD_DIM),
                         1.0 / HEAD_DIM, 0.0)

    def finish(t, carry):
        s = pl.multiple_of(t * tile, tile)
        o = o_s[pl.ds(s, tile), :]
        ms = _dot_hi(o * o, head_avg)
        o_ref[0, pl.ds(s, tile), :] = (o * lax.rsqrt(ms + EPS) * ng_ref[...]
                                       * _silu(g_ref[0, pl.ds(s, tile), :]))
        return carry

    lax.fori_loop(0, l // tile, finish, 0)


def _hgrn2(proj, lb, ng):
    b, l, _ = proj.shape
    w = GROUP_W
    full = lambda shape: pl.BlockSpec(shape, lambda bi: (0,) * len(shape))
    col = lambda j: pl.BlockSpec((1, l, w), lambda bi: (bi, 0, j))
    return pl.pallas_call(
        _hgrn2_kernel,
        grid=(b,),
        in_specs=[col(5), col(6), col(7), col(8), full((1, w)), full((1, w))],
        out_specs=pl.BlockSpec((1, l, w), lambda bi: (bi, 0, 0)),
        out_shape=jax.ShapeDtypeStruct((b, l, w), F32),
        scratch_shapes=[pltpu.VMEM((l, w), BF16), pltpu.VMEM((l, w), BF16), pltpu.VMEM((l, w), F32),
                        pltpu.VMEM((l, w), F32), pltpu.VMEM((l, w), F32)],
        compiler_params=_params("parallel"),
        name="hgrn2",
    )(proj, proj, proj, proj, lb.reshape(1, w), ng.reshape(1, w))


def _ssd_kernel(z_ref, x_ref, bm_ref, cm_ref, dt_ref, cw_ref, cb_ref, dtb_ref, aneg_ref,
                dskip_ref, ng_ref, o_ref):
    l = z_ref.shape[1]
    w = GROUP_W
    q = SSD_CHUNK
    p = HEAD_DIM
    r_q = _iota((q, q), 0)
    c_q = _iota((q, q), 1)
    tril = c_q <= r_q
    cum_m = jnp.where(tril, 1.0, 0.0)
    expand = jnp.where(jnp.logical_and(_iota((w, w), 0) == _div_pow2(_iota((w, w), 1), p),
                                       _iota((w, w), 0) < N_HEADS), 1.0, 0.0)
    pick = jnp.where(_iota((8, w), 1) == _iota((8, w), 0) * p, 1.0, 0.0)
    cw = cw_ref[...]
    cb = cb_ref[...]

    def chunk(n, carry):
        state, tails = carry
        s = pl.multiple_of(n * q, q)
        parts = []
        new_tails = []
        for j, ref in enumerate((x_ref, bm_ref, cm_ref)):
            raw = ref[0, pl.ds(s, q), :]
            sl = slice(j * w, (j + 1) * w)
            parts.append(_silu(_causal_conv(raw, tails[j], cw[:, sl], cb[:, sl])))
            new_tails.append(raw[q - 8:q])
        xs, bm, cm = parts
        dt = _softplus(dt_ref[0, pl.ds(s, q), :] + dtb_ref[...])
        dt_rep = _dot_hi(dt, expand)
        da_rep = _dot_hi(dt * aneg_ref[...], expand)
        a_cs = _dot_hi(cum_m, da_rep)
        a_row = _dot_nt(pick, a_cs, precision=HIGHEST)
        xd = xs * dt_rep
        ys = []
        new_state = []
        for h in range(N_HEADS):
            grp = h // (N_HEADS // 2)
            gs = slice(grp * SSD_STATE, (grp + 1) * SSD_STATE)
            hs = slice(h * p, (h + 1) * p)
            cg = cm[:, gs].astype(BF16)
            bg = bm[:, gs].astype(BF16)
            cb_qq = _dot_nt(cg, bg)
            a_col = a_cs[:, h * p:h * p + 1]
            decay = jnp.exp(jnp.where(tril, a_col - a_row[h:h + 1, :], -jnp.inf))
            y = _dot((cb_qq * decay).astype(BF16), xd[:, hs].astype(BF16))
            st = state[h]
            y = y + jnp.exp(a_cs[:, hs]) * _dot_nt(cg, st.astype(BF16))
            a_last = a_cs[q - 1:q, hs]
            to_end = jnp.exp(a_last - a_cs[:, hs])
            upd = _dot_tn(xd[:, hs] * to_end, bm[:, gs])
            chunk_decay = jnp.exp(a_row[h:h + 1, q - 1:q])
            new_state.append(chunk_decay * st + upd)
            ys.append(y)
        y = jnp.concatenate(ys, axis=1) + xs * dskip_ref[...]
        y = y * _silu(z_ref[0, pl.ds(s, q), :])
        halves = []
        for gi in range(2):
            yg = y[:, gi * LANES:(gi + 1) * LANES]
            ms = jnp.mean(yg * yg, axis=-1, keepdims=True)
            halves.append(yg * lax.rsqrt(ms + EPS))
        o_ref[0, pl.ds(s, q), :] = jnp.concatenate(halves, axis=1) * ng_ref[...]
        return tuple(new_state), tuple(new_tails)

    zero = jnp.zeros((p, SSD_STATE), F32)
    tail0 = jnp.zeros((8, w), F32)
    lax.fori_loop(0, l // q, chunk, ((zero,) * N_HEADS, (tail0,) * 3))


def _ssd(proj, conv_w, conv_b, dt_bias, a_log, d_skip, ng):
    b, l, _ = proj.shape
    w = GROUP_W
    full = lambda shape: pl.BlockSpec(shape, lambda bi: (0,) * len(shape))
    col = lambda j: pl.BlockSpec((1, l, w), lambda bi: (bi, 0, j))
    pad_heads = lambda v: jnp.zeros((1, w), F32).at[0, :N_HEADS].set(v)
    return pl.pallas_call(
        _ssd_kernel,
        grid=(b,),
        in_specs=[col(9), col(10), col(11), col(12), col(13),
                  full((conv_w.shape[0], 3 * w)), full((1, 3 * w)), full((1, w)), full((1, w)),
                  full((1, w)), full((1, w))],
        out_specs=pl.BlockSpec((1, l, w), lambda bi: (bi, 0, 0)),
        out_shape=jax.ShapeDtypeStruct((b, l, w), F32),
        compiler_params=_params("parallel"),
        name="ssd",
    )(proj, proj, proj, proj, proj, conv_w, conv_b.reshape(1, 3 * w), pad_heads(dt_bias),
      pad_heads(-jnp.exp(a_log)), jnp.repeat(d_skip, HEAD_DIM).reshape(1, w), ng.reshape(1, w))


def _outproj_router_kernel(h_ref, oa_ref, ob_ref, oc_ref, od_ref, wo_ref, g_ref, rwh_ref, rwl_ref, rb_ref,
                           h_out, xn_out, sel_out, gate_out, rank_out, cnt_out, cnt_s):
    i = pl.program_id(0)
    tm = h_ref.shape[0]

    @pl.when(i == 0)
    def _():
        cnt_s[...] = jnp.zeros_like(cnt_s)

    acc = h_ref[...]
    for j, ref in enumerate((oa_ref, ob_ref, oc_ref, od_ref)):
        acc = acc + _dot(ref[...].astype(BF16), wo_ref[j * GROUP_W:(j + 1) * GROUP_W, :])
    h_out[...] = acc
    ms = jnp.mean(acc * acc, axis=-1, keepdims=True)
    xn = acc * lax.rsqrt(ms + EPS) * g_ref[...]
    xn_out[...] = xn
    lane = _iota((tm, LANES), 1)
    lane_f = lane.astype(F32)
    xn_hi = xn.astype(BF16)
    xn_lo = (xn - xn_hi.astype(F32)).astype(BF16)
    router = _dot(xn_hi, rwh_ref[...]) + _dot(xn_lo, rwh_ref[...]) + _dot(xn_hi, rwl_ref[...])
    logits = jnp.where(lane < N_EXPERTS, router + rb_ref[...], -jnp.inf)
    sel = jnp.zeros((tm, LANES), F32)
    tops = []
    for k in range(TOP_K):
        m = jnp.max(logits, axis=-1, keepdims=True)
        idx = jnp.min(jnp.where(logits == m, lane_f, float(LANES)), axis=-1, keepdims=True)
        hit = lane_f == idx
        sel = jnp.where(hit, float(k + 1), sel)
        logits = jnp.where(hit, -jnp.inf, logits)
        tops.append(m)
    exps = [jnp.exp(m - tops[0]) for m in tops]
    denom = exps[0] + exps[1] + exps[2] + exps[3]
    gate = jnp.zeros((tm, LANES), F32)
    for k in range(TOP_K):
        gate = jnp.where(sel == float(k + 1), exps[k] / denom, gate)
    sel_out[...] = sel
    gate_out[...] = gate
    member = jnp.where(sel > 0.0, 1.0, 0.0)
    before = jnp.where(_iota((tm, tm), 1) < _iota((tm, tm), 0), 1.0, 0.0).astype(BF16)
    carry = cnt_s[0:1, :]
    rank_out[...] = _dot(before, member.astype(BF16)) + carry
    total = carry + jnp.sum(member, axis=0, keepdims=True)
    cnt_s[...] = jnp.broadcast_to(total, cnt_s.shape)
    cnt_out[...] = jnp.broadcast_to(total, cnt_out.shape)


def _outproj_router(h, outs, w_out, g, router_w, router_b):
    t, d = h.shape
    tm = ROW_TILE
    rw = jnp.zeros((d, LANES), F32).at[:, :N_EXPERTS].set(router_w)
    rb = jnp.zeros((1, LANES), F32).at[0, :N_EXPERTS].set(router_b)
    rw_hi = rw.astype(BF16)
    rw_lo = (rw - rw_hi.astype(F32)).astype(BF16)
    tile = lambda width: pl.BlockSpec((tm, width), lambda i: (i, 0))
    full = lambda shape: pl.BlockSpec(shape, lambda i: (0,) * len(shape))
    flat = [o.reshape(t, GROUP_W) for o in outs]
    return pl.pallas_call(
        _outproj_router_kernel,
        grid=(t // tm,),
        in_specs=[tile(d)] + [tile(GROUP_W)] * 4 + [full((d, d)), full((1, d)), full((d, LANES)),
                                                   full((d, LANES)), full((1, LANES))],
        out_specs=[tile(d), tile(d), tile(LANES), tile(LANES), tile(LANES), full((8, LANES))],
        out_shape=[jax.ShapeDtypeStruct((t, d), F32), jax.ShapeDtypeStruct((t, d), F32),
                   jax.ShapeDtypeStruct((t, LANES), F32), jax.ShapeDtypeStruct((t, LANES), F32),
                   jax.ShapeDtypeStruct((t, LANES), F32), jax.ShapeDtypeStruct((8, LANES), F32)],
        scratch_shapes=[pltpu.VMEM((8, LANES), F32)],
        compiler_params=_params("arbitrary"),
        name="outproj_router",
    )(h, *flat, w_out.astype(BF16), g.reshape(1, d), rw_hi, rw_lo, rb)


def _plan_kernel(cnt_ref, start_out, plan_out):
    cnt = cnt_ref[...]
    padded = jnp.ceil(cnt * (1.0 / MOE_BLOCK)) * MOE_BLOCK
    upto = jnp.where(_iota((LANES, LANES), 0) <= _iota((LANES, LANES), 1), 1.0, 0.0)
    pad_end = _dot_hi(padded, upto)
    pad_start = pad_end - padded
    start_out[...] = pad_start
    ends = jnp.transpose(jnp.broadcast_to(pad_end[0:1, :], (LANES, LANES)))
    expert_ok = _iota((LANES, LANES), 0) < N_EXPERTS
    n_active = pad_end[0:1, N_EXPERTS - 1:N_EXPERTS] * (1.0 / MOE_BLOCK)
    rows = [jnp.broadcast_to(n_active, (1, LANES)), cnt[0:1, :], pad_start[0:1, :], padded[0:1, :]]
    for r in range(PLAN_EXPERT_ROWS):
        blk_start = ((_iota((LANES, LANES), 1) + r * LANES) * MOE_BLOCK).astype(F32)
        done = jnp.where(jnp.logical_and(expert_ok, ends <= blk_start), 1.0, 0.0)
        rows.append(jnp.minimum(jnp.sum(done, axis=0, keepdims=True), N_EXPERTS - 1.0))
    plan_out[...] = jnp.concatenate(rows, axis=0).astype(I32)


def _plan(counts, n_blocks):
    assert n_blocks <= PLAN_EXPERT_ROWS * LANES
    return pl.pallas_call(
        _plan_kernel,
        out_shape=[jax.ShapeDtypeStruct((8, LANES), F32), jax.ShapeDtypeStruct((8, LANES), I32)],
        name="moe_plan",
    )(counts)


def _dest_kernel(sel_ref, gate_ref, rank_ref, start_ref, dest_out, gk_out):
    tm = sel_ref.shape[0]
    sel = sel_ref[...]
    gate = gate_ref[...]
    slot = rank_ref[...] + start_ref[0:1, :]
    lane = _iota((tm, LANES), 1)
    pieces = []
    gk = jnp.zeros((tm, LANES), F32)
    for k in range(TOP_K):
        hit = sel == float(k + 1)
        pieces.append(jnp.where(hit, slot, 0.0))
        gk = jnp.where(lane == k, jnp.sum(jnp.where(hit, gate, 0.0), axis=-1, keepdims=True), gk)
    stacked = jnp.concatenate(pieces, axis=1)
    picker = jnp.where(_div_pow2(_iota((8, TOP_K * LANES), 1), LANES) == _iota((8, TOP_K * LANES), 0),
                       1.0, 0.0)
    dest_out[...] = _dot_nt(picker, stacked, precision=HIGHEST).astype(I32)
    gk_out[...] = gk


def _dest(sel, gate, rank, pad_start):
    t = sel.shape[0]
    tm = ROW_TILE
    tile = pl.BlockSpec((tm, LANES), lambda i: (i, 0))
    return pl.pallas_call(
        _dest_kernel,
        grid=(t // tm,),
        in_specs=[tile, tile, tile, pl.BlockSpec((8, LANES), lambda i: (0, 0))],
        out_specs=[pl.BlockSpec((8, tm), lambda i: (0, i)), tile],
        out_shape=[jax.ShapeDtypeStruct((8, t), I32), jax.ShapeDtypeStruct((t, LANES), F32)],
        compiler_params=_params("parallel"),
        name="moe_dest",
    )(sel, gate, rank, pad_start)


def _dispatch_kernel(plan_ref, dest_ref, xn_ref, xs_hbm, sem):
    i = pl.program_id(0)
    tm = dest_ref.shape[1]

    def row_copy(src_row, dst_row):
        return pltpu.make_async_copy(xn_ref.at[pl.ds(src_row, 1)], xs_hbm.at[pl.ds(dst_row, 1)], sem)

    def issue(t, carry):
        for k in range(TOP_K):
            row_copy(t, dest_ref[k, t]).start(priority=k % 2)
        return carry

    lax.fori_loop(0, tm, issue, 0, unroll=DMA_UNROLL)

    def drain(t, carry):
        for k in range(TOP_K):
            row_copy(0, 0).wait()
        return carry

    lax.fori_loop(0, tm, drain, 0)

    @pl.when(i == 0)
    def _():
        def per_expert(e, carry):
            cnt = plan_ref[PLAN_COUNT, e]
            first = plan_ref[PLAN_START, e] + cnt
            n_pad = plan_ref[PLAN_PADDED, e] - cnt

            def start(r, c):
                row_copy(0, first + r).start()
                return c

            def wait(r, c):
                row_copy(0, 0).wait()
                return c

            lax.fori_loop(0, n_pad, start, 0)
            lax.fori_loop(0, n_pad, wait, 0)
            return carry

        lax.fori_loop(0, N_EXPERTS, per_expert, 0)


def _dispatch(plan, dest, xn, n_slots):
    t, d = xn.shape
    tm = ROW_TILE
    return pl.pallas_call(
        _dispatch_kernel,
        grid=(t // tm,),
        in_specs=[pl.BlockSpec(memory_space=pltpu.SMEM),
                  pl.BlockSpec((8, tm), lambda i: (0, i), memory_space=pltpu.SMEM),
                  pl.BlockSpec((tm, d), lambda i: (i, 0))],
        out_specs=pl.BlockSpec(memory_space=pl.ANY),
        out_shape=jax.ShapeDtypeStruct((n_slots, d), F32),
        scratch_shapes=[pltpu.SemaphoreType.DMA(())],
        compiler_params=_params("arbitrary"),
        name="moe_dispatch",
    )(plan, dest, xn)


def _ffn_kernel(be_ref, na_ref, x_ref, wgu_ref, bgu_ref, wd_ref, bd_ref, o_ref, wgu_s, wd32_s, wd_s):
    j = pl.program_id(0)
    active = j < na_ref[0]
    new_expert = jnp.logical_or(j == 0, be_ref[j] != be_ref[jnp.maximum(j - 1, 0)])

    @pl.when(jnp.logical_and(active, new_expert))
    def _():
        wgu_s[...] = wgu_ref[0].astype(BF16)
        half = LANES // 2
        for c in range(wd_s.shape[0]):
            cols = slice(c * LANES, (c + 1) * LANES)
            for k in range(wd_s.shape[1] // LANES):
                for hb in range(2):
                    src = k * LANES + hb * half
                    wd32_s[c, pl.ds(k * LANES + hb, half, stride=2), :] = wd_ref[0, src:src + half, cols]
            wd_s[c] = wd32_s[c].astype(BF16)

    @pl.when(active)
    def _():
        x = x_ref[...].astype(BF16)
        hmid = _dot(x, wgu_s[...]) + bgu_ref[0]
        m = hmid.shape[0]
        even = (_iota((m, LANES), 1) & 1) == 0
        acts = []
        for k in range(hmid.shape[1] // (2 * LANES)):
            h0 = hmid[:, (2 * k) * LANES:(2 * k + 1) * LANES]
            h1 = hmid[:, (2 * k + 1) * LANES:(2 * k + 2) * LANES]
            glu = jnp.where(even, h0, pltpu.roll(h1, 1, 1))
            lin = jnp.where(even, pltpu.roll(h0, LANES - 1, 1), h1)
            glu = jnp.minimum(glu, SWIGLU_LIMIT)
            lin = jnp.clip(lin, -SWIGLU_LIMIT, SWIGLU_LIMIT)
            acts.append((glu * _sigmoid(SWIGLU_ALPHA * glu) * (lin + 1.0)).astype(BF16))
        act = jnp.concatenate(acts, axis=1)
        wd = jnp.concatenate([wd_s[c] for c in range(wd_s.shape[0])], axis=1)
        o_ref[...] = _dot(act, wd) + bd_ref[0]


def _ffn(block_expert, n_active, xs, wgu, bgu, wd, bd):
    n_slots, d = xs.shape
    dff2 = wgu.shape[2]
    nb = n_slots // MOE_BLOCK
    rows = lambda j, be, na: (jnp.minimum(j, na[0] - 1), 0)
    per_expert = lambda j, be, na: (be[j], 0, 0)
    grid_spec = pltpu.PrefetchScalarGridSpec(
        num_scalar_prefetch=2,
        grid=(nb,),
        in_specs=[pl.BlockSpec((MOE_BLOCK, d), rows),
                  pl.BlockSpec((1, d, dff2), per_expert), pl.BlockSpec((1, 1, dff2), per_expert),
                  pl.BlockSpec((1, dff2 // 2, d), per_expert), pl.BlockSpec((1, 1, d), per_expert)],
        out_specs=pl.BlockSpec((MOE_BLOCK, d), rows),
        scratch_shapes=[pltpu.VMEM((d, dff2), BF16), pltpu.VMEM((d // LANES, dff2 // 2, LANES), F32),
                        pltpu.VMEM((d // LANES, dff2 // 2, LANES), BF16)],
    )
    return pl.pallas_call(
        _ffn_kernel,
        grid_spec=grid_spec,
        out_shape=jax.ShapeDtypeStruct((n_slots, d), F32),
        compiler_params=_params("arbitrary"),
        name="moe_ffn",
    )(block_expert, n_active, xs, wgu, bgu, wd, bd)


def _combine_kernel(dest_ref, gk_ref, h_ref, g_ref, ys_hbm, o_ref, buf, sem, *, final_norm):
    tm = h_ref.shape[0]

    def row_copy(src_row, k, t):
        return pltpu.make_async_copy(ys_hbm.at[pl.ds(src_row, 1)], buf.at[k, pl.ds(t, 1)], sem)

    def issue(t, carry):
        for k in range(TOP_K):
            row_copy(dest_ref[k, t], k, t).start(priority=k % 2)
        return carry

    lax.fori_loop(0, tm, issue, 0, unroll=DMA_UNROLL)

    def drain(t, carry):
        for k in range(TOP_K):
            row_copy(0, 0, 0).wait()
        return carry

    lax.fori_loop(0, tm, drain, 0)
    acc = h_ref[...]
    gk = gk_ref[...]
    for k in range(TOP_K):
        acc = acc + gk[:, k:k + 1] * buf[k]
    if final_norm:
        ms = jnp.mean(acc * acc, axis=-1, keepdims=True)
        acc = acc * lax.rsqrt(ms + EPS) * g_ref[...]
    o_ref[...] = acc


def _combine(dest, gk, h, ys, g, final_norm):
    t, d = h.shape
    tm = COMBINE_TILE
    return pl.pallas_call(
        functools.partial(_combine_kernel, final_norm=final_norm),
        grid=(t // tm,),
        in_specs=[pl.BlockSpec((8, tm), lambda i: (0, i), memory_space=pltpu.SMEM),
                  pl.BlockSpec((tm, LANES), lambda i: (i, 0)),
                  pl.BlockSpec((tm, d), lambda i: (i, 0)),
                  pl.BlockSpec((1, d), lambda i: (0, 0)),
                  pl.BlockSpec(memory_space=pl.ANY)],
        out_specs=pl.BlockSpec((tm, d), lambda i: (i, 0)),
        out_shape=jax.ShapeDtypeStruct((t, d), F32),
        scratch_shapes=[pltpu.VMEM((TOP_K, tm, d), F32), pltpu.SemaphoreType.DMA(())],
        compiler_params=_params("arbitrary"),
        name="moe_combine",
    )(dest, gk, h, g.reshape(1, d), ys)


def _moe(h, xn, sel, gate, rank, counts, w_gu, b_gu, w_down, b_down, g_final, final_norm):
    t, d = h.shape
    n_slots = t * TOP_K + N_EXPERTS * MOE_BLOCK
    n_blocks = n_slots // MOE_BLOCK
    pad_start, plan = _plan(counts, n_blocks)
    dest, gk = _dest(sel, gate, rank, pad_start)
    xs = _dispatch(plan, dest, xn, n_slots)
    block_expert = plan[PLAN_EXPERTS:].reshape(-1)[:n_blocks]
    n_active = plan[PLAN_ACTIVE, :1]
    ys = _ffn(block_expert, n_active, xs, w_gu, b_gu.reshape(N_EXPERTS, 1, -1), w_down,
              b_down.reshape(N_EXPERTS, 1, d))
    return _combine(dest, gk, h, ys, g_final, final_norm)


def kernel(x, norm_mix_g, w_in, sb_norm_g, rg_conv_w, rg_conv_b, rg_wa, rg_ba, rg_wx, rg_bx,
           rg_lambda, rg_norm_g, hg_lower_bounds, hg_norm_g, m2_conv_w, m2_conv_b, m2_dt_bias,
           m2_a_log, m2_d, m2_norm_g, w_out, norm_ffn_g, router_w, router_b, moe_w_gu, moe_b_gu,
           moe_w_down, moe_b_down, final_norm_g):
    b, l, d = x.shape
    depth = w_in.shape[0]
    t = b * l
    lbs = jnp.cumsum(jax.nn.softmax(hg_lower_bounds.astype(F32), axis=0), axis=0)
    lbs = lbs - lbs[0]
    d_in = w_in.shape[2]
    w_in_p = jnp.zeros((depth, d, PROJ_BLOCKS * GROUP_W), BF16).at[:, :, :d_in].set(w_in.astype(BF16))
    h = x.reshape(t, d)
    for layer in range(depth):
        proj = _norm_inproj(h, norm_mix_g[layer], w_in_p[layer]).reshape(b, l, PROJ_BLOCKS * GROUP_W)
        o_a = _sb_attn(proj, sb_norm_g[layer])
        o_b = _rglru(proj, rg_conv_w[layer], rg_conv_b[layer], rg_wa[layer], rg_ba[layer],
                     rg_wx[layer], rg_bx[layer], rg_lambda[layer], rg_norm_g[layer])
        o_c = _hgrn2(proj, lbs[layer], hg_norm_g[layer])
        o_d = _ssd(proj, m2_conv_w[layer], m2_conv_b[layer], m2_dt_bias[layer], m2_a_log[layer],
                   m2_d[layer], m2_norm_g[layer])
        h, xn, sel, gate, rank, counts = _outproj_router(
            h, (o_a, o_b, o_c, o_d), w_out[layer], norm_ffn_g[layer], router_w[layer],
            router_b[layer])
        h = _moe(h, xn, sel, gate, rank, counts, moe_w_gu[layer], moe_b_gu[layer],
                 moe_w_down[layer], moe_b_down[layer], final_norm_g, layer == depth - 1)
    return h.reshape(b, l, d)
```

```python
import functools
import math

import jax
import jax.numpy as jnp
from jax import lax
from jax.experimental import pallas as pl
from jax.experimental.pallas import tpu as pltpu

F32 = jnp.float32
BF16 = jnp.bfloat16
I32 = jnp.int32
HIGHEST = lax.Precision.HIGHEST
EPS = 1e-6

LANES = 128
GROUP_W = 256
HEAD_DIM = 64
N_HEADS = 4
SSD_STATE = 128
RG_C = 8.0
N_EXPERTS = 32
TOP_K = 4
SWIGLU_LIMIT = 7.0
SWIGLU_ALPHA = 1.702
PROJ_BLOCKS = 14
VMEM_LIMIT = 56 * 1024 * 1024

ROW_TILE = 256
ATTN_BLOCK = 256
SCAN_TILE = 256
HG_CHUNK = 16
SSD_CHUNK = 128
MOE_BLOCK = 256
HG_UNROLL = 8

PLAN_ACTIVE, PLAN_COUNT, PLAN_START, PLAN_PADDED, PLAN_EXPERTS = 0, 1, 2, 3, 4
PLAN_EXPERT_ROWS = 8 - PLAN_EXPERTS
RUN_LEN, RUN_START, RUN_SLOT = 0, 1, 2
RUN_ALIGN = 8
RUN_BITS = (ROW_TILE // RUN_ALIGN).bit_length()
PAD_BITS = (MOE_BLOCK // RUN_ALIGN - 1).bit_length()
LOCAL_ROWS = ROW_TILE * TOP_K + N_EXPERTS * RUN_ALIGN


def _params(*sem):
    return pltpu.CompilerParams(dimension_semantics=sem, vmem_limit_bytes=VMEM_LIMIT)


def _dot(a, b):
    return jnp.dot(a, b, preferred_element_type=F32)


def _dot_hi(a, b):
    return jnp.dot(a, b, preferred_element_type=F32, precision=HIGHEST)


def _dot_nt(a, b, precision=None):
    return lax.dot_general(a, b, (((1,), (1,)), ((), ())), preferred_element_type=F32,
                           precision=precision)


def _dot_tn(a, b):
    return lax.dot_general(a, b, (((0,), (0,)), ((), ())), preferred_element_type=F32)


def _sigmoid(x):
    return 1.0 / (1.0 + jnp.exp(-x))


def _silu(x):
    return x * _sigmoid(x)


def _softplus(x):
    return jnp.maximum(x, 0.0) + jnp.log(1.0 + jnp.exp(-jnp.abs(x)))


def _iota(shape, dim):
    return lax.broadcasted_iota(I32, shape, dim)


def _div_pow2(x, n):
    assert n & (n - 1) == 0
    return jnp.right_shift(x, int(math.log2(n)))


def _shift_rows(x, d, prev8):
    r = pltpu.roll(x, d, 0)
    p = pltpu.roll(prev8, d, 0)
    head = jnp.where(_iota(prev8.shape, 0) < d, p, r[:8])
    return jnp.concatenate([head, r[8:]], axis=0)


def _causal_conv(x, prev8, w, b):
    k = w.shape[0]
    y = x * w[k - 1:k] + b
    for d in range(1, k):
        y = y + _shift_rows(x, d, prev8) * w[k - 1 - d:k - d]
    return y


def _norm_inproj_kernel(x_ref, g_ref, w_ref, o_ref):
    x = x_ref[...]
    ms = jnp.mean(x * x, axis=-1, keepdims=True)
    xn = (x * lax.rsqrt(ms + EPS) * g_ref[...]).astype(BF16)
    o_ref[...] = _dot(xn, w_ref[...])


def _norm_inproj(h, g, w):
    t, d = h.shape
    n = w.shape[1]
    return pl.pallas_call(
        _norm_inproj_kernel,
        grid=(t // ROW_TILE,),
        in_specs=[pl.BlockSpec((ROW_TILE, d), lambda i: (i, 0)),
                  pl.BlockSpec((1, d), lambda i: (0, 0)),
                  pl.BlockSpec((d, n), lambda i: (0, 0))],
        out_specs=pl.BlockSpec((ROW_TILE, n), lambda i: (i, 0)),
        out_shape=jax.ShapeDtypeStruct((t, n), F32),
        compiler_params=_params("parallel"),
        name="norm_inproj",
    )(h, g.reshape(1, d), w)


def _sb_attn_kernel(q_ref, k_ref, v_ref, g_ref, o_ref, acc_ref, run_ref):
    i = pl.program_id(1)
    blk = ATTN_BLOCK
    scale = HEAD_DIM ** -0.5
    row = _iota((blk, blk), 0)
    col = _iota((blk, blk), 1)
    below = col < row
    r_w = _iota((blk, blk + LANES), 0)
    c_w = _iota((blk, blk + LANES), 1)
    later_sum = jnp.where(jnp.logical_or(c_w >= blk, r_w > c_w), 1.0, 0.0).astype(BF16)
    acc_ref[...] = jnp.zeros_like(acc_ref)
    run_ref[...] = jnp.zeros_like(run_ref)
    qs = [(q_ref[0, :, h * HEAD_DIM:(h + 1) * HEAD_DIM] * scale).astype(BF16) for h in range(N_HEADS)]

    def key_block(j, diagonal):
        ks = pl.multiple_of(j * blk, blk)
        for h in range(N_HEADS):
            sl = slice(h * HEAD_DIM, (h + 1) * HEAD_DIM)
            kh = k_ref[0, pl.ds(ks, blk), sl].astype(BF16)
            vh = v_ref[0, pl.ds(ks, blk), sl].astype(BF16)
            z = _dot_nt(qs[h], kh)
            ls = jnp.minimum(z, 0.0) - jnp.log(1.0 + jnp.exp(-jnp.abs(z)))
            lk = ls - z
            if diagonal:
                lk = jnp.where(below, lk, 0.0)
            lk_hi = lk.astype(BF16)
            lk_lo = (lk - lk_hi.astype(F32)).astype(BF16)
            sums = _dot(lk_hi, later_sum) + _dot(lk_lo, later_sum)
            run = run_ref[h]
            rem = sums[:, :blk] + jnp.concatenate([run] * (blk // LANES), axis=1)
            a = jnp.exp(ls + rem)
            if diagonal:
                a = jnp.where(below, a, 0.0)
            acc_ref[:, sl] += _dot(a.astype(BF16), vh)
            run_ref[h] = run + sums[:, blk:]

    key_block(i, True)

    def body(jj, carry):
        key_block(i - jj, False)
        return carry

    lax.fori_loop(1, i + 1, body, 0)
    o = acc_ref[...]
    ms = jnp.mean(o * o, axis=-1, keepdims=True)
    o_ref[0] = o * lax.rsqrt(ms + EPS) * g_ref[...]


def _sb_attn(proj, g):
    b, l, _ = proj.shape
    blk = ATTN_BLOCK
    return pl.pallas_call(
        _sb_attn_kernel,
        grid=(b, l // blk),
        in_specs=[pl.BlockSpec((1, blk, GROUP_W), lambda bi, i: (bi, i, 0)),
                  pl.BlockSpec((1, l, GROUP_W), lambda bi, i: (bi, 0, 1)),
                  pl.BlockSpec((1, l, GROUP_W), lambda bi, i: (bi, 0, 2)),
                  pl.BlockSpec((1, GROUP_W), lambda bi, i: (0, 0))],
        out_specs=pl.BlockSpec((1, blk, GROUP_W), lambda bi, i: (bi, i, 0)),
        out_shape=jax.ShapeDtypeStruct((b, l, GROUP_W), F32),
        scratch_shapes=[pltpu.VMEM((blk, GROUP_W), F32), pltpu.VMEM((N_HEADS, blk, LANES), F32)],
        compiler_params=_params("parallel", "arbitrary"),
        name="sb_attn",
    )(proj, proj, proj, g.reshape(1, GROUP_W))


def _rglru_kernel(x_ref, gate_ref, cw_ref, cb_ref, wa_ref, ba_ref, wx_ref, bx_ref, lam_ref,
                  ng_ref, o_ref):
    l = x_ref.shape[1]
    w = x_ref.shape[2]
    tc = SCAN_TILE
    rows = _iota((tc, w), 0)
    cw = cw_ref[...]
    cb = cb_ref[...]
    neg_c_sp = -RG_C * _softplus(-lam_ref[...])

    def chunk(c, carry):
        h_prev, tail = carry
        s = pl.multiple_of(c * tc, tc)
        x = x_ref[0, pl.ds(s, tc), :]
        xc = _causal_conv(x, tail, cw, cb)
        xcb = xc.astype(BF16)
        r = _sigmoid(_dot(xcb, wa_ref[...]) + ba_ref[...])
        ig = _sigmoid(_dot(xcb, wx_ref[...]) + bx_ref[...])
        log_a = r * neg_c_sp
        a = jnp.exp(log_a)
        th = jnp.tanh(log_a)
        mult = jnp.sqrt(-2.0 * th / (1.0 - th))
        mult = jnp.where(rows + s == 0, 1.0, mult)
        u = mult * ig * xc
        d = 1
        while d < tc:
            keep = rows >= d
            a_s = jnp.where(keep, pltpu.roll(a, d, 0), 1.0)
            u_s = jnp.where(keep, pltpu.roll(u, d, 0), 0.0)
            u = a * u_s + u
            a = a * a_s
            d *= 2
        hs = u + a * h_prev
        gate = gate_ref[0, pl.ds(s, tc), :]
        gelu = 0.5 * gate * (1.0 + jnp.tanh(math.sqrt(2.0 / math.pi) * (gate + 0.044715 * gate * gate * gate)))
        o = hs * gelu
        ms = jnp.mean(o * o, axis=-1, keepdims=True)
        o_ref[0, pl.ds(s, tc), :] = o * lax.rsqrt(ms + EPS) * ng_ref[...]
        return hs[tc - 1:tc], x[tc - 8:tc]

    lax.fori_loop(0, l // tc, chunk, (jnp.zeros((1, w), F32), jnp.zeros((8, w), F32)))


def _block_diag(wb):
    n, d, _ = wb.shape
    out = jnp.zeros((n * d, n * d), wb.dtype)
    for i in range(n):
        out = lax.dynamic_update_slice(out, wb[i], (i * d, i * d))
    return out


def _rglru(proj, conv_w, conv_b, wa, ba, wx, bx, lam, ng):
    b, l, _ = proj.shape
    w = GROUP_W
    row = lambda v: v.reshape(1, w)
    full = lambda shape: pl.BlockSpec(shape, lambda bi: (0,) * len(shape))
    return pl.pallas_call(
        _rglru_kernel,
        grid=(b,),
        in_specs=[pl.BlockSpec((1, l, w), lambda bi: (bi, 0, 3)),
                  pl.BlockSpec((1, l, w), lambda bi: (bi, 0, 4)),
                  full((conv_w.shape[0], w)), full((1, w)), full((w, w)), full((1, w)),
                  full((w, w)), full((1, w)), full((1, w)), full((1, w))],
        out_specs=pl.BlockSpec((1, l, w), lambda bi: (bi, 0, 0)),
        out_shape=jax.ShapeDtypeStruct((b, l, w), F32),
        compiler_params=_params("parallel"),
        name="rglru",
    )(proj, proj, conv_w, row(conv_b), _block_diag(wa).astype(BF16), row(ba),
      _block_diag(wx).astype(BF16), row(bx), row(lam), row(ng))


def _hgrn2_kernel(q_ref, f_ref, i_ref, g_ref, lb_ref, ng_ref, o_ref,
                  qe_s, ke_s, kl_s, dl_s, o_s):
    l = q_ref.shape[1]
    w = q_ref.shape[2]
    tile = SCAN_TILE
    c = HG_CHUNK
    r_t = _iota((tile, tile), 0)
    c_t = _iota((tile, tile), 1)
    same = _div_pow2(r_t, c) == _div_pow2(c_t, c)
    cum_m = jnp.where(jnp.logical_and(same, c_t <= r_t), 1.0, 0.0)
    tot_m = jnp.where(same, 1.0, 0.0)
    lb = lb_ref[...]

    def prep(t, carry):
        s = pl.multiple_of(t * tile, tile)
        forget = lb + (1.0 - lb) * _sigmoid(f_ref[0, pl.ds(s, tile), :])
        log_f = jnp.log(forget)
        key = 1.0 - forget
        bcum = _dot_hi(cum_m, log_f)
        blast = _dot_hi(tot_m, log_f)
        q = _silu(q_ref[0, pl.ds(s, tile), :])
        qe_s[pl.ds(s, tile), :] = (q * jnp.exp(bcum)).astype(BF16)
        ke_s[pl.ds(s, tile), :] = (key * jnp.exp(-bcum)).astype(BF16)
        kl_s[pl.ds(s, tile), :] = key * jnp.exp(blast - bcum)
        dl_s[pl.ds(s, tile), :] = jnp.exp(blast)
        return carry

    lax.fori_loop(0, l // tile, prep, 0)

    tril = _iota((c, c), 1) <= _iota((c, c), 0)

    def step(n, state):
        s = pl.multiple_of(n * c, c)
        qe = qe_s[pl.ds(s, c), :]
        ke = ke_s[pl.ds(s, c), :]
        kl = kl_s[pl.ds(s, c), :]
        v = i_ref[0, pl.ds(s, c), :]
        dl = dl_s[pl.ds(s, 8), :][0:1]
        vb = v.astype(BF16)
        outs = []
        new_state = []
        for h in range(N_HEADS):
            sl = slice(h * HEAD_DIM, (h + 1) * HEAD_DIM)
            sc = jnp.where(tril, _dot_nt(qe[:, sl], ke[:, sl]), 0.0)
            st = state[h]
            o = _dot(sc.astype(BF16), vb[:, sl]) + _dot_nt(qe[:, sl], st.astype(BF16))
            new_state.append(dl[:, sl] * st + _dot_tn(v[:, sl], kl[:, sl]))
            outs.append(o)
        o_s[pl.ds(s, c), :] = jnp.concatenate(outs, axis=1)
        return tuple(new_state)

    zero = jnp.zeros((HEAD_DIM, HEAD_DIM), F32)
    lax.fori_loop(0, l // c, step, (zero,) * N_HEADS, unroll=HG_UNROLL)

    head_avg = jnp.where(_div_pow2(_iota((w, w), 0), HEAD_DIM) == _div_pow2(_iota((w, w), 1), HEAD_DIM),
                         1.0 / HEAD_DIM, 0.0)

    def finish(t, carry):
        s = pl.multiple_of(t * tile, tile)
        o = o_s[pl.ds(s, tile), :]
        ms = _dot_hi(o * o, head_avg)
        o_ref[0, pl.ds(s, tile), :] = (o * lax.rsqrt(ms + EPS) * ng_ref[...]
                                       * _silu(g_ref[0, pl.ds(s, tile), :]))
        return carry

    lax.fori_loop(0, l // tile, finish, 0)


def _hgrn2(proj, lb, ng):
    b, l, _ = proj.shape
    w = GROUP_W
    full = lambda shape: pl.BlockSpec(shape, lambda bi: (0,) * len(shape))
    col = lambda j: pl.BlockSpec((1, l, w), lambda bi: (bi, 0, j))
    return pl.pallas_call(
        _hgrn2_kernel,
        grid=(b,),
        in_specs=[col(5), col(6), col(7), col(8), full((1, w)), full((1, w))],
        out_specs=pl.BlockSpec((1, l, w), lambda bi: (bi, 0, 0)),
        out_shape=jax.ShapeDtypeStruct((b, l, w), F32),
        scratch_shapes=[pltpu.VMEM((l, w), BF16), pltpu.VMEM((l, w), BF16), pltpu.VMEM((l, w), F32),
                        pltpu.VMEM((l, w), F32), pltpu.VMEM((l, w), F32)],
        compiler_params=_params("parallel"),
        name="hgrn2",
    )(proj, proj, proj, proj, lb.reshape(1, w), ng.reshape(1, w))


def _ssd_kernel(z_ref, x_ref, bm_ref, cm_ref, dt_ref, cw_ref, cb_ref, dtb_ref, aneg_ref,
                dskip_ref, ng_ref, o_ref):
    l = z_ref.shape[1]
    w = GROUP_W
    q = SSD_CHUNK
    p = HEAD_DIM
    r_q = _iota((q, q), 0)
    c_q = _iota((q, q), 1)
    tril = c_q <= r_q
    cum_m = jnp.where(tril, 1.0, 0.0)
    expand = jnp.where(jnp.logical_and(_iota((w, w), 0) == _div_pow2(_iota((w, w), 1), p),
                                       _iota((w, w), 0) < N_HEADS), 1.0, 0.0)
    pick = jnp.where(_iota((8, w), 1) == _iota((8, w), 0) * p, 1.0, 0.0)
    cw = cw_ref[...]
    cb = cb_ref[...]

    def chunk(n, carry):
        state, tails = carry
        s = pl.multiple_of(n * q, q)
        parts = []
        new_tails = []
        for j, ref in enumerate((x_ref, bm_ref, cm_ref)):
            raw = ref[0, pl.ds(s, q), :]
            sl = slice(j * w, (j + 1) * w)
            parts.append(_silu(_causal_conv(raw, tails[j], cw[:, sl], cb[:, sl])))
            new_tails.append(raw[q - 8:q])
        xs, bm, cm = parts
        dt = _softplus(dt_ref[0, pl.ds(s, q), :] + dtb_ref[...])
        dt_rep = _dot_hi(dt, expand)
        da_rep = _dot_hi(dt * aneg_ref[...], expand)
        a_cs = _dot_hi(cum_m, da_rep)
        a_row = _dot_nt(pick, a_cs, precision=HIGHEST)
        xd = xs * dt_rep
        ys = []
        new_state = []
        for h in range(N_HEADS):
            grp = h // (N_HEADS // 2)
            gs = slice(grp * SSD_STATE, (grp + 1) * SSD_STATE)
            hs = slice(h * p, (h + 1) * p)
            cg = cm[:, gs].astype(BF16)
            bg = bm[:, gs].astype(BF16)
            cb_qq = _dot_nt(cg, bg)
            a_col = a_cs[:, h * p:h * p + 1]
            decay = jnp.exp(jnp.where(tril, a_col - a_row[h:h + 1, :], -jnp.inf))
            y = _dot((cb_qq * decay).astype(BF16), xd[:, hs].astype(BF16))
            st = state[h]
            y = y + jnp.exp(a_cs[:, hs]) * _dot_nt(cg, st.astype(BF16))
            a_last = a_cs[q - 1:q, hs]
            to_end = jnp.exp(a_last - a_cs[:, hs])
            upd = _dot_tn(xd[:, hs] * to_end, bm[:, gs])
            chunk_decay = jnp.exp(a_row[h:h + 1, q - 1:q])
            new_state.append(chunk_decay * st + upd)
            ys.append(y)
        y = jnp.concatenate(ys, axis=1) + xs * dskip_ref[...]
        y = y * _silu(z_ref[0, pl.ds(s, q), :])
        halves = []
        for gi in range(2):
            yg = y[:, gi * LANES:(gi + 1) * LANES]
            ms = jnp.mean(yg * yg, axis=-1, keepdims=True)
            halves.append(yg * lax.rsqrt(ms + EPS))
        o_ref[0, pl.ds(s, q), :] = jnp.concatenate(halves, axis=1) * ng_ref[...]
        return tuple(new_state), tuple(new_tails)

    zero = jnp.zeros((p, SSD_STATE), F32)
    tail0 = jnp.zeros((8, w), F32)
    lax.fori_loop(0, l // q, chunk, ((zero,) * N_HEADS, (tail0,) * 3))


def _ssd(proj, conv_w, conv_b, dt_bias, a_log, d_skip, ng):
    b, l, _ = proj.shape
    w = GROUP_W
    full = lambda shape: pl.BlockSpec(shape, lambda bi: (0,) * len(shape))
    col = lambda j: pl.BlockSpec((1, l, w), lambda bi: (bi, 0, j))
    pad_heads = lambda v: jnp.zeros((1, w), F32).at[0, :N_HEADS].set(v)
    return pl.pallas_call(
        _ssd_kernel,
        grid=(b,),
        in_specs=[col(9), col(10), col(11), col(12), col(13),
                  full((conv_w.shape[0], 3 * w)), full((1, 3 * w)), full((1, w)), full((1, w)),
                  full((1, w)), full((1, w))],
        out_specs=pl.BlockSpec((1, l, w), lambda bi: (bi, 0, 0)),
        out_shape=jax.ShapeDtypeStruct((b, l, w), F32),
        compiler_params=_params("parallel"),
        name="ssd",
    )(proj, proj, proj, proj, proj, conv_w, conv_b.reshape(1, 3 * w), pad_heads(dt_bias),
      pad_heads(-jnp.exp(a_log)), jnp.repeat(d_skip, HEAD_DIM).reshape(1, w), ng.reshape(1, w))


def _outproj_router_kernel(h_ref, oa_ref, ob_ref, oc_ref, od_ref, wo_ref, g_ref, rwh_ref, rwl_ref, rb_ref,
                           h_out, xn_out, sel_out, gate_out, rank_out, cnt_out, tile_out, cnt_s):
    i = pl.program_id(0)
    tm = h_ref.shape[0]

    @pl.when(i == 0)
    def _():
        cnt_s[...] = jnp.zeros_like(cnt_s)

    acc = h_ref[...]
    for j, ref in enumerate((oa_ref, ob_ref, oc_ref, od_ref)):
        acc = acc + _dot(ref[...].astype(BF16), wo_ref[j * GROUP_W:(j + 1) * GROUP_W, :])
    h_out[...] = acc
    ms = jnp.mean(acc * acc, axis=-1, keepdims=True)
    xn = acc * lax.rsqrt(ms + EPS) * g_ref[...]
    xn_out[...] = xn
    lane = _iota((tm, LANES), 1)
    lane_f = lane.astype(F32)
    xn_hi = xn.astype(BF16)
    xn_lo = (xn - xn_hi.astype(F32)).astype(BF16)
    router = _dot(xn_hi, rwh_ref[...]) + _dot(xn_lo, rwh_ref[...]) + _dot(xn_hi, rwl_ref[...])
    logits = jnp.where(lane < N_EXPERTS, router + rb_ref[...], -jnp.inf)
    sel = jnp.zeros((tm, LANES), F32)
    tops = []
    for k in range(TOP_K):
        m = jnp.max(logits, axis=-1, keepdims=True)
        idx = jnp.min(jnp.where(logits == m, lane_f, float(LANES)), axis=-1, keepdims=True)
        hit = lane_f == idx
        sel = jnp.where(hit, float(k + 1), sel)
        logits = jnp.where(hit, -jnp.inf, logits)
        tops.append(m)
    exps = [jnp.exp(m - tops[0]) for m in tops]
    denom = exps[0] + exps[1] + exps[2] + exps[3]
    gate = jnp.zeros((tm, LANES), F32)
    for k in range(TOP_K):
        gate = jnp.where(sel == float(k + 1), exps[k] / denom, gate)
    sel_out[...] = sel
    gate_out[...] = gate
    member = jnp.where(sel > 0.0, 1.0, 0.0)
    before = jnp.where(_iota((tm, tm), 1) < _iota((tm, tm), 0), 1.0, 0.0).astype(BF16)
    carry = cnt_s[0:1, :]
    rank_out[...] = _dot(before, member.astype(BF16))
    in_tile = jnp.ceil(jnp.sum(member, axis=0, keepdims=True) * (1.0 / RUN_ALIGN)) * RUN_ALIGN
    total = carry + in_tile
    tile_out[...] = jnp.concatenate([carry, in_tile, jnp.zeros((6, LANES), F32)], axis=0)
    cnt_s[...] = jnp.broadcast_to(total, cnt_s.shape)
    cnt_out[...] = jnp.broadcast_to(total, cnt_out.shape)


def _outproj_router(h, outs, w_out, g, router_w, router_b):
    t, d = h.shape
    tm = ROW_TILE
    rw = jnp.zeros((d, LANES), F32).at[:, :N_EXPERTS].set(router_w)
    rb = jnp.zeros((1, LANES), F32).at[0, :N_EXPERTS].set(router_b)
    rw_hi = rw.astype(BF16)
    rw_lo = (rw - rw_hi.astype(F32)).astype(BF16)
    tile = lambda width: pl.BlockSpec((tm, width), lambda i: (i, 0))
    full = lambda shape: pl.BlockSpec(shape, lambda i: (0,) * len(shape))
    flat = [o.reshape(t, GROUP_W) for o in outs]
    return pl.pallas_call(
        _outproj_router_kernel,
        grid=(t // tm,),
        in_specs=[tile(d)] + [tile(GROUP_W)] * 4 + [full((d, d)), full((1, d)), full((d, LANES)),
                                                   full((d, LANES)), full((1, LANES))],
        out_specs=[tile(d), tile(d), tile(LANES), tile(LANES), tile(LANES), full((8, LANES)),
                   pl.BlockSpec((8, LANES), lambda i: (i, 0))],
        out_shape=[jax.ShapeDtypeStruct((t, d), F32), jax.ShapeDtypeStruct((t, d), F32),
                   jax.ShapeDtypeStruct((t, LANES), F32), jax.ShapeDtypeStruct((t, LANES), F32),
                   jax.ShapeDtypeStruct((t, LANES), F32), jax.ShapeDtypeStruct((8, LANES), F32),
                   jax.ShapeDtypeStruct((t // tm * 8, LANES), F32)],
        scratch_shapes=[pltpu.VMEM((8, LANES), F32)],
        compiler_params=_params("arbitrary"),
        name="outproj_router",
    )(h, *flat, w_out.astype(BF16), g.reshape(1, d), rw_hi, rw_lo, rb)


def _plan_kernel(cnt_ref, start_out, plan_out):
    cnt = cnt_ref[...]
    padded = jnp.ceil(cnt * (1.0 / MOE_BLOCK)) * MOE_BLOCK
    upto = jnp.where(_iota((LANES, LANES), 0) <= _iota((LANES, LANES), 1), 1.0, 0.0)
    pad_end = _dot_hi(padded, upto)
    pad_start = pad_end - padded
    start_out[...] = pad_start
    ends = jnp.transpose(jnp.broadcast_to(pad_end[0:1, :], (LANES, LANES)))
    expert_ok = _iota((LANES, LANES), 0) < N_EXPERTS
    n_active = pad_end[0:1, N_EXPERTS - 1:N_EXPERTS] * (1.0 / MOE_BLOCK)
    rows = [jnp.broadcast_to(n_active, (1, LANES)), cnt[0:1, :], pad_start[0:1, :], padded[0:1, :]]
    for r in range(PLAN_EXPERT_ROWS):
        blk_start = ((_iota((LANES, LANES), 1) + r * LANES) * MOE_BLOCK).astype(F32)
        done = jnp.where(jnp.logical_and(expert_ok, ends <= blk_start), 1.0, 0.0)
        rows.append(jnp.minimum(jnp.sum(done, axis=0, keepdims=True), N_EXPERTS - 1.0))
    plan_out[...] = jnp.concatenate(rows, axis=0).astype(I32)


def _plan(counts, n_blocks):
    assert n_blocks <= PLAN_EXPERT_ROWS * LANES
    return pl.pallas_call(
        _plan_kernel,
        out_shape=[jax.ShapeDtypeStruct((8, LANES), F32), jax.ShapeDtypeStruct((8, LANES), I32)],
        name="moe_plan",
    )(counts)


def _dest_kernel(sel_ref, gate_ref, rank_ref, tile_ref, start_ref, lpos_out, lk_out, gk_out, runs_out):
    tm = sel_ref.shape[0]
    sel = sel_ref[...]
    gate = gate_ref[...]
    before_tile = tile_ref[0:1, :]
    in_tile = tile_ref[1:2, :]
    earlier = jnp.where(_iota((LANES, LANES), 0) < _iota((LANES, LANES), 1), 1.0, 0.0)
    run_start = _dot_hi(jnp.broadcast_to(in_tile, (8, LANES)), earlier)[0:1, :]
    lpos = rank_ref[...] + run_start
    lane = _iota((tm, LANES), 1)
    pieces = []
    gk = jnp.zeros((tm, LANES), F32)
    lk = jnp.zeros((tm, LANES), F32)
    for k in range(TOP_K):
        hit = sel == float(k + 1)
        mine = jnp.where(hit, lpos, 0.0)
        pieces.append(mine)
        gk = jnp.where(lane == k, jnp.sum(jnp.where(hit, gate, 0.0), axis=-1, keepdims=True), gk)
        lk = jnp.where(lane == k, jnp.sum(mine, axis=-1, keepdims=True), lk)
    stacked = jnp.concatenate(pieces, axis=1)
    picker = jnp.where(_div_pow2(_iota((8, TOP_K * LANES), 1), LANES) == _iota((8, TOP_K * LANES), 0),
                       1.0, 0.0)
    lpos_out[...] = _dot_nt(picker, stacked, precision=HIGHEST).astype(I32)
    lk_out[...] = lk
    gk_out[...] = gk
    first_slot = start_ref[0:1, :] + before_tile
    runs_out[...] = jnp.concatenate([in_tile, run_start, first_slot, jnp.zeros((5, LANES), F32)],
                                    axis=0).astype(I32)


def _dest(sel, gate, rank, tile_counts, pad_start):
    t = sel.shape[0]
    tm = ROW_TILE
    tile = pl.BlockSpec((tm, LANES), lambda i: (i, 0))
    per_tile = pl.BlockSpec((8, LANES), lambda i: (i, 0))
    return pl.pallas_call(
        _dest_kernel,
        grid=(t // tm,),
        in_specs=[tile, tile, tile, per_tile, pl.BlockSpec((8, LANES), lambda i: (0, 0))],
        out_specs=[pl.BlockSpec((8, tm), lambda i: (0, i)), tile, tile, per_tile],
        out_shape=[jax.ShapeDtypeStruct((8, t), I32), jax.ShapeDtypeStruct((t, LANES), F32),
                   jax.ShapeDtypeStruct((t, LANES), F32), jax.ShapeDtypeStruct((t // tm * 8, LANES), I32)],
        compiler_params=_params("parallel"),
        name="moe_dest",
    )(sel, gate, rank, tile_counts, pad_start)


def _copy_run(src, dst, length, n_bits, make_copy, wait):
    units = jnp.right_shift(length, RUN_ALIGN.bit_length() - 1)
    for b in range(n_bits):
        offset = jnp.left_shift(jnp.right_shift(units, b + 1), b + 1) * RUN_ALIGN

        @pl.when(jnp.bitwise_and(jnp.right_shift(units, b), 1) == 1)
        def _():
            copy = make_copy(pl.multiple_of(src + offset, RUN_ALIGN), pl.multiple_of(dst + offset, RUN_ALIGN),
                             RUN_ALIGN << b)
            if wait:
                copy.wait()
            else:
                copy.start()


def _dispatch_kernel(plan_ref, runs_ref, lpos_ref, xn_ref, xs_hbm, sorted_s, sem):
    i = pl.program_id(0)
    tm = xn_ref.shape[0]
    n_local = sorted_s.shape[0]
    position = _iota((n_local, tm), 0)
    lpos = lpos_ref[...]
    place = jnp.zeros((n_local, tm), F32)
    for k in range(TOP_K):
        place = place + jnp.where(position == lpos[k:k + 1, :], 1.0, 0.0)
    sorted_s[...] = _dot(place.astype(BF16), xn_ref[...].astype(BF16))

    def to_slots(src, dst, size):
        return pltpu.make_async_copy(sorted_s.at[pl.ds(src, size)], xs_hbm.at[pl.ds(dst, size)], sem)

    def expert_runs(wait):
        def body(e, carry):
            _copy_run(runs_ref[RUN_START, e], runs_ref[RUN_SLOT, e], runs_ref[RUN_LEN, e], RUN_BITS,
                      to_slots, wait)
            return carry
        lax.fori_loop(0, N_EXPERTS, body, 0)

    expert_runs(False)
    expert_runs(True)

    @pl.when(i == 0)
    def _():
        def padding(wait):
            def body(e, carry):
                cnt = plan_ref[PLAN_COUNT, e]
                _copy_run(0, plan_ref[PLAN_START, e] + cnt, plan_ref[PLAN_PADDED, e] - cnt, PAD_BITS,
                          lambda src, dst, size: to_slots(0, dst, size), wait)
                return carry
            lax.fori_loop(0, N_EXPERTS, body, 0)

        padding(False)
        padding(True)


def _dispatch(plan, runs, lpos, xn, n_slots):
    t, d = xn.shape
    tm = ROW_TILE
    return pl.pallas_call(
        _dispatch_kernel,
        grid=(t // tm,),
        in_specs=[pl.BlockSpec(memory_space=pltpu.SMEM),
                  pl.BlockSpec((8, LANES), lambda i: (i, 0), memory_space=pltpu.SMEM),
                  pl.BlockSpec((8, tm), lambda i: (0, i)),
                  pl.BlockSpec((tm, d), lambda i: (i, 0))],
        out_specs=pl.BlockSpec(memory_space=pl.ANY),
        out_shape=jax.ShapeDtypeStruct((n_slots, d), F32),
        scratch_shapes=[pltpu.VMEM((LOCAL_ROWS, d), F32), pltpu.SemaphoreType.DMA(())],
        compiler_params=_params("arbitrary"),
        name="moe_dispatch",
    )(plan, runs, lpos, xn)


def _ffn_kernel(be_ref, na_ref, x_ref, wgu_ref, bgu_ref, wd_ref, bd_ref, o_ref, wgu_s, wd32_s, wd_s):
    j = pl.program_id(0)
    active = j < na_ref[0]
    new_expert = jnp.logical_or(j == 0, be_ref[j] != be_ref[jnp.maximum(j - 1, 0)])

    @pl.when(jnp.logical_and(active, new_expert))
    def _():
        wgu_s[...] = wgu_ref[0].astype(BF16)
        half = LANES // 2
        for c in range(wd_s.shape[0]):
            cols = slice(c * LANES, (c + 1) * LANES)
            for k in range(wd_s.shape[1] // LANES):
                for hb in range(2):
                    src = k * LANES + hb * half
                    wd32_s[c, pl.ds(k * LANES + hb, half, stride=2), :] = wd_ref[0, src:src + half, cols]
            wd_s[c] = wd32_s[c].astype(BF16)

    @pl.when(active)
    def _():
        x = x_ref[...].astype(BF16)
        hmid = _dot(x, wgu_s[...]) + bgu_ref[0]
        m = hmid.shape[0]
        even = (_iota((m, LANES), 1) & 1) == 0
        acts = []
        for k in range(hmid.shape[1] // (2 * LANES)):
            h0 = hmid[:, (2 * k) * LANES:(2 * k + 1) * LANES]
            h1 = hmid[:, (2 * k + 1) * LANES:(2 * k + 2) * LANES]
            glu = jnp.where(even, h0, pltpu.roll(h1, 1, 1))
            lin = jnp.where(even, pltpu.roll(h0, LANES - 1, 1), h1)
            glu = jnp.minimum(glu, SWIGLU_LIMIT)
            lin = jnp.clip(lin, -SWIGLU_LIMIT, SWIGLU_LIMIT)
            acts.append((glu * _sigmoid(SWIGLU_ALPHA * glu) * (lin + 1.0)).astype(BF16))
        act = jnp.concatenate(acts, axis=1)
        wd = jnp.concatenate([wd_s[c] for c in range(wd_s.shape[0])], axis=1)
        o_ref[...] = _dot(act, wd) + bd_ref[0]


def _ffn(layer, block_expert, n_active, xs, wgu, bgu, wd, bd):
    n_slots, d = xs.shape
    dff2 = wgu.shape[3]
    nb = n_slots // MOE_BLOCK
    rows = lambda j, be, na: (jnp.minimum(j, na[0] - 1), 0)
    per_expert = lambda j, be, na: (layer, be[j], 0, 0)
    grid_spec = pltpu.PrefetchScalarGridSpec(
        num_scalar_prefetch=2,
        grid=(nb,),
        in_specs=[pl.BlockSpec((MOE_BLOCK, d), rows),
                  pl.BlockSpec((None, 1, d, dff2), per_expert), pl.BlockSpec((None, 1, 1, dff2), per_expert),
                  pl.BlockSpec((None, 1, dff2 // 2, d), per_expert), pl.BlockSpec((None, 1, 1, d), per_expert)],
        out_specs=pl.BlockSpec((MOE_BLOCK, d), rows),
        scratch_shapes=[pltpu.VMEM((d, dff2), BF16), pltpu.VMEM((d // LANES, dff2 // 2, LANES), F32),
                        pltpu.VMEM((d // LANES, dff2 // 2, LANES), BF16)],
    )
    return pl.pallas_call(
        _ffn_kernel,
        grid_spec=grid_spec,
        out_shape=jax.ShapeDtypeStruct((n_slots, d), F32),
        compiler_params=_params("arbitrary"),
        name="moe_ffn",
    )(block_expert, n_active, xs, wgu, bgu, wd, bd)


def _combine_kernel(runs_ref, lk_ref, gk_ref, h_ref, g_ref, ys_hbm, o_ref, sorted_s, sem, *, final_norm):
    tm = h_ref.shape[0]
    n_local = sorted_s.shape[0]

    @pl.when(pl.program_id(0) == 0)
    def _():
        sorted_s[...] = jnp.zeros_like(sorted_s)

    def from_slots(src, dst, size):
        return pltpu.make_async_copy(ys_hbm.at[pl.ds(src, size)], sorted_s.at[pl.ds(dst, size)], sem)

    def expert_runs(wait):
        def body(e, carry):
            _copy_run(runs_ref[RUN_SLOT, e], runs_ref[RUN_START, e], runs_ref[RUN_LEN, e], RUN_BITS,
                      from_slots, wait)
            return carry
        lax.fori_loop(0, N_EXPERTS, body, 0)

    expert_runs(False)
    expert_runs(True)
    position = _iota((tm, n_local), 1).astype(F32)
    lk = lk_ref[...]
    gk = gk_ref[...]
    weights = jnp.zeros((tm, n_local), F32)
    for k in range(TOP_K):
        weights = weights + jnp.where(position == lk[:, k:k + 1], gk[:, k:k + 1], 0.0)
    acc = h_ref[...] + _dot(weights.astype(BF16), sorted_s[...].astype(BF16))
    if final_norm:
        ms = jnp.mean(acc * acc, axis=-1, keepdims=True)
        acc = acc * lax.rsqrt(ms + EPS) * g_ref[...]
    o_ref[...] = acc


def _combine(runs, lk, gk, h, ys, g, final_norm):
    t, d = h.shape
    tm = ROW_TILE
    return pl.pallas_call(
        functools.partial(_combine_kernel, final_norm=final_norm),
        grid=(t // tm,),
        in_specs=[pl.BlockSpec((8, LANES), lambda i: (i, 0), memory_space=pltpu.SMEM),
                  pl.BlockSpec((tm, LANES), lambda i: (i, 0)),
                  pl.BlockSpec((tm, LANES), lambda i: (i, 0)),
                  pl.BlockSpec((tm, d), lambda i: (i, 0)),
                  pl.BlockSpec((1, d), lambda i: (0, 0)),
                  pl.BlockSpec(memory_space=pl.ANY)],
        out_specs=pl.BlockSpec((tm, d), lambda i: (i, 0)),
        out_shape=jax.ShapeDtypeStruct((t, d), F32),
        scratch_shapes=[pltpu.VMEM((LOCAL_ROWS, d), F32), pltpu.SemaphoreType.DMA(())],
        compiler_params=_params("arbitrary"),
        name="moe_combine",
    )(runs, lk, gk, h, g.reshape(1, d), ys)


def _moe(h, xn, sel, gate, rank, counts, tile_counts, layer, w_gu, b_gu, w_down, b_down, g_final,
         final_norm):
    t, d = h.shape
    n_slots = t * TOP_K + (t // ROW_TILE) * N_EXPERTS * RUN_ALIGN + N_EXPERTS * MOE_BLOCK
    n_blocks = n_slots // MOE_BLOCK
    pad_start, plan = _plan(counts, n_blocks)
    lpos, lk, gk, runs = _dest(sel, gate, rank, tile_counts, pad_start)
    xs = _dispatch(plan, runs, lpos, xn, n_slots)
    block_expert = plan[PLAN_EXPERTS:].reshape(-1)[:n_blocks]
    n_active = plan[PLAN_ACTIVE, :1]
    depth = w_gu.shape[0]
    ys = _ffn(layer, block_expert, n_active, xs, w_gu, b_gu.reshape(depth, N_EXPERTS, 1, -1), w_down,
              b_down.reshape(depth, N_EXPERTS, 1, d))
    return _combine(runs, lk, gk, h, ys, g_final, final_norm)


def kernel(x, norm_mix_g, w_in, sb_norm_g, rg_conv_w, rg_conv_b, rg_wa, rg_ba, rg_wx, rg_bx,
           rg_lambda, rg_norm_g, hg_lower_bounds, hg_norm_g, m2_conv_w, m2_conv_b, m2_dt_bias,
           m2_a_log, m2_d, m2_norm_g, w_out, norm_ffn_g, router_w, router_b, moe_w_gu, moe_b_gu,
           moe_w_down, moe_b_down, final_norm_g):
    b, l, d = x.shape
    depth = w_in.shape[0]
    t = b * l
    lbs = jnp.cumsum(jax.nn.softmax(hg_lower_bounds.astype(F32), axis=0), axis=0)
    lbs = lbs - lbs[0]
    d_in = w_in.shape[2]
    w_in_p = jnp.zeros((depth, d, PROJ_BLOCKS * GROUP_W), BF16).at[:, :, :d_in].set(w_in.astype(BF16))
    h = x.reshape(t, d)
    for layer in range(depth):
        proj = _norm_inproj(h, norm_mix_g[layer], w_in_p[layer]).reshape(b, l, PROJ_BLOCKS * GROUP_W)
        o_a = _sb_attn(proj, sb_norm_g[layer])
        o_b = _rglru(proj, rg_conv_w[layer], rg_conv_b[layer], rg_wa[layer], rg_ba[layer],
                     rg_wx[layer], rg_bx[layer], rg_lambda[layer], rg_norm_g[layer])
        o_c = _hgrn2(proj, lbs[layer], hg_norm_g[layer])
        o_d = _ssd(proj, m2_conv_w[layer], m2_conv_b[layer], m2_dt_bias[layer], m2_a_log[layer],
                   m2_d[layer], m2_norm_g[layer])
        h, xn, sel, gate, rank, counts, tile_counts = _outproj_router(
            h, (o_a, o_b, o_c, o_d), w_out[layer], norm_ffn_g[layer], router_w[layer],
            router_b[layer])
        h = _moe(h, xn, sel, gate, rank, counts, tile_counts, layer, moe_w_gu, moe_b_gu, moe_w_down,
                 moe_b_down, final_norm_g, layer == depth - 1)
    return h.reshape(b, l, d)
```

```python
import functools
import math

import jax
import jax.numpy as jnp
from jax import lax
from jax.experimental import pallas as pl
from jax.experimental.pallas import tpu as pltpu

F32 = jnp.float32
BF16 = jnp.bfloat16
I32 = jnp.int32
HIGHEST = lax.Precision.HIGHEST
EPS = 1e-6
EXP_UNDERFLOW = -104.0

LANES = 128
GROUP_W = 256
HEAD_DIM = 64
N_HEADS = 4
SSD_STATE = 128
RG_C = 8.0
N_EXPERTS = 32
TOP_K = 4
SWIGLU_LIMIT = 7.0
SWIGLU_ALPHA = 1.702
PROJ_BLOCKS = 14
VMEM_LIMIT = 56 * 1024 * 1024

ROW_TILE = 256
ATTN_BLOCK = 256
SCAN_TILE = 256
HG_CHUNK = 16
SSD_CHUNK = 128
MOE_BLOCK = 512
HG_UNROLL = 8

PLAN_ACTIVE, PLAN_COUNT, PLAN_START, PLAN_PADDED, PLAN_EXPERTS = 0, 1, 2, 3, 4
PLAN_EXPERT_ROWS = 8 - PLAN_EXPERTS
RUN_LEN, RUN_START, RUN_SLOT = 0, 1, 2
RUN_ALIGN = 8
RUN_BITS = (ROW_TILE // RUN_ALIGN).bit_length()
PAD_BITS = (MOE_BLOCK // RUN_ALIGN - 1).bit_length()
LOCAL_ROWS = ROW_TILE * TOP_K + N_EXPERTS * RUN_ALIGN


def _params(*sem):
    return pltpu.CompilerParams(dimension_semantics=sem, vmem_limit_bytes=VMEM_LIMIT)


def _dot(a, b):
    return jnp.dot(a, b, preferred_element_type=F32)


def _dot_hi(a, b):
    return jnp.dot(a, b, preferred_element_type=F32, precision=HIGHEST)


def _dot_nt(a, b, precision=None):
    return lax.dot_general(a, b, (((1,), (1,)), ((), ())), preferred_element_type=F32,
                           precision=precision)


def _dot_tn(a, b):
    return lax.dot_general(a, b, (((0,), (0,)), ((), ())), preferred_element_type=F32)


def _sigmoid(x):
    return 1.0 / (1.0 + jnp.exp(-x))


def _silu(x):
    return x * _sigmoid(x)


def _softplus(x):
    return jnp.maximum(x, 0.0) + jnp.log(1.0 + jnp.exp(-jnp.abs(x)))


def _iota(shape, dim):
    return lax.broadcasted_iota(I32, shape, dim)


def _div_pow2(x, n):
    assert n & (n - 1) == 0
    return jnp.right_shift(x, int(math.log2(n)))


def _shift_rows(x, d, prev8):
    r = pltpu.roll(x, d, 0)
    p = pltpu.roll(prev8, d, 0)
    head = jnp.where(_iota(prev8.shape, 0) < d, p, r[:8])
    return jnp.concatenate([head, r[8:]], axis=0)


def _causal_conv(x, prev8, w, b):
    k = w.shape[0]
    y = x * w[k - 1:k] + b
    for d in range(1, k):
        y = y + _shift_rows(x, d, prev8) * w[k - 1 - d:k - d]
    return y


def _norm_inproj_kernel(x_ref, g_ref, w_ref, o_ref):
    x = x_ref[...]
    ms = jnp.mean(x * x, axis=-1, keepdims=True)
    xn = (x * lax.rsqrt(ms + EPS) * g_ref[...]).astype(BF16)
    o_ref[...] = _dot(xn, w_ref[...])


def _norm_inproj(h, g, w):
    t, d = h.shape
    n = w.shape[1]
    return pl.pallas_call(
        _norm_inproj_kernel,
        grid=(t // ROW_TILE,),
        in_specs=[pl.BlockSpec((ROW_TILE, d), lambda i: (i, 0)),
                  pl.BlockSpec((1, d), lambda i: (0, 0)),
                  pl.BlockSpec((d, n), lambda i: (0, 0))],
        out_specs=pl.BlockSpec((ROW_TILE, n), lambda i: (i, 0)),
        out_shape=jax.ShapeDtypeStruct((t, n), F32),
        compiler_params=_params("parallel"),
        name="norm_inproj",
    )(h, g.reshape(1, d), w)


def _sb_attn_kernel(q_ref, k_ref, v_ref, g_ref, o_ref, acc_ref, run_ref):
    i = pl.program_id(1)
    blk = ATTN_BLOCK
    scale = HEAD_DIM ** -0.5
    row = _iota((blk, blk), 0)
    col = _iota((blk, blk), 1)
    below = col < row
    r_w = _iota((blk, blk + LANES), 0)
    c_w = _iota((blk, blk + LANES), 1)
    later_sum = jnp.where(jnp.logical_or(c_w >= blk, r_w > c_w), 1.0, 0.0).astype(BF16)
    acc_ref[...] = jnp.zeros_like(acc_ref)
    run_ref[...] = jnp.zeros_like(run_ref)
    qs = [(q_ref[0, :, h * HEAD_DIM:(h + 1) * HEAD_DIM] * scale).astype(BF16) for h in range(N_HEADS)]

    def key_block(j, diagonal):
        ks = pl.multiple_of(j * blk, blk)
        for h in range(N_HEADS):
            sl = slice(h * HEAD_DIM, (h + 1) * HEAD_DIM)
            kh = k_ref[0, pl.ds(ks, blk), sl].astype(BF16)
            vh = v_ref[0, pl.ds(ks, blk), sl].astype(BF16)
            z = _dot_nt(qs[h], kh)
            ls = jnp.minimum(z, 0.0) - jnp.log(1.0 + jnp.exp(-jnp.abs(z)))
            lk = ls - z
            if diagonal:
                lk = jnp.where(below, lk, 0.0)
            lk_hi = lk.astype(BF16)
            lk_lo = (lk - lk_hi.astype(F32)).astype(BF16)
            sums = _dot(lk_hi, later_sum) + _dot(lk_lo, later_sum)
            run = run_ref[h]
            rem = sums[:, :blk] + jnp.concatenate([run] * (blk // LANES), axis=1)
            a = jnp.exp(ls + rem)
            if diagonal:
                a = jnp.where(below, a, 0.0)
            acc_ref[:, sl] += _dot(a.astype(BF16), vh)
            run_ref[h] = run + sums[:, blk:]

    key_block(i, True)

    def live():
        return jnp.max(run_ref[...]) > EXP_UNDERFLOW

    def more(carry):
        jj, alive = carry
        return jnp.logical_and(jj <= i, alive)

    def body(carry):
        jj, _ = carry
        key_block(i - jj, False)
        return jj + 1, live()

    lax.while_loop(more, body, (jnp.int32(1), live()))
    o = acc_ref[...]
    ms = jnp.mean(o * o, axis=-1, keepdims=True)
    o_ref[0] = o * lax.rsqrt(ms + EPS) * g_ref[...]


def _sb_attn(proj, g):
    b, l, _ = proj.shape
    blk = ATTN_BLOCK
    return pl.pallas_call(
        _sb_attn_kernel,
        grid=(b, l // blk),
        in_specs=[pl.BlockSpec((1, blk, GROUP_W), lambda bi, i: (bi, i, 0)),
                  pl.BlockSpec((1, l, GROUP_W), lambda bi, i: (bi, 0, 1)),
                  pl.BlockSpec((1, l, GROUP_W), lambda bi, i: (bi, 0, 2)),
                  pl.BlockSpec((1, GROUP_W), lambda bi, i: (0, 0))],
        out_specs=pl.BlockSpec((1, blk, GROUP_W), lambda bi, i: (bi, i, 0)),
        out_shape=jax.ShapeDtypeStruct((b, l, GROUP_W), F32),
        scratch_shapes=[pltpu.VMEM((blk, GROUP_W), F32), pltpu.VMEM((N_HEADS, blk, LANES), F32)],
        compiler_params=_params("parallel", "arbitrary"),
        name="sb_attn",
    )(proj, proj, proj, g.reshape(1, GROUP_W))


def _rglru_kernel(x_ref, gate_ref, cw_ref, cb_ref, wa_ref, ba_ref, wx_ref, bx_ref, lam_ref,
                  ng_ref, o_ref):
    l = x_ref.shape[1]
    w = x_ref.shape[2]
    tc = SCAN_TILE
    rows = _iota((tc, w), 0)
    cw = cw_ref[...]
    cb = cb_ref[...]
    neg_c_sp = -RG_C * _softplus(-lam_ref[...])

    def chunk(c, carry):
        h_prev, tail = carry
        s = pl.multiple_of(c * tc, tc)
        x = x_ref[0, pl.ds(s, tc), :]
        xc = _causal_conv(x, tail, cw, cb)
        xcb = xc.astype(BF16)
        r = _sigmoid(_dot(xcb, wa_ref[...]) + ba_ref[...])
        ig = _sigmoid(_dot(xcb, wx_ref[...]) + bx_ref[...])
        log_a = r * neg_c_sp
        a = jnp.exp(log_a)
        th = jnp.tanh(log_a)
        mult = jnp.sqrt(-2.0 * th / (1.0 - th))
        mult = jnp.where(rows + s == 0, 1.0, mult)
        u = mult * ig * xc
        d = 1
        while d < tc:
            keep = rows >= d
            a_s = jnp.where(keep, pltpu.roll(a, d, 0), 1.0)
            u_s = jnp.where(keep, pltpu.roll(u, d, 0), 0.0)
            u = a * u_s + u
            a = a * a_s
            d *= 2
        hs = u + a * h_prev
        gate = gate_ref[0, pl.ds(s, tc), :]
        gelu = 0.5 * gate * (1.0 + jnp.tanh(math.sqrt(2.0 / math.pi) * (gate + 0.044715 * gate * gate * gate)))
        o = hs * gelu
        ms = jnp.mean(o * o, axis=-1, keepdims=True)
        o_ref[0, pl.ds(s, tc), :] = o * lax.rsqrt(ms + EPS) * ng_ref[...]
        return hs[tc - 1:tc], x[tc - 8:tc]

    lax.fori_loop(0, l // tc, chunk, (jnp.zeros((1, w), F32), jnp.zeros((8, w), F32)))


def _block_diag(wb):
    n, d, _ = wb.shape
    out = jnp.zeros((n * d, n * d), wb.dtype)
    for i in range(n):
        out = lax.dynamic_update_slice(out, wb[i], (i * d, i * d))
    return out


def _rglru(proj, conv_w, conv_b, wa, ba, wx, bx, lam, ng):
    b, l, _ = proj.shape
    w = GROUP_W
    row = lambda v: v.reshape(1, w)
    full = lambda shape: pl.BlockSpec(shape, lambda bi: (0,) * len(shape))
    return pl.pallas_call(
        _rglru_kernel,
        grid=(b,),
        in_specs=[pl.BlockSpec((1, l, w), lambda bi: (bi, 0, 3)),
                  pl.BlockSpec((1, l, w), lambda bi: (bi, 0, 4)),
                  full((conv_w.shape[0], w)), full((1, w)), full((w, w)), full((1, w)),
                  full((w, w)), full((1, w)), full((1, w)), full((1, w))],
        out_specs=pl.BlockSpec((1, l, w), lambda bi: (bi, 0, 0)),
        out_shape=jax.ShapeDtypeStruct((b, l, w), F32),
        compiler_params=_params("parallel"),
        name="rglru",
    )(proj, proj, conv_w, row(conv_b), _block_diag(wa).astype(BF16), row(ba),
      _block_diag(wx).astype(BF16), row(bx), row(lam), row(ng))


def _hgrn2_kernel(q_ref, f_ref, i_ref, g_ref, lb_ref, ng_ref, o_ref,
                  qe_s, ke_s, kl_s, dl_s, o_s):
    l = q_ref.shape[1]
    w = q_ref.shape[2]
    tile = SCAN_TILE
    c = HG_CHUNK
    r_t = _iota((tile, tile), 0)
    c_t = _iota((tile, tile), 1)
    same = _div_pow2(r_t, c) == _div_pow2(c_t, c)
    cum_m = jnp.where(jnp.logical_and(same, c_t <= r_t), 1.0, 0.0)
    tot_m = jnp.where(same, 1.0, 0.0)
    lb = lb_ref[...]

    def prep(t, carry):
        s = pl.multiple_of(t * tile, tile)
        forget = lb + (1.0 - lb) * _sigmoid(f_ref[0, pl.ds(s, tile), :])
        log_f = jnp.log(forget)
        key = 1.0 - forget
        bcum = _dot_hi(cum_m, log_f)
        blast = _dot_hi(tot_m, log_f)
        q = _silu(q_ref[0, pl.ds(s, tile), :])
        qe_s[pl.ds(s, tile), :] = (q * jnp.exp(bcum)).astype(BF16)
        ke_s[pl.ds(s, tile), :] = (key * jnp.exp(-bcum)).astype(BF16)
        kl_s[pl.ds(s, tile), :] = key * jnp.exp(blast - bcum)
        dl_s[pl.ds(s, tile), :] = jnp.exp(blast)
        return carry

    lax.fori_loop(0, l // tile, prep, 0)

    tril = _iota((c, c), 1) <= _iota((c, c), 0)

    def step(n, state):
        s = pl.multiple_of(n * c, c)
        qe = qe_s[pl.ds(s, c), :]
        ke = ke_s[pl.ds(s, c), :]
        kl = kl_s[pl.ds(s, c), :]
        v = i_ref[0, pl.ds(s, c), :]
        dl = dl_s[pl.ds(s, 8), :][0:1]
        vb = v.astype(BF16)
        outs = []
        new_state = []
        for h in range(N_HEADS):
            sl = slice(h * HEAD_DIM, (h + 1) * HEAD_DIM)
            sc = jnp.where(tril, _dot_nt(qe[:, sl], ke[:, sl]), 0.0)
            st = state[h]
            o = _dot(sc.astype(BF16), vb[:, sl]) + _dot_nt(qe[:, sl], st.astype(BF16))
            new_state.append(dl[:, sl] * st + _dot_tn(v[:, sl], kl[:, sl]))
            outs.append(o)
        o_s[pl.ds(s, c), :] = jnp.concatenate(outs, axis=1)
        return tuple(new_state)

    zero = jnp.zeros((HEAD_DIM, HEAD_DIM), F32)
    lax.fori_loop(0, l // c, step, (zero,) * N_HEADS, unroll=HG_UNROLL)

    head_avg = jnp.where(_div_pow2(_iota((w, w), 0), HEAD_DIM) == _div_pow2(_iota((w, w), 1), HEAD_DIM),
                         1.0 / HEAD_DIM, 0.0)

    def finish(t, carry):
        s = pl.multiple_of(t * tile, tile)
        o = o_s[pl.ds(s, tile), :]
        ms = _dot_hi(o * o, head_avg)
        o_ref[0, pl.ds(s, tile), :] = (o * lax.rsqrt(ms + EPS) * ng_ref[...]
                                       * _silu(g_ref[0, pl.ds(s, tile), :]))
        return carry

    lax.fori_loop(0, l // tile, finish, 0)


def _hgrn2(proj, lb, ng):
    b, l, _ = proj.shape
    w = GROUP_W
    full = lambda shape: pl.BlockSpec(shape, lambda bi: (0,) * len(shape))
    col = lambda j: pl.BlockSpec((1, l, w), lambda bi: (bi, 0, j))
    return pl.pallas_call(
        _hgrn2_kernel,
        grid=(b,),
        in_specs=[col(5), col(6), col(7), col(8), full((1, w)), full((1, w))],
        out_specs=pl.BlockSpec((1, l, w), lambda bi: (bi, 0, 0)),
        out_shape=jax.ShapeDtypeStruct((b, l, w), F32),
        scratch_shapes=[pltpu.VMEM((l, w), BF16), pltpu.VMEM((l, w), BF16), pltpu.VMEM((l, w), F32),
                        pltpu.VMEM((l, w), F32), pltpu.VMEM((l, w), F32)],
        compiler_params=_params("parallel"),
        name="hgrn2",
    )(proj, proj, proj, proj, lb.reshape(1, w), ng.reshape(1, w))


def _ssd_kernel(z_ref, x_ref, bm_ref, cm_ref, dt_ref, cw_ref, cb_ref, dtb_ref, aneg_ref,
                dskip_ref, ng_ref, o_ref):
    l = z_ref.shape[1]
    w = GROUP_W
    q = SSD_CHUNK
    p = HEAD_DIM
    r_q = _iota((q, q), 0)
    c_q = _iota((q, q), 1)
    tril = c_q <= r_q
    cum_m = jnp.where(tril, 1.0, 0.0)
    expand = jnp.where(jnp.logical_and(_iota((w, w), 0) == _div_pow2(_iota((w, w), 1), p),
                                       _iota((w, w), 0) < N_HEADS), 1.0, 0.0)
    pick = jnp.where(_iota((8, w), 1) == _iota((8, w), 0) * p, 1.0, 0.0)
    cw = cw_ref[...]
    cb = cb_ref[...]

    def chunk(n, carry):
        state, tails = carry
        s = pl.multiple_of(n * q, q)
        parts = []
        new_tails = []
        for j, ref in enumerate((x_ref, bm_ref, cm_ref)):
            raw = ref[0, pl.ds(s, q), :]
            sl = slice(j * w, (j + 1) * w)
            parts.append(_silu(_causal_conv(raw, tails[j], cw[:, sl], cb[:, sl])))
            new_tails.append(raw[q - 8:q])
        xs, bm, cm = parts
        dt = _softplus(dt_ref[0, pl.ds(s, q), :] + dtb_ref[...])
        dt_rep = _dot_hi(dt, expand)
        da_rep = _dot_hi(dt * aneg_ref[...], expand)
        a_cs = _dot_hi(cum_m, da_rep)
        a_row = _dot_nt(pick, a_cs, precision=HIGHEST)
        xd = xs * dt_rep
        ys = []
        new_state = []
        for h in range(N_HEADS):
            grp = h // (N_HEADS // 2)
            gs = slice(grp * SSD_STATE, (grp + 1) * SSD_STATE)
            hs = slice(h * p, (h + 1) * p)
            cg = cm[:, gs].astype(BF16)
            bg = bm[:, gs].astype(BF16)
            cb_qq = _dot_nt(cg, bg)
            a_col = a_cs[:, h * p:h * p + 1]
            decay = jnp.exp(jnp.where(tril, a_col - a_row[h:h + 1, :], -jnp.inf))
            y = _dot((cb_qq * decay).astype(BF16), xd[:, hs].astype(BF16))
            st = state[h]
            y = y + jnp.exp(a_cs[:, hs]) * _dot_nt(cg, st.astype(BF16))
            a_last = a_cs[q - 1:q, hs]
            to_end = jnp.exp(a_last - a_cs[:, hs])
            upd = _dot_tn(xd[:, hs] * to_end, bm[:, gs])
            chunk_decay = jnp.exp(a_row[h:h + 1, q - 1:q])
            new_state.append(chunk_decay * st + upd)
            ys.append(y)
        y = jnp.concatenate(ys, axis=1) + xs * dskip_ref[...]
        y = y * _silu(z_ref[0, pl.ds(s, q), :])
        halves = []
        for gi in range(2):
            yg = y[:, gi * LANES:(gi + 1) * LANES]
            ms = jnp.mean(yg * yg, axis=-1, keepdims=True)
            halves.append(yg * lax.rsqrt(ms + EPS))
        o_ref[0, pl.ds(s, q), :] = jnp.concatenate(halves, axis=1) * ng_ref[...]
        return tuple(new_state), tuple(new_tails)

    zero = jnp.zeros((p, SSD_STATE), F32)
    tail0 = jnp.zeros((8, w), F32)
    lax.fori_loop(0, l // q, chunk, ((zero,) * N_HEADS, (tail0,) * 3), unroll=2)


def _ssd(proj, conv_w, conv_b, dt_bias, a_log, d_skip, ng):
    b, l, _ = proj.shape
    w = GROUP_W
    full = lambda shape: pl.BlockSpec(shape, lambda bi: (0,) * len(shape))
    col = lambda j: pl.BlockSpec((1, l, w), lambda bi: (bi, 0, j))
    pad_heads = lambda v: jnp.zeros((1, w), F32).at[0, :N_HEADS].set(v)
    return pl.pallas_call(
        _ssd_kernel,
        grid=(b,),
        in_specs=[col(9), col(10), col(11), col(12), col(13),
                  full((conv_w.shape[0], 3 * w)), full((1, 3 * w)), full((1, w)), full((1, w)),
                  full((1, w)), full((1, w))],
        out_specs=pl.BlockSpec((1, l, w), lambda bi: (bi, 0, 0)),
        out_shape=jax.ShapeDtypeStruct((b, l, w), F32),
        compiler_params=_params("parallel"),
        name="ssd",
    )(proj, proj, proj, proj, proj, conv_w, conv_b.reshape(1, 3 * w), pad_heads(dt_bias),
      pad_heads(-jnp.exp(a_log)), jnp.repeat(d_skip, HEAD_DIM).reshape(1, w), ng.reshape(1, w))


def _outproj_router_kernel(h_ref, oa_ref, ob_ref, oc_ref, od_ref, wo_ref, g_ref, rwh_ref, rwl_ref, rb_ref,
                           h_out, xn_out, sel_out, gate_out, rank_out, cnt_out, tile_out, cnt_s):
    i = pl.program_id(0)
    tm = h_ref.shape[0]

    @pl.when(i == 0)
    def _():
        cnt_s[...] = jnp.zeros_like(cnt_s)

    acc = h_ref[...]
    for j, ref in enumerate((oa_ref, ob_ref, oc_ref, od_ref)):
        acc = acc + _dot(ref[...].astype(BF16), wo_ref[j * GROUP_W:(j + 1) * GROUP_W, :])
    h_out[...] = acc
    ms = jnp.mean(acc * acc, axis=-1, keepdims=True)
    xn = acc * lax.rsqrt(ms + EPS) * g_ref[...]
    xn_out[...] = xn
    lane = _iota((tm, LANES), 1)
    lane_f = lane.astype(F32)
    xn_hi = xn.astype(BF16)
    xn_lo = (xn - xn_hi.astype(F32)).astype(BF16)
    router = _dot(xn_hi, rwh_ref[...]) + _dot(xn_lo, rwh_ref[...]) + _dot(xn_hi, rwl_ref[...])
    logits = jnp.where(lane < N_EXPERTS, router + rb_ref[...], -jnp.inf)
    sel = jnp.zeros((tm, LANES), F32)
    tops = []
    for k in range(TOP_K):
        m = jnp.max(logits, axis=-1, keepdims=True)
        idx = jnp.min(jnp.where(logits == m, lane_f, float(LANES)), axis=-1, keepdims=True)
        hit = lane_f == idx
        sel = jnp.where(hit, float(k + 1), sel)
        logits = jnp.where(hit, -jnp.inf, logits)
        tops.append(m)
    exps = [jnp.exp(m - tops[0]) for m in tops]
    denom = exps[0] + exps[1] + exps[2] + exps[3]
    gate = jnp.zeros((tm, LANES), F32)
    for k in range(TOP_K):
        gate = jnp.where(sel == float(k + 1), exps[k] / denom, gate)
    sel_out[...] = sel
    gate_out[...] = gate
    member = jnp.where(sel > 0.0, 1.0, 0.0)
    before = jnp.where(_iota((tm, tm), 1) < _iota((tm, tm), 0), 1.0, 0.0).astype(BF16)
    carry = cnt_s[0:1, :]
    rank_out[...] = _dot(before, member.astype(BF16))
    in_tile = jnp.ceil(jnp.sum(member, axis=0, keepdims=True) * (1.0 / RUN_ALIGN)) * RUN_ALIGN
    total = carry + in_tile
    tile_out[...] = jnp.concatenate([carry, in_tile, jnp.zeros((6, LANES), F32)], axis=0)
    cnt_s[...] = jnp.broadcast_to(total, cnt_s.shape)
    cnt_out[...] = jnp.broadcast_to(total, cnt_out.shape)


def _outproj_router(h, outs, w_out, g, router_w, router_b):
    t, d = h.shape
    tm = ROW_TILE
    rw = jnp.zeros((d, LANES), F32).at[:, :N_EXPERTS].set(router_w)
    rb = jnp.zeros((1, LANES), F32).at[0, :N_EXPERTS].set(router_b)
    rw_hi = rw.astype(BF16)
    rw_lo = (rw - rw_hi.astype(F32)).astype(BF16)
    tile = lambda width: pl.BlockSpec((tm, width), lambda i: (i, 0))
    full = lambda shape: pl.BlockSpec(shape, lambda i: (0,) * len(shape))
    flat = [o.reshape(t, GROUP_W) for o in outs]
    return pl.pallas_call(
        _outproj_router_kernel,
        grid=(t // tm,),
        in_specs=[tile(d)] + [tile(GROUP_W)] * 4 + [full((d, d)), full((1, d)), full((d, LANES)),
                                                   full((d, LANES)), full((1, LANES))],
        out_specs=[tile(d), tile(d), tile(LANES), tile(LANES), tile(LANES), full((8, LANES)),
                   pl.BlockSpec((8, LANES), lambda i: (i, 0))],
        out_shape=[jax.ShapeDtypeStruct((t, d), F32), jax.ShapeDtypeStruct((t, d), F32),
                   jax.ShapeDtypeStruct((t, LANES), F32), jax.ShapeDtypeStruct((t, LANES), F32),
                   jax.ShapeDtypeStruct((t, LANES), F32), jax.ShapeDtypeStruct((8, LANES), F32),
                   jax.ShapeDtypeStruct((t // tm * 8, LANES), F32)],
        scratch_shapes=[pltpu.VMEM((8, LANES), F32)],
        compiler_params=_params("arbitrary"),
        name="outproj_router",
    )(h, *flat, w_out.astype(BF16), g.reshape(1, d), rw_hi, rw_lo, rb)


def _plan_kernel(cnt_ref, start_out, plan_out):
    cnt = cnt_ref[...]
    padded = jnp.ceil(cnt * (1.0 / MOE_BLOCK)) * MOE_BLOCK
    upto = jnp.where(_iota((LANES, LANES), 0) <= _iota((LANES, LANES), 1), 1.0, 0.0)
    pad_end = _dot_hi(padded, upto)
    pad_start = pad_end - padded
    start_out[...] = pad_start
    ends = jnp.transpose(jnp.broadcast_to(pad_end[0:1, :], (LANES, LANES)))
    expert_ok = _iota((LANES, LANES), 0) < N_EXPERTS
    n_active = pad_end[0:1, N_EXPERTS - 1:N_EXPERTS] * (1.0 / MOE_BLOCK)
    rows = [jnp.broadcast_to(n_active, (1, LANES)), cnt[0:1, :], pad_start[0:1, :], padded[0:1, :]]
    for r in range(PLAN_EXPERT_ROWS):
        blk_start = ((_iota((LANES, LANES), 1) + r * LANES) * MOE_BLOCK).astype(F32)
        done = jnp.where(jnp.logical_and(expert_ok, ends <= blk_start), 1.0, 0.0)
        rows.append(jnp.minimum(jnp.sum(done, axis=0, keepdims=True), N_EXPERTS - 1.0))
    plan_out[...] = jnp.concatenate(rows, axis=0).astype(I32)


def _plan(counts, n_blocks):
    assert n_blocks <= PLAN_EXPERT_ROWS * LANES
    return pl.pallas_call(
        _plan_kernel,
        out_shape=[jax.ShapeDtypeStruct((8, LANES), F32), jax.ShapeDtypeStruct((8, LANES), I32)],
        name="moe_plan",
    )(counts)


def _dest_kernel(sel_ref, gate_ref, rank_ref, tile_ref, start_ref, lpos_out, lk_out, gk_out, runs_out):
    tm = sel_ref.shape[0]
    sel = sel_ref[...]
    gate = gate_ref[...]
    before_tile = tile_ref[0:1, :]
    in_tile = tile_ref[1:2, :]
    earlier = jnp.where(_iota((LANES, LANES), 0) < _iota((LANES, LANES), 1), 1.0, 0.0)
    run_start = _dot_hi(jnp.broadcast_to(in_tile, (8, LANES)), earlier)[0:1, :]
    lpos = rank_ref[...] + run_start
    lane = _iota((tm, LANES), 1)
    pieces = []
    gk = jnp.zeros((tm, LANES), F32)
    lk = jnp.zeros((tm, LANES), F32)
    for k in range(TOP_K):
        hit = sel == float(k + 1)
        mine = jnp.where(hit, lpos, 0.0)
        pieces.append(mine)
        gk = jnp.where(lane == k, jnp.sum(jnp.where(hit, gate, 0.0), axis=-1, keepdims=True), gk)
        lk = jnp.where(lane == k, jnp.sum(mine, axis=-1, keepdims=True), lk)
    stacked = jnp.concatenate(pieces, axis=1)
    picker = jnp.where(_div_pow2(_iota((8, TOP_K * LANES), 1), LANES) == _iota((8, TOP_K * LANES), 0),
                       1.0, 0.0)
    lpos_out[...] = _dot_nt(picker, stacked, precision=HIGHEST).astype(I32)
    lk_out[...] = lk
    gk_out[...] = gk
    first_slot = start_ref[0:1, :] + before_tile
    runs_out[...] = jnp.concatenate([in_tile, run_start, first_slot, jnp.zeros((5, LANES), F32)],
                                    axis=0).astype(I32)


def _dest(sel, gate, rank, tile_counts, pad_start):
    t = sel.shape[0]
    tm = ROW_TILE
    tile = pl.BlockSpec((tm, LANES), lambda i: (i, 0))
    per_tile = pl.BlockSpec((8, LANES), lambda i: (i, 0))
    return pl.pallas_call(
        _dest_kernel,
        grid=(t // tm,),
        in_specs=[tile, tile, tile, per_tile, pl.BlockSpec((8, LANES), lambda i: (0, 0))],
        out_specs=[pl.BlockSpec((8, tm), lambda i: (0, i)), tile, tile, per_tile],
        out_shape=[jax.ShapeDtypeStruct((8, t), I32), jax.ShapeDtypeStruct((t, LANES), F32),
                   jax.ShapeDtypeStruct((t, LANES), F32), jax.ShapeDtypeStruct((t // tm * 8, LANES), I32)],
        compiler_params=_params("parallel"),
        name="moe_dest",
    )(sel, gate, rank, tile_counts, pad_start)


def _copy_run(src, dst, length, n_bits, make_copy, wait):
    units = jnp.right_shift(length, RUN_ALIGN.bit_length() - 1)
    for b in range(n_bits):
        offset = jnp.left_shift(jnp.right_shift(units, b + 1), b + 1) * RUN_ALIGN

        @pl.when(jnp.bitwise_and(jnp.right_shift(units, b), 1) == 1)
        def _():
            copy = make_copy(pl.multiple_of(src + offset, RUN_ALIGN), pl.multiple_of(dst + offset, RUN_ALIGN),
                             RUN_ALIGN << b)
            if wait:
                copy.wait()
            else:
                copy.start()


def _dispatch_kernel(plan_ref, runs_ref, prev_runs_ref, lpos_ref, xn_ref, xs_hbm, sorted_s, sem):
    i = pl.program_id(0)
    tm = xn_ref.shape[0]
    n_local = sorted_s.shape[1]
    slot = i % 2

    def expert_runs(table, buf, wait):
        def to_slots(src, dst, size):
            return pltpu.make_async_copy(sorted_s.at[buf, pl.ds(src, size)], xs_hbm.at[pl.ds(dst, size)],
                                         sem.at[buf])

        def body(e, carry):
            _copy_run(table[RUN_START, e], table[RUN_SLOT, e], table[RUN_LEN, e], RUN_BITS, to_slots, wait)
            return carry
        lax.fori_loop(0, N_EXPERTS, body, 0)

    position = _iota((n_local, tm), 0)
    lpos = lpos_ref[...]
    place = jnp.zeros((n_local, tm), F32)
    for k in range(TOP_K):
        place = place + jnp.where(position == lpos[k:k + 1, :], 1.0, 0.0)
    sorted_s[slot] = _dot(place.astype(BF16), xn_ref[...].astype(BF16))
    expert_runs(runs_ref, slot, False)

    @pl.when(i > 0)
    def _():
        expert_runs(prev_runs_ref, 1 - slot, True)

    @pl.when(i == pl.num_programs(0) - 1)
    def _():
        expert_runs(runs_ref, slot, True)

    @pl.when(i == 0)
    def _():
        def padding(wait):
            def fill(src, dst, size):
                return pltpu.make_async_copy(sorted_s.at[0, pl.ds(0, size)], xs_hbm.at[pl.ds(dst, size)],
                                             sem.at[1])

            def body(e, carry):
                cnt = plan_ref[PLAN_COUNT, e]
                _copy_run(0, plan_ref[PLAN_START, e] + cnt, plan_ref[PLAN_PADDED, e] - cnt, PAD_BITS, fill,
                          wait)
                return carry
            lax.fori_loop(0, N_EXPERTS, body, 0)

        padding(False)
        padding(True)


def _dispatch(plan, runs, lpos, xn, n_slots):
    t, d = xn.shape
    tm = ROW_TILE
    return pl.pallas_call(
        _dispatch_kernel,
        grid=(t // tm,),
        in_specs=[pl.BlockSpec(memory_space=pltpu.SMEM),
                  pl.BlockSpec((8, LANES), lambda i: (i, 0), memory_space=pltpu.SMEM),
                  pl.BlockSpec((8, LANES), lambda i: (jnp.maximum(i - 1, 0), 0), memory_space=pltpu.SMEM),
                  pl.BlockSpec((8, tm), lambda i: (0, i)),
                  pl.BlockSpec((tm, d), lambda i: (i, 0))],
        out_specs=pl.BlockSpec(memory_space=pl.ANY),
        out_shape=jax.ShapeDtypeStruct((n_slots, d), F32),
        scratch_shapes=[pltpu.VMEM((2, LOCAL_ROWS, d), F32), pltpu.SemaphoreType.DMA((2,))],
        compiler_params=_params("arbitrary"),
        name="moe_dispatch",
    )(plan, runs, runs, lpos, xn)


def _ffn_kernel(be_ref, na_ref, x_ref, wgu_ref, bgu_ref, wd_ref, bd_ref, o_ref, wgu_s, wd32_s, wd_s):
    j = pl.program_id(0)
    active = j < na_ref[0]
    new_expert = jnp.logical_or(j == 0, be_ref[j] != be_ref[jnp.maximum(j - 1, 0)])

    @pl.when(jnp.logical_and(active, new_expert))
    def _():
        wgu_s[...] = wgu_ref[0].astype(BF16)
        half = LANES // 2
        for c in range(wd_s.shape[0]):
            cols = slice(c * LANES, (c + 1) * LANES)
            for k in range(wd_s.shape[1] // LANES):
                for hb in range(2):
                    src = k * LANES + hb * half
                    wd32_s[c, pl.ds(k * LANES + hb, half, stride=2), :] = wd_ref[0, src:src + half, cols]
            wd_s[c] = wd32_s[c].astype(BF16)

    @pl.when(active)
    def _():
        x = x_ref[...].astype(BF16)
        hmid = _dot(x, wgu_s[...]) + bgu_ref[0]
        m = hmid.shape[0]
        even = (_iota((m, LANES), 1) & 1) == 0
        acts = []
        for k in range(hmid.shape[1] // (2 * LANES)):
            h0 = hmid[:, (2 * k) * LANES:(2 * k + 1) * LANES]
            h1 = hmid[:, (2 * k + 1) * LANES:(2 * k + 2) * LANES]
            glu = jnp.where(even, h0, pltpu.roll(h1, 1, 1))
            lin = jnp.where(even, pltpu.roll(h0, LANES - 1, 1), h1)
            glu = jnp.minimum(glu, SWIGLU_LIMIT)
            lin = jnp.clip(lin, -SWIGLU_LIMIT, SWIGLU_LIMIT)
            acts.append((glu * _sigmoid(SWIGLU_ALPHA * glu) * (lin + 1.0)).astype(BF16))
        act = jnp.concatenate(acts, axis=1)
        wd = jnp.concatenate([wd_s[c] for c in range(wd_s.shape[0])], axis=1)
        o_ref[...] = _dot(act, wd) + bd_ref[0]


def _ffn(layer, block_expert, n_active, xs, wgu, bgu, wd, bd):
    n_slots, d = xs.shape
    dff2 = wgu.shape[3]
    nb = n_slots // MOE_BLOCK
    rows = lambda j, be, na: (jnp.maximum(jnp.minimum(j, na[0] - 1), 0), 0)
    per_expert = lambda j, be, na: (layer, be[j], 0, 0)
    grid_spec = pltpu.PrefetchScalarGridSpec(
        num_scalar_prefetch=2,
        grid=(nb,),
        in_specs=[pl.BlockSpec((MOE_BLOCK, d), rows),
                  pl.BlockSpec((None, 1, d, dff2), per_expert), pl.BlockSpec((None, 1, 1, dff2), per_expert),
                  pl.BlockSpec((None, 1, dff2 // 2, d), per_expert), pl.BlockSpec((None, 1, 1, d), per_expert)],
        out_specs=pl.BlockSpec((MOE_BLOCK, d), rows),
        scratch_shapes=[pltpu.VMEM((d, dff2), BF16), pltpu.VMEM((d // LANES, dff2 // 2, LANES), F32),
                        pltpu.VMEM((d // LANES, dff2 // 2, LANES), BF16)],
    )
    return pl.pallas_call(
        _ffn_kernel,
        grid_spec=grid_spec,
        out_shape=jax.ShapeDtypeStruct((n_slots, d), F32),
        compiler_params=_params("arbitrary"),
        name="moe_ffn",
    )(block_expert, n_active, xs, wgu, bgu, wd, bd)


def _combine_kernel(runs_ref, next_runs_ref, lk_ref, gk_ref, h_ref, g_ref, ys_hbm, o_ref, sorted_s, sem, *,
                    final_norm):
    i = pl.program_id(0)
    tm = h_ref.shape[0]
    n_local = sorted_s.shape[1]
    slot = i % 2

    def expert_runs(table, buf, wait):
        def from_slots(src, dst, size):
            return pltpu.make_async_copy(ys_hbm.at[pl.ds(src, size)], sorted_s.at[buf, pl.ds(dst, size)],
                                         sem.at[buf])

        def body(e, carry):
            _copy_run(table[RUN_SLOT, e], table[RUN_START, e], table[RUN_LEN, e], RUN_BITS, from_slots, wait)
            return carry
        lax.fori_loop(0, N_EXPERTS, body, 0)

    @pl.when(i == 0)
    def _():
        sorted_s[...] = jnp.zeros_like(sorted_s)
        expert_runs(runs_ref, slot, False)

    @pl.when(i + 1 < pl.num_programs(0))
    def _():
        expert_runs(next_runs_ref, 1 - slot, False)

    expert_runs(runs_ref, slot, True)
    position = _iota((tm, n_local), 1).astype(F32)
    lk = lk_ref[...]
    gk = gk_ref[...]
    weights = jnp.zeros((tm, n_local), F32)
    for k in range(TOP_K):
        weights = weights + jnp.where(position == lk[:, k:k + 1], gk[:, k:k + 1], 0.0)
    acc = h_ref[...] + _dot(weights.astype(BF16), sorted_s[slot].astype(BF16))
    if final_norm:
        ms = jnp.mean(acc * acc, axis=-1, keepdims=True)
        acc = acc * lax.rsqrt(ms + EPS) * g_ref[...]
    o_ref[...] = acc


def _combine(runs, lk, gk, h, ys, g, final_norm):
    t, d = h.shape
    tm = ROW_TILE
    return pl.pallas_call(
        functools.partial(_combine_kernel, final_norm=final_norm),
        grid=(t // tm,),
        in_specs=[pl.BlockSpec((8, LANES), lambda i: (i, 0), memory_space=pltpu.SMEM),
                  pl.BlockSpec((8, LANES), lambda i: (jnp.minimum(i + 1, t // tm - 1), 0),
                               memory_space=pltpu.SMEM),
                  pl.BlockSpec((tm, LANES), lambda i: (i, 0)),
                  pl.BlockSpec((tm, LANES), lambda i: (i, 0)),
                  pl.BlockSpec((tm, d), lambda i: (i, 0)),
                  pl.BlockSpec((1, d), lambda i: (0, 0)),
                  pl.BlockSpec(memory_space=pl.ANY)],
        out_specs=pl.BlockSpec((tm, d), lambda i: (i, 0)),
        out_shape=jax.ShapeDtypeStruct((t, d), F32),
        scratch_shapes=[pltpu.VMEM((2, LOCAL_ROWS, d), F32), pltpu.SemaphoreType.DMA((2,))],
        compiler_params=_params("arbitrary"),
        name="moe_combine",
    )(runs, runs, lk, gk, h, g.reshape(1, d), ys)


def _moe(h, xn, sel, gate, rank, counts, tile_counts, layer, w_gu, b_gu, w_down, b_down, g_final,
         final_norm):
    t, d = h.shape
    n_slots = t * TOP_K + (t // ROW_TILE) * N_EXPERTS * RUN_ALIGN + N_EXPERTS * MOE_BLOCK
    n_blocks = n_slots // MOE_BLOCK
    pad_start, plan = _plan(counts, n_blocks)
    lpos, lk, gk, runs = _dest(sel, gate, rank, tile_counts, pad_start)
    xs = _dispatch(plan, runs, lpos, xn, n_slots)
    block_expert = plan[PLAN_EXPERTS:].reshape(-1)[:n_blocks]
    n_active = plan[PLAN_ACTIVE, :1]
    depth = w_gu.shape[0]
    ys = _ffn(layer, block_expert, n_active, xs, w_gu, b_gu.reshape(depth, N_EXPERTS, 1, -1), w_down,
              b_down.reshape(depth, N_EXPERTS, 1, d))
    return _combine(runs, lk, gk, h, ys, g_final, final_norm)


def kernel(x, norm_mix_g, w_in, sb_norm_g, rg_conv_w, rg_conv_b, rg_wa, rg_ba, rg_wx, rg_bx,
           rg_lambda, rg_norm_g, hg_lower_bounds, hg_norm_g, m2_conv_w, m2_conv_b, m2_dt_bias,
           m2_a_log, m2_d, m2_norm_g, w_out, norm_ffn_g, router_w, router_b, moe_w_gu, moe_b_gu,
           moe_w_down, moe_b_down, final_norm_g):
    b, l, d = x.shape
    depth = w_in.shape[0]
    t = b * l
    lbs = jnp.cumsum(jax.nn.softmax(hg_lower_bounds.astype(F32), axis=0), axis=0)
    lbs = lbs - lbs[0]
    d_in = w_in.shape[2]
    w_in_p = jnp.zeros((depth, d, PROJ_BLOCKS * GROUP_W), BF16).at[:, :, :d_in].set(w_in.astype(BF16))
    h = x.reshape(t, d)
    for layer in range(depth):
        proj = _norm_inproj(h, norm_mix_g[layer], w_in_p[layer]).reshape(b, l, PROJ_BLOCKS * GROUP_W)
        o_a = _sb_attn(proj, sb_norm_g[layer])
        o_b = _rglru(proj, rg_conv_w[layer], rg_conv_b[layer], rg_wa[layer], rg_ba[layer],
                     rg_wx[layer], rg_bx[layer], rg_lambda[layer], rg_norm_g[layer])
        o_c = _hgrn2(proj, lbs[layer], hg_norm_g[layer])
        o_d = _ssd(proj, m2_conv_w[layer], m2_conv_b[layer], m2_dt_bias[layer], m2_a_log[layer],
                   m2_d[layer], m2_norm_g[layer])
        h, xn, sel, gate, rank, counts, tile_counts = _outproj_router(
            h, (o_a, o_b, o_c, o_d), w_out[layer], norm_ffn_g[layer], router_w[layer],
            router_b[layer])
        h = _moe(h, xn, sel, gate, rank, counts, tile_counts, layer, moe_w_gu, moe_b_gu, moe_w_down,
                 moe_b_down, final_norm_g, layer == depth - 1)
    return h.reshape(b, l, d)
```

```python
import functools
import math

import jax
import jax.numpy as jnp
from jax import lax
from jax.experimental import pallas as pl
from jax.experimental.pallas import tpu as pltpu

F32 = jnp.float32
BF16 = jnp.bfloat16
I32 = jnp.int32
HIGHEST = lax.Precision.HIGHEST
EPS = 1e-6
EXP_UNDERFLOW = -104.0

LANES = 128
GROUP_W = 256
HEAD_DIM = 64
N_HEADS = 4
SSD_STATE = 128
RG_C = 8.0
N_EXPERTS = 32
TOP_K = 4
SWIGLU_LIMIT = 7.0
SWIGLU_ALPHA = 1.702
PROJ_BLOCKS = 14
VMEM_LIMIT = 56 * 1024 * 1024

ROW_TILE = 256
INPROJ_TILE = 512
ATTN_BLOCK = 256
SCAN_TILE = 256
HG_CHUNK = 16
SSD_CHUNK = 128
MOE_BLOCK = 512
HG_UNROLL = 8

PLAN_ACTIVE, PLAN_COUNT, PLAN_START, PLAN_PADDED, PLAN_EXPERTS = 0, 1, 2, 3, 4
PLAN_EXPERT_ROWS = 8 - PLAN_EXPERTS
RUN_LEN, RUN_START, RUN_SLOT, RUN_TOTAL = 0, 1, 2, 3
RUN_ALIGN = 8
RUN_BITS = (ROW_TILE // RUN_ALIGN).bit_length()
PAD_BITS = (MOE_BLOCK // RUN_ALIGN - 1).bit_length()
LOCAL_ROWS = ROW_TILE * TOP_K + N_EXPERTS * RUN_ALIGN
TOTAL_BITS = (LOCAL_ROWS // RUN_ALIGN).bit_length()


def _params(*sem):
    return pltpu.CompilerParams(dimension_semantics=sem, vmem_limit_bytes=VMEM_LIMIT)


def _dot(a, b):
    return jnp.dot(a, b, preferred_element_type=F32)


def _dot_hi(a, b):
    return jnp.dot(a, b, preferred_element_type=F32, precision=HIGHEST)


def _dot_nt(a, b, precision=None):
    return lax.dot_general(a, b, (((1,), (1,)), ((), ())), preferred_element_type=F32,
                           precision=precision)


def _dot_tn(a, b):
    return lax.dot_general(a, b, (((0,), (0,)), ((), ())), preferred_element_type=F32)


def _sigmoid(x):
    return 1.0 / (1.0 + jnp.exp(-x))


def _silu(x):
    return x * _sigmoid(x)


def _softplus(x):
    return jnp.maximum(x, 0.0) + jnp.log(1.0 + jnp.exp(-jnp.abs(x)))


def _iota(shape, dim):
    return lax.broadcasted_iota(I32, shape, dim)


def _div_pow2(x, n):
    assert n & (n - 1) == 0
    return jnp.right_shift(x, int(math.log2(n)))


def _shift_rows(x, d, prev8):
    r = pltpu.roll(x, d, 0)
    p = pltpu.roll(prev8, d, 0)
    head = jnp.where(_iota(prev8.shape, 0) < d, p, r[:8])
    return jnp.concatenate([head, r[8:]], axis=0)


def _causal_conv(x, prev8, w, b):
    k = w.shape[0]
    y = x * w[k - 1:k] + b
    for d in range(1, k):
        y = y + _shift_rows(x, d, prev8) * w[k - 1 - d:k - d]
    return y


def _norm_inproj_kernel(x_ref, g_ref, w_ref, o_ref):
    x = x_ref[...]
    ms = jnp.mean(x * x, axis=-1, keepdims=True)
    xn = (x * lax.rsqrt(ms + EPS) * g_ref[...]).astype(BF16)
    o_ref[...] = _dot(xn, w_ref[...])


def _norm_inproj(h, g, w):
    t, d = h.shape
    n = w.shape[1]
    return pl.pallas_call(
        _norm_inproj_kernel,
        grid=(t // INPROJ_TILE,),
        in_specs=[pl.BlockSpec((INPROJ_TILE, d), lambda i: (i, 0)),
                  pl.BlockSpec((1, d), lambda i: (0, 0)),
                  pl.BlockSpec((d, n), lambda i: (0, 0))],
        out_specs=pl.BlockSpec((INPROJ_TILE, n), lambda i: (i, 0)),
        out_shape=jax.ShapeDtypeStruct((t, n), F32),
        compiler_params=_params("parallel"),
        name="norm_inproj",
    )(h, g.reshape(1, d), w)


def _sb_attn_kernel(q_ref, k_ref, v_ref, g_ref, o_ref, acc_ref, run_ref):
    i = pl.program_id(1)
    blk = ATTN_BLOCK
    scale = HEAD_DIM ** -0.5
    row = _iota((blk, blk), 0)
    col = _iota((blk, blk), 1)
    below = col < row
    r_w = _iota((blk, blk + LANES), 0)
    c_w = _iota((blk, blk + LANES), 1)
    later_sum = jnp.where(jnp.logical_or(c_w >= blk, r_w > c_w), 1.0, 0.0).astype(BF16)
    acc_ref[...] = jnp.zeros_like(acc_ref)
    run_ref[...] = jnp.zeros_like(run_ref)
    qs = [(q_ref[0, :, h * HEAD_DIM:(h + 1) * HEAD_DIM] * scale).astype(BF16) for h in range(N_HEADS)]

    def key_block(j, diagonal):
        ks = pl.multiple_of(j * blk, blk)
        for h in range(N_HEADS):
            sl = slice(h * HEAD_DIM, (h + 1) * HEAD_DIM)
            kh = k_ref[0, pl.ds(ks, blk), sl].astype(BF16)
            vh = v_ref[0, pl.ds(ks, blk), sl].astype(BF16)
            z = _dot_nt(qs[h], kh)
            ls = jnp.minimum(z, 0.0) - jnp.log(1.0 + jnp.exp(-jnp.abs(z)))
            lk = ls - z
            if diagonal:
                lk = jnp.where(below, lk, 0.0)
            lk_hi = lk.astype(BF16)
            lk_lo = (lk - lk_hi.astype(F32)).astype(BF16)
            sums = _dot(lk_hi, later_sum) + _dot(lk_lo, later_sum)
            run = run_ref[h]
            rem = sums[:, :blk] + jnp.concatenate([run] * (blk // LANES), axis=1)
            a = jnp.exp(ls + rem)
            if diagonal:
                a = jnp.where(below, a, 0.0)
            acc_ref[:, sl] += _dot(a.astype(BF16), vh)
            run_ref[h] = run + sums[:, blk:]

    key_block(i, True)

    def live():
        return jnp.max(run_ref[...]) > EXP_UNDERFLOW

    def more(carry):
        jj, alive = carry
        return jnp.logical_and(jj <= i, alive)

    def body(carry):
        jj, _ = carry
        key_block(i - jj, False)
        return jj + 1, live()

    lax.while_loop(more, body, (jnp.int32(1), live()))
    o = acc_ref[...]
    ms = jnp.mean(o * o, axis=-1, keepdims=True)
    o_ref[0] = o * lax.rsqrt(ms + EPS) * g_ref[...]


def _sb_attn(proj, g):
    b, l, _ = proj.shape
    blk = ATTN_BLOCK
    return pl.pallas_call(
        _sb_attn_kernel,
        grid=(b, l // blk),
        in_specs=[pl.BlockSpec((1, blk, GROUP_W), lambda bi, i: (bi, i, 0)),
                  pl.BlockSpec((1, l, GROUP_W), lambda bi, i: (bi, 0, 1)),
                  pl.BlockSpec((1, l, GROUP_W), lambda bi, i: (bi, 0, 2)),
                  pl.BlockSpec((1, GROUP_W), lambda bi, i: (0, 0))],
        out_specs=pl.BlockSpec((1, blk, GROUP_W), lambda bi, i: (bi, i, 0)),
        out_shape=jax.ShapeDtypeStruct((b, l, GROUP_W), F32),
        scratch_shapes=[pltpu.VMEM((blk, GROUP_W), F32), pltpu.VMEM((N_HEADS, blk, LANES), F32)],
        compiler_params=_params("parallel", "arbitrary"),
        name="sb_attn",
    )(proj, proj, proj, g.reshape(1, GROUP_W))


def _rglru_kernel(x_ref, gate_ref, cw_ref, cb_ref, wa_ref, ba_ref, wx_ref, bx_ref, lam_ref,
                  ng_ref, o_ref):
    l = x_ref.shape[1]
    w = x_ref.shape[2]
    tc = SCAN_TILE
    rows = _iota((tc, w), 0)
    cw = cw_ref[...]
    cb = cb_ref[...]
    neg_c_sp = -RG_C * _softplus(-lam_ref[...])

    def chunk(c, carry):
        h_prev, tail = carry
        s = pl.multiple_of(c * tc, tc)
        x = x_ref[0, pl.ds(s, tc), :]
        xc = _causal_conv(x, tail, cw, cb)
        xcb = xc.astype(BF16)
        r = _sigmoid(_dot(xcb, wa_ref[...]) + ba_ref[...])
        ig = _sigmoid(_dot(xcb, wx_ref[...]) + bx_ref[...])
        log_a = r * neg_c_sp
        a = jnp.exp(log_a)
        th = jnp.tanh(log_a)
        mult = jnp.sqrt(-2.0 * th / (1.0 - th))
        mult = jnp.where(rows + s == 0, 1.0, mult)
        u = mult * ig * xc
        d = 1
        while d < tc:
            keep = rows >= d
            a_s = jnp.where(keep, pltpu.roll(a, d, 0), 1.0)
            u_s = jnp.where(keep, pltpu.roll(u, d, 0), 0.0)
            u = a * u_s + u
            a = a * a_s
            d *= 2
        hs = u + a * h_prev
        gate = gate_ref[0, pl.ds(s, tc), :]
        gelu = 0.5 * gate * (1.0 + jnp.tanh(math.sqrt(2.0 / math.pi) * (gate + 0.044715 * gate * gate * gate)))
        o = hs * gelu
        ms = jnp.mean(o * o, axis=-1, keepdims=True)
        o_ref[0, pl.ds(s, tc), :] = o * lax.rsqrt(ms + EPS) * ng_ref[...]
        return hs[tc - 1:tc], x[tc - 8:tc]

    lax.fori_loop(0, l // tc, chunk, (jnp.zeros((1, w), F32), jnp.zeros((8, w), F32)))


def _block_diag(wb):
    n, d, _ = wb.shape
    out = jnp.zeros((n * d, n * d), wb.dtype)
    for i in range(n):
        out = lax.dynamic_update_slice(out, wb[i], (i * d, i * d))
    return out


def _rglru(proj, conv_w, conv_b, wa, ba, wx, bx, lam, ng):
    b, l, _ = proj.shape
    w = GROUP_W
    row = lambda v: v.reshape(1, w)
    full = lambda shape: pl.BlockSpec(shape, lambda bi: (0,) * len(shape))
    return pl.pallas_call(
        _rglru_kernel,
        grid=(b,),
        in_specs=[pl.BlockSpec((1, l, w), lambda bi: (bi, 0, 3)),
                  pl.BlockSpec((1, l, w), lambda bi: (bi, 0, 4)),
                  full((conv_w.shape[0], w)), full((1, w)), full((w, w)), full((1, w)),
                  full((w, w)), full((1, w)), full((1, w)), full((1, w))],
        out_specs=pl.BlockSpec((1, l, w), lambda bi: (bi, 0, 0)),
        out_shape=jax.ShapeDtypeStruct((b, l, w), F32),
        compiler_params=_params("parallel"),
        name="rglru",
    )(proj, proj, conv_w, row(conv_b), _block_diag(wa).astype(BF16), row(ba),
      _block_diag(wx).astype(BF16), row(bx), row(lam), row(ng))


def _hgrn2_kernel(q_ref, f_ref, i_ref, g_ref, lb_ref, ng_ref, o_ref,
                  qe_s, ke_s, kl_s, dl_s, o_s):
    l = q_ref.shape[1]
    w = q_ref.shape[2]
    tile = SCAN_TILE
    c = HG_CHUNK
    r_t = _iota((tile, tile), 0)
    c_t = _iota((tile, tile), 1)
    same = _div_pow2(r_t, c) == _div_pow2(c_t, c)
    cum_m = jnp.where(jnp.logical_and(same, c_t <= r_t), 1.0, 0.0)
    tot_m = jnp.where(same, 1.0, 0.0)
    lb = lb_ref[...]

    def prep(t, carry):
        s = pl.multiple_of(t * tile, tile)
        forget = lb + (1.0 - lb) * _sigmoid(f_ref[0, pl.ds(s, tile), :])
        log_f = jnp.log(forget)
        key = 1.0 - forget
        bcum = _dot_hi(cum_m, log_f)
        blast = _dot_hi(tot_m, log_f)
        q = _silu(q_ref[0, pl.ds(s, tile), :])
        qe_s[pl.ds(s, tile), :] = (q * jnp.exp(bcum)).astype(BF16)
        ke_s[pl.ds(s, tile), :] = (key * jnp.exp(-bcum)).astype(BF16)
        kl_s[pl.ds(s, tile), :] = key * jnp.exp(blast - bcum)
        dl_s[pl.ds(s, tile), :] = jnp.exp(blast)
        return carry

    lax.fori_loop(0, l // tile, prep, 0)

    tril = _iota((c, c), 1) <= _iota((c, c), 0)

    def step(n, state):
        s = pl.multiple_of(n * c, c)
        qe = qe_s[pl.ds(s, c), :]
        ke = ke_s[pl.ds(s, c), :]
        kl = kl_s[pl.ds(s, c), :]
        v = i_ref[0, pl.ds(s, c), :]
        dl = dl_s[pl.ds(s, 8), :][0:1]
        vb = v.astype(BF16)
        outs = []
        new_state = []
        for h in range(N_HEADS):
            sl = slice(h * HEAD_DIM, (h + 1) * HEAD_DIM)
            sc = jnp.where(tril, _dot_nt(qe[:, sl], ke[:, sl]), 0.0)
            st = state[h]
            o = _dot(sc.astype(BF16), vb[:, sl]) + _dot_nt(qe[:, sl], st.astype(BF16))
            new_state.append(dl[:, sl] * st + _dot_tn(v[:, sl], kl[:, sl]))
            outs.append(o)
        o_s[pl.ds(s, c), :] = jnp.concatenate(outs, axis=1)
        return tuple(new_state)

    zero = jnp.zeros((HEAD_DIM, HEAD_DIM), F32)
    lax.fori_loop(0, l // c, step, (zero,) * N_HEADS, unroll=HG_UNROLL)

    head_avg = jnp.where(_div_pow2(_iota((w, w), 0), HEAD_DIM) == _div_pow2(_iota((w, w), 1), HEAD_DIM),
                         1.0 / HEAD_DIM, 0.0)

    def finish(t, carry):
        s = pl.multiple_of(t * tile, tile)
        o = o_s[pl.ds(s, tile), :]
        ms = _dot_hi(o * o, head_avg)
        o_ref[0, pl.ds(s, tile), :] = (o * lax.rsqrt(ms + EPS) * ng_ref[...]
                                       * _silu(g_ref[0, pl.ds(s, tile), :]))
        return carry

    lax.fori_loop(0, l // tile, finish, 0)


def _hgrn2(proj, lb, ng):
    b, l, _ = proj.shape
    w = GROUP_W
    full = lambda shape: pl.BlockSpec(shape, lambda bi: (0,) * len(shape))
    col = lambda j: pl.BlockSpec((1, l, w), lambda bi: (bi, 0, j))
    return pl.pallas_call(
        _hgrn2_kernel,
        grid=(b,),
        in_specs=[col(5), col(6), col(7), col(8), full((1, w)), full((1, w))],
        out_specs=pl.BlockSpec((1, l, w), lambda bi: (bi, 0, 0)),
        out_shape=jax.ShapeDtypeStruct((b, l, w), F32),
        scratch_shapes=[pltpu.VMEM((l, w), BF16), pltpu.VMEM((l, w), BF16), pltpu.VMEM((l, w), F32),
                        pltpu.VMEM((l, w), F32), pltpu.VMEM((l, w), F32)],
        compiler_params=_params("parallel"),
        name="hgrn2",
    )(proj, proj, proj, proj, lb.reshape(1, w), ng.reshape(1, w))


def _ssd_kernel(z_ref, x_ref, bm_ref, cm_ref, dt_ref, cw_ref, cb_ref, dtb_ref, aneg_ref,
                dskip_ref, ng_ref, o_ref):
    l = z_ref.shape[1]
    w = GROUP_W
    q = SSD_CHUNK
    p = HEAD_DIM
    r_q = _iota((q, q), 0)
    c_q = _iota((q, q), 1)
    tril = c_q <= r_q
    cum_m = jnp.where(tril, 1.0, 0.0)
    expand = jnp.where(jnp.logical_and(_iota((w, w), 0) == _div_pow2(_iota((w, w), 1), p),
                                       _iota((w, w), 0) < N_HEADS), 1.0, 0.0)
    pick = jnp.where(_iota((8, w), 1) == _iota((8, w), 0) * p, 1.0, 0.0)
    cw = cw_ref[...]
    cb = cb_ref[...]

    def chunk(n, carry):
        state, tails = carry
        s = pl.multiple_of(n * q, q)
        parts = []
        new_tails = []
        for j, ref in enumerate((x_ref, bm_ref, cm_ref)):
            raw = ref[0, pl.ds(s, q), :]
            sl = slice(j * w, (j + 1) * w)
            parts.append(_silu(_causal_conv(raw, tails[j], cw[:, sl], cb[:, sl])))
            new_tails.append(raw[q - 8:q])
        xs, bm, cm = parts
        dt = _softplus(dt_ref[0, pl.ds(s, q), :] + dtb_ref[...])
        dt_rep = _dot_hi(dt, expand)
        da_rep = _dot_hi(dt * aneg_ref[...], expand)
        a_cs = _dot_hi(cum_m, da_rep)
        a_row = _dot_nt(pick, a_cs, precision=HIGHEST)
        xd = xs * dt_rep
        ys = []
        new_state = []
        for h in range(N_HEADS):
            grp = h // (N_HEADS // 2)
            gs = slice(grp * SSD_STATE, (grp + 1) * SSD_STATE)
            hs = slice(h * p, (h + 1) * p)
            cg = cm[:, gs].astype(BF16)
            bg = bm[:, gs].astype(BF16)
            cb_qq = _dot_nt(cg, bg)
            a_col = a_cs[:, h * p:h * p + 1]
            decay = jnp.exp(jnp.where(tril, a_col - a_row[h:h + 1, :], -jnp.inf))
            y = _dot((cb_qq * decay).astype(BF16), xd[:, hs].astype(BF16))
            st = state[h]
            y = y + jnp.exp(a_cs[:, hs]) * _dot_nt(cg, st.astype(BF16))
            a_last = a_cs[q - 1:q, hs]
            to_end = jnp.exp(a_last - a_cs[:, hs])
            upd = _dot_tn(xd[:, hs] * to_end, bm[:, gs])
            chunk_decay = jnp.exp(a_row[h:h + 1, q - 1:q])
            new_state.append(chunk_decay * st + upd)
            ys.append(y)
        y = jnp.concatenate(ys, axis=1) + xs * dskip_ref[...]
        y = y * _silu(z_ref[0, pl.ds(s, q), :])
        halves = []
        for gi in range(2):
            yg = y[:, gi * LANES:(gi + 1) * LANES]
            ms = jnp.mean(yg * yg, axis=-1, keepdims=True)
            halves.append(yg * lax.rsqrt(ms + EPS))
        o_ref[0, pl.ds(s, q), :] = jnp.concatenate(halves, axis=1) * ng_ref[...]
        return tuple(new_state), tuple(new_tails)

    zero = jnp.zeros((p, SSD_STATE), F32)
    tail0 = jnp.zeros((8, w), F32)
    lax.fori_loop(0, l // q, chunk, ((zero,) * N_HEADS, (tail0,) * 3), unroll=2)


def _ssd(proj, conv_w, conv_b, dt_bias, a_log, d_skip, ng):
    b, l, _ = proj.shape
    w = GROUP_W
    full = lambda shape: pl.BlockSpec(shape, lambda bi: (0,) * len(shape))
    col = lambda j: pl.BlockSpec((1, l, w), lambda bi: (bi, 0, j))
    pad_heads = lambda v: jnp.zeros((1, w), F32).at[0, :N_HEADS].set(v)
    return pl.pallas_call(
        _ssd_kernel,
        grid=(b,),
        in_specs=[col(9), col(10), col(11), col(12), col(13),
                  full((conv_w.shape[0], 3 * w)), full((1, 3 * w)), full((1, w)), full((1, w)),
                  full((1, w)), full((1, w))],
        out_specs=pl.BlockSpec((1, l, w), lambda bi: (bi, 0, 0)),
        out_shape=jax.ShapeDtypeStruct((b, l, w), F32),
        compiler_params=_params("parallel"),
        name="ssd",
    )(proj, proj, proj, proj, proj, conv_w, conv_b.reshape(1, 3 * w), pad_heads(dt_bias),
      pad_heads(-jnp.exp(a_log)), jnp.repeat(d_skip, HEAD_DIM).reshape(1, w), ng.reshape(1, w))


def _outproj_router_kernel(h_ref, oa_ref, ob_ref, oc_ref, od_ref, wo_ref, g_ref, rwh_ref, rwl_ref, rb_ref,
                           h_out, xn_out, sel_out, gate_out, rank_out, cnt_out, tile_out, cnt_s):
    i = pl.program_id(0)
    tm = h_ref.shape[0]

    @pl.when(i == 0)
    def _():
        cnt_s[...] = jnp.zeros_like(cnt_s)

    acc = h_ref[...]
    for j, ref in enumerate((oa_ref, ob_ref, oc_ref, od_ref)):
        acc = acc + _dot(ref[...].astype(BF16), wo_ref[j * GROUP_W:(j + 1) * GROUP_W, :])
    h_out[...] = acc
    ms = jnp.mean(acc * acc, axis=-1, keepdims=True)
    xn = acc * lax.rsqrt(ms + EPS) * g_ref[...]
    xn_out[...] = xn
    lane = _iota((tm, LANES), 1)
    lane_f = lane.astype(F32)
    xn_hi = xn.astype(BF16)
    xn_lo = (xn - xn_hi.astype(F32)).astype(BF16)
    router = _dot(xn_hi, rwh_ref[...]) + _dot(xn_lo, rwh_ref[...]) + _dot(xn_hi, rwl_ref[...])
    logits = jnp.where(lane < N_EXPERTS, router + rb_ref[...], -jnp.inf)
    sel = jnp.zeros((tm, LANES), F32)
    tops = []
    for k in range(TOP_K):
        m = jnp.max(logits, axis=-1, keepdims=True)
        idx = jnp.min(jnp.where(logits == m, lane_f, float(LANES)), axis=-1, keepdims=True)
        hit = lane_f == idx
        sel = jnp.where(hit, float(k + 1), sel)
        logits = jnp.where(hit, -jnp.inf, logits)
        tops.append(m)
    exps = [jnp.exp(m - tops[0]) for m in tops]
    denom = exps[0] + exps[1] + exps[2] + exps[3]
    gate = jnp.zeros((tm, LANES), F32)
    for k in range(TOP_K):
        gate = jnp.where(sel == float(k + 1), exps[k] / denom, gate)
    sel_out[...] = sel
    gate_out[...] = gate
    member = jnp.where(sel > 0.0, 1.0, 0.0)
    before = jnp.where(_iota((tm, tm), 1) < _iota((tm, tm), 0), 1.0, 0.0).astype(BF16)
    carry = cnt_s[0:1, :]
    rank_out[...] = _dot(before, member.astype(BF16))
    in_tile = jnp.ceil(jnp.sum(member, axis=0, keepdims=True) * (1.0 / RUN_ALIGN)) * RUN_ALIGN
    total = carry + in_tile
    tile_out[...] = jnp.concatenate([carry, in_tile, jnp.zeros((6, LANES), F32)], axis=0)
    cnt_s[...] = jnp.broadcast_to(total, cnt_s.shape)
    cnt_out[...] = jnp.broadcast_to(total, cnt_out.shape)


def _outproj_router(h, outs, w_out, g, router_w, router_b):
    t, d = h.shape
    tm = ROW_TILE
    rw = jnp.zeros((d, LANES), F32).at[:, :N_EXPERTS].set(router_w)
    rb = jnp.zeros((1, LANES), F32).at[0, :N_EXPERTS].set(router_b)
    rw_hi = rw.astype(BF16)
    rw_lo = (rw - rw_hi.astype(F32)).astype(BF16)
    tile = lambda width: pl.BlockSpec((tm, width), lambda i: (i, 0))
    full = lambda shape: pl.BlockSpec(shape, lambda i: (0,) * len(shape))
    flat = [o.reshape(t, GROUP_W) for o in outs]
    return pl.pallas_call(
        _outproj_router_kernel,
        grid=(t // tm,),
        in_specs=[tile(d)] + [tile(GROUP_W)] * 4 + [full((d, d)), full((1, d)), full((d, LANES)),
                                                   full((d, LANES)), full((1, LANES))],
        out_specs=[tile(d), tile(d), tile(LANES), tile(LANES), tile(LANES), full((8, LANES)),
                   pl.BlockSpec((8, LANES), lambda i: (i, 0))],
        out_shape=[jax.ShapeDtypeStruct((t, d), F32), jax.ShapeDtypeStruct((t, d), F32),
                   jax.ShapeDtypeStruct((t, LANES), F32), jax.ShapeDtypeStruct((t, LANES), F32),
                   jax.ShapeDtypeStruct((t, LANES), F32), jax.ShapeDtypeStruct((8, LANES), F32),
                   jax.ShapeDtypeStruct((t // tm * 8, LANES), F32)],
        scratch_shapes=[pltpu.VMEM((8, LANES), F32)],
        compiler_params=_params("arbitrary"),
        name="outproj_router",
    )(h, *flat, w_out.astype(BF16), g.reshape(1, d), rw_hi, rw_lo, rb)


def _plan_kernel(cnt_ref, start_out, plan_out):
    cnt = cnt_ref[...]
    padded = jnp.ceil(cnt * (1.0 / MOE_BLOCK)) * MOE_BLOCK
    upto = jnp.where(_iota((LANES, LANES), 0) <= _iota((LANES, LANES), 1), 1.0, 0.0)
    pad_end = _dot_hi(padded, upto)
    pad_start = pad_end - padded
    start_out[...] = pad_start
    ends = jnp.transpose(jnp.broadcast_to(pad_end[0:1, :], (LANES, LANES)))
    expert_ok = _iota((LANES, LANES), 0) < N_EXPERTS
    n_active = pad_end[0:1, N_EXPERTS - 1:N_EXPERTS] * (1.0 / MOE_BLOCK)
    rows = [jnp.broadcast_to(n_active, (1, LANES)), cnt[0:1, :], pad_start[0:1, :], padded[0:1, :]]
    for r in range(PLAN_EXPERT_ROWS):
        blk_start = ((_iota((LANES, LANES), 1) + r * LANES) * MOE_BLOCK).astype(F32)
        done = jnp.where(jnp.logical_and(expert_ok, ends <= blk_start), 1.0, 0.0)
        rows.append(jnp.minimum(jnp.sum(done, axis=0, keepdims=True), N_EXPERTS - 1.0))
    plan_out[...] = jnp.concatenate(rows, axis=0).astype(I32)


def _plan(counts, n_blocks):
    assert n_blocks <= PLAN_EXPERT_ROWS * LANES
    return pl.pallas_call(
        _plan_kernel,
        out_shape=[jax.ShapeDtypeStruct((8, LANES), F32), jax.ShapeDtypeStruct((8, LANES), I32)],
        name="moe_plan",
    )(counts)


def _dest_kernel(sel_ref, gate_ref, rank_ref, tile_ref, start_ref, lpos_out, lk_out, gk_out, runs_out):
    tm = sel_ref.shape[0]
    sel = sel_ref[...]
    gate = gate_ref[...]
    before_tile = tile_ref[0:1, :]
    in_tile = tile_ref[1:2, :]
    earlier = jnp.where(_iota((LANES, LANES), 0) < _iota((LANES, LANES), 1), 1.0, 0.0)
    run_start = _dot_hi(jnp.broadcast_to(in_tile, (8, LANES)), earlier)[0:1, :]
    lpos = rank_ref[...] + run_start
    lane = _iota((tm, LANES), 1)
    pieces = []
    gk = jnp.zeros((tm, LANES), F32)
    lk = jnp.zeros((tm, LANES), F32)
    for k in range(TOP_K):
        hit = sel == float(k + 1)
        mine = jnp.where(hit, lpos, 0.0)
        pieces.append(mine)
        gk = jnp.where(lane == k, jnp.sum(jnp.where(hit, gate, 0.0), axis=-1, keepdims=True), gk)
        lk = jnp.where(lane == k, jnp.sum(mine, axis=-1, keepdims=True), lk)
    stacked = jnp.concatenate(pieces, axis=1)
    picker = jnp.where(_div_pow2(_iota((8, TOP_K * LANES), 1), LANES) == _iota((8, TOP_K * LANES), 0),
                       1.0, 0.0)
    lpos_out[...] = _dot_nt(picker, stacked, precision=HIGHEST).astype(I32)
    lk_out[...] = lk
    gk_out[...] = gk
    first_slot = start_ref[0:1, :] + before_tile
    total = jnp.broadcast_to(jnp.sum(in_tile, axis=-1, keepdims=True), (1, LANES))
    runs_out[...] = jnp.concatenate([in_tile, run_start, first_slot, total, jnp.zeros((4, LANES), F32)],
                                    axis=0).astype(I32)


def _dest(sel, gate, rank, tile_counts, pad_start):
    t = sel.shape[0]
    tm = ROW_TILE
    tile = pl.BlockSpec((tm, LANES), lambda i: (i, 0))
    per_tile = pl.BlockSpec((8, LANES), lambda i: (i, 0))
    return pl.pallas_call(
        _dest_kernel,
        grid=(t // tm,),
        in_specs=[tile, tile, tile, per_tile, pl.BlockSpec((8, LANES), lambda i: (0, 0))],
        out_specs=[pl.BlockSpec((8, tm), lambda i: (0, i)), tile, tile, per_tile],
        out_shape=[jax.ShapeDtypeStruct((8, t), I32), jax.ShapeDtypeStruct((t, LANES), F32),
                   jax.ShapeDtypeStruct((t, LANES), F32), jax.ShapeDtypeStruct((t // tm * 8, LANES), I32)],
        compiler_params=_params("parallel"),
        name="moe_dest",
    )(sel, gate, rank, tile_counts, pad_start)


def _copy_run(src, dst, length, n_bits, make_copy, wait):
    units = jnp.right_shift(length, RUN_ALIGN.bit_length() - 1)
    for b in range(n_bits):
        offset = jnp.left_shift(jnp.right_shift(units, b + 1), b + 1) * RUN_ALIGN

        @pl.when(jnp.bitwise_and(jnp.right_shift(units, b), 1) == 1)
        def _():
            copy = make_copy(pl.multiple_of(src + offset, RUN_ALIGN), pl.multiple_of(dst + offset, RUN_ALIGN),
                             RUN_ALIGN << b)
            if wait:
                copy.wait()
            else:
                copy.start()


def _dispatch_kernel(plan_ref, runs_ref, prev_runs_ref, lpos_ref, xn_ref, xs_hbm, sorted_s, sem):
    i = pl.program_id(0)
    tm = xn_ref.shape[0]
    n_local = sorted_s.shape[1]
    slot = i % 2

    def to_slots(buf):
        return lambda src, dst, size: pltpu.make_async_copy(
            sorted_s.at[buf, pl.ds(src, size)], xs_hbm.at[pl.ds(dst, size)], sem.at[buf])

    def start_runs(table, buf):
        def body(e, carry):
            _copy_run(table[RUN_START, e], table[RUN_SLOT, e], table[RUN_LEN, e], RUN_BITS, to_slots(buf),
                      False)
            return carry
        lax.fori_loop(0, N_EXPERTS, body, 0)

    def wait_runs(table, buf):
        _copy_run(0, 0, table[RUN_TOTAL, 0], TOTAL_BITS, lambda src, dst, size: to_slots(buf)(0, 0, size),
                  True)

    position = _iota((n_local, tm), 0)
    lpos = lpos_ref[...]
    place = jnp.zeros((n_local, tm), F32)
    for k in range(TOP_K):
        place = place + jnp.where(position == lpos[k:k + 1, :], 1.0, 0.0)
    sorted_s[slot] = _dot(place.astype(BF16), xn_ref[...].astype(BF16))
    start_runs(runs_ref, slot)

    @pl.when(i > 0)
    def _():
        wait_runs(prev_runs_ref, 1 - slot)

    @pl.when(i == pl.num_programs(0) - 1)
    def _():
        wait_runs(runs_ref, slot)

    @pl.when(i == 0)
    def _():
        def padding(wait):
            def fill(src, dst, size):
                return pltpu.make_async_copy(sorted_s.at[0, pl.ds(0, size)], xs_hbm.at[pl.ds(dst, size)],
                                             sem.at[1])

            def body(e, carry):
                cnt = plan_ref[PLAN_COUNT, e]
                _copy_run(0, plan_ref[PLAN_START, e] + cnt, plan_ref[PLAN_PADDED, e] - cnt, PAD_BITS, fill,
                          wait)
                return carry
            lax.fori_loop(0, N_EXPERTS, body, 0)

        padding(False)
        padding(True)


def _dispatch(plan, runs, lpos, xn, n_slots):
    t, d = xn.shape
    tm = ROW_TILE
    return pl.pallas_call(
        _dispatch_kernel,
        grid=(t // tm,),
        in_specs=[pl.BlockSpec(memory_space=pltpu.SMEM),
                  pl.BlockSpec((8, LANES), lambda i: (i, 0), memory_space=pltpu.SMEM),
                  pl.BlockSpec((8, LANES), lambda i: (jnp.maximum(i - 1, 0), 0), memory_space=pltpu.SMEM),
                  pl.BlockSpec((8, tm), lambda i: (0, i)),
                  pl.BlockSpec((tm, d), lambda i: (i, 0))],
        out_specs=pl.BlockSpec(memory_space=pl.ANY),
        out_shape=jax.ShapeDtypeStruct((n_slots, d), F32),
        scratch_shapes=[pltpu.VMEM((2, LOCAL_ROWS, d), F32), pltpu.SemaphoreType.DMA((2,))],
        compiler_params=_params("arbitrary"),
        name="moe_dispatch",
    )(plan, runs, runs, lpos, xn)


def _ffn_kernel(be_ref, na_ref, plan_ref, x_ref, wgu_ref, bgu_ref, wd_ref, bd_ref, o_ref, wgu_s, wd32_s,
                wd_s):
    j = pl.program_id(0)
    active = j < na_ref[0]
    expert = be_ref[j]
    new_expert = jnp.logical_or(j == 0, expert != be_ref[jnp.maximum(j - 1, 0)])
    rows_left = plan_ref[PLAN_START, expert] + plan_ref[PLAN_COUNT, expert] - j * MOE_BLOCK

    @pl.when(jnp.logical_and(active, new_expert))
    def _():
        wgu_s[...] = wgu_ref[0].astype(BF16)
        half = LANES // 2
        for c in range(wd_s.shape[0]):
            cols = slice(c * LANES, (c + 1) * LANES)
            for k in range(wd_s.shape[1] // LANES):
                for hb in range(2):
                    src = k * LANES + hb * half
                    wd32_s[c, pl.ds(k * LANES + hb, half, stride=2), :] = wd_ref[0, src:src + half, cols]
            wd_s[c] = wd32_s[c].astype(BF16)

    def ffn_rows(n_rows):
        x = x_ref[0:n_rows, :].astype(BF16)
        hmid = _dot(x, wgu_s[...]) + bgu_ref[0]
        m = hmid.shape[0]
        even = (_iota((m, LANES), 1) & 1) == 0
        acts = []
        for k in range(hmid.shape[1] // (2 * LANES)):
            h0 = hmid[:, (2 * k) * LANES:(2 * k + 1) * LANES]
            h1 = hmid[:, (2 * k + 1) * LANES:(2 * k + 2) * LANES]
            glu = jnp.where(even, h0, pltpu.roll(h1, 1, 1))
            lin = jnp.where(even, pltpu.roll(h0, LANES - 1, 1), h1)
            glu = jnp.minimum(glu, SWIGLU_LIMIT)
            lin = jnp.clip(lin, -SWIGLU_LIMIT, SWIGLU_LIMIT)
            acts.append((glu * _sigmoid(SWIGLU_ALPHA * glu) * (lin + 1.0)).astype(BF16))
        act = jnp.concatenate(acts, axis=1)
        wd = jnp.concatenate([wd_s[c] for c in range(wd_s.shape[0])], axis=1)
        o_ref[0:n_rows, :] = _dot(act, wd) + bd_ref[0]

    half_block = MOE_BLOCK // 2

    @pl.when(jnp.logical_and(active, rows_left > half_block))
    def _():
        ffn_rows(MOE_BLOCK)

    @pl.when(jnp.logical_and(active, rows_left <= half_block))
    def _():
        ffn_rows(half_block)


def _ffn(layer, block_expert, n_active, plan, xs, wgu, bgu, wd, bd):
    n_slots, d = xs.shape
    dff2 = wgu.shape[3]
    nb = n_slots // MOE_BLOCK
    rows = lambda j, be, na, plan_: (jnp.maximum(jnp.minimum(j, na[0] - 1), 0), 0)
    per_expert = lambda j, be, na, plan_: (layer, be[j], 0, 0)
    grid_spec = pltpu.PrefetchScalarGridSpec(
        num_scalar_prefetch=3,
        grid=(nb,),
        in_specs=[pl.BlockSpec((MOE_BLOCK, d), rows),
                  pl.BlockSpec((None, 1, d, dff2), per_expert), pl.BlockSpec((None, 1, 1, dff2), per_expert),
                  pl.BlockSpec((None, 1, dff2 // 2, d), per_expert), pl.BlockSpec((None, 1, 1, d), per_expert)],
        out_specs=pl.BlockSpec((MOE_BLOCK, d), rows),
        scratch_shapes=[pltpu.VMEM((d, dff2), BF16), pltpu.VMEM((d // LANES, dff2 // 2, LANES), F32),
                        pltpu.VMEM((d // LANES, dff2 // 2, LANES), BF16)],
    )
    return pl.pallas_call(
        _ffn_kernel,
        grid_spec=grid_spec,
        out_shape=jax.ShapeDtypeStruct((n_slots, d), F32),
        compiler_params=_params("arbitrary"),
        name="moe_ffn",
    )(block_expert, n_active, plan, xs, wgu, bgu, wd, bd)


def _combine_kernel(runs_ref, next_runs_ref, lk_ref, gk_ref, h_ref, g_ref, ys_hbm, o_ref, sorted_s, sem, *,
                    final_norm):
    i = pl.program_id(0)
    tm = h_ref.shape[0]
    n_local = sorted_s.shape[1]
    slot = i % 2

    def from_slots(buf):
        return lambda src, dst, size: pltpu.make_async_copy(
            ys_hbm.at[pl.ds(src, size)], sorted_s.at[buf, pl.ds(dst, size)], sem.at[buf])

    def start_runs(table, buf):
        def body(e, carry):
            _copy_run(table[RUN_SLOT, e], table[RUN_START, e], table[RUN_LEN, e], RUN_BITS, from_slots(buf),
                      False)
            return carry
        lax.fori_loop(0, N_EXPERTS, body, 0)

    @pl.when(i == 0)
    def _():
        sorted_s[...] = jnp.zeros_like(sorted_s)
        start_runs(runs_ref, slot)

    @pl.when(i + 1 < pl.num_programs(0))
    def _():
        start_runs(next_runs_ref, 1 - slot)

    _copy_run(0, 0, runs_ref[RUN_TOTAL, 0], TOTAL_BITS, lambda src, dst, size: from_slots(slot)(0, 0, size),
              True)
    position = _iota((tm, n_local), 1).astype(F32)
    lk = lk_ref[...]
    gk = gk_ref[...]
    weights = jnp.zeros((tm, n_local), F32)
    for k in range(TOP_K):
        weights = weights + jnp.where(position == lk[:, k:k + 1], gk[:, k:k + 1], 0.0)
    acc = h_ref[...] + _dot(weights.astype(BF16), sorted_s[slot].astype(BF16))
    if final_norm:
        ms = jnp.mean(acc * acc, axis=-1, keepdims=True)
        acc = acc * lax.rsqrt(ms + EPS) * g_ref[...]
    o_ref[...] = acc


def _combine(runs, lk, gk, h, ys, g, final_norm):
    t, d = h.shape
    tm = ROW_TILE
    return pl.pallas_call(
        functools.partial(_combine_kernel, final_norm=final_norm),
        grid=(t // tm,),
        in_specs=[pl.BlockSpec((8, LANES), lambda i: (i, 0), memory_space=pltpu.SMEM),
                  pl.BlockSpec((8, LANES), lambda i: (jnp.minimum(i + 1, t // tm - 1), 0),
                               memory_space=pltpu.SMEM),
                  pl.BlockSpec((tm, LANES), lambda i: (i, 0)),
                  pl.BlockSpec((tm, LANES), lambda i: (i, 0)),
                  pl.BlockSpec((tm, d), lambda i: (i, 0)),
                  pl.BlockSpec((1, d), lambda i: (0, 0)),
                  pl.BlockSpec(memory_space=pl.ANY)],
        out_specs=pl.BlockSpec((tm, d), lambda i: (i, 0)),
        out_shape=jax.ShapeDtypeStruct((t, d), F32),
        scratch_shapes=[pltpu.VMEM((2, LOCAL_ROWS, d), F32), pltpu.SemaphoreType.DMA((2,))],
        compiler_params=_params("arbitrary"),
        name="moe_combine",
    )(runs, runs, lk, gk, h, g.reshape(1, d), ys)


def _moe(h, xn, sel, gate, rank, counts, tile_counts, layer, w_gu, b_gu, w_down, b_down, g_final,
         final_norm):
    t, d = h.shape
    n_slots = t * TOP_K + (t // ROW_TILE) * N_EXPERTS * RUN_ALIGN + N_EXPERTS * MOE_BLOCK
    n_blocks = n_slots // MOE_BLOCK
    pad_start, plan = _plan(counts, n_blocks)
    lpos, lk, gk, runs = _dest(sel, gate, rank, tile_counts, pad_start)
    xs = _dispatch(plan, runs, lpos, xn, n_slots)
    block_expert = plan[PLAN_EXPERTS:].reshape(-1)[:n_blocks]
    n_active = plan[PLAN_ACTIVE, :1]
    depth = w_gu.shape[0]
    ys = _ffn(layer, block_expert, n_active, plan, xs, w_gu, b_gu.reshape(depth, N_EXPERTS, 1, -1), w_down,
              b_down.reshape(depth, N_EXPERTS, 1, d))
    return _combine(runs, lk, gk, h, ys, g_final, final_norm)


def kernel(x, norm_mix_g, w_in, sb_norm_g, rg_conv_w, rg_conv_b, rg_wa, rg_ba, rg_wx, rg_bx,
           rg_lambda, rg_norm_g, hg_lower_bounds, hg_norm_g, m2_conv_w, m2_conv_b, m2_dt_bias,
           m2_a_log, m2_d, m2_norm_g, w_out, norm_ffn_g, router_w, router_b, moe_w_gu, moe_b_gu,
           moe_w_down, moe_b_down, final_norm_g):
    b, l, d = x.shape
    depth = w_in.shape[0]
    t = b * l
    lbs = jnp.cumsum(jax.nn.softmax(hg_lower_bounds.astype(F32), axis=0), axis=0)
    lbs = lbs - lbs[0]
    d_in = w_in.shape[2]
    w_in_p = jnp.zeros((depth, d, PROJ_BLOCKS * GROUP_W), BF16).at[:, :, :d_in].set(w_in.astype(BF16))
    h = x.reshape(t, d)
    for layer in range(depth):
        proj = _norm_inproj(h, norm_mix_g[layer], w_in_p[layer]).reshape(b, l, PROJ_BLOCKS * GROUP_W)
        o_a = _sb_attn(proj, sb_norm_g[layer])
        o_b = _rglru(proj, rg_conv_w[layer], rg_conv_b[layer], rg_wa[layer], rg_ba[layer],
                     rg_wx[layer], rg_bx[layer], rg_lambda[layer], rg_norm_g[layer])
        o_c = _hgrn2(proj, lbs[layer], hg_norm_g[layer])
        o_d = _ssd(proj, m2_conv_w[layer], m2_conv_b[layer], m2_dt_bias[layer], m2_a_log[layer],
                   m2_d[layer], m2_norm_g[layer])
        h, xn, sel, gate, rank, counts, tile_counts = _outproj_router(
            h, (o_a, o_b, o_c, o_d), w_out[layer], norm_ffn_g[layer], router_w[layer],
            router_b[layer])
        h = _moe(h, xn, sel, gate, rank, counts, tile_counts, layer, moe_w_gu, moe_b_gu, moe_w_down,
                 moe_b_down, final_norm_g, layer == depth - 1)
    return h.reshape(b, l, d)
```

```python
import functools
import math

import jax
import jax.numpy as jnp
from jax import lax
from jax.experimental import pallas as pl
from jax.experimental.pallas import tpu as pltpu

F32 = jnp.float32
BF16 = jnp.bfloat16
I32 = jnp.int32
HIGHEST = lax.Precision.HIGHEST
EPS = 1e-6
EXP_UNDERFLOW = -104.0

LANES = 128
GROUP_W = 256
HEAD_DIM = 64
N_HEADS = 4
SSD_STATE = 128
RG_C = 8.0
N_EXPERTS = 32
TOP_K = 4
SWIGLU_LIMIT = 7.0
SWIGLU_ALPHA = 1.702
PROJ_BLOCKS = 14
VMEM_LIMIT = 56 * 1024 * 1024

ROW_TILE = 256
INPROJ_TILE = 512
ATTN_BLOCK = 256
SCAN_TILE = 256
HG_CHUNK = 16
HG_BLOCK = 128
SSD_CHUNK = 128
MOE_BLOCK = 512
HG_UNROLL = 8

PLAN_ACTIVE, PLAN_COUNT, PLAN_START, PLAN_PADDED, PLAN_EXPERTS = 0, 1, 2, 3, 4
PLAN_EXPERT_ROWS = 8 - PLAN_EXPERTS
RUN_LEN, RUN_START, RUN_SLOT, RUN_TOTAL = 0, 1, 2, 3
RUN_ALIGN = 8
RUN_BITS = (ROW_TILE // RUN_ALIGN).bit_length()
PAD_BITS = (MOE_BLOCK // RUN_ALIGN - 1).bit_length()
LOCAL_ROWS = ROW_TILE * TOP_K + N_EXPERTS * RUN_ALIGN
TOTAL_BITS = (LOCAL_ROWS // RUN_ALIGN).bit_length()


def _params(*sem):
    return pltpu.CompilerParams(dimension_semantics=sem, vmem_limit_bytes=VMEM_LIMIT)


def _dot(a, b):
    return jnp.dot(a, b, preferred_element_type=F32)


def _dot_hi(a, b):
    return jnp.dot(a, b, preferred_element_type=F32, precision=HIGHEST)


def _dot_nt(a, b, precision=None):
    return lax.dot_general(a, b, (((1,), (1,)), ((), ())), preferred_element_type=F32,
                           precision=precision)


def _dot_tn(a, b):
    return lax.dot_general(a, b, (((0,), (0,)), ((), ())), preferred_element_type=F32)


def _sigmoid(x):
    return 1.0 / (1.0 + jnp.exp(-x))


def _silu(x):
    return x * _sigmoid(x)


def _softplus(x):
    return jnp.maximum(x, 0.0) + jnp.log(1.0 + jnp.exp(-jnp.abs(x)))


def _iota(shape, dim):
    return lax.broadcasted_iota(I32, shape, dim)


def _div_pow2(x, n):
    assert n & (n - 1) == 0
    return jnp.right_shift(x, int(math.log2(n)))


def _shift_rows(x, d, prev8):
    r = pltpu.roll(x, d, 0)
    p = pltpu.roll(prev8, d, 0)
    head = jnp.where(_iota(prev8.shape, 0) < d, p, r[:8])
    return jnp.concatenate([head, r[8:]], axis=0)


def _causal_conv(x, prev8, w, b):
    k = w.shape[0]
    y = x * w[k - 1:k] + b
    for d in range(1, k):
        y = y + _shift_rows(x, d, prev8) * w[k - 1 - d:k - d]
    return y


def _norm_inproj_kernel(x_ref, g_ref, w_ref, o_ref):
    x = x_ref[...]
    ms = jnp.mean(x * x, axis=-1, keepdims=True)
    xn = (x * lax.rsqrt(ms + EPS) * g_ref[...]).astype(BF16)
    o_ref[...] = _dot(xn, w_ref[...])


def _norm_inproj(h, g, w):
    t, d = h.shape
    n = w.shape[1]
    return pl.pallas_call(
        _norm_inproj_kernel,
        grid=(t // INPROJ_TILE,),
        in_specs=[pl.BlockSpec((INPROJ_TILE, d), lambda i: (i, 0)),
                  pl.BlockSpec((1, d), lambda i: (0, 0)),
                  pl.BlockSpec((d, n), lambda i: (0, 0))],
        out_specs=pl.BlockSpec((INPROJ_TILE, n), lambda i: (i, 0)),
        out_shape=jax.ShapeDtypeStruct((t, n), F32),
        compiler_params=_params("parallel"),
        name="norm_inproj",
    )(h, g.reshape(1, d), w)


def _sb_attn_kernel(q_ref, k_ref, v_ref, g_ref, o_ref, acc_ref, run_ref):
    i = pl.program_id(1)
    blk = ATTN_BLOCK
    scale = HEAD_DIM ** -0.5
    row = _iota((blk, blk), 0)
    col = _iota((blk, blk), 1)
    below = col < row
    r_w = _iota((blk, blk + LANES), 0)
    c_w = _iota((blk, blk + LANES), 1)
    later_sum = jnp.where(jnp.logical_or(c_w >= blk, r_w > c_w), 1.0, 0.0).astype(BF16)
    acc_ref[...] = jnp.zeros_like(acc_ref)
    run_ref[...] = jnp.zeros_like(run_ref)
    qs = [(q_ref[0, :, h * HEAD_DIM:(h + 1) * HEAD_DIM] * scale).astype(BF16) for h in range(N_HEADS)]

    def key_block(j, diagonal):
        ks = pl.multiple_of(j * blk, blk)
        for h in range(N_HEADS):
            sl = slice(h * HEAD_DIM, (h + 1) * HEAD_DIM)
            kh = k_ref[0, pl.ds(ks, blk), sl].astype(BF16)
            vh = v_ref[0, pl.ds(ks, blk), sl].astype(BF16)
            z = _dot_nt(qs[h], kh)
            ls = jnp.minimum(z, 0.0) - jnp.log(1.0 + jnp.exp(-jnp.abs(z)))
            lk = ls - z
            if diagonal:
                lk = jnp.where(below, lk, 0.0)
            lk_hi = lk.astype(BF16)
            lk_lo = (lk - lk_hi.astype(F32)).astype(BF16)
            sums = _dot(lk_hi, later_sum) + _dot(lk_lo, later_sum)
            run = run_ref[h]
            rem = sums[:, :blk] + jnp.concatenate([run] * (blk // LANES), axis=1)
            a = jnp.exp(ls + rem)
            if diagonal:
                a = jnp.where(below, a, 0.0)
            acc_ref[:, sl] += _dot(a.astype(BF16), vh)
            run_ref[h] = run + sums[:, blk:]

    key_block(i, True)

    def live():
        return jnp.max(run_ref[...]) > EXP_UNDERFLOW

    def more(carry):
        jj, alive = carry
        return jnp.logical_and(jj <= i, alive)

    def body(carry):
        jj, _ = carry
        key_block(i - jj, False)
        return jj + 1, live()

    lax.while_loop(more, body, (jnp.int32(1), live()))
    o = acc_ref[...]
    ms = jnp.mean(o * o, axis=-1, keepdims=True)
    o_ref[0] = o * lax.rsqrt(ms + EPS) * g_ref[...]


def _sb_attn(proj, g):
    b, l, _ = proj.shape
    blk = ATTN_BLOCK
    return pl.pallas_call(
        _sb_attn_kernel,
        grid=(b, l // blk),
        in_specs=[pl.BlockSpec((1, blk, GROUP_W), lambda bi, i: (bi, i, 0)),
                  pl.BlockSpec((1, l, GROUP_W), lambda bi, i: (bi, 0, 1)),
                  pl.BlockSpec((1, l, GROUP_W), lambda bi, i: (bi, 0, 2)),
                  pl.BlockSpec((1, GROUP_W), lambda bi, i: (0, 0))],
        out_specs=pl.BlockSpec((1, blk, GROUP_W), lambda bi, i: (bi, i, 0)),
        out_shape=jax.ShapeDtypeStruct((b, l, GROUP_W), F32),
        scratch_shapes=[pltpu.VMEM((blk, GROUP_W), F32), pltpu.VMEM((N_HEADS, blk, LANES), F32)],
        compiler_params=_params("parallel", "arbitrary"),
        name="sb_attn",
    )(proj, proj, proj, g.reshape(1, GROUP_W))


def _rglru_kernel(x_ref, gate_ref, cw_ref, cb_ref, wa_ref, ba_ref, wx_ref, bx_ref, lam_ref,
                  ng_ref, o_ref):
    l = x_ref.shape[1]
    w = x_ref.shape[2]
    tc = SCAN_TILE
    rows = _iota((tc, w), 0)
    cw = cw_ref[...]
    cb = cb_ref[...]
    neg_c_sp = -RG_C * _softplus(-lam_ref[...])

    def chunk(c, carry):
        h_prev, tail = carry
        s = pl.multiple_of(c * tc, tc)
        x = x_ref[0, pl.ds(s, tc), :]
        xc = _causal_conv(x, tail, cw, cb)
        xcb = xc.astype(BF16)
        r = _sigmoid(_dot(xcb, wa_ref[...]) + ba_ref[...])
        ig = _sigmoid(_dot(xcb, wx_ref[...]) + bx_ref[...])
        log_a = r * neg_c_sp
        a = jnp.exp(log_a)
        th = jnp.tanh(log_a)
        mult = jnp.sqrt(-2.0 * th / (1.0 - th))
        mult = jnp.where(rows + s == 0, 1.0, mult)
        u = mult * ig * xc
        d = 1
        while d < tc:
            keep = rows >= d
            a_s = jnp.where(keep, pltpu.roll(a, d, 0), 1.0)
            u_s = jnp.where(keep, pltpu.roll(u, d, 0), 0.0)
            u = a * u_s + u
            a = a * a_s
            d *= 2
        hs = u + a * h_prev
        gate = gate_ref[0, pl.ds(s, tc), :]
        gelu = 0.5 * gate * (1.0 + jnp.tanh(math.sqrt(2.0 / math.pi) * (gate + 0.044715 * gate * gate * gate)))
        o = hs * gelu
        ms = jnp.mean(o * o, axis=-1, keepdims=True)
        o_ref[0, pl.ds(s, tc), :] = o * lax.rsqrt(ms + EPS) * ng_ref[...]
        return hs[tc - 1:tc], x[tc - 8:tc]

    lax.fori_loop(0, l // tc, chunk, (jnp.zeros((1, w), F32), jnp.zeros((8, w), F32)))


def _block_diag(wb):
    n, d, _ = wb.shape
    out = jnp.zeros((n * d, n * d), wb.dtype)
    for i in range(n):
        out = lax.dynamic_update_slice(out, wb[i], (i * d, i * d))
    return out


def _rglru(proj, conv_w, conv_b, wa, ba, wx, bx, lam, ng):
    b, l, _ = proj.shape
    w = GROUP_W
    row = lambda v: v.reshape(1, w)
    full = lambda shape: pl.BlockSpec(shape, lambda bi: (0,) * len(shape))
    return pl.pallas_call(
        _rglru_kernel,
        grid=(b,),
        in_specs=[pl.BlockSpec((1, l, w), lambda bi: (bi, 0, 3)),
                  pl.BlockSpec((1, l, w), lambda bi: (bi, 0, 4)),
                  full((conv_w.shape[0], w)), full((1, w)), full((w, w)), full((1, w)),
                  full((w, w)), full((1, w)), full((1, w)), full((1, w))],
        out_specs=pl.BlockSpec((1, l, w), lambda bi: (bi, 0, 0)),
        out_shape=jax.ShapeDtypeStruct((b, l, w), F32),
        compiler_params=_params("parallel"),
        name="rglru",
    )(proj, proj, conv_w, row(conv_b), _block_diag(wa).astype(BF16), row(ba),
      _block_diag(wx).astype(BF16), row(bx), row(lam), row(ng))


def _hgrn2_kernel(q_ref, f_ref, i_ref, g_ref, lb_ref, ng_ref, o_ref,
                  qe_s, ke_s, kl_s, dl_s, o_s, inc_s):
    l = q_ref.shape[1]
    w = q_ref.shape[2]
    tile = SCAN_TILE
    c = HG_CHUNK
    r_t = _iota((tile, tile), 0)
    c_t = _iota((tile, tile), 1)
    same = _div_pow2(r_t, c) == _div_pow2(c_t, c)
    cum_m = jnp.where(jnp.logical_and(same, c_t <= r_t), 1.0, 0.0)
    tot_m = jnp.where(same, 1.0, 0.0)
    lb = lb_ref[...]

    def prep(t, carry):
        s = pl.multiple_of(t * tile, tile)
        forget = lb + (1.0 - lb) * _sigmoid(f_ref[0, pl.ds(s, tile), :])
        log_f = jnp.log(forget)
        key = 1.0 - forget
        bcum = _dot_hi(cum_m, log_f)
        blast = _dot_hi(tot_m, log_f)
        q = _silu(q_ref[0, pl.ds(s, tile), :])
        qe_s[pl.ds(s, tile), :] = (q * jnp.exp(bcum)).astype(BF16)
        ke_s[pl.ds(s, tile), :] = (key * jnp.exp(-bcum)).astype(BF16)
        kl_s[pl.ds(s, tile), :] = key * jnp.exp(blast - bcum)
        dl_s[pl.ds(s, tile), :] = jnp.exp(blast)
        return carry

    lax.fori_loop(0, l // tile, prep, 0)

    blk = HG_BLOCK
    per = blk // c
    within = jnp.logical_and(_div_pow2(_iota((blk, blk), 0), c) == _div_pow2(_iota((blk, blk), 1), c),
                             _iota((blk, blk), 1) <= _iota((blk, blk), 0))
    own = (_div_pow2(_iota((blk, per * HEAD_DIM), 0), c)
           == _div_pow2(_iota((blk, per * HEAD_DIM), 1), HEAD_DIM))

    def local(m, carry):
        s = pl.multiple_of(m * blk, blk)
        qe = qe_s[pl.ds(s, blk), :]
        ke = ke_s[pl.ds(s, blk), :]
        kl = kl_s[pl.ds(s, blk), :]
        v = i_ref[0, pl.ds(s, blk), :]
        vb = v.astype(BF16)
        outs = []
        for h in range(N_HEADS):
            sl = slice(h * HEAD_DIM, (h + 1) * HEAD_DIM)
            sc = jnp.where(within, _dot_nt(qe[:, sl], ke[:, sl]), 0.0)
            outs.append(_dot(sc.astype(BF16), vb[:, sl]))
            wide = jnp.where(own, jnp.concatenate([kl[:, sl]] * per, axis=1), 0.0)
            inc = _dot_tn(v[:, sl], wide)
            for j in range(per):
                inc_s[m * per + j, :, sl] = inc[:, j * HEAD_DIM:(j + 1) * HEAD_DIM]
        o_s[pl.ds(s, blk), :] = jnp.concatenate(outs, axis=1)
        return carry

    lax.fori_loop(0, l // blk, local, 0)

    def step(n, state):
        s = pl.multiple_of(n * c, c)
        qe = qe_s[pl.ds(s, c), :]
        dl = dl_s[pl.ds(s, 8), :][0:1]
        sb = state.astype(BF16)
        outs = [_dot_nt(qe[:, h * HEAD_DIM:(h + 1) * HEAD_DIM], sb[:, h * HEAD_DIM:(h + 1) * HEAD_DIM])
                for h in range(N_HEADS)]
        o_s[pl.ds(s, c), :] += jnp.concatenate(outs, axis=1)
        return dl * state + inc_s[n]

    lax.fori_loop(0, l // c, step, jnp.zeros((HEAD_DIM, w), F32), unroll=HG_UNROLL)

    head_avg = jnp.where(_div_pow2(_iota((w, w), 0), HEAD_DIM) == _div_pow2(_iota((w, w), 1), HEAD_DIM),
                         1.0 / HEAD_DIM, 0.0)

    def finish(t, carry):
        s = pl.multiple_of(t * tile, tile)
        o = o_s[pl.ds(s, tile), :]
        ms = _dot_hi(o * o, head_avg)
        o_ref[0, pl.ds(s, tile), :] = (o * lax.rsqrt(ms + EPS) * ng_ref[...]
                                       * _silu(g_ref[0, pl.ds(s, tile), :]))
        return carry

    lax.fori_loop(0, l // tile, finish, 0)


def _hgrn2(proj, lb, ng):
    b, l, _ = proj.shape
    w = GROUP_W
    full = lambda shape: pl.BlockSpec(shape, lambda bi: (0,) * len(shape))
    col = lambda j: pl.BlockSpec((1, l, w), lambda bi: (bi, 0, j))
    return pl.pallas_call(
        _hgrn2_kernel,
        grid=(b,),
        in_specs=[col(5), col(6), col(7), col(8), full((1, w)), full((1, w))],
        out_specs=pl.BlockSpec((1, l, w), lambda bi: (bi, 0, 0)),
        out_shape=jax.ShapeDtypeStruct((b, l, w), F32),
        scratch_shapes=[pltpu.VMEM((l, w), BF16), pltpu.VMEM((l, w), BF16), pltpu.VMEM((l, w), F32),
                        pltpu.VMEM((l, w), F32), pltpu.VMEM((l, w), F32),
                        pltpu.VMEM((l // HG_CHUNK, HEAD_DIM, w), F32)],
        compiler_params=_params("parallel"),
        name="hgrn2",
    )(proj, proj, proj, proj, lb.reshape(1, w), ng.reshape(1, w))


def _ssd_kernel(z_ref, x_ref, bm_ref, cm_ref, dt_ref, cw_ref, cb_ref, dtb_ref, aneg_ref,
                dskip_ref, ng_ref, o_ref):
    l = z_ref.shape[1]
    w = GROUP_W
    q = SSD_CHUNK
    p = HEAD_DIM
    r_q = _iota((q, q), 0)
    c_q = _iota((q, q), 1)
    tril = c_q <= r_q
    cum_m = jnp.where(tril, 1.0, 0.0)
    expand = jnp.where(jnp.logical_and(_iota((w, w), 0) == _div_pow2(_iota((w, w), 1), p),
                                       _iota((w, w), 0) < N_HEADS), 1.0, 0.0)
    pick = jnp.where(_iota((8, w), 1) == _iota((8, w), 0) * p, 1.0, 0.0)
    cw = cw_ref[...]
    cb = cb_ref[...]

    def chunk(n, carry):
        state, tails = carry
        s = pl.multiple_of(n * q, q)
        parts = []
        new_tails = []
        for j, ref in enumerate((x_ref, bm_ref, cm_ref)):
            raw = ref[0, pl.ds(s, q), :]
            sl = slice(j * w, (j + 1) * w)
            parts.append(_silu(_causal_conv(raw, tails[j], cw[:, sl], cb[:, sl])))
            new_tails.append(raw[q - 8:q])
        xs, bm, cm = parts
        dt = _softplus(dt_ref[0, pl.ds(s, q), :] + dtb_ref[...])
        dt_rep = _dot_hi(dt, expand)
        da_rep = _dot_hi(dt * aneg_ref[...], expand)
        a_cs = _dot_hi(cum_m, da_rep)
        a_row = _dot_nt(pick, a_cs, precision=HIGHEST)
        xd = xs * dt_rep
        ys = []
        new_state = []
        for h in range(N_HEADS):
            grp = h // (N_HEADS // 2)
            gs = slice(grp * SSD_STATE, (grp + 1) * SSD_STATE)
            hs = slice(h * p, (h + 1) * p)
            cg = cm[:, gs].astype(BF16)
            bg = bm[:, gs].astype(BF16)
            cb_qq = _dot_nt(cg, bg)
            a_col = a_cs[:, h * p:h * p + 1]
            decay = jnp.exp(jnp.where(tril, a_col - a_row[h:h + 1, :], -jnp.inf))
            y = _dot((cb_qq * decay).astype(BF16), xd[:, hs].astype(BF16))
            st = state[h]
            y = y + jnp.exp(a_cs[:, hs]) * _dot_nt(cg, st.astype(BF16))
            a_last = a_cs[q - 1:q, hs]
            to_end = jnp.exp(a_last - a_cs[:, hs])
            upd = _dot_tn(xd[:, hs] * to_end, bm[:, gs])
            chunk_decay = jnp.exp(a_row[h:h + 1, q - 1:q])
            new_state.append(chunk_decay * st + upd)
            ys.append(y)
        y = jnp.concatenate(ys, axis=1) + xs * dskip_ref[...]
        y = y * _silu(z_ref[0, pl.ds(s, q), :])
        halves = []
        for gi in range(2):
            yg = y[:, gi * LANES:(gi + 1) * LANES]
            ms = jnp.mean(yg * yg, axis=-1, keepdims=True)
            halves.append(yg * lax.rsqrt(ms + EPS))
        o_ref[0, pl.ds(s, q), :] = jnp.concatenate(halves, axis=1) * ng_ref[...]
        return tuple(new_state), tuple(new_tails)

    zero = jnp.zeros((p, SSD_STATE), F32)
    tail0 = jnp.zeros((8, w), F32)
    lax.fori_loop(0, l // q, chunk, ((zero,) * N_HEADS, (tail0,) * 3), unroll=2)


def _ssd(proj, conv_w, conv_b, dt_bias, a_log, d_skip, ng):
    b, l, _ = proj.shape
    w = GROUP_W
    full = lambda shape: pl.BlockSpec(shape, lambda bi: (0,) * len(shape))
    col = lambda j: pl.BlockSpec((1, l, w), lambda bi: (bi, 0, j))
    pad_heads = lambda v: jnp.zeros((1, w), F32).at[0, :N_HEADS].set(v)
    return pl.pallas_call(
        _ssd_kernel,
        grid=(b,),
        in_specs=[col(9), col(10), col(11), col(12), col(13),
                  full((conv_w.shape[0], 3 * w)), full((1, 3 * w)), full((1, w)), full((1, w)),
                  full((1, w)), full((1, w))],
        out_specs=pl.BlockSpec((1, l, w), lambda bi: (bi, 0, 0)),
        out_shape=jax.ShapeDtypeStruct((b, l, w), F32),
        compiler_params=_params("parallel"),
        name="ssd",
    )(proj, proj, proj, proj, proj, conv_w, conv_b.reshape(1, 3 * w), pad_heads(dt_bias),
      pad_heads(-jnp.exp(a_log)), jnp.repeat(d_skip, HEAD_DIM).reshape(1, w), ng.reshape(1, w))


def _outproj_router_kernel(h_ref, oa_ref, ob_ref, oc_ref, od_ref, wo_ref, g_ref, rwh_ref, rwl_ref, rb_ref,
                           h_out, xn_out, sel_out, gate_out, rank_out, cnt_out, tile_out, cnt_s):
    i = pl.program_id(0)
    tm = h_ref.shape[0]

    @pl.when(i == 0)
    def _():
        cnt_s[...] = jnp.zeros_like(cnt_s)

    acc = h_ref[...]
    for j, ref in enumerate((oa_ref, ob_ref, oc_ref, od_ref)):
        acc = acc + _dot(ref[...].astype(BF16), wo_ref[j * GROUP_W:(j + 1) * GROUP_W, :])
    h_out[...] = acc
    ms = jnp.mean(acc * acc, axis=-1, keepdims=True)
    xn = acc * lax.rsqrt(ms + EPS) * g_ref[...]
    xn_out[...] = xn
    lane = _iota((tm, LANES), 1)
    lane_f = lane.astype(F32)
    xn_hi = xn.astype(BF16)
    xn_lo = (xn - xn_hi.astype(F32)).astype(BF16)
    router = _dot(xn_hi, rwh_ref[...]) + _dot(xn_lo, rwh_ref[...]) + _dot(xn_hi, rwl_ref[...])
    logits = jnp.where(lane < N_EXPERTS, router + rb_ref[...], -jnp.inf)
    sel = jnp.zeros((tm, LANES), F32)
    tops = []
    for k in range(TOP_K):
        m = jnp.max(logits, axis=-1, keepdims=True)
        idx = jnp.min(jnp.where(logits == m, lane_f, float(LANES)), axis=-1, keepdims=True)
        hit = lane_f == idx
        sel = jnp.where(hit, float(k + 1), sel)
        logits = jnp.where(hit, -jnp.inf, logits)
        tops.append(m)
    exps = [jnp.exp(m - tops[0]) for m in tops]
    denom = exps[0] + exps[1] + exps[2] + exps[3]
    gate = jnp.zeros((tm, LANES), F32)
    for k in range(TOP_K):
        gate = jnp.where(sel == float(k + 1), exps[k] / denom, gate)
    sel_out[...] = sel
    gate_out[...] = gate
    member = jnp.where(sel > 0.0, 1.0, 0.0)
    before = jnp.where(_iota((tm, tm), 1) < _iota((tm, tm), 0), 1.0, 0.0).astype(BF16)
    carry = cnt_s[0:1, :]
    rank_out[...] = _dot(before, member.astype(BF16))
    in_tile = jnp.ceil(jnp.sum(member, axis=0, keepdims=True) * (1.0 / RUN_ALIGN)) * RUN_ALIGN
    total = carry + in_tile
    tile_out[...] = jnp.concatenate([carry, in_tile, jnp.zeros((6, LANES), F32)], axis=0)
    cnt_s[...] = jnp.broadcast_to(total, cnt_s.shape)
    cnt_out[...] = jnp.broadcast_to(total, cnt_out.shape)


def _outproj_router(h, outs, w_out, g, router_w, router_b):
    t, d = h.shape
    tm = ROW_TILE
    rw = jnp.zeros((d, LANES), F32).at[:, :N_EXPERTS].set(router_w)
    rb = jnp.zeros((1, LANES), F32).at[0, :N_EXPERTS].set(router_b)
    rw_hi = rw.astype(BF16)
    rw_lo = (rw - rw_hi.astype(F32)).astype(BF16)
    tile = lambda width: pl.BlockSpec((tm, width), lambda i: (i, 0))
    full = lambda shape: pl.BlockSpec(shape, lambda i: (0,) * len(shape))
    flat = [o.reshape(t, GROUP_W) for o in outs]
    return pl.pallas_call(
        _outproj_router_kernel,
        grid=(t // tm,),
        in_specs=[tile(d)] + [tile(GROUP_W)] * 4 + [full((d, d)), full((1, d)), full((d, LANES)),
                                                   full((d, LANES)), full((1, LANES))],
        out_specs=[tile(d), tile(d), tile(LANES), tile(LANES), tile(LANES), full((8, LANES)),
                   pl.BlockSpec((8, LANES), lambda i: (i, 0))],
        out_shape=[jax.ShapeDtypeStruct((t, d), F32), jax.ShapeDtypeStruct((t, d), F32),
                   jax.ShapeDtypeStruct((t, LANES), F32), jax.ShapeDtypeStruct((t, LANES), F32),
                   jax.ShapeDtypeStruct((t, LANES), F32), jax.ShapeDtypeStruct((8, LANES), F32),
                   jax.ShapeDtypeStruct((t // tm * 8, LANES), F32)],
        scratch_shapes=[pltpu.VMEM((8, LANES), F32)],
        compiler_params=_params("arbitrary"),
        name="outproj_router",
    )(h, *flat, w_out.astype(BF16), g.reshape(1, d), rw_hi, rw_lo, rb)


def _plan_kernel(cnt_ref, start_out, plan_out):
    cnt = cnt_ref[...]
    padded = jnp.ceil(cnt * (1.0 / MOE_BLOCK)) * MOE_BLOCK
    upto = jnp.where(_iota((LANES, LANES), 0) <= _iota((LANES, LANES), 1), 1.0, 0.0)
    pad_end = _dot_hi(padded, upto)
    pad_start = pad_end - padded
    start_out[...] = pad_start
    ends = jnp.transpose(jnp.broadcast_to(pad_end[0:1, :], (LANES, LANES)))
    expert_ok = _iota((LANES, LANES), 0) < N_EXPERTS
    n_active = pad_end[0:1, N_EXPERTS - 1:N_EXPERTS] * (1.0 / MOE_BLOCK)
    rows = [jnp.broadcast_to(n_active, (1, LANES)), cnt[0:1, :], pad_start[0:1, :], padded[0:1, :]]
    for r in range(PLAN_EXPERT_ROWS):
        blk_start = ((_iota((LANES, LANES), 1) + r * LANES) * MOE_BLOCK).astype(F32)
        done = jnp.where(jnp.logical_and(expert_ok, ends <= blk_start), 1.0, 0.0)
        rows.append(jnp.minimum(jnp.sum(done, axis=0, keepdims=True), N_EXPERTS - 1.0))
    plan_out[...] = jnp.concatenate(rows, axis=0).astype(I32)


def _plan(counts, n_blocks):
    assert n_blocks <= PLAN_EXPERT_ROWS * LANES
    return pl.pallas_call(
        _plan_kernel,
        out_shape=[jax.ShapeDtypeStruct((8, LANES), F32), jax.ShapeDtypeStruct((8, LANES), I32)],
        name="moe_plan",
    )(counts)


def _dest_kernel(sel_ref, gate_ref, rank_ref, tile_ref, start_ref, lpos_out, lk_out, gk_out, runs_out):
    tm = sel_ref.shape[0]
    sel = sel_ref[...]
    gate = gate_ref[...]
    before_tile = tile_ref[0:1, :]
    in_tile = tile_ref[1:2, :]
    earlier = jnp.where(_iota((LANES, LANES), 0) < _iota((LANES, LANES), 1), 1.0, 0.0)
    run_start = _dot_hi(jnp.broadcast_to(in_tile, (8, LANES)), earlier)[0:1, :]
    lpos = rank_ref[...] + run_start
    lane = _iota((tm, LANES), 1)
    pieces = []
    gk = jnp.zeros((tm, LANES), F32)
    lk = jnp.zeros((tm, LANES), F32)
    for k in range(TOP_K):
        hit = sel == float(k + 1)
        mine = jnp.where(hit, lpos, 0.0)
        pieces.append(mine)
        gk = jnp.where(lane == k, jnp.sum(jnp.where(hit, gate, 0.0), axis=-1, keepdims=True), gk)
        lk = jnp.where(lane == k, jnp.sum(mine, axis=-1, keepdims=True), lk)
    stacked = jnp.concatenate(pieces, axis=1)
    picker = jnp.where(_div_pow2(_iota((8, TOP_K * LANES), 1), LANES) == _iota((8, TOP_K * LANES), 0),
                       1.0, 0.0)
    lpos_out[...] = _dot_nt(picker, stacked, precision=HIGHEST).astype(I32)
    lk_out[...] = lk
    gk_out[...] = gk
    first_slot = start_ref[0:1, :] + before_tile
    total = jnp.broadcast_to(jnp.sum(in_tile, axis=-1, keepdims=True), (1, LANES))
    runs_out[...] = jnp.concatenate([in_tile, run_start, first_slot, total, jnp.zeros((4, LANES), F32)],
                                    axis=0).astype(I32)


def _dest(sel, gate, rank, tile_counts, pad_start):
    t = sel.shape[0]
    tm = ROW_TILE
    tile = pl.BlockSpec((tm, LANES), lambda i: (i, 0))
    per_tile = pl.BlockSpec((8, LANES), lambda i: (i, 0))
    return pl.pallas_call(
        _dest_kernel,
        grid=(t // tm,),
        in_specs=[tile, tile, tile, per_tile, pl.BlockSpec((8, LANES), lambda i: (0, 0))],
        out_specs=[pl.BlockSpec((8, tm), lambda i: (0, i)), tile, tile, per_tile],
        out_shape=[jax.ShapeDtypeStruct((8, t), I32), jax.ShapeDtypeStruct((t, LANES), F32),
                   jax.ShapeDtypeStruct((t, LANES), F32), jax.ShapeDtypeStruct((t // tm * 8, LANES), I32)],
        compiler_params=_params("parallel"),
        name="moe_dest",
    )(sel, gate, rank, tile_counts, pad_start)


def _copy_run(src, dst, length, n_bits, make_copy, wait):
    units = jnp.right_shift(length, RUN_ALIGN.bit_length() - 1)
    for b in range(n_bits):
        offset = jnp.left_shift(jnp.right_shift(units, b + 1), b + 1) * RUN_ALIGN

        @pl.when(jnp.bitwise_and(jnp.right_shift(units, b), 1) == 1)
        def _():
            copy = make_copy(pl.multiple_of(src + offset, RUN_ALIGN), pl.multiple_of(dst + offset, RUN_ALIGN),
                             RUN_ALIGN << b)
            if wait:
                copy.wait()
            else:
                copy.start()


def _dispatch_kernel(plan_ref, runs_ref, prev_runs_ref, lpos_ref, xn_ref, xs_hbm, sorted_s, sem):
    i = pl.program_id(0)
    tm = xn_ref.shape[0]
    n_local = sorted_s.shape[1]
    slot = i % 2

    def to_slots(buf):
        return lambda src, dst, size: pltpu.make_async_copy(
            sorted_s.at[buf, pl.ds(src, size)], xs_hbm.at[pl.ds(dst, size)], sem.at[buf])

    def start_runs(table, buf):
        def body(e, carry):
            _copy_run(table[RUN_START, e], table[RUN_SLOT, e], table[RUN_LEN, e], RUN_BITS, to_slots(buf),
                      False)
            return carry
        lax.fori_loop(0, N_EXPERTS, body, 0)

    def wait_runs(table, buf):
        _copy_run(0, 0, table[RUN_TOTAL, 0], TOTAL_BITS, lambda src, dst, size: to_slots(buf)(0, 0, size),
                  True)

    position = _iota((n_local, tm), 0)
    lpos = lpos_ref[...]
    place = jnp.zeros((n_local, tm), F32)
    for k in range(TOP_K):
        place = place + jnp.where(position == lpos[k:k + 1, :], 1.0, 0.0)
    sorted_s[slot] = _dot(place.astype(BF16), xn_ref[...].astype(BF16))
    start_runs(runs_ref, slot)

    @pl.when(i > 0)
    def _():
        wait_runs(prev_runs_ref, 1 - slot)

    @pl.when(i == pl.num_programs(0) - 1)
    def _():
        wait_runs(runs_ref, slot)

    @pl.when(i == 0)
    def _():
        def padding(wait):
            def fill(src, dst, size):
                return pltpu.make_async_copy(sorted_s.at[0, pl.ds(0, size)], xs_hbm.at[pl.ds(dst, size)],
                                             sem.at[1])

            def body(e, carry):
                cnt = plan_ref[PLAN_COUNT, e]
                _copy_run(0, plan_ref[PLAN_START, e] + cnt, plan_ref[PLAN_PADDED, e] - cnt, PAD_BITS, fill,
                          wait)
                return carry
            lax.fori_loop(0, N_EXPERTS, body, 0)

        padding(False)
        padding(True)


def _dispatch(plan, runs, lpos, xn, n_slots):
    t, d = xn.shape
    tm = ROW_TILE
    return pl.pallas_call(
        _dispatch_kernel,
        grid=(t // tm,),
        in_specs=[pl.BlockSpec(memory_space=pltpu.SMEM),
                  pl.BlockSpec((8, LANES), lambda i: (i, 0), memory_space=pltpu.SMEM),
                  pl.BlockSpec((8, LANES), lambda i: (jnp.maximum(i - 1, 0), 0), memory_space=pltpu.SMEM),
                  pl.BlockSpec((8, tm), lambda i: (0, i)),
                  pl.BlockSpec((tm, d), lambda i: (i, 0))],
        out_specs=pl.BlockSpec(memory_space=pl.ANY),
        out_shape=jax.ShapeDtypeStruct((n_slots, d), F32),
        scratch_shapes=[pltpu.VMEM((2, LOCAL_ROWS, d), F32), pltpu.SemaphoreType.DMA((2,))],
        compiler_params=_params("arbitrary"),
        name="moe_dispatch",
    )(plan, runs, runs, lpos, xn)


def _ffn_kernel(be_ref, na_ref, plan_ref, x_ref, wgu_hbm, bgu_ref, wd_hbm, bd_ref, o_ref, wgu_in_s, wd_in_s,
                wgu_s, wd32_s, wd_s, sem, *, layer):
    j = pl.program_id(0)
    n_active = na_ref[0]
    active = j < n_active
    expert = be_ref[j]
    new_expert = jnp.logical_or(j == 0, expert != be_ref[jnp.maximum(j - 1, 0)])
    rows_left = plan_ref[PLAN_START, expert] + plan_ref[PLAN_COUNT, expert] - j * MOE_BLOCK

    def fetch(e):
        return (pltpu.make_async_copy(wgu_hbm.at[layer, e], wgu_in_s, sem.at[0]),
                pltpu.make_async_copy(wd_hbm.at[layer, e], wd_in_s, sem.at[1]))

    @pl.when(j == 0)
    def _():
        for copy in fetch(expert):
            copy.start()

    @pl.when(jnp.logical_and(active, new_expert))
    def _():
        for copy in fetch(expert):
            copy.wait()
        wgu_s[...] = wgu_in_s[...].astype(BF16)
        half = LANES // 2
        for c in range(wd_s.shape[0]):
            cols = slice(c * LANES, (c + 1) * LANES)
            for k in range(wd_s.shape[1] // LANES):
                for hb in range(2):
                    src = k * LANES + hb * half
                    wd32_s[c, pl.ds(k * LANES + hb, half, stride=2), :] = wd_in_s[src:src + half, cols]
            wd_s[c] = wd32_s[c].astype(BF16)
        nxt = lax.while_loop(lambda b: jnp.logical_and(b < n_active, be_ref[jnp.minimum(b, n_active - 1)] == expert),
                             lambda b: b + 1, j + 1)

        @pl.when(nxt < n_active)
        def _():
            for copy in fetch(be_ref[jnp.minimum(nxt, n_active - 1)]):
                copy.start()

    def ffn_rows(n_rows):
        x = x_ref[0:n_rows, :].astype(BF16)
        hmid = _dot(x, wgu_s[...]) + bgu_ref[0]
        m = hmid.shape[0]
        even = (_iota((m, LANES), 1) & 1) == 0
        acts = []
        for k in range(hmid.shape[1] // (2 * LANES)):
            h0 = hmid[:, (2 * k) * LANES:(2 * k + 1) * LANES]
            h1 = hmid[:, (2 * k + 1) * LANES:(2 * k + 2) * LANES]
            glu = jnp.where(even, h0, pltpu.roll(h1, 1, 1))
            lin = jnp.where(even, pltpu.roll(h0, LANES - 1, 1), h1)
            glu = jnp.minimum(glu, SWIGLU_LIMIT)
            lin = jnp.clip(lin, -SWIGLU_LIMIT, SWIGLU_LIMIT)
            acts.append((glu * _sigmoid(SWIGLU_ALPHA * glu) * (lin + 1.0)).astype(BF16))
        act = jnp.concatenate(acts, axis=1)
        wd = jnp.concatenate([wd_s[c] for c in range(wd_s.shape[0])], axis=1)
        o_ref[0:n_rows, :] = _dot(act, wd) + bd_ref[0]

    half_block = MOE_BLOCK // 2

    @pl.when(jnp.logical_and(active, rows_left > half_block))
    def _():
        ffn_rows(MOE_BLOCK)

    @pl.when(jnp.logical_and(active, rows_left <= half_block))
    def _():
        ffn_rows(half_block)


def _ffn(layer, block_expert, n_active, plan, xs, wgu, bgu, wd, bd):
    n_slots, d = xs.shape
    dff2 = wgu.shape[3]
    nb = n_slots // MOE_BLOCK
    rows = lambda j, be, na, plan_: (jnp.maximum(jnp.minimum(j, na[0] - 1), 0), 0)
    per_expert = lambda j, be, na, plan_: (layer, be[j], 0, 0)
    grid_spec = pltpu.PrefetchScalarGridSpec(
        num_scalar_prefetch=3,
        grid=(nb,),
        in_specs=[pl.BlockSpec((MOE_BLOCK, d), rows),
                  pl.BlockSpec(memory_space=pl.ANY), pl.BlockSpec((None, 1, 1, dff2), per_expert),
                  pl.BlockSpec(memory_space=pl.ANY), pl.BlockSpec((None, 1, 1, d), per_expert)],
        out_specs=pl.BlockSpec((MOE_BLOCK, d), rows),
        scratch_shapes=[pltpu.VMEM((d, dff2), F32), pltpu.VMEM((dff2 // 2, d), F32),
                        pltpu.VMEM((d, dff2), BF16), pltpu.VMEM((d // LANES, dff2 // 2, LANES), F32),
                        pltpu.VMEM((d // LANES, dff2 // 2, LANES), BF16), pltpu.SemaphoreType.DMA((2,))],
    )
    return pl.pallas_call(
        functools.partial(_ffn_kernel, layer=layer),
        grid_spec=grid_spec,
        out_shape=jax.ShapeDtypeStruct((n_slots, d), F32),
        compiler_params=_params("arbitrary"),
        name="moe_ffn",
    )(block_expert, n_active, plan, xs, wgu, bgu, wd, bd)


def _combine_kernel(runs_ref, next_runs_ref, lk_ref, gk_ref, h_ref, g_ref, ys_hbm, o_ref, sorted_s, sem, *,
                    final_norm):
    i = pl.program_id(0)
    tm = h_ref.shape[0]
    n_local = sorted_s.shape[1]
    slot = i % 2

    def from_slots(buf):
        return lambda src, dst, size: pltpu.make_async_copy(
            ys_hbm.at[pl.ds(src, size)], sorted_s.at[buf, pl.ds(dst, size)], sem.at[buf])

    def start_runs(table, buf):
        def body(e, carry):
            _copy_run(table[RUN_SLOT, e], table[RUN_START, e], table[RUN_LEN, e], RUN_BITS, from_slots(buf),
                      False)
            return carry
        lax.fori_loop(0, N_EXPERTS, body, 0)

    @pl.when(i == 0)
    def _():
        sorted_s[...] = jnp.zeros_like(sorted_s)
        start_runs(runs_ref, slot)

    @pl.when(i + 1 < pl.num_programs(0))
    def _():
        start_runs(next_runs_ref, 1 - slot)

    _copy_run(0, 0, runs_ref[RUN_TOTAL, 0], TOTAL_BITS, lambda src, dst, size: from_slots(slot)(0, 0, size),
              True)
    position = _iota((tm, n_local), 1).astype(F32)
    lk = lk_ref[...]
    gk = gk_ref[...]
    weights = jnp.zeros((tm, n_local), F32)
    for k in range(TOP_K):
        weights = weights + jnp.where(position == lk[:, k:k + 1], gk[:, k:k + 1], 0.0)
    acc = h_ref[...] + _dot(weights.astype(BF16), sorted_s[slot].astype(BF16))
    if final_norm:
        ms = jnp.mean(acc * acc, axis=-1, keepdims=True)
        acc = acc * lax.rsqrt(ms + EPS) * g_ref[...]
    o_ref[...] = acc


def _combine(runs, lk, gk, h, ys, g, final_norm):
    t, d = h.shape
    tm = ROW_TILE
    return pl.pallas_call(
        functools.partial(_combine_kernel, final_norm=final_norm),
        grid=(t // tm,),
        in_specs=[pl.BlockSpec((8, LANES), lambda i: (i, 0), memory_space=pltpu.SMEM),
                  pl.BlockSpec((8, LANES), lambda i: (jnp.minimum(i + 1, t // tm - 1), 0),
                               memory_space=pltpu.SMEM),
                  pl.BlockSpec((tm, LANES), lambda i: (i, 0)),
                  pl.BlockSpec((tm, LANES), lambda i: (i, 0)),
                  pl.BlockSpec((tm, d), lambda i: (i, 0)),
                  pl.BlockSpec((1, d), lambda i: (0, 0)),
                  pl.BlockSpec(memory_space=pl.ANY)],
        out_specs=pl.BlockSpec((tm, d), lambda i: (i, 0)),
        out_shape=jax.ShapeDtypeStruct((t, d), F32),
        scratch_shapes=[pltpu.VMEM((2, LOCAL_ROWS, d), F32), pltpu.SemaphoreType.DMA((2,))],
        compiler_params=_params("arbitrary"),
        name="moe_combine",
    )(runs, runs, lk, gk, h, g.reshape(1, d), ys)


def _moe(h, xn, sel, gate, rank, counts, tile_counts, layer, w_gu, b_gu, w_down, b_down, g_final,
         final_norm):
    t, d = h.shape
    n_slots = t * TOP_K + (t // ROW_TILE) * N_EXPERTS * RUN_ALIGN + N_EXPERTS * MOE_BLOCK
    n_blocks = n_slots // MOE_BLOCK
    pad_start, plan = _plan(counts, n_blocks)
    lpos, lk, gk, runs = _dest(sel, gate, rank, tile_counts, pad_start)
    xs = _dispatch(plan, runs, lpos, xn, n_slots)
    block_expert = plan[PLAN_EXPERTS:].reshape(-1)[:n_blocks]
    n_active = plan[PLAN_ACTIVE, :1]
    depth = w_gu.shape[0]
    ys = _ffn(layer, block_expert, n_active, plan, xs, w_gu, b_gu.reshape(depth, N_EXPERTS, 1, -1), w_down,
              b_down.reshape(depth, N_EXPERTS, 1, d))
    return _combine(runs, lk, gk, h, ys, g_final, final_norm)


def kernel(x, norm_mix_g, w_in, sb_norm_g, rg_conv_w, rg_conv_b, rg_wa, rg_ba, rg_wx, rg_bx,
           rg_lambda, rg_norm_g, hg_lower_bounds, hg_norm_g, m2_conv_w, m2_conv_b, m2_dt_bias,
           m2_a_log, m2_d, m2_norm_g, w_out, norm_ffn_g, router_w, router_b, moe_w_gu, moe_b_gu,
           moe_w_down, moe_b_down, final_norm_g):
    b, l, d = x.shape
    depth = w_in.shape[0]
    t = b * l
    lbs = jnp.cumsum(jax.nn.softmax(hg_lower_bounds.astype(F32), axis=0), axis=0)
    lbs = lbs - lbs[0]
    d_in = w_in.shape[2]
    w_in_p = jnp.zeros((depth, d, PROJ_BLOCKS * GROUP_W), BF16).at[:, :, :d_in].set(w_in.astype(BF16))
    h = x.reshape(t, d)
    for layer in range(depth):
        proj = _norm_inproj(h, norm_mix_g[layer], w_in_p[layer]).reshape(b, l, PROJ_BLOCKS * GROUP_W)
        o_a = _sb_attn(proj, sb_norm_g[layer])
        o_b = _rglru(proj, rg_conv_w[layer], rg_conv_b[layer], rg_wa[layer], rg_ba[layer],
                     rg_wx[layer], rg_bx[layer], rg_lambda[layer], rg_norm_g[layer])
        o_c = _hgrn2(proj, lbs[layer], hg_norm_g[layer])
        o_d = _ssd(proj, m2_conv_w[layer], m2_conv_b[layer], m2_dt_bias[layer], m2_a_log[layer],
                   m2_d[layer], m2_norm_g[layer])
        h, xn, sel, gate, rank, counts, tile_counts = _outproj_router(
            h, (o_a, o_b, o_c, o_d), w_out[layer], norm_ffn_g[layer], router_w[layer],
            router_b[layer])
        h = _moe(h, xn, sel, gate, rank, counts, tile_counts, layer, moe_w_gu, moe_b_gu, moe_w_down,
                 moe_b_down, final_norm_g, layer == depth - 1)
    return h.reshape(b, l, d)
```

```python
import functools
import math

import jax
import jax.numpy as jnp
from jax import lax
from jax.experimental import pallas as pl
from jax.experimental.pallas import tpu as pltpu

F32 = jnp.float32
BF16 = jnp.bfloat16
I32 = jnp.int32
HIGHEST = lax.Precision.HIGHEST
EPS = 1e-6
EXP_UNDERFLOW = -104.0

LANES = 128
GROUP_W = 256
HEAD_DIM = 64
N_HEADS = 4
SSD_STATE = 128
RG_C = 8.0
N_EXPERTS = 32
TOP_K = 4
SWIGLU_LIMIT = 7.0
SWIGLU_ALPHA = 1.702
PROJ_BLOCKS = 14
VMEM_LIMIT = 56 * 1024 * 1024

ROW_TILE = 256
INPROJ_TILE = 512
ATTN_BLOCK = 256
SCAN_TILE = 256
HG_CHUNK = 16
HG_BLOCK = 128
SSD_CHUNK = LANES
MOE_BLOCK = 512
HG_UNROLL = 8

PLAN_ACTIVE, PLAN_COUNT, PLAN_START, PLAN_PADDED, PLAN_EXPERTS = 0, 1, 2, 3, 4
PLAN_EXPERT_ROWS = 8 - PLAN_EXPERTS
RUN_LEN, RUN_START, RUN_SLOT, RUN_TOTAL = 0, 1, 2, 3
RUN_ALIGN = 8
RUN_BITS = (ROW_TILE // RUN_ALIGN).bit_length()
PAD_BITS = (MOE_BLOCK // RUN_ALIGN - 1).bit_length()
LOCAL_ROWS = ROW_TILE * TOP_K + N_EXPERTS * RUN_ALIGN
TOTAL_BITS = (LOCAL_ROWS // RUN_ALIGN).bit_length()


def _params(*sem):
    return pltpu.CompilerParams(dimension_semantics=sem, vmem_limit_bytes=VMEM_LIMIT)


def _dot(a, b):
    return jnp.dot(a, b, preferred_element_type=F32)


def _dot_hi(a, b):
    return jnp.dot(a, b, preferred_element_type=F32, precision=HIGHEST)


def _split3(x):
    hi = x.astype(BF16)
    rest = x - hi.astype(F32)
    mid = rest.astype(BF16)
    lo = (rest - mid.astype(F32)).astype(BF16)
    return hi, mid, lo


def _dot_exact_lhs(m, x):
    return sum(_dot(m, part) for part in _split3(x))


def _dot_exact_rhs(x, m):
    return sum(_dot(part, m) for part in _split3(x))


def _dot_nt(a, b, precision=None):
    return lax.dot_general(a, b, (((1,), (1,)), ((), ())), preferred_element_type=F32,
                           precision=precision)


def _dot_tn(a, b):
    return lax.dot_general(a, b, (((0,), (0,)), ((), ())), preferred_element_type=F32)


def _sigmoid(x):
    return 1.0 / (1.0 + jnp.exp(-x))


def _silu(x):
    return x * _sigmoid(x)


def _softplus(x):
    return jnp.maximum(x, 0.0) + jnp.log(1.0 + jnp.exp(-jnp.abs(x)))


def _iota(shape, dim):
    return lax.broadcasted_iota(I32, shape, dim)


def _div_pow2(x, n):
    assert n & (n - 1) == 0
    return jnp.right_shift(x, int(math.log2(n)))


def _shift_rows(x, d, prev8):
    r = pltpu.roll(x, d, 0)
    p = pltpu.roll(prev8, d, 0)
    head = jnp.where(_iota(prev8.shape, 0) < d, p, r[:8])
    return jnp.concatenate([head, r[8:]], axis=0)


def _causal_conv(x, prev8, w, b):
    k = w.shape[0]
    y = x * w[k - 1:k] + b
    for d in range(1, k):
        y = y + _shift_rows(x, d, prev8) * w[k - 1 - d:k - d]
    return y


def _norm_inproj_kernel(x_ref, g_ref, w_ref, o_ref):
    x = x_ref[...]
    ms = jnp.mean(x * x, axis=-1, keepdims=True)
    xn = (x * lax.rsqrt(ms + EPS) * g_ref[...]).astype(BF16)
    o_ref[...] = _dot(xn, w_ref[...])


def _norm_inproj(h, g, w):
    t, d = h.shape
    n = w.shape[1]
    return pl.pallas_call(
        _norm_inproj_kernel,
        grid=(t // INPROJ_TILE,),
        in_specs=[pl.BlockSpec((INPROJ_TILE, d), lambda i: (i, 0)),
                  pl.BlockSpec((1, d), lambda i: (0, 0)),
                  pl.BlockSpec((d, n), lambda i: (0, 0))],
        out_specs=pl.BlockSpec((INPROJ_TILE, n), lambda i: (i, 0)),
        out_shape=jax.ShapeDtypeStruct((t, n), F32),
        compiler_params=_params("parallel"),
        name="norm_inproj",
    )(h, g.reshape(1, d), w)


def _sb_attn_kernel(q_ref, k_ref, v_ref, g_ref, o_ref, acc_ref, run_ref):
    i = pl.program_id(1)
    blk = ATTN_BLOCK
    scale = HEAD_DIM ** -0.5
    row = _iota((blk, blk), 0)
    col = _iota((blk, blk), 1)
    below = col < row
    r_w = _iota((blk, blk + LANES), 0)
    c_w = _iota((blk, blk + LANES), 1)
    later_sum = jnp.where(jnp.logical_or(c_w >= blk, r_w > c_w), 1.0, 0.0).astype(BF16)
    acc_ref[...] = jnp.zeros_like(acc_ref)
    run_ref[...] = jnp.zeros_like(run_ref)
    qs = [(q_ref[0, :, h * HEAD_DIM:(h + 1) * HEAD_DIM] * scale).astype(BF16) for h in range(N_HEADS)]

    def key_block(j, diagonal):
        ks = pl.multiple_of(j * blk, blk)
        for h in range(N_HEADS):
            sl = slice(h * HEAD_DIM, (h + 1) * HEAD_DIM)
            kh = k_ref[0, pl.ds(ks, blk), sl].astype(BF16)
            vh = v_ref[0, pl.ds(ks, blk), sl].astype(BF16)
            z = _dot_nt(qs[h], kh)
            ls = jnp.minimum(z, 0.0) - jnp.log(1.0 + jnp.exp(-jnp.abs(z)))
            lk = ls - z
            if diagonal:
                lk = jnp.where(below, lk, 0.0)
            lk_hi = lk.astype(BF16)
            lk_lo = (lk - lk_hi.astype(F32)).astype(BF16)
            sums = _dot(lk_hi, later_sum) + _dot(lk_lo, later_sum)
            run = run_ref[h]
            rem = sums[:, :blk] + jnp.concatenate([run] * (blk // LANES), axis=1)
            a = jnp.exp(ls + rem)
            if diagonal:
                a = jnp.where(below, a, 0.0)
            acc_ref[:, sl] += _dot(a.astype(BF16), vh)
            run_ref[h] = run + sums[:, blk:]

    key_block(i, True)

    def live():
        return jnp.max(run_ref[...]) > EXP_UNDERFLOW

    def more(carry):
        jj, alive = carry
        return jnp.logical_and(jj <= i, alive)

    def body(carry):
        jj, _ = carry
        key_block(i - jj, False)
        return jj + 1, live()

    lax.while_loop(more, body, (jnp.int32(1), live()))
    o = acc_ref[...]
    ms = jnp.mean(o * o, axis=-1, keepdims=True)
    o_ref[0] = o * lax.rsqrt(ms + EPS) * g_ref[...]


def _sb_attn(proj, g):
    b, l, _ = proj.shape
    blk = ATTN_BLOCK
    return pl.pallas_call(
        _sb_attn_kernel,
        grid=(b, l // blk),
        in_specs=[pl.BlockSpec((1, blk, GROUP_W), lambda bi, i: (bi, i, 0)),
                  pl.BlockSpec((1, l, GROUP_W), lambda bi, i: (bi, 0, 1)),
                  pl.BlockSpec((1, l, GROUP_W), lambda bi, i: (bi, 0, 2)),
                  pl.BlockSpec((1, GROUP_W), lambda bi, i: (0, 0))],
        out_specs=pl.BlockSpec((1, blk, GROUP_W), lambda bi, i: (bi, i, 0)),
        out_shape=jax.ShapeDtypeStruct((b, l, GROUP_W), F32),
        scratch_shapes=[pltpu.VMEM((blk, GROUP_W), F32), pltpu.VMEM((N_HEADS, blk, LANES), F32)],
        compiler_params=_params("parallel", "arbitrary"),
        name="sb_attn",
    )(proj, proj, proj, g.reshape(1, GROUP_W))


def _rglru_kernel(x_ref, gate_ref, cw_ref, cb_ref, wa_ref, ba_ref, wx_ref, bx_ref, lam_ref,
                  ng_ref, o_ref):
    l = x_ref.shape[1]
    w = x_ref.shape[2]
    tc = SCAN_TILE
    rows = _iota((tc, w), 0)
    cw = cw_ref[...]
    cb = cb_ref[...]
    neg_c_sp = -RG_C * _softplus(-lam_ref[...])

    def chunk(c, carry):
        h_prev, tail = carry
        s = pl.multiple_of(c * tc, tc)
        x = x_ref[0, pl.ds(s, tc), :]
        xc = _causal_conv(x, tail, cw, cb)
        xcb = xc.astype(BF16)
        r = _sigmoid(_dot(xcb, wa_ref[...]) + ba_ref[...])
        ig = _sigmoid(_dot(xcb, wx_ref[...]) + bx_ref[...])
        log_a = r * neg_c_sp
        a = jnp.exp(log_a)
        th = jnp.tanh(log_a)
        mult = jnp.sqrt(-2.0 * th / (1.0 - th))
        mult = jnp.where(rows + s == 0, 1.0, mult)
        u = mult * ig * xc
        d = 1
        while d < tc:
            keep = rows >= d
            a_s = jnp.where(keep, pltpu.roll(a, d, 0), 1.0)
            u_s = jnp.where(keep, pltpu.roll(u, d, 0), 0.0)
            u = a * u_s + u
            a = a * a_s
            d *= 2
        hs = u + a * h_prev
        gate = gate_ref[0, pl.ds(s, tc), :]
        gelu = 0.5 * gate * (1.0 + jnp.tanh(math.sqrt(2.0 / math.pi) * (gate + 0.044715 * gate * gate * gate)))
        o = hs * gelu
        ms = jnp.mean(o * o, axis=-1, keepdims=True)
        o_ref[0, pl.ds(s, tc), :] = o * lax.rsqrt(ms + EPS) * ng_ref[...]
        return hs[tc - 1:tc], x[tc - 8:tc]

    lax.fori_loop(0, l // tc, chunk, (jnp.zeros((1, w), F32), jnp.zeros((8, w), F32)))


def _block_diag(wb):
    n, d, _ = wb.shape
    out = jnp.zeros((n * d, n * d), wb.dtype)
    for i in range(n):
        out = lax.dynamic_update_slice(out, wb[i], (i * d, i * d))
    return out


def _rglru(proj, conv_w, conv_b, wa, ba, wx, bx, lam, ng):
    b, l, _ = proj.shape
    w = GROUP_W
    row = lambda v: v.reshape(1, w)
    full = lambda shape: pl.BlockSpec(shape, lambda bi: (0,) * len(shape))
    return pl.pallas_call(
        _rglru_kernel,
        grid=(b,),
        in_specs=[pl.BlockSpec((1, l, w), lambda bi: (bi, 0, 3)),
                  pl.BlockSpec((1, l, w), lambda bi: (bi, 0, 4)),
                  full((conv_w.shape[0], w)), full((1, w)), full((w, w)), full((1, w)),
                  full((w, w)), full((1, w)), full((1, w)), full((1, w))],
        out_specs=pl.BlockSpec((1, l, w), lambda bi: (bi, 0, 0)),
        out_shape=jax.ShapeDtypeStruct((b, l, w), F32),
        compiler_params=_params("parallel"),
        name="rglru",
    )(proj, proj, conv_w, row(conv_b), _block_diag(wa).astype(BF16), row(ba),
      _block_diag(wx).astype(BF16), row(bx), row(lam), row(ng))


def _hgrn2_kernel(q_ref, f_ref, i_ref, g_ref, lb_ref, ng_ref, o_ref,
                  qe_s, ke_s, kl_s, dl_s, o_s, inc_s):
    l = q_ref.shape[1]
    w = q_ref.shape[2]
    tile = SCAN_TILE
    c = HG_CHUNK
    r_t = _iota((tile, tile), 0)
    c_t = _iota((tile, tile), 1)
    same = _div_pow2(r_t, c) == _div_pow2(c_t, c)
    cum_m = jnp.where(jnp.logical_and(same, c_t <= r_t), 1.0, 0.0).astype(BF16)
    tot_m = jnp.where(same, 1.0, 0.0).astype(BF16)
    lb = lb_ref[...]

    def prep(t, carry):
        s = pl.multiple_of(t * tile, tile)
        forget = lb + (1.0 - lb) * _sigmoid(f_ref[0, pl.ds(s, tile), :])
        log_f = jnp.log(forget)
        key = 1.0 - forget
        parts = _split3(log_f)
        bcum = sum(_dot(cum_m, part) for part in parts)
        blast = sum(_dot(tot_m, part) for part in parts)
        q = _silu(q_ref[0, pl.ds(s, tile), :])
        qe_s[pl.ds(s, tile), :] = (q * jnp.exp(bcum)).astype(BF16)
        ke_s[pl.ds(s, tile), :] = (key * jnp.exp(-bcum)).astype(BF16)
        kl_s[pl.ds(s, tile), :] = key * jnp.exp(blast - bcum)
        dl_s[pl.ds(s, tile), :] = jnp.exp(blast)
        return carry

    lax.fori_loop(0, l // tile, prep, 0)

    blk = HG_BLOCK
    per = blk // c
    within = jnp.logical_and(_div_pow2(_iota((blk, blk), 0), c) == _div_pow2(_iota((blk, blk), 1), c),
                             _iota((blk, blk), 1) <= _iota((blk, blk), 0))
    own = (_div_pow2(_iota((blk, per * HEAD_DIM), 0), c)
           == _div_pow2(_iota((blk, per * HEAD_DIM), 1), HEAD_DIM))

    def local(m, carry):
        s = pl.multiple_of(m * blk, blk)
        qe = qe_s[pl.ds(s, blk), :]
        ke = ke_s[pl.ds(s, blk), :]
        kl = kl_s[pl.ds(s, blk), :]
        v = i_ref[0, pl.ds(s, blk), :]
        vb = v.astype(BF16)
        outs = []
        for h in range(N_HEADS):
            sl = slice(h * HEAD_DIM, (h + 1) * HEAD_DIM)
            sc = jnp.where(within, _dot_nt(qe[:, sl], ke[:, sl]), 0.0)
            outs.append(_dot(sc.astype(BF16), vb[:, sl]))
            wide = jnp.where(own, jnp.concatenate([kl[:, sl]] * per, axis=1), 0.0)
            inc = _dot_tn(v[:, sl], wide)
            for j in range(per):
                inc_s[m * per + j, :, sl] = inc[:, j * HEAD_DIM:(j + 1) * HEAD_DIM]
        o_s[pl.ds(s, blk), :] = jnp.concatenate(outs, axis=1)
        return carry

    lax.fori_loop(0, l // blk, local, 0)

    def step(n, state):
        s = pl.multiple_of(n * c, c)
        qe = qe_s[pl.ds(s, c), :]
        dl = dl_s[pl.ds(s, 8), :][0:1]
        sb = state.astype(BF16)
        outs = [_dot_nt(qe[:, h * HEAD_DIM:(h + 1) * HEAD_DIM], sb[:, h * HEAD_DIM:(h + 1) * HEAD_DIM])
                for h in range(N_HEADS)]
        o_s[pl.ds(s, c), :] += jnp.concatenate(outs, axis=1)
        return dl * state + inc_s[n]

    lax.fori_loop(0, l // c, step, jnp.zeros((HEAD_DIM, w), F32), unroll=HG_UNROLL)

    head_avg = jnp.where(_div_pow2(_iota((w, w), 0), HEAD_DIM) == _div_pow2(_iota((w, w), 1), HEAD_DIM),
                         1.0 / HEAD_DIM, 0.0).astype(BF16)

    def finish(t, carry):
        s = pl.multiple_of(t * tile, tile)
        o = o_s[pl.ds(s, tile), :]
        ms = _dot_exact_rhs(o * o, head_avg)
        o_ref[0, pl.ds(s, tile), :] = (o * lax.rsqrt(ms + EPS) * ng_ref[...]
                                       * _silu(g_ref[0, pl.ds(s, tile), :]))
        return carry

    lax.fori_loop(0, l // tile, finish, 0)


def _hgrn2(proj, lb, ng):
    b, l, _ = proj.shape
    w = GROUP_W
    full = lambda shape: pl.BlockSpec(shape, lambda bi: (0,) * len(shape))
    col = lambda j: pl.BlockSpec((1, l, w), lambda bi: (bi, 0, j))
    return pl.pallas_call(
        _hgrn2_kernel,
        grid=(b,),
        in_specs=[col(5), col(6), col(7), col(8), full((1, w)), full((1, w))],
        out_specs=pl.BlockSpec((1, l, w), lambda bi: (bi, 0, 0)),
        out_shape=jax.ShapeDtypeStruct((b, l, w), F32),
        scratch_shapes=[pltpu.VMEM((l, w), BF16), pltpu.VMEM((l, w), BF16), pltpu.VMEM((l, w), F32),
                        pltpu.VMEM((l, w), F32), pltpu.VMEM((l, w), F32),
                        pltpu.VMEM((l // HG_CHUNK, HEAD_DIM, w), F32)],
        compiler_params=_params("parallel"),
        name="hgrn2",
    )(proj, proj, proj, proj, lb.reshape(1, w), ng.reshape(1, w))


def _ssd_kernel(z_ref, x_ref, bm_ref, cm_ref, dt_ref, cw_ref, cb_ref, dtb_ref, aneg_ref,
                dskip_ref, ng_ref, o_ref):
    l = z_ref.shape[1]
    w = GROUP_W
    q = SSD_CHUNK
    p = HEAD_DIM
    r_q = _iota((q, q), 0)
    c_q = _iota((q, q), 1)
    tril = c_q <= r_q
    cum_m = jnp.where(tril, 1.0, 0.0).astype(BF16)
    cw = cw_ref[...]
    cb = cb_ref[...]

    def chunk(n, carry):
        state, tails = carry
        s = pl.multiple_of(n * q, q)
        parts = []
        new_tails = []
        for j, ref in enumerate((x_ref, bm_ref, cm_ref)):
            raw = ref[0, pl.ds(s, q), :]
            sl = slice(j * w, (j + 1) * w)
            parts.append(_silu(_causal_conv(raw, tails[j], cw[:, sl], cb[:, sl])))
            new_tails.append(raw[q - 8:q])
        xs, bm, cm = parts
        dt = _softplus(dt_ref[0, pl.ds(s, q), :] + dtb_ref[...])
        dt_rep = jnp.concatenate([jnp.broadcast_to(dt[:, h:h + 1], (q, p)) for h in range(N_HEADS)],
                                 axis=1)
        a_cs = _dot_exact_lhs(cum_m, dt_rep * aneg_ref[...])
        a_rows = [jnp.transpose(a_cs[:, g * LANES:(g + 1) * LANES]) for g in range(w // LANES)]
        xd = xs * dt_rep
        heads_per_group = N_HEADS // (w // SSD_STATE)
        cgs = [cm[:, g * SSD_STATE:(g + 1) * SSD_STATE].astype(BF16) for g in range(w // SSD_STATE)]
        cbs = [_dot_nt(cgs[g], bm[:, g * SSD_STATE:(g + 1) * SSD_STATE].astype(BF16))
               for g in range(w // SSD_STATE)]
        ys = []
        new_state = []
        for h in range(N_HEADS):
            grp = h // heads_per_group
            gs = slice(grp * SSD_STATE, (grp + 1) * SSD_STATE)
            hs = slice(h * p, (h + 1) * p)
            cg = cgs[grp]
            row = (h * p) % LANES
            a_row = a_rows[(h * p) // LANES][row:row + 1, :]
            a_col = a_cs[:, h * p:h * p + 1]
            decay = jnp.exp(jnp.where(tril, a_col - a_row, -jnp.inf))
            y = _dot((cbs[grp] * decay).astype(BF16), xd[:, hs].astype(BF16))
            st = state[h]
            y = y + jnp.exp(a_cs[:, hs]) * _dot_nt(cg, st.astype(BF16))
            a_last = a_cs[q - 1:q, hs]
            to_end = jnp.exp(a_last - a_cs[:, hs])
            upd = _dot_tn(xd[:, hs] * to_end, bm[:, gs])
            chunk_decay = jnp.exp(a_row[:, q - 1:q])
            new_state.append(chunk_decay * st + upd)
            ys.append(y)
        y = jnp.concatenate(ys, axis=1) + xs * dskip_ref[...]
        y = y * _silu(z_ref[0, pl.ds(s, q), :])
        halves = []
        for gi in range(2):
            yg = y[:, gi * LANES:(gi + 1) * LANES]
            ms = jnp.mean(yg * yg, axis=-1, keepdims=True)
            halves.append(yg * lax.rsqrt(ms + EPS))
        o_ref[0, pl.ds(s, q), :] = jnp.concatenate(halves, axis=1) * ng_ref[...]
        return tuple(new_state), tuple(new_tails)

    zero = jnp.zeros((p, SSD_STATE), F32)
    tail0 = jnp.zeros((8, w), F32)
    lax.fori_loop(0, l // q, chunk, ((zero,) * N_HEADS, (tail0,) * 3), unroll=2)


def _ssd(proj, conv_w, conv_b, dt_bias, a_log, d_skip, ng):
    b, l, _ = proj.shape
    w = GROUP_W
    full = lambda shape: pl.BlockSpec(shape, lambda bi: (0,) * len(shape))
    col = lambda j: pl.BlockSpec((1, l, w), lambda bi: (bi, 0, j))
    pad_heads = lambda v: jnp.zeros((1, w), F32).at[0, :N_HEADS].set(v)
    return pl.pallas_call(
        _ssd_kernel,
        grid=(b,),
        in_specs=[col(9), col(10), col(11), col(12), col(13),
                  full((conv_w.shape[0], 3 * w)), full((1, 3 * w)), full((1, w)), full((1, w)),
                  full((1, w)), full((1, w))],
        out_specs=pl.BlockSpec((1, l, w), lambda bi: (bi, 0, 0)),
        out_shape=jax.ShapeDtypeStruct((b, l, w), F32),
        compiler_params=_params("parallel"),
        name="ssd",
    )(proj, proj, proj, proj, proj, conv_w, conv_b.reshape(1, 3 * w), pad_heads(dt_bias),
      jnp.repeat(-jnp.exp(a_log), HEAD_DIM).reshape(1, w), jnp.repeat(d_skip, HEAD_DIM).reshape(1, w),
      ng.reshape(1, w))


def _outproj_router_kernel(h_ref, oa_ref, ob_ref, oc_ref, od_ref, wo_ref, g_ref, rwh_ref, rwl_ref, rb_ref,
                           h_out, xn_out, sel_out, gate_out, rank_out, cnt_out, tile_out, cnt_s):
    i = pl.program_id(0)
    tm = h_ref.shape[0]

    @pl.when(i == 0)
    def _():
        cnt_s[...] = jnp.zeros_like(cnt_s)

    acc = h_ref[...]
    for j, ref in enumerate((oa_ref, ob_ref, oc_ref, od_ref)):
        acc = acc + _dot(ref[...].astype(BF16), wo_ref[j * GROUP_W:(j + 1) * GROUP_W, :])
    h_out[...] = acc
    ms = jnp.mean(acc * acc, axis=-1, keepdims=True)
    xn = acc * lax.rsqrt(ms + EPS) * g_ref[...]
    xn_out[...] = xn
    lane = _iota((tm, LANES), 1)
    lane_f = lane.astype(F32)
    xn_hi = xn.astype(BF16)
    xn_lo = (xn - xn_hi.astype(F32)).astype(BF16)
    router = _dot(xn_hi, rwh_ref[...]) + _dot(xn_lo, rwh_ref[...]) + _dot(xn_hi, rwl_ref[...])
    logits = jnp.where(lane < N_EXPERTS, router + rb_ref[...], -jnp.inf)
    sel = jnp.zeros((tm, LANES), F32)
    tops = []
    for k in range(TOP_K):
        m = jnp.max(logits, axis=-1, keepdims=True)
        idx = jnp.min(jnp.where(logits == m, lane_f, float(LANES)), axis=-1, keepdims=True)
        hit = lane_f == idx
        sel = jnp.where(hit, float(k + 1), sel)
        logits = jnp.where(hit, -jnp.inf, logits)
        tops.append(m)
    exps = [jnp.exp(m - tops[0]) for m in tops]
    denom = exps[0] + exps[1] + exps[2] + exps[3]
    gate = jnp.zeros((tm, LANES), F32)
    for k in range(TOP_K):
        gate = jnp.where(sel == float(k + 1), exps[k] / denom, gate)
    sel_out[...] = sel
    gate_out[...] = gate
    member = jnp.where(sel > 0.0, 1.0, 0.0)
    before = jnp.where(_iota((tm, tm), 1) < _iota((tm, tm), 0), 1.0, 0.0).astype(BF16)
    carry = cnt_s[0:1, :]
    rank_out[...] = _dot(before, member.astype(BF16))
    in_tile = jnp.ceil(jnp.sum(member, axis=0, keepdims=True) * (1.0 / RUN_ALIGN)) * RUN_ALIGN
    total = carry + in_tile
    tile_out[...] = jnp.concatenate([carry, in_tile, jnp.zeros((6, LANES), F32)], axis=0)
    cnt_s[...] = jnp.broadcast_to(total, cnt_s.shape)
    cnt_out[...] = jnp.broadcast_to(total, cnt_out.shape)


def _outproj_router(h, outs, w_out, g, router_w, router_b):
    t, d = h.shape
    tm = ROW_TILE
    rw = jnp.zeros((d, LANES), F32).at[:, :N_EXPERTS].set(router_w)
    rb = jnp.zeros((1, LANES), F32).at[0, :N_EXPERTS].set(router_b)
    rw_hi = rw.astype(BF16)
    rw_lo = (rw - rw_hi.astype(F32)).astype(BF16)
    tile = lambda width: pl.BlockSpec((tm, width), lambda i: (i, 0))
    full = lambda shape: pl.BlockSpec(shape, lambda i: (0,) * len(shape))
    flat = [o.reshape(t, GROUP_W) for o in outs]
    return pl.pallas_call(
        _outproj_router_kernel,
        grid=(t // tm,),
        in_specs=[tile(d)] + [tile(GROUP_W)] * 4 + [full((d, d)), full((1, d)), full((d, LANES)),
                                                   full((d, LANES)), full((1, LANES))],
        out_specs=[tile(d), tile(d), tile(LANES), tile(LANES), tile(LANES), full((8, LANES)),
                   pl.BlockSpec((8, LANES), lambda i: (i, 0))],
        out_shape=[jax.ShapeDtypeStruct((t, d), F32), jax.ShapeDtypeStruct((t, d), F32),
                   jax.ShapeDtypeStruct((t, LANES), F32), jax.ShapeDtypeStruct((t, LANES), F32),
                   jax.ShapeDtypeStruct((t, LANES), F32), jax.ShapeDtypeStruct((8, LANES), F32),
                   jax.ShapeDtypeStruct((t // tm * 8, LANES), F32)],
        scratch_shapes=[pltpu.VMEM((8, LANES), F32)],
        compiler_params=_params("arbitrary"),
        name="outproj_router",
    )(h, *flat, w_out.astype(BF16), g.reshape(1, d), rw_hi, rw_lo, rb)


def _plan_kernel(cnt_ref, start_out, plan_out):
    cnt = cnt_ref[...]
    padded = jnp.ceil(cnt * (1.0 / MOE_BLOCK)) * MOE_BLOCK
    upto = jnp.where(_iota((LANES, LANES), 0) <= _iota((LANES, LANES), 1), 1.0, 0.0)
    pad_end = _dot_hi(padded, upto)
    pad_start = pad_end - padded
    start_out[...] = pad_start
    ends = jnp.transpose(jnp.broadcast_to(pad_end[0:1, :], (LANES, LANES)))
    expert_ok = _iota((LANES, LANES), 0) < N_EXPERTS
    n_active = pad_end[0:1, N_EXPERTS - 1:N_EXPERTS] * (1.0 / MOE_BLOCK)
    rows = [jnp.broadcast_to(n_active, (1, LANES)), cnt[0:1, :], pad_start[0:1, :], padded[0:1, :]]
    for r in range(PLAN_EXPERT_ROWS):
        blk_start = ((_iota((LANES, LANES), 1) + r * LANES) * MOE_BLOCK).astype(F32)
        done = jnp.where(jnp.logical_and(expert_ok, ends <= blk_start), 1.0, 0.0)
        rows.append(jnp.minimum(jnp.sum(done, axis=0, keepdims=True), N_EXPERTS - 1.0))
    plan_out[...] = jnp.concatenate(rows, axis=0).astype(I32)


def _plan(counts, n_blocks):
    assert n_blocks <= PLAN_EXPERT_ROWS * LANES
    return pl.pallas_call(
        _plan_kernel,
        out_shape=[jax.ShapeDtypeStruct((8, LANES), F32), jax.ShapeDtypeStruct((8, LANES), I32)],
        name="moe_plan",
    )(counts)


def _dest_kernel(sel_ref, gate_ref, rank_ref, tile_ref, start_ref, lpos_out, lk_out, gk_out, runs_out):
    tm = sel_ref.shape[0]
    sel = sel_ref[...]
    gate = gate_ref[...]
    before_tile = tile_ref[0:1, :]
    in_tile = tile_ref[1:2, :]
    earlier = jnp.where(_iota((LANES, LANES), 0) < _iota((LANES, LANES), 1), 1.0, 0.0)
    run_start = _dot_hi(jnp.broadcast_to(in_tile, (8, LANES)), earlier)[0:1, :]
    lpos = rank_ref[...] + run_start
    lane = _iota((tm, LANES), 1)
    pieces = []
    gk = jnp.zeros((tm, LANES), F32)
    lk = jnp.zeros((tm, LANES), F32)
    for k in range(TOP_K):
        hit = sel == float(k + 1)
        mine = jnp.where(hit, lpos, 0.0)
        pieces.append(mine)
        gk = jnp.where(lane == k, jnp.sum(jnp.where(hit, gate, 0.0), axis=-1, keepdims=True), gk)
        lk = jnp.where(lane == k, jnp.sum(mine, axis=-1, keepdims=True), lk)
    stacked = jnp.concatenate(pieces, axis=1)
    picker = jnp.where(_div_pow2(_iota((8, TOP_K * LANES), 1), LANES) == _iota((8, TOP_K * LANES), 0),
                       1.0, 0.0).astype(BF16)
    lpos_out[...] = sum(_dot_nt(picker, part) for part in _split3(stacked)).astype(I32)
    lk_out[...] = lk
    gk_out[...] = gk
    first_slot = start_ref[0:1, :] + before_tile
    total = jnp.broadcast_to(jnp.sum(in_tile, axis=-1, keepdims=True), (1, LANES))
    runs_out[...] = jnp.concatenate([in_tile, run_start, first_slot, total, jnp.zeros((4, LANES), F32)],
                                    axis=0).astype(I32)


def _dest(sel, gate, rank, tile_counts, pad_start):
    t = sel.shape[0]
    tm = ROW_TILE
    tile = pl.BlockSpec((tm, LANES), lambda i: (i, 0))
    per_tile = pl.BlockSpec((8, LANES), lambda i: (i, 0))
    return pl.pallas_call(
        _dest_kernel,
        grid=(t // tm,),
        in_specs=[tile, tile, tile, per_tile, pl.BlockSpec((8, LANES), lambda i: (0, 0))],
        out_specs=[pl.BlockSpec((8, tm), lambda i: (0, i)), tile, tile, per_tile],
        out_shape=[jax.ShapeDtypeStruct((8, t), I32), jax.ShapeDtypeStruct((t, LANES), F32),
                   jax.ShapeDtypeStruct((t, LANES), F32), jax.ShapeDtypeStruct((t // tm * 8, LANES), I32)],
        compiler_params=_params("parallel"),
        name="moe_dest",
    )(sel, gate, rank, tile_counts, pad_start)


def _copy_run(src, dst, length, n_bits, make_copy, wait):
    units = jnp.right_shift(length, RUN_ALIGN.bit_length() - 1)
    for b in range(n_bits):
        offset = jnp.left_shift(jnp.right_shift(units, b + 1), b + 1) * RUN_ALIGN

        @pl.when(jnp.bitwise_and(jnp.right_shift(units, b), 1) == 1)
        def _():
            copy = make_copy(pl.multiple_of(src + offset, RUN_ALIGN), pl.multiple_of(dst + offset, RUN_ALIGN),
                             RUN_ALIGN << b)
            if wait:
                copy.wait()
            else:
                copy.start()


def _dispatch_kernel(plan_ref, runs_ref, prev_runs_ref, lpos_ref, xn_ref, xs_hbm, sorted_s, sem):
    i = pl.program_id(0)
    tm = xn_ref.shape[0]
    n_local = sorted_s.shape[1]
    slot = i % 2

    def to_slots(buf):
        return lambda src, dst, size: pltpu.make_async_copy(
            sorted_s.at[buf, pl.ds(src, size)], xs_hbm.at[pl.ds(dst, size)], sem.at[buf])

    def start_runs(table, buf):
        def body(e, carry):
            _copy_run(table[RUN_START, e], table[RUN_SLOT, e], table[RUN_LEN, e], RUN_BITS, to_slots(buf),
                      False)
            return carry
        lax.fori_loop(0, N_EXPERTS, body, 0)

    def wait_runs(table, buf):
        _copy_run(0, 0, table[RUN_TOTAL, 0], TOTAL_BITS, lambda src, dst, size: to_slots(buf)(0, 0, size),
                  True)

    position = _iota((n_local, tm), 0)
    lpos = lpos_ref[...]
    place = jnp.zeros((n_local, tm), F32)
    for k in range(TOP_K):
        place = place + jnp.where(position == lpos[k:k + 1, :], 1.0, 0.0)
    sorted_s[slot] = _dot(place.astype(BF16), xn_ref[...].astype(BF16))
    start_runs(runs_ref, slot)

    @pl.when(i > 0)
    def _():
        wait_runs(prev_runs_ref, 1 - slot)

    @pl.when(i == pl.num_programs(0) - 1)
    def _():
        wait_runs(runs_ref, slot)

    @pl.when(i == 0)
    def _():
        def padding(wait):
            def fill(src, dst, size):
                return pltpu.make_async_copy(sorted_s.at[0, pl.ds(0, size)], xs_hbm.at[pl.ds(dst, size)],
                                             sem.at[1])

            def body(e, carry):
                cnt = plan_ref[PLAN_COUNT, e]
                _copy_run(0, plan_ref[PLAN_START, e] + cnt, plan_ref[PLAN_PADDED, e] - cnt, PAD_BITS, fill,
                          wait)
                return carry
            lax.fori_loop(0, N_EXPERTS, body, 0)

        padding(False)
        padding(True)


def _dispatch(plan, runs, lpos, xn, n_slots):
    t, d = xn.shape
    tm = ROW_TILE
    return pl.pallas_call(
        _dispatch_kernel,
        grid=(t // tm,),
        in_specs=[pl.BlockSpec(memory_space=pltpu.SMEM),
                  pl.BlockSpec((8, LANES), lambda i: (i, 0), memory_space=pltpu.SMEM),
                  pl.BlockSpec((8, LANES), lambda i: (jnp.maximum(i - 1, 0), 0), memory_space=pltpu.SMEM),
                  pl.BlockSpec((8, tm), lambda i: (0, i)),
                  pl.BlockSpec((tm, d), lambda i: (i, 0))],
        out_specs=pl.BlockSpec(memory_space=pl.ANY),
        out_shape=jax.ShapeDtypeStruct((n_slots, d), F32),
        scratch_shapes=[pltpu.VMEM((2, LOCAL_ROWS, d), F32), pltpu.SemaphoreType.DMA((2,))],
        compiler_params=_params("arbitrary"),
        name="moe_dispatch",
    )(plan, runs, runs, lpos, xn)


def _ffn_kernel(be_ref, na_ref, plan_ref, x_ref, wgu_hbm, bgu_ref, wd_hbm, bd_ref, o_ref, wgu_in_s, wd_in_s,
                wgu_s, wd32_s, wd_s, sem, *, layer):
    j = pl.program_id(0)
    n_active = na_ref[0]
    active = j < n_active
    expert = be_ref[j]
    new_expert = jnp.logical_or(j == 0, expert != be_ref[jnp.maximum(j - 1, 0)])
    rows_left = plan_ref[PLAN_START, expert] + plan_ref[PLAN_COUNT, expert] - j * MOE_BLOCK

    def fetch(e):
        return (pltpu.make_async_copy(wgu_hbm.at[layer, e], wgu_in_s, sem.at[0]),
                pltpu.make_async_copy(wd_hbm.at[layer, e], wd_in_s, sem.at[1]))

    @pl.when(j == 0)
    def _():
        for copy in fetch(expert):
            copy.start()

    @pl.when(jnp.logical_and(active, new_expert))
    def _():
        for copy in fetch(expert):
            copy.wait()
        wgu_s[...] = wgu_in_s[...].astype(BF16)
        half = LANES // 2
        for c in range(wd_s.shape[0]):
            cols = slice(c * LANES, (c + 1) * LANES)
            for k in range(wd_s.shape[1] // LANES):
                for hb in range(2):
                    src = k * LANES + hb * half
                    wd32_s[c, pl.ds(k * LANES + hb, half, stride=2), :] = wd_in_s[src:src + half, cols]
            wd_s[c] = wd32_s[c].astype(BF16)
        nxt = lax.while_loop(lambda b: jnp.logical_and(b < n_active, be_ref[jnp.minimum(b, n_active - 1)] == expert),
                             lambda b: b + 1, j + 1)

        @pl.when(nxt < n_active)
        def _():
            for copy in fetch(be_ref[jnp.minimum(nxt, n_active - 1)]):
                copy.start()

    def ffn_rows(n_rows):
        x = x_ref[0:n_rows, :].astype(BF16)
        hmid = _dot(x, wgu_s[...]) + bgu_ref[0]
        m = hmid.shape[0]
        even = (_iota((m, LANES), 1) & 1) == 0
        acts = []
        for k in range(hmid.shape[1] // (2 * LANES)):
            h0 = hmid[:, (2 * k) * LANES:(2 * k + 1) * LANES]
            h1 = hmid[:, (2 * k + 1) * LANES:(2 * k + 2) * LANES]
            glu = jnp.where(even, h0, pltpu.roll(h1, 1, 1))
            lin = jnp.where(even, pltpu.roll(h0, LANES - 1, 1), h1)
            glu = jnp.minimum(glu, SWIGLU_LIMIT)
            lin = jnp.clip(lin, -SWIGLU_LIMIT, SWIGLU_LIMIT)
            acts.append((glu * _sigmoid(SWIGLU_ALPHA * glu) * (lin + 1.0)).astype(BF16))
        act = jnp.concatenate(acts, axis=1)
        wd = jnp.concatenate([wd_s[c] for c in range(wd_s.shape[0])], axis=1)
        o_ref[0:n_rows, :] = _dot(act, wd) + bd_ref[0]

    half_block = MOE_BLOCK // 2

    @pl.when(jnp.logical_and(active, rows_left > half_block))
    def _():
        ffn_rows(MOE_BLOCK)

    @pl.when(jnp.logical_and(active, rows_left <= half_block))
    def _():
        ffn_rows(half_block)


def _ffn(layer, block_expert, n_active, plan, xs, wgu, bgu, wd, bd):
    n_slots, d = xs.shape
    dff2 = wgu.shape[3]
    nb = n_slots // MOE_BLOCK
    rows = lambda j, be, na, plan_: (jnp.maximum(jnp.minimum(j, na[0] - 1), 0), 0)
    per_expert = lambda j, be, na, plan_: (layer, be[j], 0, 0)
    grid_spec = pltpu.PrefetchScalarGridSpec(
        num_scalar_prefetch=3,
        grid=(nb,),
        in_specs=[pl.BlockSpec((MOE_BLOCK, d), rows),
                  pl.BlockSpec(memory_space=pl.ANY), pl.BlockSpec((None, 1, 1, dff2), per_expert),
                  pl.BlockSpec(memory_space=pl.ANY), pl.BlockSpec((None, 1, 1, d), per_expert)],
        out_specs=pl.BlockSpec((MOE_BLOCK, d), rows),
        scratch_shapes=[pltpu.VMEM((d, dff2), F32), pltpu.VMEM((dff2 // 2, d), F32),
                        pltpu.VMEM((d, dff2), BF16), pltpu.VMEM((d // LANES, dff2 // 2, LANES), F32),
                        pltpu.VMEM((d // LANES, dff2 // 2, LANES), BF16), pltpu.SemaphoreType.DMA((2,))],
    )
    return pl.pallas_call(
        functools.partial(_ffn_kernel, layer=layer),
        grid_spec=grid_spec,
        out_shape=jax.ShapeDtypeStruct((n_slots, d), F32),
        compiler_params=_params("arbitrary"),
        name="moe_ffn",
    )(block_expert, n_active, plan, xs, wgu, bgu, wd, bd)


def _combine_kernel(runs_ref, next_runs_ref, lk_ref, gk_ref, h_ref, g_ref, ys_hbm, o_ref, sorted_s, sem, *,
                    final_norm):
    i = pl.program_id(0)
    tm = h_ref.shape[0]
    n_local = sorted_s.shape[1]
    slot = i % 2

    def from_slots(buf):
        return lambda src, dst, size: pltpu.make_async_copy(
            ys_hbm.at[pl.ds(src, size)], sorted_s.at[buf, pl.ds(dst, size)], sem.at[buf])

    def start_runs(table, buf):
        def body(e, carry):
            _copy_run(table[RUN_SLOT, e], table[RUN_START, e], table[RUN_LEN, e], RUN_BITS, from_slots(buf),
                      False)
            return carry
        lax.fori_loop(0, N_EXPERTS, body, 0)

    @pl.when(i == 0)
    def _():
        sorted_s[...] = jnp.zeros_like(sorted_s)
        start_runs(runs_ref, slot)

    @pl.when(i + 1 < pl.num_programs(0))
    def _():
        start_runs(next_runs_ref, 1 - slot)

    _copy_run(0, 0, runs_ref[RUN_TOTAL, 0], TOTAL_BITS, lambda src, dst, size: from_slots(slot)(0, 0, size),
              True)
    position = _iota((tm, n_local), 1).astype(F32)
    lk = lk_ref[...]
    gk = gk_ref[...]
    weights = jnp.zeros((tm, n_local), F32)
    for k in range(TOP_K):
        weights = weights + jnp.where(position == lk[:, k:k + 1], gk[:, k:k + 1], 0.0)
    acc = h_ref[...] + _dot(weights.astype(BF16), sorted_s[slot].astype(BF16))
    if final_norm:
        ms = jnp.mean(acc * acc, axis=-1, keepdims=True)
        acc = acc * lax.rsqrt(ms + EPS) * g_ref[...]
    o_ref[...] = acc


def _combine(runs, lk, gk, h, ys, g, final_norm):
    t, d = h.shape
    tm = ROW_TILE
    return pl.pallas_call(
        functools.partial(_combine_kernel, final_norm=final_norm),
        grid=(t // tm,),
        in_specs=[pl.BlockSpec((8, LANES), lambda i: (i, 0), memory_space=pltpu.SMEM),
                  pl.BlockSpec((8, LANES), lambda i: (jnp.minimum(i + 1, t // tm - 1), 0),
                               memory_space=pltpu.SMEM),
                  pl.BlockSpec((tm, LANES), lambda i: (i, 0)),
                  pl.BlockSpec((tm, LANES), lambda i: (i, 0)),
                  pl.BlockSpec((tm, d), lambda i: (i, 0)),
                  pl.BlockSpec((1, d), lambda i: (0, 0)),
                  pl.BlockSpec(memory_space=pl.ANY)],
        out_specs=pl.BlockSpec((tm, d), lambda i: (i, 0)),
        out_shape=jax.ShapeDtypeStruct((t, d), F32),
        scratch_shapes=[pltpu.VMEM((2, LOCAL_ROWS, d), F32), pltpu.SemaphoreType.DMA((2,))],
        compiler_params=_params("arbitrary"),
        name="moe_combine",
    )(runs, runs, lk, gk, h, g.reshape(1, d), ys)


def _moe(h, xn, sel, gate, rank, counts, tile_counts, layer, w_gu, b_gu, w_down, b_down, g_final,
         final_norm):
    t, d = h.shape
    n_slots = t * TOP_K + (t // ROW_TILE) * N_EXPERTS * RUN_ALIGN + N_EXPERTS * MOE_BLOCK
    n_blocks = n_slots // MOE_BLOCK
    pad_start, plan = _plan(counts, n_blocks)
    lpos, lk, gk, runs = _dest(sel, gate, rank, tile_counts, pad_start)
    xs = _dispatch(plan, runs, lpos, xn, n_slots)
    block_expert = plan[PLAN_EXPERTS:].reshape(-1)[:n_blocks]
    n_active = plan[PLAN_ACTIVE, :1]
    depth = w_gu.shape[0]
    ys = _ffn(layer, block_expert, n_active, plan, xs, w_gu, b_gu.reshape(depth, N_EXPERTS, 1, -1), w_down,
              b_down.reshape(depth, N_EXPERTS, 1, d))
    return _combine(runs, lk, gk, h, ys, g_final, final_norm)


def kernel(x, norm_mix_g, w_in, sb_norm_g, rg_conv_w, rg_conv_b, rg_wa, rg_ba, rg_wx, rg_bx,
           rg_lambda, rg_norm_g, hg_lower_bounds, hg_norm_g, m2_conv_w, m2_conv_b, m2_dt_bias,
           m2_a_log, m2_d, m2_norm_g, w_out, norm_ffn_g, router_w, router_b, moe_w_gu, moe_b_gu,
           moe_w_down, moe_b_down, final_norm_g):
    b, l, d = x.shape
    depth = w_in.shape[0]
    t = b * l
    lbs = jnp.cumsum(jax.nn.softmax(hg_lower_bounds.astype(F32), axis=0), axis=0)
    lbs = lbs - lbs[0]
    d_in = w_in.shape[2]
    w_in_p = jnp.zeros((depth, d, PROJ_BLOCKS * GROUP_W), BF16).at[:, :, :d_in].set(w_in.astype(BF16))
    h = x.reshape(t, d)
    for layer in range(depth):
        proj = _norm_inproj(h, norm_mix_g[layer], w_in_p[layer]).reshape(b, l, PROJ_BLOCKS * GROUP_W)
        o_a = _sb_attn(proj, sb_norm_g[layer])
        o_b = _rglru(proj, rg_conv_w[layer], rg_conv_b[layer], rg_wa[layer], rg_ba[layer],
                     rg_wx[layer], rg_bx[layer], rg_lambda[layer], rg_norm_g[layer])
        o_c = _hgrn2(proj, lbs[layer], hg_norm_g[layer])
        o_d = _ssd(proj, m2_conv_w[layer], m2_conv_b[layer], m2_dt_bias[layer], m2_a_log[layer],
                   m2_d[layer], m2_norm_g[layer])
        h, xn, sel, gate, rank, counts, tile_counts = _outproj_router(
            h, (o_a, o_b, o_c, o_d), w_out[layer], norm_ffn_g[layer], router_w[layer],
            router_b[layer])
        h = _moe(h, xn, sel, gate, rank, counts, tile_counts, layer, moe_w_gu, moe_b_gu, moe_w_down,
                 moe_b_down, final_norm_g, layer == depth - 1)
    return h.reshape(b, l, d)
```

```python
import functools
import math

import jax
import jax.numpy as jnp
from jax import lax
from jax.experimental import pallas as pl
from jax.experimental.pallas import tpu as pltpu

F32 = jnp.float32
BF16 = jnp.bfloat16
I32 = jnp.int32
HIGHEST = lax.Precision.HIGHEST
EPS = 1e-6
EXP_UNDERFLOW = -104.0

LANES = 128
GROUP_W = 256
HEAD_DIM = 64
N_HEADS = 4
SSD_STATE = 128
RG_C = 8.0
N_EXPERTS = 32
TOP_K = 4
SWIGLU_LIMIT = 7.0
SWIGLU_ALPHA = 1.702
PROJ_BLOCKS = 14
VMEM_LIMIT = 56 * 1024 * 1024

ROW_TILE = 256
INPROJ_TILE = 512
ATTN_BLOCK = 256
SCAN_TILE = 256
HG_CHUNK = 16
HG_BLOCK = 256
SSD_CHUNK = LANES
MOE_BLOCK = 512
HG_UNROLL = 8

PLAN_ACTIVE, PLAN_COUNT, PLAN_START, PLAN_PADDED, PLAN_EXPERTS = 0, 1, 2, 3, 4
PLAN_EXPERT_ROWS = 8 - PLAN_EXPERTS
RUN_LEN, RUN_START, RUN_SLOT, RUN_TOTAL = 0, 1, 2, 3
RUN_ALIGN = 8
RUN_BITS = (ROW_TILE // RUN_ALIGN).bit_length()
PAD_BITS = (MOE_BLOCK // RUN_ALIGN - 1).bit_length()
LOCAL_ROWS = ROW_TILE * TOP_K + N_EXPERTS * RUN_ALIGN
TOTAL_BITS = (LOCAL_ROWS // RUN_ALIGN).bit_length()


def _params(*sem):
    return pltpu.CompilerParams(dimension_semantics=sem, vmem_limit_bytes=VMEM_LIMIT)


def _dot(a, b):
    return jnp.dot(a, b, preferred_element_type=F32)


def _dot_hi(a, b):
    return jnp.dot(a, b, preferred_element_type=F32, precision=HIGHEST)


def _split3(x):
    hi = x.astype(BF16)
    rest = x - hi.astype(F32)
    mid = rest.astype(BF16)
    lo = (rest - mid.astype(F32)).astype(BF16)
    return hi, mid, lo


def _dot_exact_lhs(m, x):
    return sum(_dot(m, part) for part in _split3(x))


def _dot_exact_rhs(x, m):
    return sum(_dot(part, m) for part in _split3(x))


def _dot_nt(a, b, precision=None):
    return lax.dot_general(a, b, (((1,), (1,)), ((), ())), preferred_element_type=F32,
                           precision=precision)


def _dot_tn(a, b):
    return lax.dot_general(a, b, (((0,), (0,)), ((), ())), preferred_element_type=F32)


def _sigmoid(x):
    return 1.0 / (1.0 + jnp.exp(-x))


def _silu(x):
    return x * _sigmoid(x)


def _softplus(x):
    return jnp.maximum(x, 0.0) + jnp.log(1.0 + jnp.exp(-jnp.abs(x)))


def _iota(shape, dim):
    return lax.broadcasted_iota(I32, shape, dim)


def _div_pow2(x, n):
    assert n & (n - 1) == 0
    return jnp.right_shift(x, int(math.log2(n)))


def _shift_rows(x, d, prev8):
    r = pltpu.roll(x, d, 0)
    p = pltpu.roll(prev8, d, 0)
    head = jnp.where(_iota(prev8.shape, 0) < d, p, r[:8])
    return jnp.concatenate([head, r[8:]], axis=0)


def _causal_conv(x, prev8, w, b):
    k = w.shape[0]
    y = x * w[k - 1:k] + b
    for d in range(1, k):
        y = y + _shift_rows(x, d, prev8) * w[k - 1 - d:k - d]
    return y


def _norm_inproj_kernel(x_ref, g_ref, w_ref, o_ref):
    x = x_ref[...]
    ms = jnp.mean(x * x, axis=-1, keepdims=True)
    xn = (x * lax.rsqrt(ms + EPS) * g_ref[...]).astype(BF16)
    o_ref[...] = _dot(xn, w_ref[...])


def _norm_inproj(h, g, w):
    t, d = h.shape
    n = w.shape[1]
    return pl.pallas_call(
        _norm_inproj_kernel,
        grid=(t // INPROJ_TILE,),
        in_specs=[pl.BlockSpec((INPROJ_TILE, d), lambda i: (i, 0)),
                  pl.BlockSpec((1, d), lambda i: (0, 0)),
                  pl.BlockSpec((d, n), lambda i: (0, 0))],
        out_specs=pl.BlockSpec((INPROJ_TILE, n), lambda i: (i, 0)),
        out_shape=jax.ShapeDtypeStruct((t, n), F32),
        compiler_params=_params("parallel"),
        name="norm_inproj",
    )(h, g.reshape(1, d), w)


def _sb_attn_kernel(q_ref, k_ref, v_ref, g_ref, o_ref, acc_ref, run_ref):
    i = pl.program_id(1)
    blk = ATTN_BLOCK
    scale = HEAD_DIM ** -0.5
    row = _iota((blk, blk), 0)
    col = _iota((blk, blk), 1)
    below = col < row
    r_w = _iota((blk, blk + LANES), 0)
    c_w = _iota((blk, blk + LANES), 1)
    later_sum = jnp.where(jnp.logical_or(c_w >= blk, r_w > c_w), 1.0, 0.0).astype(BF16)
    acc_ref[...] = jnp.zeros_like(acc_ref)
    run_ref[...] = jnp.zeros_like(run_ref)
    qs = [(q_ref[0, :, h * HEAD_DIM:(h + 1) * HEAD_DIM] * scale).astype(BF16) for h in range(N_HEADS)]

    def key_block(j, diagonal):
        ks = pl.multiple_of(j * blk, blk)
        for h in range(N_HEADS):
            sl = slice(h * HEAD_DIM, (h + 1) * HEAD_DIM)
            kh = k_ref[0, pl.ds(ks, blk), sl].astype(BF16)
            vh = v_ref[0, pl.ds(ks, blk), sl].astype(BF16)
            z = _dot_nt(qs[h], kh)
            ls = jnp.minimum(z, 0.0) - jnp.log(1.0 + jnp.exp(-jnp.abs(z)))
            lk = ls - z
            if diagonal:
                lk = jnp.where(below, lk, 0.0)
            lk_hi = lk.astype(BF16)
            lk_lo = (lk - lk_hi.astype(F32)).astype(BF16)
            sums = _dot(lk_hi, later_sum) + _dot(lk_lo, later_sum)
            run = run_ref[h]
            rem = sums[:, :blk] + jnp.concatenate([run] * (blk // LANES), axis=1)
            a = jnp.exp(ls + rem)
            if diagonal:
                a = jnp.where(below, a, 0.0)
            acc_ref[:, sl] += _dot(a.astype(BF16), vh)
            run_ref[h] = run + sums[:, blk:]

    key_block(i, True)

    def live():
        return jnp.max(run_ref[...]) > EXP_UNDERFLOW

    def more(carry):
        jj, alive = carry
        return jnp.logical_and(jj <= i, alive)

    def body(carry):
        jj, _ = carry
        key_block(i - jj, False)
        return jj + 1, live()

    lax.while_loop(more, body, (jnp.int32(1), live()))
    o = acc_ref[...]
    ms = jnp.mean(o * o, axis=-1, keepdims=True)
    o_ref[0] = o * lax.rsqrt(ms + EPS) * g_ref[...]


def _sb_attn(proj, g):
    b, l, _ = proj.shape
    blk = ATTN_BLOCK
    return pl.pallas_call(
        _sb_attn_kernel,
        grid=(b, l // blk),
        in_specs=[pl.BlockSpec((1, blk, GROUP_W), lambda bi, i: (bi, i, 0)),
                  pl.BlockSpec((1, l, GROUP_W), lambda bi, i: (bi, 0, 1)),
                  pl.BlockSpec((1, l, GROUP_W), lambda bi, i: (bi, 0, 2)),
                  pl.BlockSpec((1, GROUP_W), lambda bi, i: (0, 0))],
        out_specs=pl.BlockSpec((1, blk, GROUP_W), lambda bi, i: (bi, i, 0)),
        out_shape=jax.ShapeDtypeStruct((b, l, GROUP_W), F32),
        scratch_shapes=[pltpu.VMEM((blk, GROUP_W), F32), pltpu.VMEM((N_HEADS, blk, LANES), F32)],
        compiler_params=_params("parallel", "arbitrary"),
        name="sb_attn",
    )(proj, proj, proj, g.reshape(1, GROUP_W))


def _rglru_kernel(x_ref, gate_ref, cw_ref, cb_ref, wa_ref, ba_ref, wx_ref, bx_ref, lam_ref,
                  ng_ref, o_ref):
    l = x_ref.shape[1]
    w = x_ref.shape[2]
    tc = SCAN_TILE
    rows = _iota((tc, w), 0)
    cw = cw_ref[...]
    cb = cb_ref[...]
    neg_c_sp = -RG_C * _softplus(-lam_ref[...])

    def chunk(c, carry):
        h_prev, tail = carry
        s = pl.multiple_of(c * tc, tc)
        x = x_ref[0, pl.ds(s, tc), :]
        xc = _causal_conv(x, tail, cw, cb)
        xcb = xc.astype(BF16)
        r = _sigmoid(_dot(xcb, wa_ref[...]) + ba_ref[...])
        ig = _sigmoid(_dot(xcb, wx_ref[...]) + bx_ref[...])
        log_a = r * neg_c_sp
        a = jnp.exp(log_a)
        th = jnp.tanh(log_a)
        mult = jnp.sqrt(-2.0 * th / (1.0 - th))
        mult = jnp.where(rows + s == 0, 1.0, mult)
        u = mult * ig * xc
        d = 1
        while d < tc:
            keep = rows >= d
            a_s = jnp.where(keep, pltpu.roll(a, d, 0), 1.0)
            u_s = jnp.where(keep, pltpu.roll(u, d, 0), 0.0)
            u = a * u_s + u
            a = a * a_s
            d *= 2
        hs = u + a * h_prev
        gate = gate_ref[0, pl.ds(s, tc), :]
        gelu = 0.5 * gate * (1.0 + jnp.tanh(math.sqrt(2.0 / math.pi) * (gate + 0.044715 * gate * gate * gate)))
        o = hs * gelu
        ms = jnp.mean(o * o, axis=-1, keepdims=True)
        o_ref[0, pl.ds(s, tc), :] = o * lax.rsqrt(ms + EPS) * ng_ref[...]
        return hs[tc - 1:tc], x[tc - 8:tc]

    lax.fori_loop(0, l // tc, chunk, (jnp.zeros((1, w), F32), jnp.zeros((8, w), F32)))


def _block_diag(wb):
    n, d, _ = wb.shape
    same = jnp.eye(n, dtype=bool)[:, None, :, None]
    return jnp.where(same, wb[:, :, None, :], 0).reshape(n * d, n * d)


def _rglru(proj, conv_w, conv_b, wa, ba, wx, bx, lam, ng):
    b, l, _ = proj.shape
    w = GROUP_W
    row = lambda v: v.reshape(1, w)
    full = lambda shape: pl.BlockSpec(shape, lambda bi: (0,) * len(shape))
    return pl.pallas_call(
        _rglru_kernel,
        grid=(b,),
        in_specs=[pl.BlockSpec((1, l, w), lambda bi: (bi, 0, 3)),
                  pl.BlockSpec((1, l, w), lambda bi: (bi, 0, 4)),
                  full((conv_w.shape[0], w)), full((1, w)), full((w, w)), full((1, w)),
                  full((w, w)), full((1, w)), full((1, w)), full((1, w))],
        out_specs=pl.BlockSpec((1, l, w), lambda bi: (bi, 0, 0)),
        out_shape=jax.ShapeDtypeStruct((b, l, w), F32),
        compiler_params=_params("parallel"),
        name="rglru",
    )(proj, proj, conv_w, row(conv_b), _block_diag(wa).astype(BF16), row(ba),
      _block_diag(wx).astype(BF16), row(bx), row(lam), row(ng))


def _hgrn2_kernel(q_ref, f_ref, i_ref, g_ref, lb_ref, ng_ref, o_ref,
                  qe_s, ke_s, kl_s, dl_s, o_s, inc_s):
    l = q_ref.shape[1]
    w = q_ref.shape[2]
    tile = SCAN_TILE
    c = HG_CHUNK
    r_t = _iota((tile, tile), 0)
    c_t = _iota((tile, tile), 1)
    same = _div_pow2(r_t, c) == _div_pow2(c_t, c)
    cum_m = jnp.where(jnp.logical_and(same, c_t <= r_t), 1.0, 0.0).astype(BF16)
    tot_m = jnp.where(same, 1.0, 0.0).astype(BF16)
    lb = lb_ref[...]

    def prep(t, carry):
        s = pl.multiple_of(t * tile, tile)
        forget = lb + (1.0 - lb) * _sigmoid(f_ref[0, pl.ds(s, tile), :])
        log_f = jnp.log(forget)
        key = 1.0 - forget
        parts = _split3(log_f)
        bcum = sum(_dot(cum_m, part) for part in parts)
        blast = sum(_dot(tot_m, part) for part in parts)
        q = _silu(q_ref[0, pl.ds(s, tile), :])
        qe_s[pl.ds(s, tile), :] = (q * jnp.exp(bcum)).astype(BF16)
        ke_s[pl.ds(s, tile), :] = (key * jnp.exp(-bcum)).astype(BF16)
        kl_s[pl.ds(s, tile), :] = key * jnp.exp(blast - bcum)
        dl_s[pl.ds(s, tile), :] = jnp.exp(blast)
        return carry

    lax.fori_loop(0, l // tile, prep, 0)

    blk = HG_BLOCK
    per = blk // c
    within = jnp.logical_and(_div_pow2(_iota((blk, blk), 0), c) == _div_pow2(_iota((blk, blk), 1), c),
                             _iota((blk, blk), 1) <= _iota((blk, blk), 0))
    own = (_div_pow2(_iota((blk, per * HEAD_DIM), 0), c)
           == _div_pow2(_iota((blk, per * HEAD_DIM), 1), HEAD_DIM))

    def local(m, carry):
        s = pl.multiple_of(m * blk, blk)
        qe = qe_s[pl.ds(s, blk), :]
        ke = ke_s[pl.ds(s, blk), :]
        kl = kl_s[pl.ds(s, blk), :]
        v = i_ref[0, pl.ds(s, blk), :]
        vb = v.astype(BF16)
        outs = []
        for h in range(N_HEADS):
            sl = slice(h * HEAD_DIM, (h + 1) * HEAD_DIM)
            sc = jnp.where(within, _dot_nt(qe[:, sl], ke[:, sl]), 0.0)
            outs.append(_dot(sc.astype(BF16), vb[:, sl]))
            wide = jnp.where(own, jnp.concatenate([kl[:, sl]] * per, axis=1), 0.0)
            inc = _dot_tn(v[:, sl], wide)
            for j in range(per):
                inc_s[m * per + j, :, sl] = inc[:, j * HEAD_DIM:(j + 1) * HEAD_DIM]
        o_s[pl.ds(s, blk), :] = jnp.concatenate(outs, axis=1)
        return carry

    lax.fori_loop(0, l // blk, local, 0)

    def step(n, state):
        s = pl.multiple_of(n * c, c)
        qe = qe_s[pl.ds(s, c), :]
        dl = dl_s[pl.ds(s, 8), :][0:1]
        sb = state.astype(BF16)
        outs = [_dot_nt(qe[:, h * HEAD_DIM:(h + 1) * HEAD_DIM], sb[:, h * HEAD_DIM:(h + 1) * HEAD_DIM])
                for h in range(N_HEADS)]
        o_s[pl.ds(s, c), :] += jnp.concatenate(outs, axis=1)
        return dl * state + inc_s[n]

    lax.fori_loop(0, l // c, step, jnp.zeros((HEAD_DIM, w), F32), unroll=HG_UNROLL)

    head_avg = jnp.where(_div_pow2(_iota((w, w), 0), HEAD_DIM) == _div_pow2(_iota((w, w), 1), HEAD_DIM),
                         1.0 / HEAD_DIM, 0.0).astype(BF16)

    def finish(t, carry):
        s = pl.multiple_of(t * tile, tile)
        o = o_s[pl.ds(s, tile), :]
        ms = _dot_exact_rhs(o * o, head_avg)
        o_ref[0, pl.ds(s, tile), :] = (o * lax.rsqrt(ms + EPS) * ng_ref[...]
                                       * _silu(g_ref[0, pl.ds(s, tile), :]))
        return carry

    lax.fori_loop(0, l // tile, finish, 0)


def _hgrn2(proj, lb, ng):
    b, l, _ = proj.shape
    w = GROUP_W
    full = lambda shape: pl.BlockSpec(shape, lambda bi: (0,) * len(shape))
    col = lambda j: pl.BlockSpec((1, l, w), lambda bi: (bi, 0, j))
    return pl.pallas_call(
        _hgrn2_kernel,
        grid=(b,),
        in_specs=[col(5), col(6), col(7), col(8), full((1, w)), full((1, w))],
        out_specs=pl.BlockSpec((1, l, w), lambda bi: (bi, 0, 0)),
        out_shape=jax.ShapeDtypeStruct((b, l, w), F32),
        scratch_shapes=[pltpu.VMEM((l, w), BF16), pltpu.VMEM((l, w), BF16), pltpu.VMEM((l, w), F32),
                        pltpu.VMEM((l, w), F32), pltpu.VMEM((l, w), F32),
                        pltpu.VMEM((l // HG_CHUNK, HEAD_DIM, w), F32)],
        compiler_params=_params("parallel"),
        name="hgrn2",
    )(proj, proj, proj, proj, lb.reshape(1, w), ng.reshape(1, w))


def _ssd_kernel(z_ref, x_ref, bm_ref, cm_ref, dt_ref, cw_ref, cb_ref, dtb_ref, aneg_ref,
                dskip_ref, ng_ref, o_ref):
    l = z_ref.shape[1]
    w = GROUP_W
    q = SSD_CHUNK
    p = HEAD_DIM
    r_q = _iota((q, q), 0)
    c_q = _iota((q, q), 1)
    tril = c_q <= r_q
    cum_m = jnp.where(tril, 1.0, 0.0).astype(BF16)
    cw = cw_ref[...]
    cb = cb_ref[...]

    def chunk(n, carry):
        state, tails = carry
        s = pl.multiple_of(n * q, q)
        parts = []
        new_tails = []
        for j, ref in enumerate((x_ref, bm_ref, cm_ref)):
            raw = ref[0, pl.ds(s, q), :]
            sl = slice(j * w, (j + 1) * w)
            parts.append(_silu(_causal_conv(raw, tails[j], cw[:, sl], cb[:, sl])))
            new_tails.append(raw[q - 8:q])
        xs, bm, cm = parts
        dt = _softplus(dt_ref[0, pl.ds(s, q), :] + dtb_ref[...])
        dt_rep = jnp.concatenate([jnp.broadcast_to(dt[:, h:h + 1], (q, p)) for h in range(N_HEADS)],
                                 axis=1)
        a_cs = _dot_exact_lhs(cum_m, dt_rep * aneg_ref[...])
        a_rows = [jnp.transpose(a_cs[:, g * LANES:(g + 1) * LANES]) for g in range(w // LANES)]
        xd = xs * dt_rep
        heads_per_group = N_HEADS // (w // SSD_STATE)
        cgs = [cm[:, g * SSD_STATE:(g + 1) * SSD_STATE].astype(BF16) for g in range(w // SSD_STATE)]
        cbs = [_dot_nt(cgs[g], bm[:, g * SSD_STATE:(g + 1) * SSD_STATE].astype(BF16))
               for g in range(w // SSD_STATE)]
        ys = []
        new_state = []
        for h in range(N_HEADS):
            grp = h // heads_per_group
            gs = slice(grp * SSD_STATE, (grp + 1) * SSD_STATE)
            hs = slice(h * p, (h + 1) * p)
            cg = cgs[grp]
            row = (h * p) % LANES
            a_row = a_rows[(h * p) // LANES][row:row + 1, :]
            a_col = a_cs[:, h * p:h * p + 1]
            decay = jnp.exp(jnp.where(tril, a_col - a_row, -jnp.inf))
            y = _dot((cbs[grp] * decay).astype(BF16), xd[:, hs].astype(BF16))
            st = state[h]
            y = y + jnp.exp(a_cs[:, hs]) * _dot_nt(cg, st.astype(BF16))
            a_last = a_cs[q - 1:q, hs]
            to_end = jnp.exp(a_last - a_cs[:, hs])
            upd = _dot_tn(xd[:, hs] * to_end, bm[:, gs])
            chunk_decay = jnp.exp(a_row[:, q - 1:q])
            new_state.append(chunk_decay * st + upd)
            ys.append(y)
        y = jnp.concatenate(ys, axis=1) + xs * dskip_ref[...]
        y = y * _silu(z_ref[0, pl.ds(s, q), :])
        halves = []
        for gi in range(2):
            yg = y[:, gi * LANES:(gi + 1) * LANES]
            ms = jnp.mean(yg * yg, axis=-1, keepdims=True)
            halves.append(yg * lax.rsqrt(ms + EPS))
        o_ref[0, pl.ds(s, q), :] = jnp.concatenate(halves, axis=1) * ng_ref[...]
        return tuple(new_state), tuple(new_tails)

    zero = jnp.zeros((p, SSD_STATE), F32)
    tail0 = jnp.zeros((8, w), F32)
    lax.fori_loop(0, l // q, chunk, ((zero,) * N_HEADS, (tail0,) * 3), unroll=2)


def _ssd(proj, conv_w, conv_b, dt_bias, a_log, d_skip, ng):
    b, l, _ = proj.shape
    w = GROUP_W
    full = lambda shape: pl.BlockSpec(shape, lambda bi: (0,) * len(shape))
    col = lambda j: pl.BlockSpec((1, l, w), lambda bi: (bi, 0, j))
    pad_heads = lambda v: jnp.zeros((1, w), F32).at[0, :N_HEADS].set(v)
    return pl.pallas_call(
        _ssd_kernel,
        grid=(b,),
        in_specs=[col(9), col(10), col(11), col(12), col(13),
                  full((conv_w.shape[0], 3 * w)), full((1, 3 * w)), full((1, w)), full((1, w)),
                  full((1, w)), full((1, w))],
        out_specs=pl.BlockSpec((1, l, w), lambda bi: (bi, 0, 0)),
        out_shape=jax.ShapeDtypeStruct((b, l, w), F32),
        compiler_params=_params("parallel"),
        name="ssd",
    )(proj, proj, proj, proj, proj, conv_w, conv_b.reshape(1, 3 * w), pad_heads(dt_bias),
      jnp.repeat(-jnp.exp(a_log), HEAD_DIM).reshape(1, w), jnp.repeat(d_skip, HEAD_DIM).reshape(1, w),
      ng.reshape(1, w))


def _outproj_router_kernel(h_ref, oa_ref, ob_ref, oc_ref, od_ref, wo_ref, g_ref, rwh_ref, rwl_ref, rb_ref,
                           h_out, xn_out, sel_out, gate_out, rank_out, cnt_out, tile_out, cnt_s):
    i = pl.program_id(0)
    tm = h_ref.shape[0]

    @pl.when(i == 0)
    def _():
        cnt_s[...] = jnp.zeros_like(cnt_s)

    acc = h_ref[...]
    for j, ref in enumerate((oa_ref, ob_ref, oc_ref, od_ref)):
        acc = acc + _dot(ref[...].astype(BF16), wo_ref[j * GROUP_W:(j + 1) * GROUP_W, :])
    h_out[...] = acc
    ms = jnp.mean(acc * acc, axis=-1, keepdims=True)
    xn = acc * lax.rsqrt(ms + EPS) * g_ref[...]
    xn_out[...] = xn
    lane = _iota((tm, LANES), 1)
    lane_f = lane.astype(F32)
    xn_hi = xn.astype(BF16)
    xn_lo = (xn - xn_hi.astype(F32)).astype(BF16)
    router = _dot(xn_hi, rwh_ref[...]) + _dot(xn_lo, rwh_ref[...]) + _dot(xn_hi, rwl_ref[...])
    logits = jnp.where(lane < N_EXPERTS, router + rb_ref[...], -jnp.inf)
    sel = jnp.zeros((tm, LANES), F32)
    tops = []
    for k in range(TOP_K):
        m = jnp.max(logits, axis=-1, keepdims=True)
        idx = jnp.min(jnp.where(logits == m, lane_f, float(LANES)), axis=-1, keepdims=True)
        hit = lane_f == idx
        sel = jnp.where(hit, float(k + 1), sel)
        logits = jnp.where(hit, -jnp.inf, logits)
        tops.append(m)
    exps = [jnp.exp(m - tops[0]) for m in tops]
    denom = exps[0] + exps[1] + exps[2] + exps[3]
    gate = jnp.zeros((tm, LANES), F32)
    for k in range(TOP_K):
        gate = jnp.where(sel == float(k + 1), exps[k] / denom, gate)
    sel_out[...] = sel
    gate_out[...] = gate
    member = jnp.where(sel > 0.0, 1.0, 0.0)
    before = jnp.where(_iota((tm, tm), 1) < _iota((tm, tm), 0), 1.0, 0.0).astype(BF16)
    carry = cnt_s[0:1, :]
    rank_out[...] = _dot(before, member.astype(BF16))
    in_tile = jnp.ceil(jnp.sum(member, axis=0, keepdims=True) * (1.0 / RUN_ALIGN)) * RUN_ALIGN
    total = carry + in_tile
    tile_out[...] = jnp.concatenate([carry, in_tile, jnp.zeros((6, LANES), F32)], axis=0)
    cnt_s[...] = jnp.broadcast_to(total, cnt_s.shape)
    cnt_out[...] = jnp.broadcast_to(total, cnt_out.shape)


def _outproj_router(h, outs, w_out, g, router_w, router_b):
    t, d = h.shape
    tm = ROW_TILE
    rw = jnp.zeros((d, LANES), F32).at[:, :N_EXPERTS].set(router_w)
    rb = jnp.zeros((1, LANES), F32).at[0, :N_EXPERTS].set(router_b)
    rw_hi = rw.astype(BF16)
    rw_lo = (rw - rw_hi.astype(F32)).astype(BF16)
    tile = lambda width: pl.BlockSpec((tm, width), lambda i: (i, 0))
    full = lambda shape: pl.BlockSpec(shape, lambda i: (0,) * len(shape))
    flat = [o.reshape(t, GROUP_W) for o in outs]
    return pl.pallas_call(
        _outproj_router_kernel,
        grid=(t // tm,),
        in_specs=[tile(d)] + [tile(GROUP_W)] * 4 + [full((d, d)), full((1, d)), full((d, LANES)),
                                                   full((d, LANES)), full((1, LANES))],
        out_specs=[tile(d), tile(d), tile(LANES), tile(LANES), tile(LANES), full((8, LANES)),
                   pl.BlockSpec((8, LANES), lambda i: (i, 0))],
        out_shape=[jax.ShapeDtypeStruct((t, d), F32), jax.ShapeDtypeStruct((t, d), F32),
                   jax.ShapeDtypeStruct((t, LANES), F32), jax.ShapeDtypeStruct((t, LANES), F32),
                   jax.ShapeDtypeStruct((t, LANES), F32), jax.ShapeDtypeStruct((8, LANES), F32),
                   jax.ShapeDtypeStruct((t // tm * 8, LANES), F32)],
        scratch_shapes=[pltpu.VMEM((8, LANES), F32)],
        compiler_params=_params("arbitrary"),
        name="outproj_router",
    )(h, *flat, w_out.astype(BF16), g.reshape(1, d), rw_hi, rw_lo, rb)


def _plan_kernel(cnt_ref, start_out, plan_out):
    cnt = cnt_ref[...]
    padded = jnp.ceil(cnt * (1.0 / MOE_BLOCK)) * MOE_BLOCK
    upto = jnp.where(_iota((LANES, LANES), 0) <= _iota((LANES, LANES), 1), 1.0, 0.0)
    pad_end = _dot_hi(padded, upto)
    pad_start = pad_end - padded
    start_out[...] = pad_start
    ends = jnp.transpose(jnp.broadcast_to(pad_end[0:1, :], (LANES, LANES)))
    expert_ok = _iota((LANES, LANES), 0) < N_EXPERTS
    n_active = pad_end[0:1, N_EXPERTS - 1:N_EXPERTS] * (1.0 / MOE_BLOCK)
    rows = [jnp.broadcast_to(n_active, (1, LANES)), cnt[0:1, :], pad_start[0:1, :], padded[0:1, :]]
    for r in range(PLAN_EXPERT_ROWS):
        blk_start = ((_iota((LANES, LANES), 1) + r * LANES) * MOE_BLOCK).astype(F32)
        done = jnp.where(jnp.logical_and(expert_ok, ends <= blk_start), 1.0, 0.0)
        rows.append(jnp.minimum(jnp.sum(done, axis=0, keepdims=True), N_EXPERTS - 1.0))
    plan_out[...] = jnp.concatenate(rows, axis=0).astype(I32)


def _plan(counts, n_blocks):
    assert n_blocks <= PLAN_EXPERT_ROWS * LANES
    return pl.pallas_call(
        _plan_kernel,
        out_shape=[jax.ShapeDtypeStruct((8, LANES), F32), jax.ShapeDtypeStruct((8, LANES), I32)],
        name="moe_plan",
    )(counts)


def _dest_kernel(sel_ref, gate_ref, rank_ref, tile_ref, start_ref, lpos_out, lk_out, gk_out, runs_out):
    tm = sel_ref.shape[0]
    sel = sel_ref[...]
    gate = gate_ref[...]
    before_tile = tile_ref[0:1, :]
    in_tile = tile_ref[1:2, :]
    earlier = jnp.where(_iota((LANES, LANES), 0) < _iota((LANES, LANES), 1), 1.0, 0.0)
    run_start = _dot_hi(jnp.broadcast_to(in_tile, (8, LANES)), earlier)[0:1, :]
    lpos = rank_ref[...] + run_start
    lane = _iota((tm, LANES), 1)
    pieces = []
    gk = jnp.zeros((tm, LANES), F32)
    lk = jnp.zeros((tm, LANES), F32)
    for k in range(TOP_K):
        hit = sel == float(k + 1)
        mine = jnp.where(hit, lpos, 0.0)
        pieces.append(mine)
        gk = jnp.where(lane == k, jnp.sum(jnp.where(hit, gate, 0.0), axis=-1, keepdims=True), gk)
        lk = jnp.where(lane == k, jnp.sum(mine, axis=-1, keepdims=True), lk)
    stacked = jnp.concatenate(pieces, axis=1)
    picker = jnp.where(_div_pow2(_iota((8, TOP_K * LANES), 1), LANES) == _iota((8, TOP_K * LANES), 0),
                       1.0, 0.0).astype(BF16)
    lpos_out[...] = sum(_dot_nt(picker, part) for part in _split3(stacked)).astype(I32)
    lk_out[...] = lk
    gk_out[...] = gk
    first_slot = start_ref[0:1, :] + before_tile
    total = jnp.broadcast_to(jnp.sum(in_tile, axis=-1, keepdims=True), (1, LANES))
    runs_out[...] = jnp.concatenate([in_tile, run_start, first_slot, total, jnp.zeros((4, LANES), F32)],
                                    axis=0).astype(I32)


def _dest(sel, gate, rank, tile_counts, pad_start):
    t = sel.shape[0]
    tm = ROW_TILE
    tile = pl.BlockSpec((tm, LANES), lambda i: (i, 0))
    per_tile = pl.BlockSpec((8, LANES), lambda i: (i, 0))
    return pl.pallas_call(
        _dest_kernel,
        grid=(t // tm,),
        in_specs=[tile, tile, tile, per_tile, pl.BlockSpec((8, LANES), lambda i: (0, 0))],
        out_specs=[pl.BlockSpec((8, tm), lambda i: (0, i)), tile, tile, per_tile],
        out_shape=[jax.ShapeDtypeStruct((8, t), I32), jax.ShapeDtypeStruct((t, LANES), F32),
                   jax.ShapeDtypeStruct((t, LANES), F32), jax.ShapeDtypeStruct((t // tm * 8, LANES), I32)],
        compiler_params=_params("parallel"),
        name="moe_dest",
    )(sel, gate, rank, tile_counts, pad_start)


def _copy_run(src, dst, length, n_bits, make_copy, wait):
    units = jnp.right_shift(length, RUN_ALIGN.bit_length() - 1)
    for b in range(n_bits):
        offset = jnp.left_shift(jnp.right_shift(units, b + 1), b + 1) * RUN_ALIGN

        @pl.when(jnp.bitwise_and(jnp.right_shift(units, b), 1) == 1)
        def _():
            copy = make_copy(pl.multiple_of(src + offset, RUN_ALIGN), pl.multiple_of(dst + offset, RUN_ALIGN),
                             RUN_ALIGN << b)
            if wait:
                copy.wait()
            else:
                copy.start()


def _dispatch_kernel(plan_ref, runs_ref, prev_runs_ref, lpos_ref, xn_ref, xs_hbm, sorted_s, sem):
    i = pl.program_id(0)
    tm = xn_ref.shape[0]
    n_local = sorted_s.shape[1]
    slot = i % 2

    def to_slots(buf):
        return lambda src, dst, size: pltpu.make_async_copy(
            sorted_s.at[buf, pl.ds(src, size)], xs_hbm.at[pl.ds(dst, size)], sem.at[buf])

    def start_runs(table, buf):
        def body(e, carry):
            _copy_run(table[RUN_START, e], table[RUN_SLOT, e], table[RUN_LEN, e], RUN_BITS, to_slots(buf),
                      False)
            return carry
        lax.fori_loop(0, N_EXPERTS, body, 0)

    def wait_runs(table, buf):
        _copy_run(0, 0, table[RUN_TOTAL, 0], TOTAL_BITS, lambda src, dst, size: to_slots(buf)(0, 0, size),
                  True)

    position = _iota((n_local, tm), 0)
    lpos = lpos_ref[...]
    place = jnp.zeros((n_local, tm), F32)
    for k in range(TOP_K):
        place = place + jnp.where(position == lpos[k:k + 1, :], 1.0, 0.0)
    sorted_s[slot] = _dot(place.astype(BF16), xn_ref[...].astype(BF16))
    start_runs(runs_ref, slot)

    @pl.when(i > 0)
    def _():
        wait_runs(prev_runs_ref, 1 - slot)

    @pl.when(i == pl.num_programs(0) - 1)
    def _():
        wait_runs(runs_ref, slot)

    @pl.when(i == 0)
    def _():
        def padding(wait):
            def fill(src, dst, size):
                return pltpu.make_async_copy(sorted_s.at[0, pl.ds(0, size)], xs_hbm.at[pl.ds(dst, size)],
                                             sem.at[1])

            def body(e, carry):
                cnt = plan_ref[PLAN_COUNT, e]
                _copy_run(0, plan_ref[PLAN_START, e] + cnt, plan_ref[PLAN_PADDED, e] - cnt, PAD_BITS, fill,
                          wait)
                return carry
            lax.fori_loop(0, N_EXPERTS, body, 0)

        padding(False)
        padding(True)


def _dispatch(plan, runs, lpos, xn, n_slots):
    t, d = xn.shape
    tm = ROW_TILE
    return pl.pallas_call(
        _dispatch_kernel,
        grid=(t // tm,),
        in_specs=[pl.BlockSpec(memory_space=pltpu.SMEM),
                  pl.BlockSpec((8, LANES), lambda i: (i, 0), memory_space=pltpu.SMEM),
                  pl.BlockSpec((8, LANES), lambda i: (jnp.maximum(i - 1, 0), 0), memory_space=pltpu.SMEM),
                  pl.BlockSpec((8, tm), lambda i: (0, i)),
                  pl.BlockSpec((tm, d), lambda i: (i, 0))],
        out_specs=pl.BlockSpec(memory_space=pl.ANY),
        out_shape=jax.ShapeDtypeStruct((n_slots, d), F32),
        scratch_shapes=[pltpu.VMEM((2, LOCAL_ROWS, d), F32), pltpu.SemaphoreType.DMA((2,))],
        compiler_params=_params("arbitrary"),
        name="moe_dispatch",
    )(plan, runs, runs, lpos, xn)


def _ffn_kernel(be_ref, na_ref, plan_ref, x_ref, wgu_hbm, bgu_ref, wd_hbm, bd_ref, o_ref, wgu_in_s, wd_in_s,
                wgu_s, wd32_s, wd_s, sem, *, layer):
    j = pl.program_id(0)
    n_active = na_ref[0]
    active = j < n_active
    expert = be_ref[j]
    new_expert = jnp.logical_or(j == 0, expert != be_ref[jnp.maximum(j - 1, 0)])
    rows_left = plan_ref[PLAN_START, expert] + plan_ref[PLAN_COUNT, expert] - j * MOE_BLOCK

    def fetch(e):
        return (pltpu.make_async_copy(wgu_hbm.at[layer, e], wgu_in_s, sem.at[0]),
                pltpu.make_async_copy(wd_hbm.at[layer, e], wd_in_s, sem.at[1]))

    @pl.when(j == 0)
    def _():
        for copy in fetch(expert):
            copy.start()

    @pl.when(jnp.logical_and(active, new_expert))
    def _():
        for copy in fetch(expert):
            copy.wait()
        wgu_s[...] = wgu_in_s[...].astype(BF16)
        half = LANES // 2
        for c in range(wd_s.shape[0]):
            cols = slice(c * LANES, (c + 1) * LANES)
            for k in range(wd_s.shape[1] // LANES):
                for hb in range(2):
                    src = k * LANES + hb * half
                    wd32_s[c, pl.ds(k * LANES + hb, half, stride=2), :] = wd_in_s[src:src + half, cols]
            wd_s[c] = wd32_s[c].astype(BF16)
        nxt = lax.while_loop(lambda b: jnp.logical_and(b < n_active, be_ref[jnp.minimum(b, n_active - 1)] == expert),
                             lambda b: b + 1, j + 1)

        @pl.when(nxt < n_active)
        def _():
            for copy in fetch(be_ref[jnp.minimum(nxt, n_active - 1)]):
                copy.start()

    def ffn_rows(n_rows):
        x = x_ref[0:n_rows, :].astype(BF16)
        hmid = _dot(x, wgu_s[...]) + bgu_ref[0]
        m = hmid.shape[0]
        even = (_iota((m, LANES), 1) & 1) == 0
        acts = []
        for k in range(hmid.shape[1] // (2 * LANES)):
            h0 = hmid[:, (2 * k) * LANES:(2 * k + 1) * LANES]
            h1 = hmid[:, (2 * k + 1) * LANES:(2 * k + 2) * LANES]
            glu = jnp.where(even, h0, pltpu.roll(h1, 1, 1))
            lin = jnp.where(even, pltpu.roll(h0, LANES - 1, 1), h1)
            glu = jnp.minimum(glu, SWIGLU_LIMIT)
            lin = jnp.clip(lin, -SWIGLU_LIMIT, SWIGLU_LIMIT)
            acts.append((glu * _sigmoid(SWIGLU_ALPHA * glu) * (lin + 1.0)).astype(BF16))
        act = jnp.concatenate(acts, axis=1)
        wd = jnp.concatenate([wd_s[c] for c in range(wd_s.shape[0])], axis=1)
        o_ref[0:n_rows, :] = _dot(act, wd) + bd_ref[0]

    half_block = MOE_BLOCK // 2

    @pl.when(jnp.logical_and(active, rows_left > half_block))
    def _():
        ffn_rows(MOE_BLOCK)

    @pl.when(jnp.logical_and(active, rows_left <= half_block))
    def _():
        ffn_rows(half_block)


def _ffn(layer, block_expert, n_active, plan, xs, wgu, bgu, wd, bd):
    n_slots, d = xs.shape
    dff2 = wgu.shape[3]
    nb = n_slots // MOE_BLOCK
    rows = lambda j, be, na, plan_: (jnp.maximum(jnp.minimum(j, na[0] - 1), 0), 0)
    per_expert = lambda j, be, na, plan_: (layer, be[j], 0, 0)
    grid_spec = pltpu.PrefetchScalarGridSpec(
        num_scalar_prefetch=3,
        grid=(nb,),
        in_specs=[pl.BlockSpec((MOE_BLOCK, d), rows),
                  pl.BlockSpec(memory_space=pl.ANY), pl.BlockSpec((None, 1, 1, dff2), per_expert),
                  pl.BlockSpec(memory_space=pl.ANY), pl.BlockSpec((None, 1, 1, d), per_expert)],
        out_specs=pl.BlockSpec((MOE_BLOCK, d), rows),
        scratch_shapes=[pltpu.VMEM((d, dff2), F32), pltpu.VMEM((dff2 // 2, d), F32),
                        pltpu.VMEM((d, dff2), BF16), pltpu.VMEM((d // LANES, dff2 // 2, LANES), F32),
                        pltpu.VMEM((d // LANES, dff2 // 2, LANES), BF16), pltpu.SemaphoreType.DMA((2,))],
    )
    return pl.pallas_call(
        functools.partial(_ffn_kernel, layer=layer),
        grid_spec=grid_spec,
        out_shape=jax.ShapeDtypeStruct((n_slots, d), F32),
        compiler_params=_params("arbitrary"),
        name="moe_ffn",
    )(block_expert, n_active, plan, xs, wgu, bgu, wd, bd)


def _combine_kernel(runs_ref, next_runs_ref, lk_ref, gk_ref, h_ref, g_ref, ys_hbm, o_ref, sorted_s, sem, *,
                    final_norm):
    i = pl.program_id(0)
    tm = h_ref.shape[0]
    n_local = sorted_s.shape[1]
    slot = i % 2

    def from_slots(buf):
        return lambda src, dst, size: pltpu.make_async_copy(
            ys_hbm.at[pl.ds(src, size)], sorted_s.at[buf, pl.ds(dst, size)], sem.at[buf])

    def start_runs(table, buf):
        def body(e, carry):
            _copy_run(table[RUN_SLOT, e], table[RUN_START, e], table[RUN_LEN, e], RUN_BITS, from_slots(buf),
                      False)
            return carry
        lax.fori_loop(0, N_EXPERTS, body, 0)

    @pl.when(i == 0)
    def _():
        sorted_s[...] = jnp.zeros_like(sorted_s)
        start_runs(runs_ref, slot)

    @pl.when(i + 1 < pl.num_programs(0))
    def _():
        start_runs(next_runs_ref, 1 - slot)

    _copy_run(0, 0, runs_ref[RUN_TOTAL, 0], TOTAL_BITS, lambda src, dst, size: from_slots(slot)(0, 0, size),
              True)
    position = _iota((tm, n_local), 1).astype(F32)
    lk = lk_ref[...]
    gk = gk_ref[...]
    weights = jnp.zeros((tm, n_local), F32)
    for k in range(TOP_K):
        weights = weights + jnp.where(position == lk[:, k:k + 1], gk[:, k:k + 1], 0.0)
    acc = h_ref[...] + _dot(weights.astype(BF16), sorted_s[slot].astype(BF16))
    if final_norm:
        ms = jnp.mean(acc * acc, axis=-1, keepdims=True)
        acc = acc * lax.rsqrt(ms + EPS) * g_ref[...]
    o_ref[...] = acc


def _combine(runs, lk, gk, h, ys, g, final_norm):
    t, d = h.shape
    tm = ROW_TILE
    return pl.pallas_call(
        functools.partial(_combine_kernel, final_norm=final_norm),
        grid=(t // tm,),
        in_specs=[pl.BlockSpec((8, LANES), lambda i: (i, 0), memory_space=pltpu.SMEM),
                  pl.BlockSpec((8, LANES), lambda i: (jnp.minimum(i + 1, t // tm - 1), 0),
                               memory_space=pltpu.SMEM),
                  pl.BlockSpec((tm, LANES), lambda i: (i, 0)),
                  pl.BlockSpec((tm, LANES), lambda i: (i, 0)),
                  pl.BlockSpec((tm, d), lambda i: (i, 0)),
                  pl.BlockSpec((1, d), lambda i: (0, 0)),
                  pl.BlockSpec(memory_space=pl.ANY)],
        out_specs=pl.BlockSpec((tm, d), lambda i: (i, 0)),
        out_shape=jax.ShapeDtypeStruct((t, d), F32),
        scratch_shapes=[pltpu.VMEM((2, LOCAL_ROWS, d), F32), pltpu.SemaphoreType.DMA((2,))],
        compiler_params=_params("arbitrary"),
        name="moe_combine",
    )(runs, runs, lk, gk, h, g.reshape(1, d), ys)


def _moe(h, xn, sel, gate, rank, counts, tile_counts, layer, w_gu, b_gu, w_down, b_down, g_final,
         final_norm):
    t, d = h.shape
    n_slots = t * TOP_K + (t // ROW_TILE) * N_EXPERTS * RUN_ALIGN + N_EXPERTS * MOE_BLOCK
    n_blocks = n_slots // MOE_BLOCK
    pad_start, plan = _plan(counts, n_blocks)
    lpos, lk, gk, runs = _dest(sel, gate, rank, tile_counts, pad_start)
    xs = _dispatch(plan, runs, lpos, xn, n_slots)
    block_expert = plan[PLAN_EXPERTS:].reshape(-1)[:n_blocks]
    n_active = plan[PLAN_ACTIVE, :1]
    depth = w_gu.shape[0]
    ys = _ffn(layer, block_expert, n_active, plan, xs, w_gu, b_gu.reshape(depth, N_EXPERTS, 1, -1), w_down,
              b_down.reshape(depth, N_EXPERTS, 1, d))
    return _combine(runs, lk, gk, h, ys, g_final, final_norm)


def kernel(x, norm_mix_g, w_in, sb_norm_g, rg_conv_w, rg_conv_b, rg_wa, rg_ba, rg_wx, rg_bx,
           rg_lambda, rg_norm_g, hg_lower_bounds, hg_norm_g, m2_conv_w, m2_conv_b, m2_dt_bias,
           m2_a_log, m2_d, m2_norm_g, w_out, norm_ffn_g, router_w, router_b, moe_w_gu, moe_b_gu,
           moe_w_down, moe_b_down, final_norm_g):
    b, l, d = x.shape
    depth = w_in.shape[0]
    t = b * l
    lbs = jnp.cumsum(jax.nn.softmax(hg_lower_bounds.astype(F32), axis=0), axis=0)
    lbs = lbs - lbs[0]
    d_in = w_in.shape[2]
    w_in_p = jnp.zeros((depth, d, PROJ_BLOCKS * GROUP_W), BF16).at[:, :, :d_in].set(w_in.astype(BF16))
    h = x.reshape(t, d)
    for layer in range(depth):
        proj = _norm_inproj(h, norm_mix_g[layer], w_in_p[layer]).reshape(b, l, PROJ_BLOCKS * GROUP_W)
        o_a = _sb_attn(proj, sb_norm_g[layer])
        o_b = _rglru(proj, rg_conv_w[layer], rg_conv_b[layer], rg_wa[layer], rg_ba[layer],
                     rg_wx[layer], rg_bx[layer], rg_lambda[layer], rg_norm_g[layer])
        o_c = _hgrn2(proj, lbs[layer], hg_norm_g[layer])
        o_d = _ssd(proj, m2_conv_w[layer], m2_conv_b[layer], m2_dt_bias[layer], m2_a_log[layer],
                   m2_d[layer], m2_norm_g[layer])
        h, xn, sel, gate, rank, counts, tile_counts = _outproj_router(
            h, (o_a, o_b, o_c, o_d), w_out[layer], norm_ffn_g[layer], router_w[layer],
            router_b[layer])
        h = _moe(h, xn, sel, gate, rank, counts, tile_counts, layer, moe_w_gu, moe_b_gu, moe_w_down,
                 moe_b_down, final_norm_g, layer == depth - 1)
    return h.reshape(b, l, d)
```

```python
import functools
import math

import jax
import jax.numpy as jnp
from jax import lax
from jax.experimental import pallas as pl
from jax.experimental.pallas import tpu as pltpu

F32 = jnp.float32
BF16 = jnp.bfloat16
I32 = jnp.int32
HIGHEST = lax.Precision.HIGHEST
EPS = 1e-6
EXP_UNDERFLOW = -104.0

LANES = 128
GROUP_W = 256
HEAD_DIM = 64
N_HEADS = 4
SSD_STATE = 128
RG_C = 8.0
N_EXPERTS = 32
TOP_K = 4
SWIGLU_LIMIT = 7.0
SWIGLU_ALPHA = 1.702
PROJ_BLOCKS = 14
VMEM_LIMIT = 56 * 1024 * 1024

ROW_TILE = 256
INPROJ_TILE = 512
ATTN_BLOCK = 256
SCAN_TILE = 256
HG_CHUNK = 16
HG_BLOCK = 256
SSD_CHUNK = LANES
MOE_BLOCK = 512
HG_UNROLL = 8

PLAN_ACTIVE, PLAN_COUNT, PLAN_START, PLAN_PADDED, PLAN_EXPERTS = 0, 1, 2, 3, 4
PLAN_EXPERT_ROWS = 8 - PLAN_EXPERTS
RUN_LEN, RUN_START, RUN_SLOT, RUN_TOTAL = 0, 1, 2, 3
RUN_ALIGN = 8
RUN_BITS = (ROW_TILE // RUN_ALIGN).bit_length()
PAD_BITS = (MOE_BLOCK // RUN_ALIGN - 1).bit_length()
LOCAL_ROWS = ROW_TILE * TOP_K + N_EXPERTS * RUN_ALIGN
TOTAL_BITS = (LOCAL_ROWS // RUN_ALIGN).bit_length()


def _params(*sem):
    return pltpu.CompilerParams(dimension_semantics=sem, vmem_limit_bytes=VMEM_LIMIT)


def _dot(a, b):
    return jnp.dot(a, b, preferred_element_type=F32)


def _dot_hi(a, b):
    return jnp.dot(a, b, preferred_element_type=F32, precision=HIGHEST)


def _split3(x):
    hi = x.astype(BF16)
    rest = x - hi.astype(F32)
    mid = rest.astype(BF16)
    lo = (rest - mid.astype(F32)).astype(BF16)
    return hi, mid, lo


def _dot_exact_lhs(m, x):
    return sum(_dot(m, part) for part in _split3(x))


def _dot_exact_rhs(x, m):
    return sum(_dot(part, m) for part in _split3(x))


def _dot_nt(a, b, precision=None):
    return lax.dot_general(a, b, (((1,), (1,)), ((), ())), preferred_element_type=F32,
                           precision=precision)


def _dot_tn(a, b):
    return lax.dot_general(a, b, (((0,), (0,)), ((), ())), preferred_element_type=F32)


def _sigmoid(x):
    return 1.0 / (1.0 + jnp.exp(-x))


def _silu(x):
    return x * _sigmoid(x)


def _softplus(x):
    return jnp.maximum(x, 0.0) + jnp.log(1.0 + jnp.exp(-jnp.abs(x)))


def _iota(shape, dim):
    return lax.broadcasted_iota(I32, shape, dim)


def _div_pow2(x, n):
    assert n & (n - 1) == 0
    return jnp.right_shift(x, int(math.log2(n)))


def _shift_rows(x, d, prev8):
    r = pltpu.roll(x, d, 0)
    p = pltpu.roll(prev8, d, 0)
    head = jnp.where(_iota(prev8.shape, 0) < d, p, r[:8])
    return jnp.concatenate([head, r[8:]], axis=0)


def _causal_conv(x, prev8, w, b):
    k = w.shape[0]
    y = x * w[k - 1:k] + b
    for d in range(1, k):
        y = y + _shift_rows(x, d, prev8) * w[k - 1 - d:k - d]
    return y


def _norm_inproj_kernel(x_ref, g_ref, w_ref, o_ref):
    x = x_ref[...]
    ms = jnp.mean(x * x, axis=-1, keepdims=True)
    xn = (x * lax.rsqrt(ms + EPS) * g_ref[...]).astype(BF16)
    o_ref[...] = _dot(xn, w_ref[...])


def _norm_inproj(h, g, w):
    t, d = h.shape
    n = w.shape[1]
    return pl.pallas_call(
        _norm_inproj_kernel,
        grid=(t // INPROJ_TILE,),
        in_specs=[pl.BlockSpec((INPROJ_TILE, d), lambda i: (i, 0)),
                  pl.BlockSpec((1, d), lambda i: (0, 0)),
                  pl.BlockSpec((d, n), lambda i: (0, 0))],
        out_specs=pl.BlockSpec((INPROJ_TILE, n), lambda i: (i, 0)),
        out_shape=jax.ShapeDtypeStruct((t, n), F32),
        compiler_params=_params("parallel"),
        name="norm_inproj",
    )(h, g.reshape(1, d), w)


def _sb_attn_kernel(q_ref, k_ref, v_ref, g_ref, o_ref, acc_ref, run_ref):
    i = pl.program_id(1)
    blk = ATTN_BLOCK
    scale = HEAD_DIM ** -0.5
    rows = N_HEADS * blk
    below = _iota((rows, blk), 1) < jnp.bitwise_and(_iota((rows, blk), 0), blk - 1)
    r_w = _iota((blk, blk + LANES), 0)
    c_w = _iota((blk, blk + LANES), 1)
    later_sum = jnp.where(jnp.logical_or(c_w >= blk, r_w > c_w), 1.0, 0.0).astype(BF16)
    acc_ref[...] = jnp.zeros_like(acc_ref)
    run_ref[...] = jnp.zeros_like(run_ref)
    q_all = q_ref[0] * scale
    head_of_lane = _div_pow2(_iota((blk, GROUP_W), 1), HEAD_DIM)
    q_heads = jnp.concatenate([jnp.where(head_of_lane == h, q_all, 0.0) for h in range(N_HEADS)],
                              axis=0).astype(BF16)

    def key_block(j, diagonal):
        ks = pl.multiple_of(j * blk, blk)
        z = _dot_nt(q_heads, k_ref[0, pl.ds(ks, blk), :].astype(BF16))
        ls = jnp.minimum(z, 0.0) - jnp.log(1.0 + jnp.exp(-jnp.abs(z)))
        lk = ls - z
        if diagonal:
            lk = jnp.where(below, lk, 0.0)
        lk_hi = lk.astype(BF16)
        lk_lo = (lk - lk_hi.astype(F32)).astype(BF16)
        stacked = _dot(jnp.concatenate([lk_hi, lk_lo], axis=0), later_sum)
        sums = stacked[:rows] + stacked[rows:]
        run = run_ref[...]
        rem = sums[:, :blk] + jnp.concatenate([run] * (blk // LANES), axis=1)
        a = jnp.exp(ls + rem)
        if diagonal:
            a = jnp.where(below, a, 0.0)
        out = _dot(a.astype(BF16), v_ref[0, pl.ds(ks, blk), :].astype(BF16))
        for h in range(N_HEADS):
            sl = slice(h * HEAD_DIM, (h + 1) * HEAD_DIM)
            acc_ref[:, sl] += out[h * blk:(h + 1) * blk, sl]
        run_ref[...] = run + sums[:, blk:]

    key_block(i, True)

    def live():
        return jnp.max(run_ref[...]) > EXP_UNDERFLOW

    def more(carry):
        jj, alive = carry
        return jnp.logical_and(jj <= i, alive)

    def body(carry):
        jj, _ = carry
        key_block(i - jj, False)
        return jj + 1, live()

    lax.while_loop(more, body, (jnp.int32(1), live()))
    o = acc_ref[...]
    ms = jnp.mean(o * o, axis=-1, keepdims=True)
    o_ref[0] = o * lax.rsqrt(ms + EPS) * g_ref[...]


def _sb_attn(proj, g):
    b, l, _ = proj.shape
    blk = ATTN_BLOCK
    return pl.pallas_call(
        _sb_attn_kernel,
        grid=(b, l // blk),
        in_specs=[pl.BlockSpec((1, blk, GROUP_W), lambda bi, i: (bi, i, 0)),
                  pl.BlockSpec((1, l, GROUP_W), lambda bi, i: (bi, 0, 1)),
                  pl.BlockSpec((1, l, GROUP_W), lambda bi, i: (bi, 0, 2)),
                  pl.BlockSpec((1, GROUP_W), lambda bi, i: (0, 0))],
        out_specs=pl.BlockSpec((1, blk, GROUP_W), lambda bi, i: (bi, i, 0)),
        out_shape=jax.ShapeDtypeStruct((b, l, GROUP_W), F32),
        scratch_shapes=[pltpu.VMEM((blk, GROUP_W), F32), pltpu.VMEM((N_HEADS * blk, LANES), F32)],
        compiler_params=_params("parallel", "arbitrary"),
        name="sb_attn",
    )(proj, proj, proj, g.reshape(1, GROUP_W))


def _rglru_kernel(x_ref, gate_ref, cw_ref, cb_ref, wa_ref, ba_ref, wx_ref, bx_ref, lam_ref,
                  ng_ref, o_ref):
    l = x_ref.shape[1]
    w = x_ref.shape[2]
    tc = SCAN_TILE
    rows = _iota((tc, w), 0)
    cw = cw_ref[...]
    cb = cb_ref[...]
    neg_c_sp = -RG_C * _softplus(-lam_ref[...])

    def chunk(c, carry):
        h_prev, tail = carry
        s = pl.multiple_of(c * tc, tc)
        x = x_ref[0, pl.ds(s, tc), :]
        xc = _causal_conv(x, tail, cw, cb)
        xcb = xc.astype(BF16)
        r = _sigmoid(_dot(xcb, wa_ref[...]) + ba_ref[...])
        ig = _sigmoid(_dot(xcb, wx_ref[...]) + bx_ref[...])
        log_a = r * neg_c_sp
        a = jnp.exp(log_a)
        th = jnp.tanh(log_a)
        mult = jnp.sqrt(-2.0 * th / (1.0 - th))
        mult = jnp.where(rows + s == 0, 1.0, mult)
        u = mult * ig * xc
        d = 1
        while d < tc:
            keep = rows >= d
            a_s = jnp.where(keep, pltpu.roll(a, d, 0), 1.0)
            u_s = jnp.where(keep, pltpu.roll(u, d, 0), 0.0)
            u = a * u_s + u
            a = a * a_s
            d *= 2
        hs = u + a * h_prev
        gate = gate_ref[0, pl.ds(s, tc), :]
        gelu = 0.5 * gate * (1.0 + jnp.tanh(math.sqrt(2.0 / math.pi) * (gate + 0.044715 * gate * gate * gate)))
        o = hs * gelu
        ms = jnp.mean(o * o, axis=-1, keepdims=True)
        o_ref[0, pl.ds(s, tc), :] = o * lax.rsqrt(ms + EPS) * ng_ref[...]
        return hs[tc - 1:tc], x[tc - 8:tc]

    lax.fori_loop(0, l // tc, chunk, (jnp.zeros((1, w), F32), jnp.zeros((8, w), F32)))


def _block_diag(wb):
    n, d, _ = wb.shape
    same = jnp.eye(n, dtype=bool)[:, None, :, None]
    return jnp.where(same, wb[:, :, None, :], 0).reshape(n * d, n * d)


def _rglru(proj, conv_w, conv_b, wa, ba, wx, bx, lam, ng):
    b, l, _ = proj.shape
    w = GROUP_W
    row = lambda v: v.reshape(1, w)
    full = lambda shape: pl.BlockSpec(shape, lambda bi: (0,) * len(shape))
    return pl.pallas_call(
        _rglru_kernel,
        grid=(b,),
        in_specs=[pl.BlockSpec((1, l, w), lambda bi: (bi, 0, 3)),
                  pl.BlockSpec((1, l, w), lambda bi: (bi, 0, 4)),
                  full((conv_w.shape[0], w)), full((1, w)), full((w, w)), full((1, w)),
                  full((w, w)), full((1, w)), full((1, w)), full((1, w))],
        out_specs=pl.BlockSpec((1, l, w), lambda bi: (bi, 0, 0)),
        out_shape=jax.ShapeDtypeStruct((b, l, w), F32),
        compiler_params=_params("parallel"),
        name="rglru",
    )(proj, proj, conv_w, row(conv_b), _block_diag(wa).astype(BF16), row(ba),
      _block_diag(wx).astype(BF16), row(bx), row(lam), row(ng))


def _hgrn2_kernel(q_ref, f_ref, i_ref, g_ref, lb_ref, ng_ref, o_ref,
                  qe_s, ke_s, kl_s, dl_s, o_s, inc_s):
    l = q_ref.shape[1]
    w = q_ref.shape[2]
    tile = SCAN_TILE
    c = HG_CHUNK
    r_t = _iota((tile, tile), 0)
    c_t = _iota((tile, tile), 1)
    same = _div_pow2(r_t, c) == _div_pow2(c_t, c)
    cum_m = jnp.where(jnp.logical_and(same, c_t <= r_t), 1.0, 0.0).astype(BF16)
    tot_m = jnp.where(same, 1.0, 0.0).astype(BF16)
    lb = lb_ref[...]

    def prep(t, carry):
        s = pl.multiple_of(t * tile, tile)
        forget = lb + (1.0 - lb) * _sigmoid(f_ref[0, pl.ds(s, tile), :])
        log_f = jnp.log(forget)
        key = 1.0 - forget
        parts = _split3(log_f)
        bcum = sum(_dot(cum_m, part) for part in parts)
        blast = sum(_dot(tot_m, part) for part in parts)
        q = _silu(q_ref[0, pl.ds(s, tile), :])
        qe_s[pl.ds(s, tile), :] = (q * jnp.exp(bcum)).astype(BF16)
        ke_s[pl.ds(s, tile), :] = (key * jnp.exp(-bcum)).astype(BF16)
        kl_s[pl.ds(s, tile), :] = key * jnp.exp(blast - bcum)
        dl_s[pl.ds(s, tile), :] = jnp.exp(blast)
        return carry

    lax.fori_loop(0, l // tile, prep, 0)

    blk = HG_BLOCK
    per = blk // c
    q_row = jnp.bitwise_and(_iota((N_HEADS * blk, blk), 0), blk - 1)
    k_row = _iota((N_HEADS * blk, blk), 1)
    within = jnp.logical_and(_div_pow2(q_row, c) == _div_pow2(k_row, c), k_row <= q_row)
    own = (_div_pow2(_iota((blk, per * HEAD_DIM), 0), c)
           == _div_pow2(_iota((blk, per * HEAD_DIM), 1), HEAD_DIM))
    head_of_lane = _div_pow2(_iota((blk, w), 1), HEAD_DIM)

    def local(m, carry):
        s = pl.multiple_of(m * blk, blk)
        qe = qe_s[pl.ds(s, blk), :]
        ke = ke_s[pl.ds(s, blk), :]
        kl = kl_s[pl.ds(s, blk), :]
        v = i_ref[0, pl.ds(s, blk), :]
        q_heads = jnp.concatenate([jnp.where(head_of_lane == h, qe, jnp.zeros_like(qe))
                                   for h in range(N_HEADS)], axis=0)
        sc = jnp.where(within, _dot_nt(q_heads, ke), 0.0)
        out = _dot(sc.astype(BF16), v.astype(BF16))
        o_s[pl.ds(s, blk), :] = jnp.concatenate(
            [out[h * blk:(h + 1) * blk, h * HEAD_DIM:(h + 1) * HEAD_DIM] for h in range(N_HEADS)], axis=1)
        for h in range(N_HEADS):
            sl = slice(h * HEAD_DIM, (h + 1) * HEAD_DIM)
            wide = jnp.where(own, jnp.concatenate([kl[:, sl]] * per, axis=1), 0.0)
            inc = _dot_tn(v[:, sl], wide)
            for j in range(per):
                inc_s[m * per + j, :, sl] = inc[:, j * HEAD_DIM:(j + 1) * HEAD_DIM]
        return carry

    lax.fori_loop(0, l // blk, local, 0)

    head_of_chunk_lane = _div_pow2(_iota((c, w), 1), HEAD_DIM)

    def step(n, state):
        s = pl.multiple_of(n * c, c)
        qe = qe_s[pl.ds(s, c), :]
        dl = dl_s[pl.ds(s, 8), :][0:1]
        q_heads = jnp.concatenate([jnp.where(head_of_chunk_lane == h, qe, jnp.zeros_like(qe))
                                   for h in range(N_HEADS)], axis=0)
        carried = _dot_nt(q_heads, state.astype(BF16))
        o_s[pl.ds(s, c), :] += jnp.concatenate([carried[h * c:(h + 1) * c] for h in range(N_HEADS)], axis=1)
        return dl * state + inc_s[n]

    lax.fori_loop(0, l // c, step, jnp.zeros((HEAD_DIM, w), F32), unroll=HG_UNROLL)

    head_avg = jnp.where(_div_pow2(_iota((w, w), 0), HEAD_DIM) == _div_pow2(_iota((w, w), 1), HEAD_DIM),
                         1.0 / HEAD_DIM, 0.0).astype(BF16)

    def finish(t, carry):
        s = pl.multiple_of(t * tile, tile)
        o = o_s[pl.ds(s, tile), :]
        ms = _dot_exact_rhs(o * o, head_avg)
        o_ref[0, pl.ds(s, tile), :] = (o * lax.rsqrt(ms + EPS) * ng_ref[...]
                                       * _silu(g_ref[0, pl.ds(s, tile), :]))
        return carry

    lax.fori_loop(0, l // tile, finish, 0)


def _hgrn2(proj, lb, ng):
    b, l, _ = proj.shape
    w = GROUP_W
    full = lambda shape: pl.BlockSpec(shape, lambda bi: (0,) * len(shape))
    col = lambda j: pl.BlockSpec((1, l, w), lambda bi: (bi, 0, j))
    return pl.pallas_call(
        _hgrn2_kernel,
        grid=(b,),
        in_specs=[col(5), col(6), col(7), col(8), full((1, w)), full((1, w))],
        out_specs=pl.BlockSpec((1, l, w), lambda bi: (bi, 0, 0)),
        out_shape=jax.ShapeDtypeStruct((b, l, w), F32),
        scratch_shapes=[pltpu.VMEM((l, w), BF16), pltpu.VMEM((l, w), BF16), pltpu.VMEM((l, w), F32),
                        pltpu.VMEM((l, w), F32), pltpu.VMEM((l, w), F32),
                        pltpu.VMEM((l // HG_CHUNK, HEAD_DIM, w), F32)],
        compiler_params=_params("parallel"),
        name="hgrn2",
    )(proj, proj, proj, proj, lb.reshape(1, w), ng.reshape(1, w))


def _ssd_kernel(z_ref, x_ref, bm_ref, cm_ref, dt_ref, cw_ref, cb_ref, dtb_ref, aneg_ref,
                dskip_ref, ng_ref, o_ref):
    l = z_ref.shape[1]
    w = GROUP_W
    q = SSD_CHUNK
    p = HEAD_DIM
    r_q = _iota((q, q), 0)
    c_q = _iota((q, q), 1)
    tril = c_q <= r_q
    cum_m = jnp.where(tril, 1.0, 0.0).astype(BF16)
    cw = cw_ref[...]
    cb = cb_ref[...]

    def chunk(n, carry):
        state, tails = carry
        s = pl.multiple_of(n * q, q)
        parts = []
        new_tails = []
        for j, ref in enumerate((x_ref, bm_ref, cm_ref)):
            raw = ref[0, pl.ds(s, q), :]
            sl = slice(j * w, (j + 1) * w)
            parts.append(_silu(_causal_conv(raw, tails[j], cw[:, sl], cb[:, sl])))
            new_tails.append(raw[q - 8:q])
        xs, bm, cm = parts
        dt = _softplus(dt_ref[0, pl.ds(s, q), :] + dtb_ref[...])
        dt_rep = jnp.concatenate([jnp.broadcast_to(dt[:, h:h + 1], (q, p)) for h in range(N_HEADS)],
                                 axis=1)
        a_cs = _dot_exact_lhs(cum_m, dt_rep * aneg_ref[...])
        a_rows = [jnp.transpose(a_cs[:, g * LANES:(g + 1) * LANES]) for g in range(w // LANES)]
        xd = xs * dt_rep
        groups = w // SSD_STATE
        heads_per_group = N_HEADS // groups
        cgs = [cm[:, g * SSD_STATE:(g + 1) * SSD_STATE].astype(BF16) for g in range(groups)]
        cbs = [_dot_nt(cgs[g], bm[:, g * SSD_STATE:(g + 1) * SSD_STATE].astype(BF16)) for g in range(groups)]
        masked = []
        for h in range(N_HEADS):
            row = (h * p) % LANES
            a_row = a_rows[(h * p) // LANES][row:row + 1, :]
            a_col = a_cs[:, h * p:h * p + 1]
            decay = jnp.exp(jnp.where(tril, a_col - a_row, -jnp.inf))
            masked.append((cbs[h // heads_per_group] * decay).astype(BF16))
        within = _dot(jnp.concatenate(masked, axis=0), xd.astype(BF16))
        y = jnp.concatenate([within[h * q:(h + 1) * q, h * p:(h + 1) * p] for h in range(N_HEADS)], axis=1)
        to_end = xd * jnp.exp(a_cs[q - 1:q, :] - a_cs)
        carried = []
        new_state = []
        for g in range(groups):
            gs = slice(g * SSD_STATE, (g + 1) * SSD_STATE)
            st = state[g]
            carried.append(_dot_nt(cgs[g], st.astype(BF16)))
            chunk_decay = jnp.exp(a_rows[g][:, q - 1:q])
            new_state.append(chunk_decay * st + _dot_tn(to_end[:, gs], bm[:, gs]))
        y = y + jnp.exp(a_cs) * jnp.concatenate(carried, axis=1) + xs * dskip_ref[...]
        y = y * _silu(z_ref[0, pl.ds(s, q), :])
        halves = []
        for gi in range(2):
            yg = y[:, gi * LANES:(gi + 1) * LANES]
            ms = jnp.mean(yg * yg, axis=-1, keepdims=True)
            halves.append(yg * lax.rsqrt(ms + EPS))
        o_ref[0, pl.ds(s, q), :] = jnp.concatenate(halves, axis=1) * ng_ref[...]
        return tuple(new_state), tuple(new_tails)

    assert SSD_STATE == LANES and (N_HEADS // (w // SSD_STATE)) * p == LANES
    zero = jnp.zeros((LANES, SSD_STATE), F32)
    tail0 = jnp.zeros((8, w), F32)
    lax.fori_loop(0, l // q, chunk, ((zero,) * (w // SSD_STATE), (tail0,) * 3), unroll=2)


def _ssd(proj, conv_w, conv_b, dt_bias, a_log, d_skip, ng):
    b, l, _ = proj.shape
    w = GROUP_W
    full = lambda shape: pl.BlockSpec(shape, lambda bi: (0,) * len(shape))
    col = lambda j: pl.BlockSpec((1, l, w), lambda bi: (bi, 0, j))
    pad_heads = lambda v: jnp.zeros((1, w), F32).at[0, :N_HEADS].set(v)
    return pl.pallas_call(
        _ssd_kernel,
        grid=(b,),
        in_specs=[col(9), col(10), col(11), col(12), col(13),
                  full((conv_w.shape[0], 3 * w)), full((1, 3 * w)), full((1, w)), full((1, w)),
                  full((1, w)), full((1, w))],
        out_specs=pl.BlockSpec((1, l, w), lambda bi: (bi, 0, 0)),
        out_shape=jax.ShapeDtypeStruct((b, l, w), F32),
        compiler_params=_params("parallel"),
        name="ssd",
    )(proj, proj, proj, proj, proj, conv_w, conv_b.reshape(1, 3 * w), pad_heads(dt_bias),
      jnp.repeat(-jnp.exp(a_log), HEAD_DIM).reshape(1, w), jnp.repeat(d_skip, HEAD_DIM).reshape(1, w),
      ng.reshape(1, w))


def _outproj_router_kernel(h_ref, oa_ref, ob_ref, oc_ref, od_ref, wo_ref, g_ref, rwh_ref, rwl_ref, rb_ref,
                           h_out, xn_out, sel_out, gate_out, rank_out, cnt_out, tile_out, cnt_s):
    i = pl.program_id(0)
    tm = h_ref.shape[0]

    @pl.when(i == 0)
    def _():
        cnt_s[...] = jnp.zeros_like(cnt_s)

    acc = h_ref[...]
    for j, ref in enumerate((oa_ref, ob_ref, oc_ref, od_ref)):
        acc = acc + _dot(ref[...].astype(BF16), wo_ref[j * GROUP_W:(j + 1) * GROUP_W, :])
    h_out[...] = acc
    ms = jnp.mean(acc * acc, axis=-1, keepdims=True)
    xn = acc * lax.rsqrt(ms + EPS) * g_ref[...]
    xn_out[...] = xn
    lane = _iota((tm, LANES), 1)
    lane_f = lane.astype(F32)
    xn_hi = xn.astype(BF16)
    xn_lo = (xn - xn_hi.astype(F32)).astype(BF16)
    router = _dot(xn_hi, rwh_ref[...]) + _dot(xn_lo, rwh_ref[...]) + _dot(xn_hi, rwl_ref[...])
    logits = jnp.where(lane < N_EXPERTS, router + rb_ref[...], -jnp.inf)
    sel = jnp.zeros((tm, LANES), F32)
    tops = []
    for k in range(TOP_K):
        m = jnp.max(logits, axis=-1, keepdims=True)
        idx = jnp.min(jnp.where(logits == m, lane_f, float(LANES)), axis=-1, keepdims=True)
        hit = lane_f == idx
        sel = jnp.where(hit, float(k + 1), sel)
        logits = jnp.where(hit, -jnp.inf, logits)
        tops.append(m)
    exps = [jnp.exp(m - tops[0]) for m in tops]
    denom = exps[0] + exps[1] + exps[2] + exps[3]
    gate = jnp.zeros((tm, LANES), F32)
    for k in range(TOP_K):
        gate = jnp.where(sel == float(k + 1), exps[k] / denom, gate)
    sel_out[...] = sel
    gate_out[...] = gate
    member = jnp.where(sel > 0.0, 1.0, 0.0)
    before = jnp.where(_iota((tm, tm), 1) < _iota((tm, tm), 0), 1.0, 0.0).astype(BF16)
    carry = cnt_s[0:1, :]
    rank_out[...] = _dot(before, member.astype(BF16))
    in_tile = jnp.ceil(jnp.sum(member, axis=0, keepdims=True) * (1.0 / RUN_ALIGN)) * RUN_ALIGN
    total = carry + in_tile
    tile_out[...] = jnp.concatenate([carry, in_tile, jnp.zeros((6, LANES), F32)], axis=0)
    cnt_s[...] = jnp.broadcast_to(total, cnt_s.shape)
    cnt_out[...] = jnp.broadcast_to(total, cnt_out.shape)


def _outproj_router(h, outs, w_out, g, router_w, router_b):
    t, d = h.shape
    tm = ROW_TILE
    rw = jnp.zeros((d, LANES), F32).at[:, :N_EXPERTS].set(router_w)
    rb = jnp.zeros((1, LANES), F32).at[0, :N_EXPERTS].set(router_b)
    rw_hi = rw.astype(BF16)
    rw_lo = (rw - rw_hi.astype(F32)).astype(BF16)
    tile = lambda width: pl.BlockSpec((tm, width), lambda i: (i, 0))
    full = lambda shape: pl.BlockSpec(shape, lambda i: (0,) * len(shape))
    flat = [o.reshape(t, GROUP_W) for o in outs]
    return pl.pallas_call(
        _outproj_router_kernel,
        grid=(t // tm,),
        in_specs=[tile(d)] + [tile(GROUP_W)] * 4 + [full((d, d)), full((1, d)), full((d, LANES)),
                                                   full((d, LANES)), full((1, LANES))],
        out_specs=[tile(d), tile(d), tile(LANES), tile(LANES), tile(LANES), full((8, LANES)),
                   pl.BlockSpec((8, LANES), lambda i: (i, 0))],
        out_shape=[jax.ShapeDtypeStruct((t, d), F32), jax.ShapeDtypeStruct((t, d), F32),
                   jax.ShapeDtypeStruct((t, LANES), F32), jax.ShapeDtypeStruct((t, LANES), F32),
                   jax.ShapeDtypeStruct((t, LANES), F32), jax.ShapeDtypeStruct((8, LANES), F32),
                   jax.ShapeDtypeStruct((t // tm * 8, LANES), F32)],
        scratch_shapes=[pltpu.VMEM((8, LANES), F32)],
        compiler_params=_params("arbitrary"),
        name="outproj_router",
    )(h, *flat, w_out.astype(BF16), g.reshape(1, d), rw_hi, rw_lo, rb)


def _plan_kernel(cnt_ref, start_out, plan_out):
    cnt = cnt_ref[...]
    padded = jnp.ceil(cnt * (1.0 / MOE_BLOCK)) * MOE_BLOCK
    upto = jnp.where(_iota((LANES, LANES), 0) <= _iota((LANES, LANES), 1), 1.0, 0.0)
    pad_end = _dot_hi(padded, upto)
    pad_start = pad_end - padded
    start_out[...] = pad_start
    ends = jnp.transpose(jnp.broadcast_to(pad_end[0:1, :], (LANES, LANES)))
    expert_ok = _iota((LANES, LANES), 0) < N_EXPERTS
    n_active = pad_end[0:1, N_EXPERTS - 1:N_EXPERTS] * (1.0 / MOE_BLOCK)
    rows = [jnp.broadcast_to(n_active, (1, LANES)), cnt[0:1, :], pad_start[0:1, :], padded[0:1, :]]
    for r in range(PLAN_EXPERT_ROWS):
        blk_start = ((_iota((LANES, LANES), 1) + r * LANES) * MOE_BLOCK).astype(F32)
        done = jnp.where(jnp.logical_and(expert_ok, ends <= blk_start), 1.0, 0.0)
        rows.append(jnp.minimum(jnp.sum(done, axis=0, keepdims=True), N_EXPERTS - 1.0))
    plan_out[...] = jnp.concatenate(rows, axis=0).astype(I32)


def _plan(counts, n_blocks):
    assert n_blocks <= PLAN_EXPERT_ROWS * LANES
    return pl.pallas_call(
        _plan_kernel,
        out_shape=[jax.ShapeDtypeStruct((8, LANES), F32), jax.ShapeDtypeStruct((8, LANES), I32)],
        name="moe_plan",
    )(counts)


def _dest_kernel(sel_ref, gate_ref, rank_ref, tile_ref, start_ref, lpos_out, lk_out, gk_out, runs_out):
    tm = sel_ref.shape[0]
    sel = sel_ref[...]
    gate = gate_ref[...]
    before_tile = tile_ref[0:1, :]
    in_tile = tile_ref[1:2, :]
    earlier = jnp.where(_iota((LANES, LANES), 0) < _iota((LANES, LANES), 1), 1.0, 0.0)
    run_start = _dot_hi(jnp.broadcast_to(in_tile, (8, LANES)), earlier)[0:1, :]
    lpos = rank_ref[...] + run_start
    lane = _iota((tm, LANES), 1)
    pieces = []
    gk = jnp.zeros((tm, LANES), F32)
    lk = jnp.zeros((tm, LANES), F32)
    for k in range(TOP_K):
        hit = sel == float(k + 1)
        mine = jnp.where(hit, lpos, 0.0)
        pieces.append(mine)
        gk = jnp.where(lane == k, jnp.sum(jnp.where(hit, gate, 0.0), axis=-1, keepdims=True), gk)
        lk = jnp.where(lane == k, jnp.sum(mine, axis=-1, keepdims=True), lk)
    stacked = jnp.concatenate(pieces, axis=1)
    picker = jnp.where(_div_pow2(_iota((8, TOP_K * LANES), 1), LANES) == _iota((8, TOP_K * LANES), 0),
                       1.0, 0.0).astype(BF16)
    lpos_out[...] = sum(_dot_nt(picker, part) for part in _split3(stacked)).astype(I32)
    lk_out[...] = lk
    gk_out[...] = gk
    first_slot = start_ref[0:1, :] + before_tile
    total = jnp.broadcast_to(jnp.sum(in_tile, axis=-1, keepdims=True), (1, LANES))
    runs_out[...] = jnp.concatenate([in_tile, run_start, first_slot, total, jnp.zeros((4, LANES), F32)],
                                    axis=0).astype(I32)


def _dest(sel, gate, rank, tile_counts, pad_start):
    t = sel.shape[0]
    tm = ROW_TILE
    tile = pl.BlockSpec((tm, LANES), lambda i: (i, 0))
    per_tile = pl.BlockSpec((8, LANES), lambda i: (i, 0))
    return pl.pallas_call(
        _dest_kernel,
        grid=(t // tm,),
        in_specs=[tile, tile, tile, per_tile, pl.BlockSpec((8, LANES), lambda i: (0, 0))],
        out_specs=[pl.BlockSpec((8, tm), lambda i: (0, i)), tile, tile, per_tile],
        out_shape=[jax.ShapeDtypeStruct((8, t), I32), jax.ShapeDtypeStruct((t, LANES), F32),
                   jax.ShapeDtypeStruct((t, LANES), F32), jax.ShapeDtypeStruct((t // tm * 8, LANES), I32)],
        compiler_params=_params("parallel"),
        name="moe_dest",
    )(sel, gate, rank, tile_counts, pad_start)


def _copy_run(src, dst, length, n_bits, make_copy, wait):
    units = jnp.right_shift(length, RUN_ALIGN.bit_length() - 1)
    for b in range(n_bits):
        offset = jnp.left_shift(jnp.right_shift(units, b + 1), b + 1) * RUN_ALIGN

        @pl.when(jnp.bitwise_and(jnp.right_shift(units, b), 1) == 1)
        def _():
            copy = make_copy(pl.multiple_of(src + offset, RUN_ALIGN), pl.multiple_of(dst + offset, RUN_ALIGN),
                             RUN_ALIGN << b)
            if wait:
                copy.wait()
            else:
                copy.start()


def _dispatch_kernel(plan_ref, runs_ref, prev_runs_ref, lpos_ref, xn_ref, xs_hbm, sorted_s, sem):
    i = pl.program_id(0)
    tm = xn_ref.shape[0]
    n_local = sorted_s.shape[1]
    slot = i % 2

    def to_slots(buf):
        return lambda src, dst, size: pltpu.make_async_copy(
            sorted_s.at[buf, pl.ds(src, size)], xs_hbm.at[pl.ds(dst, size)], sem.at[buf])

    def start_runs(table, buf):
        def body(e, carry):
            _copy_run(table[RUN_START, e], table[RUN_SLOT, e], table[RUN_LEN, e], RUN_BITS, to_slots(buf),
                      False)
            return carry
        lax.fori_loop(0, N_EXPERTS, body, 0)

    def wait_runs(table, buf):
        _copy_run(0, 0, table[RUN_TOTAL, 0], TOTAL_BITS, lambda src, dst, size: to_slots(buf)(0, 0, size),
                  True)

    position = _iota((n_local, tm), 0)
    lpos = lpos_ref[...]
    place = jnp.zeros((n_local, tm), F32)
    for k in range(TOP_K):
        place = place + jnp.where(position == lpos[k:k + 1, :], 1.0, 0.0)
    sorted_s[slot] = _dot(place.astype(BF16), xn_ref[...].astype(BF16))
    start_runs(runs_ref, slot)

    @pl.when(i > 0)
    def _():
        wait_runs(prev_runs_ref, 1 - slot)

    @pl.when(i == pl.num_programs(0) - 1)
    def _():
        wait_runs(runs_ref, slot)

    @pl.when(i == 0)
    def _():
        def padding(wait):
            def fill(src, dst, size):
                return pltpu.make_async_copy(sorted_s.at[0, pl.ds(0, size)], xs_hbm.at[pl.ds(dst, size)],
                                             sem.at[1])

            def body(e, carry):
                cnt = plan_ref[PLAN_COUNT, e]
                _copy_run(0, plan_ref[PLAN_START, e] + cnt, plan_ref[PLAN_PADDED, e] - cnt, PAD_BITS, fill,
                          wait)
                return carry
            lax.fori_loop(0, N_EXPERTS, body, 0)

        padding(False)
        padding(True)


def _dispatch(plan, runs, lpos, xn, n_slots):
    t, d = xn.shape
    tm = ROW_TILE
    return pl.pallas_call(
        _dispatch_kernel,
        grid=(t // tm,),
        in_specs=[pl.BlockSpec(memory_space=pltpu.SMEM),
                  pl.BlockSpec((8, LANES), lambda i: (i, 0), memory_space=pltpu.SMEM),
                  pl.BlockSpec((8, LANES), lambda i: (jnp.maximum(i - 1, 0), 0), memory_space=pltpu.SMEM),
                  pl.BlockSpec((8, tm), lambda i: (0, i)),
                  pl.BlockSpec((tm, d), lambda i: (i, 0))],
        out_specs=pl.BlockSpec(memory_space=pl.ANY),
        out_shape=jax.ShapeDtypeStruct((n_slots, d), F32),
        scratch_shapes=[pltpu.VMEM((2, LOCAL_ROWS, d), F32), pltpu.SemaphoreType.DMA((2,))],
        compiler_params=_params("arbitrary"),
        name="moe_dispatch",
    )(plan, runs, runs, lpos, xn)


def _ffn_kernel(be_ref, na_ref, plan_ref, x_ref, wgu_hbm, bgu_ref, wd_hbm, bd_ref, o_ref, wgu_in_s, wd_in_s,
                wgu_s, wd32_s, wd_s, sem, *, layer):
    j = pl.program_id(0)
    n_active = na_ref[0]
    active = j < n_active
    expert = be_ref[j]
    new_expert = jnp.logical_or(j == 0, expert != be_ref[jnp.maximum(j - 1, 0)])
    rows_left = plan_ref[PLAN_START, expert] + plan_ref[PLAN_COUNT, expert] - j * MOE_BLOCK

    def fetch(e):
        return (pltpu.make_async_copy(wgu_hbm.at[layer, e], wgu_in_s, sem.at[0]),
                pltpu.make_async_copy(wd_hbm.at[layer, e], wd_in_s, sem.at[1]))

    @pl.when(j == 0)
    def _():
        for copy in fetch(expert):
            copy.start()

    @pl.when(jnp.logical_and(active, new_expert))
    def _():
        for copy in fetch(expert):
            copy.wait()
        wgu_s[...] = wgu_in_s[...].astype(BF16)
        half = LANES // 2
        for c in range(wd_s.shape[0]):
            cols = slice(c * LANES, (c + 1) * LANES)
            for k in range(wd_s.shape[1] // LANES):
                for hb in range(2):
                    src = k * LANES + hb * half
                    wd32_s[c, pl.ds(k * LANES + hb, half, stride=2), :] = wd_in_s[src:src + half, cols]
            wd_s[c] = wd32_s[c].astype(BF16)
        nxt = lax.while_loop(lambda b: jnp.logical_and(b < n_active, be_ref[jnp.minimum(b, n_active - 1)] == expert),
                             lambda b: b + 1, j + 1)

        @pl.when(nxt < n_active)
        def _():
            for copy in fetch(be_ref[jnp.minimum(nxt, n_active - 1)]):
                copy.start()

    def ffn_rows(n_rows):
        x = x_ref[0:n_rows, :].astype(BF16)
        hmid = _dot(x, wgu_s[...]) + bgu_ref[0]
        m = hmid.shape[0]
        even = (_iota((m, LANES), 1) & 1) == 0
        acts = []
        for k in range(hmid.shape[1] // (2 * LANES)):
            h0 = hmid[:, (2 * k) * LANES:(2 * k + 1) * LANES]
            h1 = hmid[:, (2 * k + 1) * LANES:(2 * k + 2) * LANES]
            glu = jnp.where(even, h0, pltpu.roll(h1, 1, 1))
            lin = jnp.where(even, pltpu.roll(h0, LANES - 1, 1), h1)
            glu = jnp.minimum(glu, SWIGLU_LIMIT)
            lin = jnp.clip(lin, -SWIGLU_LIMIT, SWIGLU_LIMIT)
            acts.append((glu * _sigmoid(SWIGLU_ALPHA * glu) * (lin + 1.0)).astype(BF16))
        act = jnp.concatenate(acts, axis=1)
        wd = jnp.concatenate([wd_s[c] for c in range(wd_s.shape[0])], axis=1)
        o_ref[0:n_rows, :] = _dot(act, wd) + bd_ref[0]

    half_block = MOE_BLOCK // 2

    @pl.when(jnp.logical_and(active, rows_left > half_block))
    def _():
        ffn_rows(MOE_BLOCK)

    @pl.when(jnp.logical_and(active, rows_left <= half_block))
    def _():
        ffn_rows(half_block)


def _ffn(layer, block_expert, n_active, plan, xs, wgu, bgu, wd, bd):
    n_slots, d = xs.shape
    dff2 = wgu.shape[3]
    nb = n_slots // MOE_BLOCK
    rows = lambda j, be, na, plan_: (jnp.maximum(jnp.minimum(j, na[0] - 1), 0), 0)
    per_expert = lambda j, be, na, plan_: (layer, be[j], 0, 0)
    grid_spec = pltpu.PrefetchScalarGridSpec(
        num_scalar_prefetch=3,
        grid=(nb,),
        in_specs=[pl.BlockSpec((MOE_BLOCK, d), rows),
                  pl.BlockSpec(memory_space=pl.ANY), pl.BlockSpec((None, 1, 1, dff2), per_expert),
                  pl.BlockSpec(memory_space=pl.ANY), pl.BlockSpec((None, 1, 1, d), per_expert)],
        out_specs=pl.BlockSpec((MOE_BLOCK, d), rows),
        scratch_shapes=[pltpu.VMEM((d, dff2), F32), pltpu.VMEM((dff2 // 2, d), F32),
                        pltpu.VMEM((d, dff2), BF16), pltpu.VMEM((d // LANES, dff2 // 2, LANES), F32),
                        pltpu.VMEM((d // LANES, dff2 // 2, LANES), BF16), pltpu.SemaphoreType.DMA((2,))],
    )
    return pl.pallas_call(
        functools.partial(_ffn_kernel, layer=layer),
        grid_spec=grid_spec,
        out_shape=jax.ShapeDtypeStruct((n_slots, d), F32),
        compiler_params=_params("arbitrary"),
        name="moe_ffn",
    )(block_expert, n_active, plan, xs, wgu, bgu, wd, bd)


def _combine_kernel(runs_ref, next_runs_ref, lk_ref, gk_ref, h_ref, g_ref, ys_hbm, o_ref, sorted_s, sem, *,
                    final_norm):
    i = pl.program_id(0)
    tm = h_ref.shape[0]
    n_local = sorted_s.shape[1]
    slot = i % 2

    def from_slots(buf):
        return lambda src, dst, size: pltpu.make_async_copy(
            ys_hbm.at[pl.ds(src, size)], sorted_s.at[buf, pl.ds(dst, size)], sem.at[buf])

    def start_runs(table, buf):
        def body(e, carry):
            _copy_run(table[RUN_SLOT, e], table[RUN_START, e], table[RUN_LEN, e], RUN_BITS, from_slots(buf),
                      False)
            return carry
        lax.fori_loop(0, N_EXPERTS, body, 0)

    @pl.when(i == 0)
    def _():
        sorted_s[...] = jnp.zeros_like(sorted_s)
        start_runs(runs_ref, slot)

    @pl.when(i + 1 < pl.num_programs(0))
    def _():
        start_runs(next_runs_ref, 1 - slot)

    _copy_run(0, 0, runs_ref[RUN_TOTAL, 0], TOTAL_BITS, lambda src, dst, size: from_slots(slot)(0, 0, size),
              True)
    position = _iota((tm, n_local), 1).astype(F32)
    lk = lk_ref[...]
    gk = gk_ref[...]
    weights = jnp.zeros((tm, n_local), F32)
    for k in range(TOP_K):
        weights = weights + jnp.where(position == lk[:, k:k + 1], gk[:, k:k + 1], 0.0)
    acc = h_ref[...] + _dot(weights.astype(BF16), sorted_s[slot].astype(BF16))
    if final_norm:
        ms = jnp.mean(acc * acc, axis=-1, keepdims=True)
        acc = acc * lax.rsqrt(ms + EPS) * g_ref[...]
    o_ref[...] = acc


def _combine(runs, lk, gk, h, ys, g, final_norm):
    t, d = h.shape
    tm = ROW_TILE
    return pl.pallas_call(
        functools.partial(_combine_kernel, final_norm=final_norm),
        grid=(t // tm,),
        in_specs=[pl.BlockSpec((8, LANES), lambda i: (i, 0), memory_space=pltpu.SMEM),
                  pl.BlockSpec((8, LANES), lambda i: (jnp.minimum(i + 1, t // tm - 1), 0),
                               memory_space=pltpu.SMEM),
                  pl.BlockSpec((tm, LANES), lambda i: (i, 0)),
                  pl.BlockSpec((tm, LANES), lambda i: (i, 0)),
                  pl.BlockSpec((tm, d), lambda i: (i, 0)),
                  pl.BlockSpec((1, d), lambda i: (0, 0)),
                  pl.BlockSpec(memory_space=pl.ANY)],
        out_specs=pl.BlockSpec((tm, d), lambda i: (i, 0)),
        out_shape=jax.ShapeDtypeStruct((t, d), F32),
        scratch_shapes=[pltpu.VMEM((2, LOCAL_ROWS, d), F32), pltpu.SemaphoreType.DMA((2,))],
        compiler_params=_params("arbitrary"),
        name="moe_combine",
    )(runs, runs, lk, gk, h, g.reshape(1, d), ys)


def _moe(h, xn, sel, gate, rank, counts, tile_counts, layer, w_gu, b_gu, w_down, b_down, g_final,
         final_norm):
    t, d = h.shape
    n_slots = t * TOP_K + (t // ROW_TILE) * N_EXPERTS * RUN_ALIGN + N_EXPERTS * MOE_BLOCK
    n_blocks = n_slots // MOE_BLOCK
    pad_start, plan = _plan(counts, n_blocks)
    lpos, lk, gk, runs = _dest(sel, gate, rank, tile_counts, pad_start)
    xs = _dispatch(plan, runs, lpos, xn, n_slots)
    block_expert = plan[PLAN_EXPERTS:].reshape(-1)[:n_blocks]
    n_active = plan[PLAN_ACTIVE, :1]
    depth = w_gu.shape[0]
    ys = _ffn(layer, block_expert, n_active, plan, xs, w_gu, b_gu.reshape(depth, N_EXPERTS, 1, -1), w_down,
              b_down.reshape(depth, N_EXPERTS, 1, d))
    return _combine(runs, lk, gk, h, ys, g_final, final_norm)


def kernel(x, norm_mix_g, w_in, sb_norm_g, rg_conv_w, rg_conv_b, rg_wa, rg_ba, rg_wx, rg_bx,
           rg_lambda, rg_norm_g, hg_lower_bounds, hg_norm_g, m2_conv_w, m2_conv_b, m2_dt_bias,
           m2_a_log, m2_d, m2_norm_g, w_out, norm_ffn_g, router_w, router_b, moe_w_gu, moe_b_gu,
           moe_w_down, moe_b_down, final_norm_g):
    b, l, d = x.shape
    depth = w_in.shape[0]
    t = b * l
    lbs = jnp.cumsum(jax.nn.softmax(hg_lower_bounds.astype(F32), axis=0), axis=0)
    lbs = lbs - lbs[0]
    d_in = w_in.shape[2]
    w_in_p = jnp.zeros((depth, d, PROJ_BLOCKS * GROUP_W), BF16).at[:, :, :d_in].set(w_in.astype(BF16))
    h = x.reshape(t, d)
    for layer in range(depth):
        proj = _norm_inproj(h, norm_mix_g[layer], w_in_p[layer]).reshape(b, l, PROJ_BLOCKS * GROUP_W)
        o_a = _sb_attn(proj, sb_norm_g[layer])
        o_b = _rglru(proj, rg_conv_w[layer], rg_conv_b[layer], rg_wa[layer], rg_ba[layer],
                     rg_wx[layer], rg_bx[layer], rg_lambda[layer], rg_norm_g[layer])
        o_c = _hgrn2(proj, lbs[layer], hg_norm_g[layer])
        o_d = _ssd(proj, m2_conv_w[layer], m2_conv_b[layer], m2_dt_bias[layer], m2_a_log[layer],
                   m2_d[layer], m2_norm_g[layer])
        h, xn, sel, gate, rank, counts, tile_counts = _outproj_router(
            h, (o_a, o_b, o_c, o_d), w_out[layer], norm_ffn_g[layer], router_w[layer],
            router_b[layer])
        h = _moe(h, xn, sel, gate, rank, counts, tile_counts, layer, moe_w_gu, moe_b_gu, moe_w_down,
                 moe_b_down, final_norm_g, layer == depth - 1)
    return h.reshape(b, l, d)
```

```python
import functools
import math

import jax
import jax.numpy as jnp
from jax import lax
from jax.experimental import pallas as pl
from jax.experimental.pallas import tpu as pltpu

F32 = jnp.float32
BF16 = jnp.bfloat16
I32 = jnp.int32
HIGHEST = lax.Precision.HIGHEST
EPS = 1e-6
EXP_UNDERFLOW = -104.0

LANES = 128
GROUP_W = 256
HEAD_DIM = 64
N_HEADS = 4
SSD_STATE = 128
RG_C = 8.0
N_EXPERTS = 32
TOP_K = 4
SWIGLU_LIMIT = 7.0
SWIGLU_ALPHA = 1.702
PROJ_BLOCKS = 14
VMEM_LIMIT = 56 * 1024 * 1024

ROW_TILE = 256
INPROJ_TILE = 512
ATTN_BLOCK = 256
SCAN_TILE = 256
HG_CHUNK = 16
HG_BLOCK = 256
SSD_CHUNK = LANES
MOE_BLOCK = 512
HG_UNROLL = 8

PLAN_ACTIVE, PLAN_COUNT, PLAN_START, PLAN_PADDED, PLAN_EXPERTS = 0, 1, 2, 3, 4
PLAN_EXPERT_ROWS = 8 - PLAN_EXPERTS
RUN_LEN, RUN_START, RUN_SLOT, RUN_TOTAL = 0, 1, 2, 3
RUN_ALIGN = 8
RUN_BITS = (ROW_TILE // RUN_ALIGN).bit_length()
PAD_BITS = (MOE_BLOCK // RUN_ALIGN - 1).bit_length()
LOCAL_ROWS = ROW_TILE * TOP_K + N_EXPERTS * RUN_ALIGN
TOTAL_BITS = (LOCAL_ROWS // RUN_ALIGN).bit_length()


def _params(*sem):
    return pltpu.CompilerParams(dimension_semantics=sem, vmem_limit_bytes=VMEM_LIMIT)


def _dot(a, b):
    return jnp.dot(a, b, preferred_element_type=F32)


def _dot_hi(a, b):
    return jnp.dot(a, b, preferred_element_type=F32, precision=HIGHEST)


def _split3(x):
    hi = x.astype(BF16)
    rest = x - hi.astype(F32)
    mid = rest.astype(BF16)
    lo = (rest - mid.astype(F32)).astype(BF16)
    return hi, mid, lo


def _dot_exact_lhs(m, x):
    return sum(_dot(m, part) for part in _split3(x))


def _dot_exact_rhs(x, m):
    return sum(_dot(part, m) for part in _split3(x))


def _dot_nt(a, b, precision=None):
    return lax.dot_general(a, b, (((1,), (1,)), ((), ())), preferred_element_type=F32,
                           precision=precision)


def _dot_tn(a, b):
    return lax.dot_general(a, b, (((0,), (0,)), ((), ())), preferred_element_type=F32)


def _sigmoid(x):
    return 1.0 / (1.0 + jnp.exp(-x))


def _silu(x):
    return x * _sigmoid(x)


def _softplus(x):
    return jnp.maximum(x, 0.0) + jnp.log(1.0 + jnp.exp(-jnp.abs(x)))


def _iota(shape, dim):
    return lax.broadcasted_iota(I32, shape, dim)


def _div_pow2(x, n):
    assert n & (n - 1) == 0
    return jnp.right_shift(x, int(math.log2(n)))


def _shift_rows(x, d, prev8):
    r = pltpu.roll(x, d, 0)
    p = pltpu.roll(prev8, d, 0)
    head = jnp.where(_iota(prev8.shape, 0) < d, p, r[:8])
    return jnp.concatenate([head, r[8:]], axis=0)


def _causal_conv(x, prev8, w, b):
    k = w.shape[0]
    y = x * w[k - 1:k] + b
    for d in range(1, k):
        y = y + _shift_rows(x, d, prev8) * w[k - 1 - d:k - d]
    return y


def _norm_inproj_kernel(x_ref, g_ref, w_ref, o_ref):
    x = x_ref[...]
    ms = jnp.mean(x * x, axis=-1, keepdims=True)
    xn = (x * lax.rsqrt(ms + EPS) * g_ref[...]).astype(BF16)
    o_ref[...] = _dot(xn, w_ref[...])


def _norm_inproj(h, g, w):
    t, d = h.shape
    n = w.shape[1]
    return pl.pallas_call(
        _norm_inproj_kernel,
        grid=(t // INPROJ_TILE,),
        in_specs=[pl.BlockSpec((INPROJ_TILE, d), lambda i: (i, 0)),
                  pl.BlockSpec((1, d), lambda i: (0, 0)),
                  pl.BlockSpec((d, n), lambda i: (0, 0))],
        out_specs=pl.BlockSpec((INPROJ_TILE, n), lambda i: (i, 0)),
        out_shape=jax.ShapeDtypeStruct((t, n), F32),
        compiler_params=_params("parallel"),
        name="norm_inproj",
    )(h, g.reshape(1, d), w)


def _sb_attn_kernel(q_ref, k_ref, v_ref, g_ref, o_ref, acc_ref, run_ref):
    i = pl.program_id(1)
    blk = ATTN_BLOCK
    scale = HEAD_DIM ** -0.5
    rows = N_HEADS * blk
    below = _iota((rows, blk), 1) < jnp.bitwise_and(_iota((rows, blk), 0), blk - 1)
    r_w = _iota((blk, blk + LANES), 0)
    c_w = _iota((blk, blk + LANES), 1)
    later_sum = jnp.where(jnp.logical_or(c_w >= blk, r_w > c_w), 1.0, 0.0).astype(BF16)
    acc_ref[...] = jnp.zeros_like(acc_ref)
    run_ref[...] = jnp.zeros_like(run_ref)
    q_all = q_ref[0] * scale
    head_of_lane = _div_pow2(_iota((blk, GROUP_W), 1), HEAD_DIM)
    q_heads = jnp.concatenate([jnp.where(head_of_lane == h, q_all, 0.0) for h in range(N_HEADS)],
                              axis=0).astype(BF16)

    def key_block(j, diagonal):
        ks = pl.multiple_of(j * blk, blk)
        z = _dot_nt(q_heads, k_ref[0, pl.ds(ks, blk), :].astype(BF16))
        ls = jnp.minimum(z, 0.0) - jnp.log(1.0 + jnp.exp(-jnp.abs(z)))
        lk = ls - z
        if diagonal:
            lk = jnp.where(below, lk, 0.0)
        lk_hi = lk.astype(BF16)
        lk_lo = (lk - lk_hi.astype(F32)).astype(BF16)
        stacked = _dot(jnp.concatenate([lk_hi, lk_lo], axis=0), later_sum)
        sums = stacked[:rows] + stacked[rows:]
        run = run_ref[...]
        rem = sums[:, :blk] + jnp.concatenate([run] * (blk // LANES), axis=1)
        a = jnp.exp(ls + rem)
        if diagonal:
            a = jnp.where(below, a, 0.0)
        out = _dot(a.astype(BF16), v_ref[0, pl.ds(ks, blk), :].astype(BF16))
        for h in range(N_HEADS):
            sl = slice(h * HEAD_DIM, (h + 1) * HEAD_DIM)
            acc_ref[:, sl] += out[h * blk:(h + 1) * blk, sl]
        run_ref[...] = run + sums[:, blk:]

    key_block(i, True)

    def live():
        return jnp.max(run_ref[...]) > EXP_UNDERFLOW

    def more(carry):
        jj, alive = carry
        return jnp.logical_and(jj <= i, alive)

    def body(carry):
        jj, _ = carry
        key_block(i - jj, False)
        return jj + 1, live()

    lax.while_loop(more, body, (jnp.int32(1), live()))
    o = acc_ref[...]
    ms = jnp.mean(o * o, axis=-1, keepdims=True)
    o_ref[0] = o * lax.rsqrt(ms + EPS) * g_ref[...]


def _sb_attn(proj, g):
    b, l, _ = proj.shape
    blk = ATTN_BLOCK
    return pl.pallas_call(
        _sb_attn_kernel,
        grid=(b, l // blk),
        in_specs=[pl.BlockSpec((1, blk, GROUP_W), lambda bi, i: (bi, i, 0)),
                  pl.BlockSpec((1, l, GROUP_W), lambda bi, i: (bi, 0, 1)),
                  pl.BlockSpec((1, l, GROUP_W), lambda bi, i: (bi, 0, 2)),
                  pl.BlockSpec((1, GROUP_W), lambda bi, i: (0, 0))],
        out_specs=pl.BlockSpec((1, blk, GROUP_W), lambda bi, i: (bi, i, 0)),
        out_shape=jax.ShapeDtypeStruct((b, l, GROUP_W), F32),
        scratch_shapes=[pltpu.VMEM((blk, GROUP_W), F32), pltpu.VMEM((N_HEADS * blk, LANES), F32)],
        compiler_params=_params("parallel", "arbitrary"),
        name="sb_attn",
    )(proj, proj, proj, g.reshape(1, GROUP_W))


def _rglru_kernel(x_ref, gate_ref, cw_ref, cb_ref, wax_ref, ba_ref, bx_ref, lam_ref,
                  ng_ref, o_ref):
    l = x_ref.shape[1]
    w = x_ref.shape[2]
    tc = SCAN_TILE
    rows = _iota((tc, w), 0)
    cw = cw_ref[...]
    cb = cb_ref[...]
    neg_c_sp = -RG_C * _softplus(-lam_ref[...])

    def chunk(c, carry):
        h_prev, tail = carry
        s = pl.multiple_of(c * tc, tc)
        x = x_ref[0, pl.ds(s, tc), :]
        xc = _causal_conv(x, tail, cw, cb)
        xcb = xc.astype(BF16)
        gates = _dot(xcb, wax_ref[...])
        r = _sigmoid(gates[:, :w] + ba_ref[...])
        ig = _sigmoid(gates[:, w:] + bx_ref[...])
        log_a = r * neg_c_sp
        a = jnp.exp(log_a)
        th = jnp.tanh(log_a)
        mult = jnp.sqrt(-2.0 * th / (1.0 - th))
        mult = jnp.where(rows + s == 0, 1.0, mult)
        u = mult * ig * xc
        d = 1
        while d < tc:
            keep = rows >= d
            a_s = jnp.where(keep, pltpu.roll(a, d, 0), 1.0)
            u_s = jnp.where(keep, pltpu.roll(u, d, 0), 0.0)
            u = a * u_s + u
            a = a * a_s
            d *= 2
        hs = u + a * h_prev
        gate = gate_ref[0, pl.ds(s, tc), :]
        gelu = 0.5 * gate * (1.0 + jnp.tanh(math.sqrt(2.0 / math.pi) * (gate + 0.044715 * gate * gate * gate)))
        o = hs * gelu
        ms = jnp.mean(o * o, axis=-1, keepdims=True)
        o_ref[0, pl.ds(s, tc), :] = o * lax.rsqrt(ms + EPS) * ng_ref[...]
        return hs[tc - 1:tc], x[tc - 8:tc]

    lax.fori_loop(0, l // tc, chunk, (jnp.zeros((1, w), F32), jnp.zeros((8, w), F32)))


def _block_diag(wb):
    n, d, _ = wb.shape
    same = jnp.eye(n, dtype=bool)[:, None, :, None]
    return jnp.where(same, wb[:, :, None, :], 0).reshape(n * d, n * d)


def _rglru(proj, conv_w, conv_b, wa, ba, wx, bx, lam, ng):
    b, l, _ = proj.shape
    w = GROUP_W
    row = lambda v: v.reshape(1, w)
    full = lambda shape: pl.BlockSpec(shape, lambda bi: (0,) * len(shape))
    return pl.pallas_call(
        _rglru_kernel,
        grid=(b,),
        in_specs=[pl.BlockSpec((1, l, w), lambda bi: (bi, 0, 3)),
                  pl.BlockSpec((1, l, w), lambda bi: (bi, 0, 4)),
                  full((conv_w.shape[0], w)), full((1, w)), full((w, 2 * w)), full((1, w)),
                  full((1, w)), full((1, w)), full((1, w))],
        out_specs=pl.BlockSpec((1, l, w), lambda bi: (bi, 0, 0)),
        out_shape=jax.ShapeDtypeStruct((b, l, w), F32),
        compiler_params=_params("parallel"),
        name="rglru",
    )(proj, proj, conv_w, row(conv_b),
      jnp.concatenate([_block_diag(wa), _block_diag(wx)], axis=1).astype(BF16), row(ba), row(bx), row(lam),
      row(ng))


def _hgrn2_kernel(q_ref, f_ref, i_ref, g_ref, lb_ref, ng_ref, o_ref,
                  qe_s, ke_s, kl_s, dl_s, o_s, inc_s):
    l = q_ref.shape[1]
    w = q_ref.shape[2]
    tile = SCAN_TILE
    c = HG_CHUNK
    r_t = _iota((tile, tile), 0)
    c_t = _iota((tile, tile), 1)
    same = _div_pow2(r_t, c) == _div_pow2(c_t, c)
    cum_m = jnp.where(jnp.logical_and(same, c_t <= r_t), 1.0, 0.0).astype(BF16)
    tot_m = jnp.where(same, 1.0, 0.0).astype(BF16)
    lb = lb_ref[...]

    def prep(t, carry):
        s = pl.multiple_of(t * tile, tile)
        forget = lb + (1.0 - lb) * _sigmoid(f_ref[0, pl.ds(s, tile), :])
        log_f = jnp.log(forget)
        key = 1.0 - forget
        parts = _split3(log_f)
        bcum = sum(_dot(cum_m, part) for part in parts)
        blast = sum(_dot(tot_m, part) for part in parts)
        q = _silu(q_ref[0, pl.ds(s, tile), :])
        qe_s[pl.ds(s, tile), :] = (q * jnp.exp(bcum)).astype(BF16)
        ke_s[pl.ds(s, tile), :] = (key * jnp.exp(-bcum)).astype(BF16)
        kl_s[pl.ds(s, tile), :] = key * jnp.exp(blast - bcum)
        dl_s[pl.ds(s, tile), :] = jnp.exp(blast)
        return carry

    lax.fori_loop(0, l // tile, prep, 0)

    blk = HG_BLOCK
    per = blk // c
    q_row = jnp.bitwise_and(_iota((N_HEADS * blk, blk), 0), blk - 1)
    k_row = _iota((N_HEADS * blk, blk), 1)
    within = jnp.logical_and(_div_pow2(q_row, c) == _div_pow2(k_row, c), k_row <= q_row)
    own = (_div_pow2(_iota((blk, per * HEAD_DIM), 0), c)
           == _div_pow2(_iota((blk, per * HEAD_DIM), 1), HEAD_DIM))
    head_of_lane = _div_pow2(_iota((blk, w), 1), HEAD_DIM)

    def local(m, carry):
        s = pl.multiple_of(m * blk, blk)
        qe = qe_s[pl.ds(s, blk), :]
        ke = ke_s[pl.ds(s, blk), :]
        kl = kl_s[pl.ds(s, blk), :]
        v = i_ref[0, pl.ds(s, blk), :]
        q_heads = jnp.concatenate([jnp.where(head_of_lane == h, qe, jnp.zeros_like(qe))
                                   for h in range(N_HEADS)], axis=0)
        sc = jnp.where(within, _dot_nt(q_heads, ke), 0.0)
        out = _dot(sc.astype(BF16), v.astype(BF16))
        o_s[pl.ds(s, blk), :] = jnp.concatenate(
            [out[h * blk:(h + 1) * blk, h * HEAD_DIM:(h + 1) * HEAD_DIM] for h in range(N_HEADS)], axis=1)
        for h in range(N_HEADS):
            sl = slice(h * HEAD_DIM, (h + 1) * HEAD_DIM)
            wide = jnp.where(own, jnp.concatenate([kl[:, sl]] * per, axis=1), 0.0)
            inc = _dot_tn(v[:, sl], wide)
            for j in range(per):
                inc_s[m * per + j, :, sl] = inc[:, j * HEAD_DIM:(j + 1) * HEAD_DIM]
        return carry

    lax.fori_loop(0, l // blk, local, 0)

    head_of_chunk_lane = _div_pow2(_iota((c, w), 1), HEAD_DIM)

    def step(n, state):
        s = pl.multiple_of(n * c, c)
        qe = qe_s[pl.ds(s, c), :]
        dl = dl_s[pl.ds(s, 8), :][0:1]
        q_heads = jnp.concatenate([jnp.where(head_of_chunk_lane == h, qe, jnp.zeros_like(qe))
                                   for h in range(N_HEADS)], axis=0)
        carried = _dot_nt(q_heads, state.astype(BF16))
        o_s[pl.ds(s, c), :] += jnp.concatenate([carried[h * c:(h + 1) * c] for h in range(N_HEADS)], axis=1)
        return dl * state + inc_s[n]

    lax.fori_loop(0, l // c, step, jnp.zeros((HEAD_DIM, w), F32), unroll=HG_UNROLL)

    head_avg = jnp.where(_div_pow2(_iota((w, w), 0), HEAD_DIM) == _div_pow2(_iota((w, w), 1), HEAD_DIM),
                         1.0 / HEAD_DIM, 0.0).astype(BF16)

    def finish(t, carry):
        s = pl.multiple_of(t * tile, tile)
        o = o_s[pl.ds(s, tile), :]
        ms = _dot_exact_rhs(o * o, head_avg)
        o_ref[0, pl.ds(s, tile), :] = (o * lax.rsqrt(ms + EPS) * ng_ref[...]
                                       * _silu(g_ref[0, pl.ds(s, tile), :]))
        return carry

    lax.fori_loop(0, l // tile, finish, 0)


def _hgrn2(proj, lb, ng):
    b, l, _ = proj.shape
    w = GROUP_W
    full = lambda shape: pl.BlockSpec(shape, lambda bi: (0,) * len(shape))
    col = lambda j: pl.BlockSpec((1, l, w), lambda bi: (bi, 0, j))
    return pl.pallas_call(
        _hgrn2_kernel,
        grid=(b,),
        in_specs=[col(5), col(6), col(7), col(8), full((1, w)), full((1, w))],
        out_specs=pl.BlockSpec((1, l, w), lambda bi: (bi, 0, 0)),
        out_shape=jax.ShapeDtypeStruct((b, l, w), F32),
        scratch_shapes=[pltpu.VMEM((l, w), BF16), pltpu.VMEM((l, w), BF16), pltpu.VMEM((l, w), F32),
                        pltpu.VMEM((l, w), F32), pltpu.VMEM((l, w), F32),
                        pltpu.VMEM((l // HG_CHUNK, HEAD_DIM, w), F32)],
        compiler_params=_params("parallel"),
        name="hgrn2",
    )(proj, proj, proj, proj, lb.reshape(1, w), ng.reshape(1, w))


def _ssd_kernel(z_ref, x_ref, bm_ref, cm_ref, dt_ref, cw_ref, cb_ref, dtb_ref, aneg_ref,
                dskip_ref, ng_ref, o_ref):
    l = z_ref.shape[1]
    w = GROUP_W
    q = SSD_CHUNK
    p = HEAD_DIM
    r_q = _iota((q, q), 0)
    c_q = _iota((q, q), 1)
    tril = c_q <= r_q
    cum_m = jnp.where(tril, 1.0, 0.0).astype(BF16)
    cw = cw_ref[...]
    cb = cb_ref[...]

    def chunk(n, carry):
        state, tails = carry
        s = pl.multiple_of(n * q, q)
        parts = []
        new_tails = []
        for j, ref in enumerate((x_ref, bm_ref, cm_ref)):
            raw = ref[0, pl.ds(s, q), :]
            sl = slice(j * w, (j + 1) * w)
            parts.append(_silu(_causal_conv(raw, tails[j], cw[:, sl], cb[:, sl])))
            new_tails.append(raw[q - 8:q])
        xs, bm, cm = parts
        dt = _softplus(dt_ref[0, pl.ds(s, q), :] + dtb_ref[...])
        dt_rep = jnp.concatenate([jnp.broadcast_to(dt[:, h:h + 1], (q, p)) for h in range(N_HEADS)],
                                 axis=1)
        a_cs = _dot_exact_lhs(cum_m, dt_rep * aneg_ref[...])
        a_rows = [jnp.transpose(a_cs[:, g * LANES:(g + 1) * LANES]) for g in range(w // LANES)]
        xd = xs * dt_rep
        groups = w // SSD_STATE
        heads_per_group = N_HEADS // groups
        cgs = [cm[:, g * SSD_STATE:(g + 1) * SSD_STATE].astype(BF16) for g in range(groups)]
        cbs = [_dot_nt(cgs[g], bm[:, g * SSD_STATE:(g + 1) * SSD_STATE].astype(BF16)) for g in range(groups)]
        masked = []
        for h in range(N_HEADS):
            row = (h * p) % LANES
            a_row = a_rows[(h * p) // LANES][row:row + 1, :]
            a_col = a_cs[:, h * p:h * p + 1]
            decay = jnp.exp(jnp.where(tril, a_col - a_row, -jnp.inf))
            masked.append((cbs[h // heads_per_group] * decay).astype(BF16))
        within = _dot(jnp.concatenate(masked, axis=0), xd.astype(BF16))
        y = jnp.concatenate([within[h * q:(h + 1) * q, h * p:(h + 1) * p] for h in range(N_HEADS)], axis=1)
        to_end = xd * jnp.exp(a_cs[q - 1:q, :] - a_cs)
        carried = []
        new_state = []
        for g in range(groups):
            gs = slice(g * SSD_STATE, (g + 1) * SSD_STATE)
            st = state[g]
            carried.append(_dot_nt(cgs[g], st.astype(BF16)))
            chunk_decay = jnp.exp(a_rows[g][:, q - 1:q])
            new_state.append(chunk_decay * st + _dot_tn(to_end[:, gs], bm[:, gs]))
        y = y + jnp.exp(a_cs) * jnp.concatenate(carried, axis=1) + xs * dskip_ref[...]
        y = y * _silu(z_ref[0, pl.ds(s, q), :])
        halves = []
        for gi in range(2):
            yg = y[:, gi * LANES:(gi + 1) * LANES]
            ms = jnp.mean(yg * yg, axis=-1, keepdims=True)
            halves.append(yg * lax.rsqrt(ms + EPS))
        o_ref[0, pl.ds(s, q), :] = jnp.concatenate(halves, axis=1) * ng_ref[...]
        return tuple(new_state), tuple(new_tails)

    assert SSD_STATE == LANES and (N_HEADS // (w // SSD_STATE)) * p == LANES
    zero = jnp.zeros((LANES, SSD_STATE), F32)
    tail0 = jnp.zeros((8, w), F32)
    lax.fori_loop(0, l // q, chunk, ((zero,) * (w // SSD_STATE), (tail0,) * 3), unroll=2)


def _ssd(proj, conv_w, conv_b, dt_bias, a_log, d_skip, ng):
    b, l, _ = proj.shape
    w = GROUP_W
    full = lambda shape: pl.BlockSpec(shape, lambda bi: (0,) * len(shape))
    col = lambda j: pl.BlockSpec((1, l, w), lambda bi: (bi, 0, j))
    pad_heads = lambda v: jnp.zeros((1, w), F32).at[0, :N_HEADS].set(v)
    return pl.pallas_call(
        _ssd_kernel,
        grid=(b,),
        in_specs=[col(9), col(10), col(11), col(12), col(13),
                  full((conv_w.shape[0], 3 * w)), full((1, 3 * w)), full((1, w)), full((1, w)),
                  full((1, w)), full((1, w))],
        out_specs=pl.BlockSpec((1, l, w), lambda bi: (bi, 0, 0)),
        out_shape=jax.ShapeDtypeStruct((b, l, w), F32),
        compiler_params=_params("parallel"),
        name="ssd",
    )(proj, proj, proj, proj, proj, conv_w, conv_b.reshape(1, 3 * w), pad_heads(dt_bias),
      jnp.repeat(-jnp.exp(a_log), HEAD_DIM).reshape(1, w), jnp.repeat(d_skip, HEAD_DIM).reshape(1, w),
      ng.reshape(1, w))


def _outproj_router_kernel(h_ref, oa_ref, ob_ref, oc_ref, od_ref, wo_ref, g_ref, rw_ref, rb_ref,
                           h_out, xn_out, sel_out, gate_out, rank_out, cnt_out, tile_out, cnt_s):
    i = pl.program_id(0)
    tm = h_ref.shape[0]

    @pl.when(i == 0)
    def _():
        cnt_s[...] = jnp.zeros_like(cnt_s)

    mix = jnp.concatenate([ref[...].astype(BF16) for ref in (oa_ref, ob_ref, oc_ref, od_ref)], axis=1)
    acc = h_ref[...] + _dot(mix, wo_ref[...])
    h_out[...] = acc
    ms = jnp.mean(acc * acc, axis=-1, keepdims=True)
    xn = acc * lax.rsqrt(ms + EPS) * g_ref[...]
    xn_out[...] = xn
    lane = _iota((tm, LANES), 1)
    lane_f = lane.astype(F32)
    xn_hi = xn.astype(BF16)
    xn_lo = (xn - xn_hi.astype(F32)).astype(BF16)
    both = _dot(jnp.concatenate([xn_hi, xn_lo], axis=0), rw_ref[...])
    router = both[:tm, :LANES] + both[tm:, :LANES] + both[:tm, LANES:]
    logits = jnp.where(lane < N_EXPERTS, router + rb_ref[...], -jnp.inf)
    sel = jnp.zeros((tm, LANES), F32)
    tops = []
    for k in range(TOP_K):
        m = jnp.max(logits, axis=-1, keepdims=True)
        idx = jnp.min(jnp.where(logits == m, lane_f, float(LANES)), axis=-1, keepdims=True)
        hit = lane_f == idx
        sel = jnp.where(hit, float(k + 1), sel)
        logits = jnp.where(hit, -jnp.inf, logits)
        tops.append(m)
    exps = [jnp.exp(m - tops[0]) for m in tops]
    denom = exps[0] + exps[1] + exps[2] + exps[3]
    gate = jnp.zeros((tm, LANES), F32)
    for k in range(TOP_K):
        gate = jnp.where(sel == float(k + 1), exps[k] / denom, gate)
    sel_out[...] = sel
    gate_out[...] = gate
    member = jnp.where(sel > 0.0, 1.0, 0.0)
    before = jnp.where(_iota((tm, tm), 1) < _iota((tm, tm), 0), 1.0, 0.0).astype(BF16)
    carry = cnt_s[0:1, :]
    rank_out[...] = _dot(before, member.astype(BF16))
    in_tile = jnp.ceil(jnp.sum(member, axis=0, keepdims=True) * (1.0 / RUN_ALIGN)) * RUN_ALIGN
    total = carry + in_tile
    tile_out[...] = jnp.concatenate([carry, in_tile, jnp.zeros((6, LANES), F32)], axis=0)
    cnt_s[...] = jnp.broadcast_to(total, cnt_s.shape)
    cnt_out[...] = jnp.broadcast_to(total, cnt_out.shape)


def _outproj_router(h, outs, w_out, g, router_w, router_b):
    t, d = h.shape
    tm = ROW_TILE
    rw = jnp.zeros((d, LANES), F32).at[:, :N_EXPERTS].set(router_w)
    rb = jnp.zeros((1, LANES), F32).at[0, :N_EXPERTS].set(router_b)
    rw_hi = rw.astype(BF16)
    rw_lo = (rw - rw_hi.astype(F32)).astype(BF16)
    tile = lambda width: pl.BlockSpec((tm, width), lambda i: (i, 0))
    full = lambda shape: pl.BlockSpec(shape, lambda i: (0,) * len(shape))
    flat = [o.reshape(t, GROUP_W) for o in outs]
    return pl.pallas_call(
        _outproj_router_kernel,
        grid=(t // tm,),
        in_specs=[tile(d)] + [tile(GROUP_W)] * 4 + [full((d, d)), full((1, d)), full((d, 2 * LANES)),
                                                   full((1, LANES))],
        out_specs=[tile(d), tile(d), tile(LANES), tile(LANES), tile(LANES), full((8, LANES)),
                   pl.BlockSpec((8, LANES), lambda i: (i, 0))],
        out_shape=[jax.ShapeDtypeStruct((t, d), F32), jax.ShapeDtypeStruct((t, d), F32),
                   jax.ShapeDtypeStruct((t, LANES), F32), jax.ShapeDtypeStruct((t, LANES), F32),
                   jax.ShapeDtypeStruct((t, LANES), F32), jax.ShapeDtypeStruct((8, LANES), F32),
                   jax.ShapeDtypeStruct((t // tm * 8, LANES), F32)],
        scratch_shapes=[pltpu.VMEM((8, LANES), F32)],
        compiler_params=_params("arbitrary"),
        name="outproj_router",
    )(h, *flat, w_out.astype(BF16), g.reshape(1, d), jnp.concatenate([rw_hi, rw_lo], axis=1), rb)


def _plan_kernel(cnt_ref, start_out, plan_out):
    cnt = cnt_ref[...]
    padded = jnp.ceil(cnt * (1.0 / MOE_BLOCK)) * MOE_BLOCK
    upto = jnp.where(_iota((LANES, LANES), 0) <= _iota((LANES, LANES), 1), 1.0, 0.0)
    pad_end = _dot_hi(padded, upto)
    pad_start = pad_end - padded
    start_out[...] = pad_start
    ends = jnp.transpose(jnp.broadcast_to(pad_end[0:1, :], (LANES, LANES)))
    expert_ok = _iota((LANES, LANES), 0) < N_EXPERTS
    n_active = pad_end[0:1, N_EXPERTS - 1:N_EXPERTS] * (1.0 / MOE_BLOCK)
    rows = [jnp.broadcast_to(n_active, (1, LANES)), cnt[0:1, :], pad_start[0:1, :], padded[0:1, :]]
    for r in range(PLAN_EXPERT_ROWS):
        blk_start = ((_iota((LANES, LANES), 1) + r * LANES) * MOE_BLOCK).astype(F32)
        done = jnp.where(jnp.logical_and(expert_ok, ends <= blk_start), 1.0, 0.0)
        rows.append(jnp.minimum(jnp.sum(done, axis=0, keepdims=True), N_EXPERTS - 1.0))
    plan_out[...] = jnp.concatenate(rows, axis=0).astype(I32)


def _plan(counts, n_blocks):
    assert n_blocks <= PLAN_EXPERT_ROWS * LANES
    return pl.pallas_call(
        _plan_kernel,
        out_shape=[jax.ShapeDtypeStruct((8, LANES), F32), jax.ShapeDtypeStruct((8, LANES), I32)],
        name="moe_plan",
    )(counts)


def _dest_kernel(sel_ref, gate_ref, rank_ref, tile_ref, start_ref, lpos_out, lk_out, gk_out, runs_out):
    tm = sel_ref.shape[0]
    sel = sel_ref[...]
    gate = gate_ref[...]
    before_tile = tile_ref[0:1, :]
    in_tile = tile_ref[1:2, :]
    earlier = jnp.where(_iota((LANES, LANES), 0) < _iota((LANES, LANES), 1), 1.0, 0.0)
    run_start = _dot_hi(jnp.broadcast_to(in_tile, (8, LANES)), earlier)[0:1, :]
    lpos = rank_ref[...] + run_start
    lane = _iota((tm, LANES), 1)
    pieces = []
    gk = jnp.zeros((tm, LANES), F32)
    lk = jnp.zeros((tm, LANES), F32)
    for k in range(TOP_K):
        hit = sel == float(k + 1)
        mine = jnp.where(hit, lpos, 0.0)
        pieces.append(mine)
        gk = jnp.where(lane == k, jnp.sum(jnp.where(hit, gate, 0.0), axis=-1, keepdims=True), gk)
        lk = jnp.where(lane == k, jnp.sum(mine, axis=-1, keepdims=True), lk)
    stacked = jnp.concatenate(pieces, axis=1)
    picker = jnp.where(_div_pow2(_iota((8, TOP_K * LANES), 1), LANES) == _iota((8, TOP_K * LANES), 0),
                       1.0, 0.0).astype(BF16)
    lpos_out[...] = sum(_dot_nt(picker, part) for part in _split3(stacked)).astype(I32)
    lk_out[...] = lk
    gk_out[...] = gk
    first_slot = start_ref[0:1, :] + before_tile
    total = jnp.broadcast_to(jnp.sum(in_tile, axis=-1, keepdims=True), (1, LANES))
    runs_out[...] = jnp.concatenate([in_tile, run_start, first_slot, total, jnp.zeros((4, LANES), F32)],
                                    axis=0).astype(I32)


def _dest(sel, gate, rank, tile_counts, pad_start):
    t = sel.shape[0]
    tm = ROW_TILE
    tile = pl.BlockSpec((tm, LANES), lambda i: (i, 0))
    per_tile = pl.BlockSpec((8, LANES), lambda i: (i, 0))
    return pl.pallas_call(
        _dest_kernel,
        grid=(t // tm,),
        in_specs=[tile, tile, tile, per_tile, pl.BlockSpec((8, LANES), lambda i: (0, 0))],
        out_specs=[pl.BlockSpec((8, tm), lambda i: (0, i)), tile, tile, per_tile],
        out_shape=[jax.ShapeDtypeStruct((8, t), I32), jax.ShapeDtypeStruct((t, LANES), F32),
                   jax.ShapeDtypeStruct((t, LANES), F32), jax.ShapeDtypeStruct((t // tm * 8, LANES), I32)],
        compiler_params=_params("parallel"),
        name="moe_dest",
    )(sel, gate, rank, tile_counts, pad_start)


def _copy_run(src, dst, length, n_bits, make_copy, wait):
    units = jnp.right_shift(length, RUN_ALIGN.bit_length() - 1)
    for b in range(n_bits):
        offset = jnp.left_shift(jnp.right_shift(units, b + 1), b + 1) * RUN_ALIGN

        @pl.when(jnp.bitwise_and(jnp.right_shift(units, b), 1) == 1)
        def _():
            copy = make_copy(pl.multiple_of(src + offset, RUN_ALIGN), pl.multiple_of(dst + offset, RUN_ALIGN),
                             RUN_ALIGN << b)
            if wait:
                copy.wait()
            else:
                copy.start()


def _dispatch_kernel(plan_ref, runs_ref, prev_runs_ref, lpos_ref, xn_ref, xs_hbm, sorted_s, sem):
    i = pl.program_id(0)
    tm = xn_ref.shape[0]
    n_local = sorted_s.shape[1]
    slot = i % 2

    def to_slots(buf):
        return lambda src, dst, size: pltpu.make_async_copy(
            sorted_s.at[buf, pl.ds(src, size)], xs_hbm.at[pl.ds(dst, size)], sem.at[buf])

    def start_runs(table, buf):
        def body(e, carry):
            _copy_run(table[RUN_START, e], table[RUN_SLOT, e], table[RUN_LEN, e], RUN_BITS, to_slots(buf),
                      False)
            return carry
        lax.fori_loop(0, N_EXPERTS, body, 0)

    def wait_runs(table, buf):
        _copy_run(0, 0, table[RUN_TOTAL, 0], TOTAL_BITS, lambda src, dst, size: to_slots(buf)(0, 0, size),
                  True)

    position = _iota((n_local, tm), 0)
    lpos = lpos_ref[...]
    place = jnp.zeros((n_local, tm), F32)
    for k in range(TOP_K):
        place = place + jnp.where(position == lpos[k:k + 1, :], 1.0, 0.0)
    sorted_s[slot] = _dot(place.astype(BF16), xn_ref[...].astype(BF16))
    start_runs(runs_ref, slot)

    @pl.when(i > 0)
    def _():
        wait_runs(prev_runs_ref, 1 - slot)

    @pl.when(i == pl.num_programs(0) - 1)
    def _():
        wait_runs(runs_ref, slot)

    @pl.when(i == 0)
    def _():
        def padding(wait):
            def fill(src, dst, size):
                return pltpu.make_async_copy(sorted_s.at[0, pl.ds(0, size)], xs_hbm.at[pl.ds(dst, size)],
                                             sem.at[1])

            def body(e, carry):
                cnt = plan_ref[PLAN_COUNT, e]
                _copy_run(0, plan_ref[PLAN_START, e] + cnt, plan_ref[PLAN_PADDED, e] - cnt, PAD_BITS, fill,
                          wait)
                return carry
            lax.fori_loop(0, N_EXPERTS, body, 0)

        padding(False)
        padding(True)


def _dispatch(plan, runs, lpos, xn, n_slots):
    t, d = xn.shape
    tm = ROW_TILE
    return pl.pallas_call(
        _dispatch_kernel,
        grid=(t // tm,),
        in_specs=[pl.BlockSpec(memory_space=pltpu.SMEM),
                  pl.BlockSpec((8, LANES), lambda i: (i, 0), memory_space=pltpu.SMEM),
                  pl.BlockSpec((8, LANES), lambda i: (jnp.maximum(i - 1, 0), 0), memory_space=pltpu.SMEM),
                  pl.BlockSpec((8, tm), lambda i: (0, i)),
                  pl.BlockSpec((tm, d), lambda i: (i, 0))],
        out_specs=pl.BlockSpec(memory_space=pl.ANY),
        out_shape=jax.ShapeDtypeStruct((n_slots, d), F32),
        scratch_shapes=[pltpu.VMEM((2, LOCAL_ROWS, d), F32), pltpu.SemaphoreType.DMA((2,))],
        compiler_params=_params("arbitrary"),
        name="moe_dispatch",
    )(plan, runs, runs, lpos, xn)


def _ffn_kernel(be_ref, na_ref, plan_ref, x_ref, wgu_hbm, bgu_ref, wd_hbm, bd_ref, o_ref, wgu_in_s, wd_in_s,
                wgu_s, wd32_s, wd_s, sem, *, layer):
    j = pl.program_id(0)
    n_active = na_ref[0]
    active = j < n_active
    expert = be_ref[j]
    new_expert = jnp.logical_or(j == 0, expert != be_ref[jnp.maximum(j - 1, 0)])
    rows_left = plan_ref[PLAN_START, expert] + plan_ref[PLAN_COUNT, expert] - j * MOE_BLOCK

    def fetch(e):
        return (pltpu.make_async_copy(wgu_hbm.at[layer, e], wgu_in_s, sem.at[0]),
                pltpu.make_async_copy(wd_hbm.at[layer, e], wd_in_s, sem.at[1]))

    @pl.when(j == 0)
    def _():
        for copy in fetch(expert):
            copy.start()

    @pl.when(jnp.logical_and(active, new_expert))
    def _():
        for copy in fetch(expert):
            copy.wait()
        wgu_s[...] = wgu_in_s[...].astype(BF16)
        half = LANES // 2
        for c in range(wd_s.shape[0]):
            cols = slice(c * LANES, (c + 1) * LANES)
            for k in range(wd_s.shape[1] // LANES):
                for hb in range(2):
                    src = k * LANES + hb * half
                    wd32_s[c, pl.ds(k * LANES + hb, half, stride=2), :] = wd_in_s[src:src + half, cols]
            wd_s[c] = wd32_s[c].astype(BF16)
        nxt = lax.while_loop(lambda b: jnp.logical_and(b < n_active, be_ref[jnp.minimum(b, n_active - 1)] == expert),
                             lambda b: b + 1, j + 1)

        @pl.when(nxt < n_active)
        def _():
            for copy in fetch(be_ref[jnp.minimum(nxt, n_active - 1)]):
                copy.start()

    def ffn_rows(n_rows):
        x = x_ref[0:n_rows, :].astype(BF16)
        hmid = _dot(x, wgu_s[...]) + bgu_ref[0]
        m = hmid.shape[0]
        even = (_iota((m, LANES), 1) & 1) == 0
        acts = []
        for k in range(hmid.shape[1] // (2 * LANES)):
            h0 = hmid[:, (2 * k) * LANES:(2 * k + 1) * LANES]
            h1 = hmid[:, (2 * k + 1) * LANES:(2 * k + 2) * LANES]
            glu = jnp.where(even, h0, pltpu.roll(h1, 1, 1))
            lin = jnp.where(even, pltpu.roll(h0, LANES - 1, 1), h1)
            glu = jnp.minimum(glu, SWIGLU_LIMIT)
            lin = jnp.clip(lin, -SWIGLU_LIMIT, SWIGLU_LIMIT)
            acts.append((glu * _sigmoid(SWIGLU_ALPHA * glu) * (lin + 1.0)).astype(BF16))
        act = jnp.concatenate(acts, axis=1)
        wd = jnp.concatenate([wd_s[c] for c in range(wd_s.shape[0])], axis=1)
        o_ref[0:n_rows, :] = _dot(act, wd) + bd_ref[0]

    half_block = MOE_BLOCK // 2

    @pl.when(jnp.logical_and(active, rows_left > half_block))
    def _():
        ffn_rows(MOE_BLOCK)

    @pl.when(jnp.logical_and(active, rows_left <= half_block))
    def _():
        ffn_rows(half_block)


def _ffn(layer, block_expert, n_active, plan, xs, wgu, bgu, wd, bd):
    n_slots, d = xs.shape
    dff2 = wgu.shape[3]
    nb = n_slots // MOE_BLOCK
    rows = lambda j, be, na, plan_: (jnp.maximum(jnp.minimum(j, na[0] - 1), 0), 0)
    per_expert = lambda j, be, na, plan_: (layer, be[j], 0, 0)
    grid_spec = pltpu.PrefetchScalarGridSpec(
        num_scalar_prefetch=3,
        grid=(nb,),
        in_specs=[pl.BlockSpec((MOE_BLOCK, d), rows),
                  pl.BlockSpec(memory_space=pl.ANY), pl.BlockSpec((None, 1, 1, dff2), per_expert),
                  pl.BlockSpec(memory_space=pl.ANY), pl.BlockSpec((None, 1, 1, d), per_expert)],
        out_specs=pl.BlockSpec((MOE_BLOCK, d), rows),
        scratch_shapes=[pltpu.VMEM((d, dff2), F32), pltpu.VMEM((dff2 // 2, d), F32),
                        pltpu.VMEM((d, dff2), BF16), pltpu.VMEM((d // LANES, dff2 // 2, LANES), F32),
                        pltpu.VMEM((d // LANES, dff2 // 2, LANES), BF16), pltpu.SemaphoreType.DMA((2,))],
    )
    return pl.pallas_call(
        functools.partial(_ffn_kernel, layer=layer),
        grid_spec=grid_spec,
        out_shape=jax.ShapeDtypeStruct((n_slots, d), F32),
        compiler_params=_params("arbitrary"),
        name="moe_ffn",
    )(block_expert, n_active, plan, xs, wgu, bgu, wd, bd)


def _combine_kernel(runs_ref, next_runs_ref, lk_ref, gk_ref, h_ref, g_ref, ys_hbm, o_ref, sorted_s, sem, *,
                    final_norm):
    i = pl.program_id(0)
    tm = h_ref.shape[0]
    n_local = sorted_s.shape[1]
    slot = i % 2

    def from_slots(buf):
        return lambda src, dst, size: pltpu.make_async_copy(
            ys_hbm.at[pl.ds(src, size)], sorted_s.at[buf, pl.ds(dst, size)], sem.at[buf])

    def start_runs(table, buf):
        def body(e, carry):
            _copy_run(table[RUN_SLOT, e], table[RUN_START, e], table[RUN_LEN, e], RUN_BITS, from_slots(buf),
                      False)
            return carry
        lax.fori_loop(0, N_EXPERTS, body, 0)

    @pl.when(i == 0)
    def _():
        sorted_s[...] = jnp.zeros_like(sorted_s)
        start_runs(runs_ref, slot)

    @pl.when(i + 1 < pl.num_programs(0))
    def _():
        start_runs(next_runs_ref, 1 - slot)

    _copy_run(0, 0, runs_ref[RUN_TOTAL, 0], TOTAL_BITS, lambda src, dst, size: from_slots(slot)(0, 0, size),
              True)
    position = _iota((tm, n_local), 1).astype(F32)
    lk = lk_ref[...]
    gk = gk_ref[...]
    weights = jnp.zeros((tm, n_local), F32)
    for k in range(TOP_K):
        weights = weights + jnp.where(position == lk[:, k:k + 1], gk[:, k:k + 1], 0.0)
    acc = h_ref[...] + _dot(weights.astype(BF16), sorted_s[slot].astype(BF16))
    if final_norm:
        ms = jnp.mean(acc * acc, axis=-1, keepdims=True)
        acc = acc * lax.rsqrt(ms + EPS) * g_ref[...]
    o_ref[...] = acc


def _combine(runs, lk, gk, h, ys, g, final_norm):
    t, d = h.shape
    tm = ROW_TILE
    return pl.pallas_call(
        functools.partial(_combine_kernel, final_norm=final_norm),
        grid=(t // tm,),
        in_specs=[pl.BlockSpec((8, LANES), lambda i: (i, 0), memory_space=pltpu.SMEM),
                  pl.BlockSpec((8, LANES), lambda i: (jnp.minimum(i + 1, t // tm - 1), 0),
                               memory_space=pltpu.SMEM),
                  pl.BlockSpec((tm, LANES), lambda i: (i, 0)),
                  pl.BlockSpec((tm, LANES), lambda i: (i, 0)),
                  pl.BlockSpec((tm, d), lambda i: (i, 0)),
                  pl.BlockSpec((1, d), lambda i: (0, 0)),
                  pl.BlockSpec(memory_space=pl.ANY)],
        out_specs=pl.BlockSpec((tm, d), lambda i: (i, 0)),
        out_shape=jax.ShapeDtypeStruct((t, d), F32),
        scratch_shapes=[pltpu.VMEM((2, LOCAL_ROWS, d), F32), pltpu.SemaphoreType.DMA((2,))],
        compiler_params=_params("arbitrary"),
        name="moe_combine",
    )(runs, runs, lk, gk, h, g.reshape(1, d), ys)


def _moe(h, xn, sel, gate, rank, counts, tile_counts, layer, w_gu, b_gu, w_down, b_down, g_final,
         final_norm):
    t, d = h.shape
    n_slots = t * TOP_K + (t // ROW_TILE) * N_EXPERTS * RUN_ALIGN + N_EXPERTS * MOE_BLOCK
    n_blocks = n_slots // MOE_BLOCK
    pad_start, plan = _plan(counts, n_blocks)
    lpos, lk, gk, runs = _dest(sel, gate, rank, tile_counts, pad_start)
    xs = _dispatch(plan, runs, lpos, xn, n_slots)
    block_expert = plan[PLAN_EXPERTS:].reshape(-1)[:n_blocks]
    n_active = plan[PLAN_ACTIVE, :1]
    depth = w_gu.shape[0]
    ys = _ffn(layer, block_expert, n_active, plan, xs, w_gu, b_gu.reshape(depth, N_EXPERTS, 1, -1), w_down,
              b_down.reshape(depth, N_EXPERTS, 1, d))
    return _combine(runs, lk, gk, h, ys, g_final, final_norm)


def kernel(x, norm_mix_g, w_in, sb_norm_g, rg_conv_w, rg_conv_b, rg_wa, rg_ba, rg_wx, rg_bx,
           rg_lambda, rg_norm_g, hg_lower_bounds, hg_norm_g, m2_conv_w, m2_conv_b, m2_dt_bias,
           m2_a_log, m2_d, m2_norm_g, w_out, norm_ffn_g, router_w, router_b, moe_w_gu, moe_b_gu,
           moe_w_down, moe_b_down, final_norm_g):
    b, l, d = x.shape
    depth = w_in.shape[0]
    t = b * l
    lbs = jnp.cumsum(jax.nn.softmax(hg_lower_bounds.astype(F32), axis=0), axis=0)
    lbs = lbs - lbs[0]
    d_in = w_in.shape[2]
    w_in_p = jnp.zeros((depth, d, PROJ_BLOCKS * GROUP_W), BF16).at[:, :, :d_in].set(w_in.astype(BF16))
    h = x.reshape(t, d)
    for layer in range(depth):
        proj = _norm_inproj(h, norm_mix_g[layer], w_in_p[layer]).reshape(b, l, PROJ_BLOCKS * GROUP_W)
        o_a = _sb_attn(proj, sb_norm_g[layer])
        o_b = _rglru(proj, rg_conv_w[layer], rg_conv_b[layer], rg_wa[layer], rg_ba[layer],
                     rg_wx[layer], rg_bx[layer], rg_lambda[layer], rg_norm_g[layer])
        o_c = _hgrn2(proj, lbs[layer], hg_norm_g[layer])
        o_d = _ssd(proj, m2_conv_w[layer], m2_conv_b[layer], m2_dt_bias[layer], m2_a_log[layer],
                   m2_d[layer], m2_norm_g[layer])
        h, xn, sel, gate, rank, counts, tile_counts = _outproj_router(
            h, (o_a, o_b, o_c, o_d), w_out[layer], norm_ffn_g[layer], router_w[layer],
            router_b[layer])
        h = _moe(h, xn, sel, gate, rank, counts, tile_counts, layer, moe_w_gu, moe_b_gu, moe_w_down,
                 moe_b_down, final_norm_g, layer == depth - 1)
    return h.reshape(b, l, d)
```

```python
import functools
import math

import jax
import jax.numpy as jnp
from jax import lax
from jax.experimental import pallas as pl
from jax.experimental.pallas import tpu as pltpu

F32 = jnp.float32
BF16 = jnp.bfloat16
I32 = jnp.int32
HIGHEST = lax.Precision.HIGHEST
EPS = 1e-6
EXP_UNDERFLOW = -104.0

LANES = 128
GROUP_W = 256
HEAD_DIM = 64
N_HEADS = 4
SSD_STATE = 128
RG_C = 8.0
N_EXPERTS = 32
TOP_K = 4
SWIGLU_LIMIT = 7.0
SWIGLU_ALPHA = 1.702
PROJ_BLOCKS = 14
VMEM_LIMIT = 56 * 1024 * 1024

ROW_TILE = 256
INPROJ_TILE = 512
ATTN_BLOCK = 256
SCAN_TILE = 256
HG_CHUNK = 16
HG_BLOCK = 256
SSD_CHUNK = LANES
MOE_BLOCK = 512
HG_UNROLL = 32

PLAN_ACTIVE, PLAN_COUNT, PLAN_START, PLAN_PADDED, PLAN_EXPERTS = 0, 1, 2, 3, 4
PLAN_EXPERT_ROWS = 8 - PLAN_EXPERTS
RUN_LEN, RUN_START, RUN_SLOT, RUN_TOTAL = 0, 1, 2, 3
RUN_ALIGN = 8
RUN_BITS = (ROW_TILE // RUN_ALIGN).bit_length()
PAD_BITS = (MOE_BLOCK // RUN_ALIGN - 1).bit_length()
LOCAL_ROWS = ROW_TILE * TOP_K + N_EXPERTS * RUN_ALIGN
TOTAL_BITS = (LOCAL_ROWS // RUN_ALIGN).bit_length()


def _params(*sem):
    return pltpu.CompilerParams(dimension_semantics=sem, vmem_limit_bytes=VMEM_LIMIT)


def _dot(a, b):
    return jnp.dot(a, b, preferred_element_type=F32)


def _dot_hi(a, b):
    return jnp.dot(a, b, preferred_element_type=F32, precision=HIGHEST)


def _split3(x):
    hi = x.astype(BF16)
    rest = x - hi.astype(F32)
    mid = rest.astype(BF16)
    lo = (rest - mid.astype(F32)).astype(BF16)
    return hi, mid, lo


def _dot_exact_lhs(m, x):
    return sum(_dot(m, part) for part in _split3(x))


def _dot_exact_rhs(x, m):
    return sum(_dot(part, m) for part in _split3(x))


def _dot_nt(a, b, precision=None):
    return lax.dot_general(a, b, (((1,), (1,)), ((), ())), preferred_element_type=F32,
                           precision=precision)


def _dot_tn(a, b):
    return lax.dot_general(a, b, (((0,), (0,)), ((), ())), preferred_element_type=F32)


def _sigmoid(x):
    return 1.0 / (1.0 + jnp.exp(-x))


def _silu(x):
    return x * _sigmoid(x)


def _softplus(x):
    return jnp.maximum(x, 0.0) + jnp.log(1.0 + jnp.exp(-jnp.abs(x)))


def _iota(shape, dim):
    return lax.broadcasted_iota(I32, shape, dim)


def _div_pow2(x, n):
    assert n & (n - 1) == 0
    return jnp.right_shift(x, int(math.log2(n)))


def _shift_rows(x, d, prev8):
    r = pltpu.roll(x, d, 0)
    p = pltpu.roll(prev8, d, 0)
    head = jnp.where(_iota(prev8.shape, 0) < d, p, r[:8])
    return jnp.concatenate([head, r[8:]], axis=0)


def _causal_conv(x, prev8, w, b):
    k = w.shape[0]
    y = x * w[k - 1:k] + b
    for d in range(1, k):
        y = y + _shift_rows(x, d, prev8) * w[k - 1 - d:k - d]
    return y


def _norm_inproj_kernel(x_ref, g_ref, w_ref, o_ref):
    x = x_ref[...]
    ms = jnp.mean(x * x, axis=-1, keepdims=True)
    xn = (x * lax.rsqrt(ms + EPS) * g_ref[...]).astype(BF16)
    o_ref[...] = _dot(xn, w_ref[...])


def _norm_inproj(h, g, w):
    t, d = h.shape
    n = w.shape[1]
    return pl.pallas_call(
        _norm_inproj_kernel,
        grid=(t // INPROJ_TILE,),
        in_specs=[pl.BlockSpec((INPROJ_TILE, d), lambda i: (i, 0)),
                  pl.BlockSpec((1, d), lambda i: (0, 0)),
                  pl.BlockSpec((d, n), lambda i: (0, 0))],
        out_specs=pl.BlockSpec((INPROJ_TILE, n), lambda i: (i, 0)),
        out_shape=jax.ShapeDtypeStruct((t, n), F32),
        compiler_params=_params("parallel"),
        name="norm_inproj",
    )(h, g.reshape(1, d), w)


def _sb_attn_kernel(q_ref, k_ref, v_ref, g_ref, o_ref, acc_ref, run_ref):
    i = pl.program_id(1)
    blk = ATTN_BLOCK
    scale = HEAD_DIM ** -0.5
    rows = N_HEADS * blk
    below = _iota((rows, blk), 1) < jnp.bitwise_and(_iota((rows, blk), 0), blk - 1)
    r_w = _iota((blk, blk + LANES), 0)
    c_w = _iota((blk, blk + LANES), 1)
    later_sum = jnp.where(jnp.logical_or(c_w >= blk, r_w > c_w), 1.0, 0.0).astype(BF16)
    acc_ref[...] = jnp.zeros_like(acc_ref)
    run_ref[...] = jnp.zeros_like(run_ref)
    q_all = q_ref[0] * scale
    head_of_lane = _div_pow2(_iota((blk, GROUP_W), 1), HEAD_DIM)
    q_heads = jnp.concatenate([jnp.where(head_of_lane == h, q_all, 0.0) for h in range(N_HEADS)],
                              axis=0).astype(BF16)

    def key_block(j, diagonal):
        ks = pl.multiple_of(j * blk, blk)
        z = _dot_nt(q_heads, k_ref[0, pl.ds(ks, blk), :].astype(BF16))
        ls = jnp.minimum(z, 0.0) - jnp.log(1.0 + jnp.exp(-jnp.abs(z)))
        lk = ls - z
        if diagonal:
            lk = jnp.where(below, lk, 0.0)
        lk_hi = lk.astype(BF16)
        lk_lo = (lk - lk_hi.astype(F32)).astype(BF16)
        stacked = _dot(jnp.concatenate([lk_hi, lk_lo], axis=0), later_sum)
        sums = stacked[:rows] + stacked[rows:]
        run = run_ref[...]
        rem = sums[:, :blk] + jnp.concatenate([run] * (blk // LANES), axis=1)
        a = jnp.exp(ls + rem)
        if diagonal:
            a = jnp.where(below, a, 0.0)
        out = _dot(a.astype(BF16), v_ref[0, pl.ds(ks, blk), :].astype(BF16))
        for h in range(N_HEADS):
            sl = slice(h * HEAD_DIM, (h + 1) * HEAD_DIM)
            acc_ref[:, sl] += out[h * blk:(h + 1) * blk, sl]
        run_ref[...] = run + sums[:, blk:]

    key_block(i, True)

    def live():
        return jnp.max(run_ref[...]) > EXP_UNDERFLOW

    def more(carry):
        jj, alive = carry
        return jnp.logical_and(jj <= i, alive)

    def body(carry):
        jj, _ = carry
        key_block(i - jj, False)
        return jj + 1, live()

    lax.while_loop(more, body, (jnp.int32(1), live()))
    o = acc_ref[...]
    ms = jnp.mean(o * o, axis=-1, keepdims=True)
    o_ref[0] = o * lax.rsqrt(ms + EPS) * g_ref[...]


def _sb_attn(proj, g):
    b, l, _ = proj.shape
    blk = ATTN_BLOCK
    return pl.pallas_call(
        _sb_attn_kernel,
        grid=(b, l // blk),
        in_specs=[pl.BlockSpec((1, blk, GROUP_W), lambda bi, i: (bi, i, 0)),
                  pl.BlockSpec((1, l, GROUP_W), lambda bi, i: (bi, 0, 1)),
                  pl.BlockSpec((1, l, GROUP_W), lambda bi, i: (bi, 0, 2)),
                  pl.BlockSpec((1, GROUP_W), lambda bi, i: (0, 0))],
        out_specs=pl.BlockSpec((1, blk, GROUP_W), lambda bi, i: (bi, i, 0)),
        out_shape=jax.ShapeDtypeStruct((b, l, GROUP_W), F32),
        scratch_shapes=[pltpu.VMEM((blk, GROUP_W), F32), pltpu.VMEM((N_HEADS * blk, LANES), F32)],
        compiler_params=_params("parallel", "arbitrary"),
        name="sb_attn",
    )(proj, proj, proj, g.reshape(1, GROUP_W))


def _rglru_kernel(x_ref, gate_ref, cw_ref, cb_ref, wax_ref, ba_ref, bx_ref, lam_ref,
                  ng_ref, o_ref):
    l = x_ref.shape[1]
    w = x_ref.shape[2]
    tc = SCAN_TILE
    rows = _iota((tc, w), 0)
    cw = cw_ref[...]
    cb = cb_ref[...]
    neg_c_sp = -RG_C * _softplus(-lam_ref[...])

    def chunk(c, carry):
        h_prev, tail = carry
        s = pl.multiple_of(c * tc, tc)
        x = x_ref[0, pl.ds(s, tc), :]
        xc = _causal_conv(x, tail, cw, cb)
        xcb = xc.astype(BF16)
        gates = _dot(xcb, wax_ref[...])
        r = _sigmoid(gates[:, :w] + ba_ref[...])
        ig = _sigmoid(gates[:, w:] + bx_ref[...])
        log_a = r * neg_c_sp
        a = jnp.exp(log_a)
        th = jnp.tanh(log_a)
        mult = jnp.sqrt(-2.0 * th / (1.0 - th))
        mult = jnp.where(rows + s == 0, 1.0, mult)
        u = mult * ig * xc
        d = 1
        while d < tc:
            keep = rows >= d
            a_s = jnp.where(keep, pltpu.roll(a, d, 0), 1.0)
            u_s = jnp.where(keep, pltpu.roll(u, d, 0), 0.0)
            u = a * u_s + u
            a = a * a_s
            d *= 2
        hs = u + a * h_prev
        gate = gate_ref[0, pl.ds(s, tc), :]
        gelu = 0.5 * gate * (1.0 + jnp.tanh(math.sqrt(2.0 / math.pi) * (gate + 0.044715 * gate * gate * gate)))
        o = hs * gelu
        ms = jnp.mean(o * o, axis=-1, keepdims=True)
        o_ref[0, pl.ds(s, tc), :] = o * lax.rsqrt(ms + EPS) * ng_ref[...]
        return hs[tc - 1:tc], x[tc - 8:tc]

    lax.fori_loop(0, l // tc, chunk, (jnp.zeros((1, w), F32), jnp.zeros((8, w), F32)))


def _block_diag(wb):
    n, d, _ = wb.shape
    same = jnp.eye(n, dtype=bool)[:, None, :, None]
    return jnp.where(same, wb[:, :, None, :], 0).reshape(n * d, n * d)


def _rglru(proj, conv_w, conv_b, wa, ba, wx, bx, lam, ng):
    b, l, _ = proj.shape
    w = GROUP_W
    row = lambda v: v.reshape(1, w)
    full = lambda shape: pl.BlockSpec(shape, lambda bi: (0,) * len(shape))
    return pl.pallas_call(
        _rglru_kernel,
        grid=(b,),
        in_specs=[pl.BlockSpec((1, l, w), lambda bi: (bi, 0, 3)),
                  pl.BlockSpec((1, l, w), lambda bi: (bi, 0, 4)),
                  full((conv_w.shape[0], w)), full((1, w)), full((w, 2 * w)), full((1, w)),
                  full((1, w)), full((1, w)), full((1, w))],
        out_specs=pl.BlockSpec((1, l, w), lambda bi: (bi, 0, 0)),
        out_shape=jax.ShapeDtypeStruct((b, l, w), F32),
        compiler_params=_params("parallel"),
        name="rglru",
    )(proj, proj, conv_w, row(conv_b),
      jnp.concatenate([_block_diag(wa), _block_diag(wx)], axis=1).astype(BF16), row(ba), row(bx), row(lam),
      row(ng))


def _hgrn2_kernel(q_ref, f_ref, i_ref, g_ref, lb_ref, ng_ref, o_ref,
                  qe_s, ke_s, kl_s, dl_s, o_s, inc_s):
    l = q_ref.shape[1]
    w = q_ref.shape[2]
    tile = SCAN_TILE
    c = HG_CHUNK
    r_t = _iota((tile, tile), 0)
    c_t = _iota((tile, tile), 1)
    same = _div_pow2(r_t, c) == _div_pow2(c_t, c)
    cum_m = jnp.where(jnp.logical_and(same, c_t <= r_t), 1.0, 0.0).astype(BF16)
    tot_m = jnp.where(same, 1.0, 0.0).astype(BF16)
    lb = lb_ref[...]

    def prep(t, carry):
        s = pl.multiple_of(t * tile, tile)
        forget = lb + (1.0 - lb) * _sigmoid(f_ref[0, pl.ds(s, tile), :])
        log_f = jnp.log(forget)
        key = 1.0 - forget
        parts = _split3(log_f)
        bcum = sum(_dot(cum_m, part) for part in parts)
        blast = sum(_dot(tot_m, part) for part in parts)
        q = _silu(q_ref[0, pl.ds(s, tile), :])
        qe_s[pl.ds(s, tile), :] = (q * jnp.exp(bcum)).astype(BF16)
        ke_s[pl.ds(s, tile), :] = (key * jnp.exp(-bcum)).astype(BF16)
        kl_s[pl.ds(s, tile), :] = key * jnp.exp(blast - bcum)
        dl_s[pl.ds(s, tile), :] = jnp.exp(blast)
        return carry

    lax.fori_loop(0, l // tile, prep, 0)

    blk = HG_BLOCK
    per = blk // c
    q_row = jnp.bitwise_and(_iota((N_HEADS * blk, blk), 0), blk - 1)
    k_row = _iota((N_HEADS * blk, blk), 1)
    within = jnp.logical_and(_div_pow2(q_row, c) == _div_pow2(k_row, c), k_row <= q_row)
    own = (_div_pow2(_iota((blk, per * HEAD_DIM), 0), c)
           == _div_pow2(_iota((blk, per * HEAD_DIM), 1), HEAD_DIM))
    head_of_lane = _div_pow2(_iota((blk, w), 1), HEAD_DIM)

    def local(m, carry):
        s = pl.multiple_of(m * blk, blk)
        qe = qe_s[pl.ds(s, blk), :]
        ke = ke_s[pl.ds(s, blk), :]
        kl = kl_s[pl.ds(s, blk), :]
        v = i_ref[0, pl.ds(s, blk), :]
        q_heads = jnp.concatenate([jnp.where(head_of_lane == h, qe, jnp.zeros_like(qe))
                                   for h in range(N_HEADS)], axis=0)
        sc = jnp.where(within, _dot_nt(q_heads, ke), 0.0)
        out = _dot(sc.astype(BF16), v.astype(BF16))
        o_s[pl.ds(s, blk), :] = jnp.concatenate(
            [out[h * blk:(h + 1) * blk, h * HEAD_DIM:(h + 1) * HEAD_DIM] for h in range(N_HEADS)], axis=1)
        for h in range(N_HEADS):
            sl = slice(h * HEAD_DIM, (h + 1) * HEAD_DIM)
            wide = jnp.where(own, jnp.concatenate([kl[:, sl]] * per, axis=1), 0.0)
            inc = _dot_tn(v[:, sl], wide)
            for j in range(per):
                inc_s[m * per + j, :, sl] = inc[:, j * HEAD_DIM:(j + 1) * HEAD_DIM]
        return carry

    lax.fori_loop(0, l // blk, local, 0, unroll=2)

    head_of_chunk_lane = _div_pow2(_iota((c, w), 1), HEAD_DIM)

    def step(n, state):
        s = pl.multiple_of(n * c, c)
        qe = qe_s[pl.ds(s, c), :]
        dl = dl_s[pl.ds(s, 8), :][0:1]
        q_heads = jnp.concatenate([jnp.where(head_of_chunk_lane == h, qe, jnp.zeros_like(qe))
                                   for h in range(N_HEADS)], axis=0)
        carried = _dot_nt(q_heads, state.astype(BF16))
        o_s[pl.ds(s, c), :] += jnp.concatenate([carried[h * c:(h + 1) * c] for h in range(N_HEADS)], axis=1)
        return dl * state + inc_s[n]

    lax.fori_loop(0, l // c, step, jnp.zeros((HEAD_DIM, w), F32), unroll=HG_UNROLL)

    head_avg = jnp.where(_div_pow2(_iota((w, w), 0), HEAD_DIM) == _div_pow2(_iota((w, w), 1), HEAD_DIM),
                         1.0 / HEAD_DIM, 0.0).astype(BF16)

    def finish(t, carry):
        s = pl.multiple_of(t * tile, tile)
        o = o_s[pl.ds(s, tile), :]
        ms = _dot_exact_rhs(o * o, head_avg)
        o_ref[0, pl.ds(s, tile), :] = (o * lax.rsqrt(ms + EPS) * ng_ref[...]
                                       * _silu(g_ref[0, pl.ds(s, tile), :]))
        return carry

    lax.fori_loop(0, l // tile, finish, 0)


def _hgrn2(proj, lb, ng):
    b, l, _ = proj.shape
    w = GROUP_W
    full = lambda shape: pl.BlockSpec(shape, lambda bi: (0,) * len(shape))
    col = lambda j: pl.BlockSpec((1, l, w), lambda bi: (bi, 0, j))
    return pl.pallas_call(
        _hgrn2_kernel,
        grid=(b,),
        in_specs=[col(5), col(6), col(7), col(8), full((1, w)), full((1, w))],
        out_specs=pl.BlockSpec((1, l, w), lambda bi: (bi, 0, 0)),
        out_shape=jax.ShapeDtypeStruct((b, l, w), F32),
        scratch_shapes=[pltpu.VMEM((l, w), BF16), pltpu.VMEM((l, w), BF16), pltpu.VMEM((l, w), F32),
                        pltpu.VMEM((l, w), F32), pltpu.VMEM((l, w), F32),
                        pltpu.VMEM((l // HG_CHUNK, HEAD_DIM, w), F32)],
        compiler_params=_params("parallel"),
        name="hgrn2",
    )(proj, proj, proj, proj, lb.reshape(1, w), ng.reshape(1, w))


def _ssd_kernel(z_ref, x_ref, bm_ref, cm_ref, dt_ref, cw_ref, cb_ref, dtb_ref, aneg_ref,
                dskip_ref, ng_ref, o_ref):
    l = z_ref.shape[1]
    w = GROUP_W
    q = SSD_CHUNK
    p = HEAD_DIM
    r_q = _iota((q, q), 0)
    c_q = _iota((q, q), 1)
    tril = c_q <= r_q
    cum_m = jnp.where(tril, 1.0, 0.0).astype(BF16)
    cw = cw_ref[...]
    cb = cb_ref[...]

    def chunk(n, carry):
        state, tails = carry
        s = pl.multiple_of(n * q, q)
        parts = []
        new_tails = []
        for j, ref in enumerate((x_ref, bm_ref, cm_ref)):
            raw = ref[0, pl.ds(s, q), :]
            sl = slice(j * w, (j + 1) * w)
            parts.append(_silu(_causal_conv(raw, tails[j], cw[:, sl], cb[:, sl])))
            new_tails.append(raw[q - 8:q])
        xs, bm, cm = parts
        dt = _softplus(dt_ref[0, pl.ds(s, q), :] + dtb_ref[...])
        dt_rep = jnp.concatenate([jnp.broadcast_to(dt[:, h:h + 1], (q, p)) for h in range(N_HEADS)],
                                 axis=1)
        a_cs = _dot_exact_lhs(cum_m, dt_rep * aneg_ref[...])
        a_rows = [jnp.transpose(a_cs[:, g * LANES:(g + 1) * LANES]) for g in range(w // LANES)]
        xd = xs * dt_rep
        groups = w // SSD_STATE
        heads_per_group = N_HEADS // groups
        cgs = [cm[:, g * SSD_STATE:(g + 1) * SSD_STATE].astype(BF16) for g in range(groups)]
        cbs = [_dot_nt(cgs[g], bm[:, g * SSD_STATE:(g + 1) * SSD_STATE].astype(BF16)) for g in range(groups)]
        masked = []
        for h in range(N_HEADS):
            row = (h * p) % LANES
            a_row = a_rows[(h * p) // LANES][row:row + 1, :]
            a_col = a_cs[:, h * p:h * p + 1]
            decay = jnp.exp(jnp.where(tril, a_col - a_row, -jnp.inf))
            masked.append((cbs[h // heads_per_group] * decay).astype(BF16))
        within = _dot(jnp.concatenate(masked, axis=0), xd.astype(BF16))
        y = jnp.concatenate([within[h * q:(h + 1) * q, h * p:(h + 1) * p] for h in range(N_HEADS)], axis=1)
        to_end = xd * jnp.exp(a_cs[q - 1:q, :] - a_cs)
        carried = []
        new_state = []
        for g in range(groups):
            gs = slice(g * SSD_STATE, (g + 1) * SSD_STATE)
            st = state[g]
            carried.append(_dot_nt(cgs[g], st.astype(BF16)))
            chunk_decay = jnp.exp(a_rows[g][:, q - 1:q])
            new_state.append(chunk_decay * st + _dot_tn(to_end[:, gs], bm[:, gs]))
        y = y + jnp.exp(a_cs) * jnp.concatenate(carried, axis=1) + xs * dskip_ref[...]
        y = y * _silu(z_ref[0, pl.ds(s, q), :])
        halves = []
        for gi in range(2):
            yg = y[:, gi * LANES:(gi + 1) * LANES]
            ms = jnp.mean(yg * yg, axis=-1, keepdims=True)
            halves.append(yg * lax.rsqrt(ms + EPS))
        o_ref[0, pl.ds(s, q), :] = jnp.concatenate(halves, axis=1) * ng_ref[...]
        return tuple(new_state), tuple(new_tails)

    assert SSD_STATE == LANES and (N_HEADS // (w // SSD_STATE)) * p == LANES
    zero = jnp.zeros((LANES, SSD_STATE), F32)
    tail0 = jnp.zeros((8, w), F32)
    lax.fori_loop(0, l // q, chunk, ((zero,) * (w // SSD_STATE), (tail0,) * 3), unroll=4)


def _ssd(proj, conv_w, conv_b, dt_bias, a_log, d_skip, ng):
    b, l, _ = proj.shape
    w = GROUP_W
    full = lambda shape: pl.BlockSpec(shape, lambda bi: (0,) * len(shape))
    col = lambda j: pl.BlockSpec((1, l, w), lambda bi: (bi, 0, j))
    pad_heads = lambda v: jnp.zeros((1, w), F32).at[0, :N_HEADS].set(v)
    return pl.pallas_call(
        _ssd_kernel,
        grid=(b,),
        in_specs=[col(9), col(10), col(11), col(12), col(13),
                  full((conv_w.shape[0], 3 * w)), full((1, 3 * w)), full((1, w)), full((1, w)),
                  full((1, w)), full((1, w))],
        out_specs=pl.BlockSpec((1, l, w), lambda bi: (bi, 0, 0)),
        out_shape=jax.ShapeDtypeStruct((b, l, w), F32),
        compiler_params=_params("parallel"),
        name="ssd",
    )(proj, proj, proj, proj, proj, conv_w, conv_b.reshape(1, 3 * w), pad_heads(dt_bias),
      jnp.repeat(-jnp.exp(a_log), HEAD_DIM).reshape(1, w), jnp.repeat(d_skip, HEAD_DIM).reshape(1, w),
      ng.reshape(1, w))


def _outproj_router_kernel(h_ref, oa_ref, ob_ref, oc_ref, od_ref, wo_ref, g_ref, rw_ref, rb_ref,
                           h_out, xn_out, sel_out, gate_out, rank_out, cnt_out, tile_out, cnt_s):
    i = pl.program_id(0)
    tm = h_ref.shape[0]

    @pl.when(i == 0)
    def _():
        cnt_s[...] = jnp.zeros_like(cnt_s)

    mix = jnp.concatenate([ref[...].astype(BF16) for ref in (oa_ref, ob_ref, oc_ref, od_ref)], axis=1)
    acc = h_ref[...] + _dot(mix, wo_ref[...])
    h_out[...] = acc
    ms = jnp.mean(acc * acc, axis=-1, keepdims=True)
    xn = acc * lax.rsqrt(ms + EPS) * g_ref[...]
    xn_out[...] = xn
    lane = _iota((tm, LANES), 1)
    lane_f = lane.astype(F32)
    xn_hi = xn.astype(BF16)
    xn_lo = (xn - xn_hi.astype(F32)).astype(BF16)
    both = _dot(jnp.concatenate([xn_hi, xn_lo], axis=0), rw_ref[...])
    router = both[:tm, :LANES] + both[tm:, :LANES] + both[:tm, LANES:]
    logits = jnp.where(lane < N_EXPERTS, router + rb_ref[...], -jnp.inf)
    sel = jnp.zeros((tm, LANES), F32)
    tops = []
    for k in range(TOP_K):
        m = jnp.max(logits, axis=-1, keepdims=True)
        idx = jnp.min(jnp.where(logits == m, lane_f, float(LANES)), axis=-1, keepdims=True)
        hit = lane_f == idx
        sel = jnp.where(hit, float(k + 1), sel)
        logits = jnp.where(hit, -jnp.inf, logits)
        tops.append(m)
    exps = [jnp.exp(m - tops[0]) for m in tops]
    denom = exps[0] + exps[1] + exps[2] + exps[3]
    gate = jnp.zeros((tm, LANES), F32)
    for k in range(TOP_K):
        gate = jnp.where(sel == float(k + 1), exps[k] / denom, gate)
    sel_out[...] = sel
    gate_out[...] = gate
    member = jnp.where(sel > 0.0, 1.0, 0.0)
    before = jnp.where(_iota((tm, tm), 1) < _iota((tm, tm), 0), 1.0, 0.0).astype(BF16)
    carry = cnt_s[0:1, :]
    rank_out[...] = _dot(before, member.astype(BF16))
    in_tile = jnp.ceil(jnp.sum(member, axis=0, keepdims=True) * (1.0 / RUN_ALIGN)) * RUN_ALIGN
    total = carry + in_tile
    tile_out[...] = jnp.concatenate([carry, in_tile, jnp.zeros((6, LANES), F32)], axis=0)
    cnt_s[...] = jnp.broadcast_to(total, cnt_s.shape)
    cnt_out[...] = jnp.broadcast_to(total, cnt_out.shape)


def _outproj_router(h, outs, w_out, g, router_w, router_b):
    t, d = h.shape
    tm = ROW_TILE
    rw = jnp.zeros((d, LANES), F32).at[:, :N_EXPERTS].set(router_w)
    rb = jnp.zeros((1, LANES), F32).at[0, :N_EXPERTS].set(router_b)
    rw_hi = rw.astype(BF16)
    rw_lo = (rw - rw_hi.astype(F32)).astype(BF16)
    tile = lambda width: pl.BlockSpec((tm, width), lambda i: (i, 0))
    full = lambda shape: pl.BlockSpec(shape, lambda i: (0,) * len(shape))
    flat = [o.reshape(t, GROUP_W) for o in outs]
    return pl.pallas_call(
        _outproj_router_kernel,
        grid=(t // tm,),
        in_specs=[tile(d)] + [tile(GROUP_W)] * 4 + [full((d, d)), full((1, d)), full((d, 2 * LANES)),
                                                   full((1, LANES))],
        out_specs=[tile(d), tile(d), tile(LANES), tile(LANES), tile(LANES), full((8, LANES)),
                   pl.BlockSpec((8, LANES), lambda i: (i, 0))],
        out_shape=[jax.ShapeDtypeStruct((t, d), F32), jax.ShapeDtypeStruct((t, d), F32),
                   jax.ShapeDtypeStruct((t, LANES), F32), jax.ShapeDtypeStruct((t, LANES), F32),
                   jax.ShapeDtypeStruct((t, LANES), F32), jax.ShapeDtypeStruct((8, LANES), F32),
                   jax.ShapeDtypeStruct((t // tm * 8, LANES), F32)],
        scratch_shapes=[pltpu.VMEM((8, LANES), F32)],
        compiler_params=_params("arbitrary"),
        name="outproj_router",
    )(h, *flat, w_out.astype(BF16), g.reshape(1, d), jnp.concatenate([rw_hi, rw_lo], axis=1), rb)


def _plan_kernel(cnt_ref, start_out, plan_out):
    cnt = cnt_ref[...]
    padded = jnp.ceil(cnt * (1.0 / MOE_BLOCK)) * MOE_BLOCK
    upto = jnp.where(_iota((LANES, LANES), 0) <= _iota((LANES, LANES), 1), 1.0, 0.0)
    pad_end = _dot_hi(padded, upto)
    pad_start = pad_end - padded
    start_out[...] = pad_start
    ends = jnp.transpose(jnp.broadcast_to(pad_end[0:1, :], (LANES, LANES)))
    expert_ok = _iota((LANES, LANES), 0) < N_EXPERTS
    n_active = pad_end[0:1, N_EXPERTS - 1:N_EXPERTS] * (1.0 / MOE_BLOCK)
    rows = [jnp.broadcast_to(n_active, (1, LANES)), cnt[0:1, :], pad_start[0:1, :], padded[0:1, :]]
    for r in range(PLAN_EXPERT_ROWS):
        blk_start = ((_iota((LANES, LANES), 1) + r * LANES) * MOE_BLOCK).astype(F32)
        done = jnp.where(jnp.logical_and(expert_ok, ends <= blk_start), 1.0, 0.0)
        rows.append(jnp.minimum(jnp.sum(done, axis=0, keepdims=True), N_EXPERTS - 1.0))
    plan_out[...] = jnp.concatenate(rows, axis=0).astype(I32)


def _plan(counts, n_blocks):
    assert n_blocks <= PLAN_EXPERT_ROWS * LANES
    return pl.pallas_call(
        _plan_kernel,
        out_shape=[jax.ShapeDtypeStruct((8, LANES), F32), jax.ShapeDtypeStruct((8, LANES), I32)],
        name="moe_plan",
    )(counts)


def _dest_kernel(sel_ref, gate_ref, rank_ref, tile_ref, start_ref, lpos_out, lk_out, gk_out, runs_out):
    tm = sel_ref.shape[0]
    sel = sel_ref[...]
    gate = gate_ref[...]
    before_tile = tile_ref[0:1, :]
    in_tile = tile_ref[1:2, :]
    earlier = jnp.where(_iota((LANES, LANES), 0) < _iota((LANES, LANES), 1), 1.0, 0.0)
    run_start = _dot_hi(jnp.broadcast_to(in_tile, (8, LANES)), earlier)[0:1, :]
    lpos = rank_ref[...] + run_start
    lane = _iota((tm, LANES), 1)
    pieces = []
    gk = jnp.zeros((tm, LANES), F32)
    lk = jnp.zeros((tm, LANES), F32)
    for k in range(TOP_K):
        hit = sel == float(k + 1)
        mine = jnp.where(hit, lpos, 0.0)
        pieces.append(mine)
        gk = jnp.where(lane == k, jnp.sum(jnp.where(hit, gate, 0.0), axis=-1, keepdims=True), gk)
        lk = jnp.where(lane == k, jnp.sum(mine, axis=-1, keepdims=True), lk)
    stacked = jnp.concatenate(pieces, axis=1)
    picker = jnp.where(_div_pow2(_iota((8, TOP_K * LANES), 1), LANES) == _iota((8, TOP_K * LANES), 0),
                       1.0, 0.0).astype(BF16)
    lpos_out[...] = sum(_dot_nt(picker, part) for part in _split3(stacked)).astype(I32)
    lk_out[...] = lk
    gk_out[...] = gk
    first_slot = start_ref[0:1, :] + before_tile
    total = jnp.broadcast_to(jnp.sum(in_tile, axis=-1, keepdims=True), (1, LANES))
    runs_out[...] = jnp.concatenate([in_tile, run_start, first_slot, total, jnp.zeros((4, LANES), F32)],
                                    axis=0).astype(I32)


def _dest(sel, gate, rank, tile_counts, pad_start):
    t = sel.shape[0]
    tm = ROW_TILE
    tile = pl.BlockSpec((tm, LANES), lambda i: (i, 0))
    per_tile = pl.BlockSpec((8, LANES), lambda i: (i, 0))
    return pl.pallas_call(
        _dest_kernel,
        grid=(t // tm,),
        in_specs=[tile, tile, tile, per_tile, pl.BlockSpec((8, LANES), lambda i: (0, 0))],
        out_specs=[pl.BlockSpec((8, tm), lambda i: (0, i)), tile, tile, per_tile],
        out_shape=[jax.ShapeDtypeStruct((8, t), I32), jax.ShapeDtypeStruct((t, LANES), F32),
                   jax.ShapeDtypeStruct((t, LANES), F32), jax.ShapeDtypeStruct((t // tm * 8, LANES), I32)],
        compiler_params=_params("parallel"),
        name="moe_dest",
    )(sel, gate, rank, tile_counts, pad_start)


def _copy_run(src, dst, length, n_bits, make_copy, wait):
    units = jnp.right_shift(length, RUN_ALIGN.bit_length() - 1)
    for b in range(n_bits):
        offset = jnp.left_shift(jnp.right_shift(units, b + 1), b + 1) * RUN_ALIGN

        @pl.when(jnp.bitwise_and(jnp.right_shift(units, b), 1) == 1)
        def _():
            copy = make_copy(pl.multiple_of(src + offset, RUN_ALIGN), pl.multiple_of(dst + offset, RUN_ALIGN),
                             RUN_ALIGN << b)
            if wait:
                copy.wait()
            else:
                copy.start()


def _dispatch_kernel(plan_ref, runs_ref, prev_runs_ref, lpos_ref, xn_ref, xs_hbm, sorted_s, sem):
    i = pl.program_id(0)
    tm = xn_ref.shape[0]
    n_local = sorted_s.shape[1]
    slot = i % 2

    def to_slots(buf):
        return lambda src, dst, size: pltpu.make_async_copy(
            sorted_s.at[buf, pl.ds(src, size)], xs_hbm.at[pl.ds(dst, size)], sem.at[buf])

    def start_runs(table, buf):
        def body(e, carry):
            _copy_run(table[RUN_START, e], table[RUN_SLOT, e], table[RUN_LEN, e], RUN_BITS, to_slots(buf),
                      False)
            return carry
        lax.fori_loop(0, N_EXPERTS, body, 0)

    def wait_runs(table, buf):
        _copy_run(0, 0, table[RUN_TOTAL, 0], TOTAL_BITS, lambda src, dst, size: to_slots(buf)(0, 0, size),
                  True)

    position = _iota((n_local, tm), 0)
    lpos = lpos_ref[...]
    place = jnp.zeros((n_local, tm), F32)
    for k in range(TOP_K):
        place = place + jnp.where(position == lpos[k:k + 1, :], 1.0, 0.0)
    sorted_s[slot] = _dot(place.astype(BF16), xn_ref[...].astype(BF16))
    start_runs(runs_ref, slot)

    @pl.when(i > 0)
    def _():
        wait_runs(prev_runs_ref, 1 - slot)

    @pl.when(i == pl.num_programs(0) - 1)
    def _():
        wait_runs(runs_ref, slot)

    @pl.when(i == 0)
    def _():
        def padding(wait):
            def fill(src, dst, size):
                return pltpu.make_async_copy(sorted_s.at[0, pl.ds(0, size)], xs_hbm.at[pl.ds(dst, size)],
                                             sem.at[1])

            def body(e, carry):
                cnt = plan_ref[PLAN_COUNT, e]
                _copy_run(0, plan_ref[PLAN_START, e] + cnt, plan_ref[PLAN_PADDED, e] - cnt, PAD_BITS, fill,
                          wait)
                return carry
            lax.fori_loop(0, N_EXPERTS, body, 0)

        padding(False)
        padding(True)


def _dispatch(plan, runs, lpos, xn, n_slots):
    t, d = xn.shape
    tm = ROW_TILE
    return pl.pallas_call(
        _dispatch_kernel,
        grid=(t // tm,),
        in_specs=[pl.BlockSpec(memory_space=pltpu.SMEM),
                  pl.BlockSpec((8, LANES), lambda i: (i, 0), memory_space=pltpu.SMEM),
                  pl.BlockSpec((8, LANES), lambda i: (jnp.maximum(i - 1, 0), 0), memory_space=pltpu.SMEM),
                  pl.BlockSpec((8, tm), lambda i: (0, i)),
                  pl.BlockSpec((tm, d), lambda i: (i, 0))],
        out_specs=pl.BlockSpec(memory_space=pl.ANY),
        out_shape=jax.ShapeDtypeStruct((n_slots, d), F32),
        scratch_shapes=[pltpu.VMEM((2, LOCAL_ROWS, d), F32), pltpu.SemaphoreType.DMA((2,))],
        compiler_params=_params("arbitrary"),
        name="moe_dispatch",
    )(plan, runs, runs, lpos, xn)


def _ffn_kernel(be_ref, na_ref, plan_ref, x_ref, wgu_hbm, bgu_ref, wd_hbm, bd_ref, o_ref, wgu_in_s, wd_in_s,
                wgu_s, wd32_s, wd_s, sem, *, layer):
    j = pl.program_id(0)
    n_active = na_ref[0]
    active = j < n_active
    expert = be_ref[j]
    new_expert = jnp.logical_or(j == 0, expert != be_ref[jnp.maximum(j - 1, 0)])
    rows_left = plan_ref[PLAN_START, expert] + plan_ref[PLAN_COUNT, expert] - j * MOE_BLOCK

    def fetch(e):
        return (pltpu.make_async_copy(wgu_hbm.at[layer, e], wgu_in_s, sem.at[0]),
                pltpu.make_async_copy(wd_hbm.at[layer, e], wd_in_s, sem.at[1]))

    @pl.when(j == 0)
    def _():
        for copy in fetch(expert):
            copy.start()

    @pl.when(jnp.logical_and(active, new_expert))
    def _():
        for copy in fetch(expert):
            copy.wait()
        wgu_s[...] = wgu_in_s[...].astype(BF16)
        half = LANES // 2
        for c in range(wd_s.shape[0]):
            cols = slice(c * LANES, (c + 1) * LANES)
            for k in range(wd_s.shape[1] // LANES):
                for hb in range(2):
                    src = k * LANES + hb * half
                    wd32_s[c, pl.ds(k * LANES + hb, half, stride=2), :] = wd_in_s[src:src + half, cols]
            wd_s[c] = wd32_s[c].astype(BF16)
        nxt = lax.while_loop(lambda b: jnp.logical_and(b < n_active, be_ref[jnp.minimum(b, n_active - 1)] == expert),
                             lambda b: b + 1, j + 1)

        @pl.when(nxt < n_active)
        def _():
            for copy in fetch(be_ref[jnp.minimum(nxt, n_active - 1)]):
                copy.start()

    def ffn_rows(n_rows):
        x = x_ref[0:n_rows, :].astype(BF16)
        hmid = _dot(x, wgu_s[...]) + bgu_ref[0]
        m = hmid.shape[0]
        even = (_iota((m, LANES), 1) & 1) == 0
        acts = []
        for k in range(hmid.shape[1] // (2 * LANES)):
            h0 = hmid[:, (2 * k) * LANES:(2 * k + 1) * LANES]
            h1 = hmid[:, (2 * k + 1) * LANES:(2 * k + 2) * LANES]
            glu = jnp.where(even, h0, pltpu.roll(h1, 1, 1))
            lin = jnp.where(even, pltpu.roll(h0, LANES - 1, 1), h1)
            glu = jnp.minimum(glu, SWIGLU_LIMIT)
            lin = jnp.clip(lin, -SWIGLU_LIMIT, SWIGLU_LIMIT)
            acts.append((glu * _sigmoid(SWIGLU_ALPHA * glu) * (lin + 1.0)).astype(BF16))
        act = jnp.concatenate(acts, axis=1)
        wd = jnp.concatenate([wd_s[c] for c in range(wd_s.shape[0])], axis=1)
        o_ref[0:n_rows, :] = _dot(act, wd) + bd_ref[0]

    half_block = MOE_BLOCK // 2

    @pl.when(jnp.logical_and(active, rows_left > half_block))
    def _():
        ffn_rows(MOE_BLOCK)

    @pl.when(jnp.logical_and(active, rows_left <= half_block))
    def _():
        ffn_rows(half_block)


def _ffn(layer, block_expert, n_active, plan, xs, wgu, bgu, wd, bd):
    n_slots, d = xs.shape
    dff2 = wgu.shape[3]
    nb = n_slots // MOE_BLOCK
    rows = lambda j, be, na, plan_: (jnp.maximum(jnp.minimum(j, na[0] - 1), 0), 0)
    per_expert = lambda j, be, na, plan_: (layer, be[j], 0, 0)
    grid_spec = pltpu.PrefetchScalarGridSpec(
        num_scalar_prefetch=3,
        grid=(nb,),
        in_specs=[pl.BlockSpec((MOE_BLOCK, d), rows),
                  pl.BlockSpec(memory_space=pl.ANY), pl.BlockSpec((None, 1, 1, dff2), per_expert),
                  pl.BlockSpec(memory_space=pl.ANY), pl.BlockSpec((None, 1, 1, d), per_expert)],
        out_specs=pl.BlockSpec((MOE_BLOCK, d), rows),
        scratch_shapes=[pltpu.VMEM((d, dff2), F32), pltpu.VMEM((dff2 // 2, d), F32),
                        pltpu.VMEM((d, dff2), BF16), pltpu.VMEM((d // LANES, dff2 // 2, LANES), F32),
                        pltpu.VMEM((d // LANES, dff2 // 2, LANES), BF16), pltpu.SemaphoreType.DMA((2,))],
    )
    return pl.pallas_call(
        functools.partial(_ffn_kernel, layer=layer),
        grid_spec=grid_spec,
        out_shape=jax.ShapeDtypeStruct((n_slots, d), F32),
        compiler_params=_params("arbitrary"),
        name="moe_ffn",
    )(block_expert, n_active, plan, xs, wgu, bgu, wd, bd)


def _combine_kernel(runs_ref, next_runs_ref, lk_ref, gk_ref, h_ref, g_ref, ys_hbm, o_ref, sorted_s, sem, *,
                    final_norm):
    i = pl.program_id(0)
    tm = h_ref.shape[0]
    n_local = sorted_s.shape[1]
    slot = i % 2

    def from_slots(buf):
        return lambda src, dst, size: pltpu.make_async_copy(
            ys_hbm.at[pl.ds(src, size)], sorted_s.at[buf, pl.ds(dst, size)], sem.at[buf])

    def start_runs(table, buf):
        def body(e, carry):
            _copy_run(table[RUN_SLOT, e], table[RUN_START, e], table[RUN_LEN, e], RUN_BITS, from_slots(buf),
                      False)
            return carry
        lax.fori_loop(0, N_EXPERTS, body, 0)

    @pl.when(i == 0)
    def _():
        sorted_s[...] = jnp.zeros_like(sorted_s)
        start_runs(runs_ref, slot)

    @pl.when(i + 1 < pl.num_programs(0))
    def _():
        start_runs(next_runs_ref, 1 - slot)

    _copy_run(0, 0, runs_ref[RUN_TOTAL, 0], TOTAL_BITS, lambda src, dst, size: from_slots(slot)(0, 0, size),
              True)
    position = _iota((tm, n_local), 1).astype(F32)
    lk = lk_ref[...]
    gk = gk_ref[...]
    weights = jnp.zeros((tm, n_local), F32)
    for k in range(TOP_K):
        weights = weights + jnp.where(position == lk[:, k:k + 1], gk[:, k:k + 1], 0.0)
    acc = h_ref[...] + _dot(weights.astype(BF16), sorted_s[slot].astype(BF16))
    if final_norm:
        ms = jnp.mean(acc * acc, axis=-1, keepdims=True)
        acc = acc * lax.rsqrt(ms + EPS) * g_ref[...]
    o_ref[...] = acc


def _combine(runs, lk, gk, h, ys, g, final_norm):
    t, d = h.shape
    tm = ROW_TILE
    return pl.pallas_call(
        functools.partial(_combine_kernel, final_norm=final_norm),
        grid=(t // tm,),
        in_specs=[pl.BlockSpec((8, LANES), lambda i: (i, 0), memory_space=pltpu.SMEM),
                  pl.BlockSpec((8, LANES), lambda i: (jnp.minimum(i + 1, t // tm - 1), 0),
                               memory_space=pltpu.SMEM),
                  pl.BlockSpec((tm, LANES), lambda i: (i, 0)),
                  pl.BlockSpec((tm, LANES), lambda i: (i, 0)),
                  pl.BlockSpec((tm, d), lambda i: (i, 0)),
                  pl.BlockSpec((1, d), lambda i: (0, 0)),
                  pl.BlockSpec(memory_space=pl.ANY)],
        out_specs=pl.BlockSpec((tm, d), lambda i: (i, 0)),
        out_shape=jax.ShapeDtypeStruct((t, d), F32),
        scratch_shapes=[pltpu.VMEM((2, LOCAL_ROWS, d), F32), pltpu.SemaphoreType.DMA((2,))],
        compiler_params=_params("arbitrary"),
        name="moe_combine",
    )(runs, runs, lk, gk, h, g.reshape(1, d), ys)


def _moe(h, xn, sel, gate, rank, counts, tile_counts, layer, w_gu, b_gu, w_down, b_down, g_final,
         final_norm):
    t, d = h.shape
    n_slots = t * TOP_K + (t // ROW_TILE) * N_EXPERTS * RUN_ALIGN + N_EXPERTS * MOE_BLOCK
    n_blocks = n_slots // MOE_BLOCK
    pad_start, plan = _plan(counts, n_blocks)
    lpos, lk, gk, runs = _dest(sel, gate, rank, tile_counts, pad_start)
    xs = _dispatch(plan, runs, lpos, xn, n_slots)
    block_expert = plan[PLAN_EXPERTS:].reshape(-1)[:n_blocks]
    n_active = plan[PLAN_ACTIVE, :1]
    depth = w_gu.shape[0]
    ys = _ffn(layer, block_expert, n_active, plan, xs, w_gu, b_gu.reshape(depth, N_EXPERTS, 1, -1), w_down,
              b_down.reshape(depth, N_EXPERTS, 1, d))
    return _combine(runs, lk, gk, h, ys, g_final, final_norm)


def kernel(x, norm_mix_g, w_in, sb_norm_g, rg_conv_w, rg_conv_b, rg_wa, rg_ba, rg_wx, rg_bx,
           rg_lambda, rg_norm_g, hg_lower_bounds, hg_norm_g, m2_conv_w, m2_conv_b, m2_dt_bias,
           m2_a_log, m2_d, m2_norm_g, w_out, norm_ffn_g, router_w, router_b, moe_w_gu, moe_b_gu,
           moe_w_down, moe_b_down, final_norm_g):
    b, l, d = x.shape
    depth = w_in.shape[0]
    t = b * l
    lbs = jnp.cumsum(jax.nn.softmax(hg_lower_bounds.astype(F32), axis=0), axis=0)
    lbs = lbs - lbs[0]
    d_in = w_in.shape[2]
    w_in_p = jnp.zeros((depth, d, PROJ_BLOCKS * GROUP_W), BF16).at[:, :, :d_in].set(w_in.astype(BF16))
    h = x.reshape(t, d)
    for layer in range(depth):
        proj = _norm_inproj(h, norm_mix_g[layer], w_in_p[layer]).reshape(b, l, PROJ_BLOCKS * GROUP_W)
        o_a = _sb_attn(proj, sb_norm_g[layer])
        o_b = _rglru(proj, rg_conv_w[layer], rg_conv_b[layer], rg_wa[layer], rg_ba[layer],
                     rg_wx[layer], rg_bx[layer], rg_lambda[layer], rg_norm_g[layer])
        o_c = _hgrn2(proj, lbs[layer], hg_norm_g[layer])
        o_d = _ssd(proj, m2_conv_w[layer], m2_conv_b[layer], m2_dt_bias[layer], m2_a_log[layer],
                   m2_d[layer], m2_norm_g[layer])
        h, xn, sel, gate, rank, counts, tile_counts = _outproj_router(
            h, (o_a, o_b, o_c, o_d), w_out[layer], norm_ffn_g[layer], router_w[layer],
            router_b[layer])
        h = _moe(h, xn, sel, gate, rank, counts, tile_counts, layer, moe_w_gu, moe_b_gu, moe_w_down,
                 moe_b_down, final_norm_g, layer == depth - 1)
    return h.reshape(b, l, d)
```

```python
import functools
import math

import jax
import jax.numpy as jnp
from jax import lax
from jax.experimental import pallas as pl
from jax.experimental.pallas import tpu as pltpu

F32 = jnp.float32
BF16 = jnp.bfloat16
I32 = jnp.int32
HIGHEST = lax.Precision.HIGHEST
EPS = 1e-6
EXP_UNDERFLOW = -104.0

LANES = 128
GROUP_W = 256
HEAD_DIM = 64
N_HEADS = 4
SSD_STATE = 128
RG_C = 8.0
N_EXPERTS = 32
TOP_K = 4
SWIGLU_LIMIT = 7.0
SWIGLU_ALPHA = 1.702
PROJ_BLOCKS = 14
VMEM_LIMIT = 56 * 1024 * 1024

ROW_TILE = 256
INPROJ_TILE = 512
ATTN_BLOCK = 256
SCAN_TILE = 256
HG_CHUNK = 16
HG_BLOCK = 256
SSD_CHUNK = LANES
MOE_BLOCK = 512
HG_UNROLL = 32

PLAN_ACTIVE, PLAN_COUNT, PLAN_START, PLAN_PADDED, PLAN_EXPERTS = 0, 1, 2, 3, 4
PLAN_EXPERT_ROWS = 8 - PLAN_EXPERTS
RUN_LEN, RUN_START, RUN_SLOT, RUN_TOTAL = 0, 1, 2, 3
RUN_ALIGN = 8
RUN_BITS = (ROW_TILE // RUN_ALIGN).bit_length()
PAD_BITS = (MOE_BLOCK // RUN_ALIGN - 1).bit_length()
LOCAL_ROWS = ROW_TILE * TOP_K + N_EXPERTS * RUN_ALIGN
TOTAL_BITS = (LOCAL_ROWS // RUN_ALIGN).bit_length()


def _params(*sem):
    return pltpu.CompilerParams(dimension_semantics=sem, vmem_limit_bytes=VMEM_LIMIT)


def _dot(a, b):
    return jnp.dot(a, b, preferred_element_type=F32)


def _dot_hi(a, b):
    return jnp.dot(a, b, preferred_element_type=F32, precision=HIGHEST)


def _split3(x):
    hi = x.astype(BF16)
    rest = x - hi.astype(F32)
    mid = rest.astype(BF16)
    lo = (rest - mid.astype(F32)).astype(BF16)
    return hi, mid, lo


def _dot_exact_lhs(m, x):
    return sum(_dot(m, part) for part in _split3(x))


def _dot_exact_rhs(x, m):
    return sum(_dot(part, m) for part in _split3(x))


def _dot_nt(a, b, precision=None):
    return lax.dot_general(a, b, (((1,), (1,)), ((), ())), preferred_element_type=F32,
                           precision=precision)


def _dot_tn(a, b):
    return lax.dot_general(a, b, (((0,), (0,)), ((), ())), preferred_element_type=F32)


def _sigmoid(x):
    return 1.0 / (1.0 + jnp.exp(-x))


def _silu(x):
    return x * _sigmoid(x)


def _softplus(x):
    return jnp.maximum(x, 0.0) + jnp.log(1.0 + jnp.exp(-jnp.abs(x)))


def _iota(shape, dim):
    return lax.broadcasted_iota(I32, shape, dim)


def _div_pow2(x, n):
    assert n & (n - 1) == 0
    return jnp.right_shift(x, int(math.log2(n)))


def _shift_rows(x, d, prev8):
    r = pltpu.roll(x, d, 0)
    p = pltpu.roll(prev8, d, 0)
    head = jnp.where(_iota(prev8.shape, 0) < d, p, r[:8])
    return jnp.concatenate([head, r[8:]], axis=0)


def _causal_conv(x, prev8, w, b):
    k = w.shape[0]
    y = x * w[k - 1:k] + b
    for d in range(1, k):
        y = y + _shift_rows(x, d, prev8) * w[k - 1 - d:k - d]
    return y


def _norm_inproj_kernel(x_ref, g_ref, w_ref, o_ref):
    x = x_ref[...]
    ms = jnp.mean(x * x, axis=-1, keepdims=True)
    xn = (x * lax.rsqrt(ms + EPS) * g_ref[...]).astype(BF16)
    o_ref[...] = _dot(xn, w_ref[...])


def _norm_inproj(h, g, w):
    t, d = h.shape
    n = w.shape[1]
    return pl.pallas_call(
        _norm_inproj_kernel,
        grid=(t // INPROJ_TILE,),
        in_specs=[pl.BlockSpec((INPROJ_TILE, d), lambda i: (i, 0)),
                  pl.BlockSpec((1, d), lambda i: (0, 0)),
                  pl.BlockSpec((d, n), lambda i: (0, 0))],
        out_specs=pl.BlockSpec((INPROJ_TILE, n), lambda i: (i, 0)),
        out_shape=jax.ShapeDtypeStruct((t, n), F32),
        compiler_params=_params("parallel"),
        name="norm_inproj",
    )(h, g.reshape(1, d), w)


def _sb_attn_kernel(q_ref, k_ref, v_ref, g_ref, o_ref, acc_ref, run_ref):
    i = pl.program_id(1)
    blk = ATTN_BLOCK
    scale = HEAD_DIM ** -0.5
    rows = N_HEADS * blk
    below = _iota((rows, blk), 1) < jnp.bitwise_and(_iota((rows, blk), 0), blk - 1)
    r_w = _iota((blk, blk + LANES), 0)
    c_w = _iota((blk, blk + LANES), 1)
    later_sum = jnp.where(jnp.logical_or(c_w >= blk, r_w > c_w), 1.0, 0.0).astype(BF16)
    acc_ref[...] = jnp.zeros_like(acc_ref)
    run_ref[...] = jnp.zeros_like(run_ref)
    q_all = q_ref[0] * scale
    head_of_lane = _div_pow2(_iota((blk, GROUP_W), 1), HEAD_DIM)
    q_heads = jnp.concatenate([jnp.where(head_of_lane == h, q_all, 0.0) for h in range(N_HEADS)],
                              axis=0).astype(BF16)

    def scores(j, diagonal):
        ks = pl.multiple_of(j * blk, blk)
        z = _dot_nt(q_heads, k_ref[0, pl.ds(ks, blk), :].astype(BF16))
        ls = jnp.minimum(z, 0.0) - jnp.log(1.0 + jnp.exp(-jnp.abs(z)))
        lk = ls - z
        if diagonal:
            lk = jnp.where(below, lk, 0.0)
        lk_hi = lk.astype(BF16)
        lk_lo = (lk - lk_hi.astype(F32)).astype(BF16)
        stacked = _dot(jnp.concatenate([lk_hi, lk_lo], axis=0), later_sum)
        return ls, stacked[:rows] + stacked[rows:]

    def accumulate(j, ls, sums, diagonal):
        ks = pl.multiple_of(j * blk, blk)
        run = run_ref[...]
        rem = sums[:, :blk] + jnp.concatenate([run] * (blk // LANES), axis=1)
        a = jnp.exp(ls + rem)
        if diagonal:
            a = jnp.where(below, a, 0.0)
        out = _dot(a.astype(BF16), v_ref[0, pl.ds(ks, blk), :].astype(BF16))
        for h in range(N_HEADS):
            sl = slice(h * HEAD_DIM, (h + 1) * HEAD_DIM)
            acc_ref[:, sl] += out[h * blk:(h + 1) * blk, sl]
        run_ref[...] = run + sums[:, blk:]

    def key_block(j, diagonal):
        accumulate(j, *scores(j, diagonal), diagonal)

    @pl.when(i == 0)
    def _():
        key_block(i, True)

    @pl.when(i > 0)
    def _():
        diag = scores(i, True)
        prev = scores(i - 1, False)
        accumulate(i, *diag, True)
        accumulate(i - 1, *prev, False)

    def live():
        return jnp.max(run_ref[...]) > EXP_UNDERFLOW

    def more(carry):
        jj, alive = carry
        return jnp.logical_and(jj <= i, alive)

    def body(carry):
        jj, _ = carry
        key_block(i - jj, False)
        return jj + 1, live()

    lax.while_loop(more, body, (jnp.int32(2), live()))
    o = acc_ref[...]
    ms = jnp.mean(o * o, axis=-1, keepdims=True)
    o_ref[0] = o * lax.rsqrt(ms + EPS) * g_ref[...]


def _sb_attn(proj, g):
    b, l, _ = proj.shape
    blk = ATTN_BLOCK
    return pl.pallas_call(
        _sb_attn_kernel,
        grid=(b, l // blk),
        in_specs=[pl.BlockSpec((1, blk, GROUP_W), lambda bi, i: (bi, i, 0)),
                  pl.BlockSpec((1, l, GROUP_W), lambda bi, i: (bi, 0, 1)),
                  pl.BlockSpec((1, l, GROUP_W), lambda bi, i: (bi, 0, 2)),
                  pl.BlockSpec((1, GROUP_W), lambda bi, i: (0, 0))],
        out_specs=pl.BlockSpec((1, blk, GROUP_W), lambda bi, i: (bi, i, 0)),
        out_shape=jax.ShapeDtypeStruct((b, l, GROUP_W), F32),
        scratch_shapes=[pltpu.VMEM((blk, GROUP_W), F32), pltpu.VMEM((N_HEADS * blk, LANES), F32)],
        compiler_params=_params("parallel", "arbitrary"),
        name="sb_attn",
    )(proj, proj, proj, g.reshape(1, GROUP_W))


def _rglru_kernel(x_ref, gate_ref, cw_ref, cb_ref, wax_ref, ba_ref, bx_ref, lam_ref,
                  ng_ref, o_ref):
    l = x_ref.shape[1]
    w = x_ref.shape[2]
    tc = SCAN_TILE
    rows = _iota((tc, w), 0)
    cw = cw_ref[...]
    cb = cb_ref[...]
    neg_c_sp = -RG_C * _softplus(-lam_ref[...])

    def chunk(c, carry):
        h_prev, tail = carry
        s = pl.multiple_of(c * tc, tc)
        x = x_ref[0, pl.ds(s, tc), :]
        xc = _causal_conv(x, tail, cw, cb)
        xcb = xc.astype(BF16)
        gates = _dot(xcb, wax_ref[...])
        r = _sigmoid(gates[:, :w] + ba_ref[...])
        ig = _sigmoid(gates[:, w:] + bx_ref[...])
        log_a = r * neg_c_sp
        a = jnp.exp(log_a)
        th = jnp.tanh(log_a)
        mult = jnp.sqrt(-2.0 * th / (1.0 - th))
        mult = jnp.where(rows + s == 0, 1.0, mult)
        u = mult * ig * xc
        d = 1
        while d < tc:
            keep = rows >= d
            a_s = jnp.where(keep, pltpu.roll(a, d, 0), 1.0)
            u_s = jnp.where(keep, pltpu.roll(u, d, 0), 0.0)
            u = a * u_s + u
            a = a * a_s
            d *= 2
        hs = u + a * h_prev
        gate = gate_ref[0, pl.ds(s, tc), :]
        gelu = 0.5 * gate * (1.0 + jnp.tanh(math.sqrt(2.0 / math.pi) * (gate + 0.044715 * gate * gate * gate)))
        o = hs * gelu
        ms = jnp.mean(o * o, axis=-1, keepdims=True)
        o_ref[0, pl.ds(s, tc), :] = o * lax.rsqrt(ms + EPS) * ng_ref[...]
        return hs[tc - 1:tc], x[tc - 8:tc]

    lax.fori_loop(0, l // tc, chunk, (jnp.zeros((1, w), F32), jnp.zeros((8, w), F32)))


def _block_diag(wb):
    n, d, _ = wb.shape
    same = jnp.eye(n, dtype=bool)[:, None, :, None]
    return jnp.where(same, wb[:, :, None, :], 0).reshape(n * d, n * d)


def _rglru(proj, conv_w, conv_b, wa, ba, wx, bx, lam, ng):
    b, l, _ = proj.shape
    w = GROUP_W
    row = lambda v: v.reshape(1, w)
    full = lambda shape: pl.BlockSpec(shape, lambda bi: (0,) * len(shape))
    return pl.pallas_call(
        _rglru_kernel,
        grid=(b,),
        in_specs=[pl.BlockSpec((1, l, w), lambda bi: (bi, 0, 3)),
                  pl.BlockSpec((1, l, w), lambda bi: (bi, 0, 4)),
                  full((conv_w.shape[0], w)), full((1, w)), full((w, 2 * w)), full((1, w)),
                  full((1, w)), full((1, w)), full((1, w))],
        out_specs=pl.BlockSpec((1, l, w), lambda bi: (bi, 0, 0)),
        out_shape=jax.ShapeDtypeStruct((b, l, w), F32),
        compiler_params=_params("parallel"),
        name="rglru",
    )(proj, proj, conv_w, row(conv_b),
      jnp.concatenate([_block_diag(wa), _block_diag(wx)], axis=1).astype(BF16), row(ba), row(bx), row(lam),
      row(ng))


def _hgrn2_kernel(q_ref, f_ref, i_ref, g_ref, lb_ref, ng_ref, o_ref,
                  qe_s, ke_s, kl_s, dl_s, o_s, inc_s):
    l = q_ref.shape[1]
    w = q_ref.shape[2]
    tile = SCAN_TILE
    c = HG_CHUNK
    r_t = _iota((tile, tile), 0)
    c_t = _iota((tile, tile), 1)
    same = _div_pow2(r_t, c) == _div_pow2(c_t, c)
    cum_m = jnp.where(jnp.logical_and(same, c_t <= r_t), 1.0, 0.0).astype(BF16)
    tot_m = jnp.where(same, 1.0, 0.0).astype(BF16)
    lb = lb_ref[...]

    def prep(t, carry):
        s = pl.multiple_of(t * tile, tile)
        forget = lb + (1.0 - lb) * _sigmoid(f_ref[0, pl.ds(s, tile), :])
        log_f = jnp.log(forget)
        key = 1.0 - forget
        parts = _split3(log_f)
        bcum = sum(_dot(cum_m, part) for part in parts)
        blast = sum(_dot(tot_m, part) for part in parts)
        q = _silu(q_ref[0, pl.ds(s, tile), :])
        qe_s[pl.ds(s, tile), :] = (q * jnp.exp(bcum)).astype(BF16)
        ke_s[pl.ds(s, tile), :] = (key * jnp.exp(-bcum)).astype(BF16)
        kl_s[pl.ds(s, tile), :] = key * jnp.exp(blast - bcum)
        dl_s[pl.ds(s, tile), :] = jnp.exp(blast)
        return carry

    lax.fori_loop(0, l // tile, prep, 0)

    blk = HG_BLOCK
    per = blk // c
    q_row = jnp.bitwise_and(_iota((N_HEADS * blk, blk), 0), blk - 1)
    k_row = _iota((N_HEADS * blk, blk), 1)
    within = jnp.logical_and(_div_pow2(q_row, c) == _div_pow2(k_row, c), k_row <= q_row)
    own = (_div_pow2(_iota((blk, per * HEAD_DIM), 0), c)
           == _div_pow2(_iota((blk, per * HEAD_DIM), 1), HEAD_DIM))
    head_of_lane = _div_pow2(_iota((blk, w), 1), HEAD_DIM)

    def local(m, carry):
        s = pl.multiple_of(m * blk, blk)
        qe = qe_s[pl.ds(s, blk), :]
        ke = ke_s[pl.ds(s, blk), :]
        kl = kl_s[pl.ds(s, blk), :]
        v = i_ref[0, pl.ds(s, blk), :]
        q_heads = jnp.concatenate([jnp.where(head_of_lane == h, qe, jnp.zeros_like(qe))
                                   for h in range(N_HEADS)], axis=0)
        sc = jnp.where(within, _dot_nt(q_heads, ke), 0.0)
        out = _dot(sc.astype(BF16), v.astype(BF16))
        o_s[pl.ds(s, blk), :] = jnp.concatenate(
            [out[h * blk:(h + 1) * blk, h * HEAD_DIM:(h + 1) * HEAD_DIM] for h in range(N_HEADS)], axis=1)
        for h in range(N_HEADS):
            sl = slice(h * HEAD_DIM, (h + 1) * HEAD_DIM)
            wide = jnp.where(own, jnp.concatenate([kl[:, sl]] * per, axis=1), 0.0)
            inc = _dot_tn(v[:, sl], wide)
            for j in range(per):
                inc_s[m * per + j, :, sl] = inc[:, j * HEAD_DIM:(j + 1) * HEAD_DIM]
        return carry

    lax.fori_loop(0, l // blk, local, 0, unroll=2)

    head_of_chunk_lane = _div_pow2(_iota((c, w), 1), HEAD_DIM)

    def step(n, state):
        s = pl.multiple_of(n * c, c)
        qe = qe_s[pl.ds(s, c), :]
        dl = dl_s[pl.ds(s, 8), :][0:1]
        q_heads = jnp.concatenate([jnp.where(head_of_chunk_lane == h, qe, jnp.zeros_like(qe))
                                   for h in range(N_HEADS)], axis=0)
        carried = _dot_nt(q_heads, state.astype(BF16))
        o_s[pl.ds(s, c), :] += jnp.concatenate([carried[h * c:(h + 1) * c] for h in range(N_HEADS)], axis=1)
        return dl * state + inc_s[n]

    lax.fori_loop(0, l // c, step, jnp.zeros((HEAD_DIM, w), F32), unroll=HG_UNROLL)

    head_avg = jnp.where(_div_pow2(_iota((w, w), 0), HEAD_DIM) == _div_pow2(_iota((w, w), 1), HEAD_DIM),
                         1.0 / HEAD_DIM, 0.0).astype(BF16)

    def finish(t, carry):
        s = pl.multiple_of(t * tile, tile)
        o = o_s[pl.ds(s, tile), :]
        ms = _dot_exact_rhs(o * o, head_avg)
        o_ref[0, pl.ds(s, tile), :] = (o * lax.rsqrt(ms + EPS) * ng_ref[...]
                                       * _silu(g_ref[0, pl.ds(s, tile), :]))
        return carry

    lax.fori_loop(0, l // tile, finish, 0)


def _hgrn2(proj, lb, ng):
    b, l, _ = proj.shape
    w = GROUP_W
    full = lambda shape: pl.BlockSpec(shape, lambda bi: (0,) * len(shape))
    col = lambda j: pl.BlockSpec((1, l, w), lambda bi: (bi, 0, j))
    return pl.pallas_call(
        _hgrn2_kernel,
        grid=(b,),
        in_specs=[col(5), col(6), col(7), col(8), full((1, w)), full((1, w))],
        out_specs=pl.BlockSpec((1, l, w), lambda bi: (bi, 0, 0)),
        out_shape=jax.ShapeDtypeStruct((b, l, w), F32),
        scratch_shapes=[pltpu.VMEM((l, w), BF16), pltpu.VMEM((l, w), BF16), pltpu.VMEM((l, w), F32),
                        pltpu.VMEM((l, w), F32), pltpu.VMEM((l, w), F32),
                        pltpu.VMEM((l // HG_CHUNK, HEAD_DIM, w), F32)],
        compiler_params=_params("parallel"),
        name="hgrn2",
    )(proj, proj, proj, proj, lb.reshape(1, w), ng.reshape(1, w))


def _ssd_kernel(z_ref, x_ref, bm_ref, cm_ref, dt_ref, cw_ref, cb_ref, dtb_ref, aneg_ref,
                dskip_ref, ng_ref, o_ref):
    l = z_ref.shape[1]
    w = GROUP_W
    q = SSD_CHUNK
    p = HEAD_DIM
    r_q = _iota((q, q), 0)
    c_q = _iota((q, q), 1)
    tril = c_q <= r_q
    cum_m = jnp.where(tril, 1.0, 0.0).astype(BF16)
    cw = cw_ref[...]
    cb = cb_ref[...]

    def chunk(n, carry):
        state, tails = carry
        s = pl.multiple_of(n * q, q)
        parts = []
        new_tails = []
        for j, ref in enumerate((x_ref, bm_ref, cm_ref)):
            raw = ref[0, pl.ds(s, q), :]
            sl = slice(j * w, (j + 1) * w)
            parts.append(_silu(_causal_conv(raw, tails[j], cw[:, sl], cb[:, sl])))
            new_tails.append(raw[q - 8:q])
        xs, bm, cm = parts
        dt = _softplus(dt_ref[0, pl.ds(s, q), :] + dtb_ref[...])
        dt_rep = jnp.concatenate([jnp.broadcast_to(dt[:, h:h + 1], (q, p)) for h in range(N_HEADS)],
                                 axis=1)
        a_cs = _dot_exact_lhs(cum_m, dt_rep * aneg_ref[...])
        a_rows = [jnp.transpose(a_cs[:, g * LANES:(g + 1) * LANES]) for g in range(w // LANES)]
        xd = xs * dt_rep
        groups = w // SSD_STATE
        heads_per_group = N_HEADS // groups
        cgs = [cm[:, g * SSD_STATE:(g + 1) * SSD_STATE].astype(BF16) for g in range(groups)]
        cbs = [_dot_nt(cgs[g], bm[:, g * SSD_STATE:(g + 1) * SSD_STATE].astype(BF16)) for g in range(groups)]
        masked = []
        for h in range(N_HEADS):
            row = (h * p) % LANES
            a_row = a_rows[(h * p) // LANES][row:row + 1, :]
            a_col = a_cs[:, h * p:h * p + 1]
            decay = jnp.exp(jnp.where(tril, a_col - a_row, -jnp.inf))
            masked.append((cbs[h // heads_per_group] * decay).astype(BF16))
        within = _dot(jnp.concatenate(masked, axis=0), xd.astype(BF16))
        y = jnp.concatenate([within[h * q:(h + 1) * q, h * p:(h + 1) * p] for h in range(N_HEADS)], axis=1)
        to_end = xd * jnp.exp(a_cs[q - 1:q, :] - a_cs)
        carried = []
        new_state = []
        for g in range(groups):
            gs = slice(g * SSD_STATE, (g + 1) * SSD_STATE)
            st = state[g]
            carried.append(_dot_nt(cgs[g], st.astype(BF16)))
            chunk_decay = jnp.exp(a_rows[g][:, q - 1:q])
            new_state.append(chunk_decay * st + _dot_tn(to_end[:, gs], bm[:, gs]))
        y = y + jnp.exp(a_cs) * jnp.concatenate(carried, axis=1) + xs * dskip_ref[...]
        y = y * _silu(z_ref[0, pl.ds(s, q), :])
        halves = []
        for gi in range(2):
            yg = y[:, gi * LANES:(gi + 1) * LANES]
            ms = jnp.mean(yg * yg, axis=-1, keepdims=True)
            halves.append(yg * lax.rsqrt(ms + EPS))
        o_ref[0, pl.ds(s, q), :] = jnp.concatenate(halves, axis=1) * ng_ref[...]
        return tuple(new_state), tuple(new_tails)

    assert SSD_STATE == LANES and (N_HEADS // (w // SSD_STATE)) * p == LANES
    zero = jnp.zeros((LANES, SSD_STATE), F32)
    tail0 = jnp.zeros((8, w), F32)
    lax.fori_loop(0, l // q, chunk, ((zero,) * (w // SSD_STATE), (tail0,) * 3), unroll=4)


def _ssd(proj, conv_w, conv_b, dt_bias, a_log, d_skip, ng):
    b, l, _ = proj.shape
    w = GROUP_W
    full = lambda shape: pl.BlockSpec(shape, lambda bi: (0,) * len(shape))
    col = lambda j: pl.BlockSpec((1, l, w), lambda bi: (bi, 0, j))
    pad_heads = lambda v: jnp.zeros((1, w), F32).at[0, :N_HEADS].set(v)
    return pl.pallas_call(
        _ssd_kernel,
        grid=(b,),
        in_specs=[col(9), col(10), col(11), col(12), col(13),
                  full((conv_w.shape[0], 3 * w)), full((1, 3 * w)), full((1, w)), full((1, w)),
                  full((1, w)), full((1, w))],
        out_specs=pl.BlockSpec((1, l, w), lambda bi: (bi, 0, 0)),
        out_shape=jax.ShapeDtypeStruct((b, l, w), F32),
        compiler_params=_params("parallel"),
        name="ssd",
    )(proj, proj, proj, proj, proj, conv_w, conv_b.reshape(1, 3 * w), pad_heads(dt_bias),
      jnp.repeat(-jnp.exp(a_log), HEAD_DIM).reshape(1, w), jnp.repeat(d_skip, HEAD_DIM).reshape(1, w),
      ng.reshape(1, w))


def _outproj_router_kernel(h_ref, oa_ref, ob_ref, oc_ref, od_ref, wo_ref, g_ref, rw_ref, rb_ref,
                           h_out, xn_out, sel_out, gate_out, rank_out, cnt_out, tile_out, cnt_s):
    i = pl.program_id(0)
    tm = h_ref.shape[0]

    @pl.when(i == 0)
    def _():
        cnt_s[...] = jnp.zeros_like(cnt_s)

    mix = jnp.concatenate([ref[...].astype(BF16) for ref in (oa_ref, ob_ref, oc_ref, od_ref)], axis=1)
    acc = h_ref[...] + _dot(mix, wo_ref[...])
    h_out[...] = acc
    ms = jnp.mean(acc * acc, axis=-1, keepdims=True)
    xn = acc * lax.rsqrt(ms + EPS) * g_ref[...]
    xn_out[...] = xn
    lane = _iota((tm, LANES), 1)
    lane_f = lane.astype(F32)
    xn_hi = xn.astype(BF16)
    xn_lo = (xn - xn_hi.astype(F32)).astype(BF16)
    both = _dot(jnp.concatenate([xn_hi, xn_lo], axis=0), rw_ref[...])
    router = both[:tm, :LANES] + both[tm:, :LANES] + both[:tm, LANES:]
    logits = jnp.where(lane < N_EXPERTS, router + rb_ref[...], -jnp.inf)
    sel = jnp.zeros((tm, LANES), F32)
    tops = []
    for k in range(TOP_K):
        m = jnp.max(logits, axis=-1, keepdims=True)
        idx = jnp.min(jnp.where(logits == m, lane_f, float(LANES)), axis=-1, keepdims=True)
        hit = lane_f == idx
        sel = jnp.where(hit, float(k + 1), sel)
        logits = jnp.where(hit, -jnp.inf, logits)
        tops.append(m)
    exps = [jnp.exp(m - tops[0]) for m in tops]
    denom = exps[0] + exps[1] + exps[2] + exps[3]
    gate = jnp.zeros((tm, LANES), F32)
    for k in range(TOP_K):
        gate = jnp.where(sel == float(k + 1), exps[k] / denom, gate)
    sel_out[...] = sel
    gate_out[...] = gate
    member = jnp.where(sel > 0.0, 1.0, 0.0)
    before = jnp.where(_iota((tm, tm), 1) < _iota((tm, tm), 0), 1.0, 0.0).astype(BF16)
    carry = cnt_s[0:1, :]
    rank_out[...] = _dot(before, member.astype(BF16))
    in_tile = jnp.ceil(jnp.sum(member, axis=0, keepdims=True) * (1.0 / RUN_ALIGN)) * RUN_ALIGN
    total = carry + in_tile
    tile_out[...] = jnp.concatenate([carry, in_tile, jnp.zeros((6, LANES), F32)], axis=0)
    cnt_s[...] = jnp.broadcast_to(total, cnt_s.shape)
    cnt_out[...] = jnp.broadcast_to(total, cnt_out.shape)


def _outproj_router(h, outs, w_out, g, router_w, router_b):
    t, d = h.shape
    tm = ROW_TILE
    rw = jnp.zeros((d, LANES), F32).at[:, :N_EXPERTS].set(router_w)
    rb = jnp.zeros((1, LANES), F32).at[0, :N_EXPERTS].set(router_b)
    rw_hi = rw.astype(BF16)
    rw_lo = (rw - rw_hi.astype(F32)).astype(BF16)
    tile = lambda width: pl.BlockSpec((tm, width), lambda i: (i, 0))
    full = lambda shape: pl.BlockSpec(shape, lambda i: (0,) * len(shape))
    flat = [o.reshape(t, GROUP_W) for o in outs]
    return pl.pallas_call(
        _outproj_router_kernel,
        grid=(t // tm,),
        in_specs=[tile(d)] + [tile(GROUP_W)] * 4 + [full((d, d)), full((1, d)), full((d, 2 * LANES)),
                                                   full((1, LANES))],
        out_specs=[tile(d), tile(d), tile(LANES), tile(LANES), tile(LANES), full((8, LANES)),
                   pl.BlockSpec((8, LANES), lambda i: (i, 0))],
        out_shape=[jax.ShapeDtypeStruct((t, d), F32), jax.ShapeDtypeStruct((t, d), F32),
                   jax.ShapeDtypeStruct((t, LANES), F32), jax.ShapeDtypeStruct((t, LANES), F32),
                   jax.ShapeDtypeStruct((t, LANES), F32), jax.ShapeDtypeStruct((8, LANES), F32),
                   jax.ShapeDtypeStruct((t // tm * 8, LANES), F32)],
        scratch_shapes=[pltpu.VMEM((8, LANES), F32)],
        compiler_params=_params("arbitrary"),
        name="outproj_router",
    )(h, *flat, w_out.astype(BF16), g.reshape(1, d), jnp.concatenate([rw_hi, rw_lo], axis=1), rb)


def _plan_kernel(cnt_ref, start_out, plan_out):
    cnt = cnt_ref[...]
    padded = jnp.ceil(cnt * (1.0 / MOE_BLOCK)) * MOE_BLOCK
    upto = jnp.where(_iota((LANES, LANES), 0) <= _iota((LANES, LANES), 1), 1.0, 0.0)
    pad_end = _dot_hi(padded, upto)
    pad_start = pad_end - padded
    start_out[...] = pad_start
    ends = jnp.transpose(jnp.broadcast_to(pad_end[0:1, :], (LANES, LANES)))
    expert_ok = _iota((LANES, LANES), 0) < N_EXPERTS
    n_active = pad_end[0:1, N_EXPERTS - 1:N_EXPERTS] * (1.0 / MOE_BLOCK)
    rows = [jnp.broadcast_to(n_active, (1, LANES)), cnt[0:1, :], pad_start[0:1, :], padded[0:1, :]]
    for r in range(PLAN_EXPERT_ROWS):
        blk_start = ((_iota((LANES, LANES), 1) + r * LANES) * MOE_BLOCK).astype(F32)
        done = jnp.where(jnp.logical_and(expert_ok, ends <= blk_start), 1.0, 0.0)
        rows.append(jnp.minimum(jnp.sum(done, axis=0, keepdims=True), N_EXPERTS - 1.0))
    plan_out[...] = jnp.concatenate(rows, axis=0).astype(I32)


def _plan(counts, n_blocks):
    assert n_blocks <= PLAN_EXPERT_ROWS * LANES
    return pl.pallas_call(
        _plan_kernel,
        out_shape=[jax.ShapeDtypeStruct((8, LANES), F32), jax.ShapeDtypeStruct((8, LANES), I32)],
        name="moe_plan",
    )(counts)


def _dest_kernel(sel_ref, gate_ref, rank_ref, tile_ref, start_ref, lpos_out, lk_out, gk_out, runs_out):
    tm = sel_ref.shape[0]
    sel = sel_ref[...]
    gate = gate_ref[...]
    before_tile = tile_ref[0:1, :]
    in_tile = tile_ref[1:2, :]
    earlier = jnp.where(_iota((LANES, LANES), 0) < _iota((LANES, LANES), 1), 1.0, 0.0)
    run_start = _dot_hi(jnp.broadcast_to(in_tile, (8, LANES)), earlier)[0:1, :]
    lpos = rank_ref[...] + run_start
    lane = _iota((tm, LANES), 1)
    pieces = []
    gk = jnp.zeros((tm, LANES), F32)
    lk = jnp.zeros((tm, LANES), F32)
    for k in range(TOP_K):
        hit = sel == float(k + 1)
        mine = jnp.where(hit, lpos, 0.0)
        pieces.append(mine)
        gk = jnp.where(lane == k, jnp.sum(jnp.where(hit, gate, 0.0), axis=-1, keepdims=True), gk)
        lk = jnp.where(lane == k, jnp.sum(mine, axis=-1, keepdims=True), lk)
    stacked = jnp.concatenate(pieces, axis=1)
    picker = jnp.where(_div_pow2(_iota((8, TOP_K * LANES), 1), LANES) == _iota((8, TOP_K * LANES), 0),
                       1.0, 0.0).astype(BF16)
    lpos_out[...] = sum(_dot_nt(picker, part) for part in _split3(stacked)).astype(I32)
    lk_out[...] = lk
    gk_out[...] = gk
    first_slot = start_ref[0:1, :] + before_tile
    total = jnp.broadcast_to(jnp.sum(in_tile, axis=-1, keepdims=True), (1, LANES))
    runs_out[...] = jnp.concatenate([in_tile, run_start, first_slot, total, jnp.zeros((4, LANES), F32)],
                                    axis=0).astype(I32)


def _dest(sel, gate, rank, tile_counts, pad_start):
    t = sel.shape[0]
    tm = ROW_TILE
    tile = pl.BlockSpec((tm, LANES), lambda i: (i, 0))
    per_tile = pl.BlockSpec((8, LANES), lambda i: (i, 0))
    return pl.pallas_call(
        _dest_kernel,
        grid=(t // tm,),
        in_specs=[tile, tile, tile, per_tile, pl.BlockSpec((8, LANES), lambda i: (0, 0))],
        out_specs=[pl.BlockSpec((8, tm), lambda i: (0, i)), tile, tile, per_tile],
        out_shape=[jax.ShapeDtypeStruct((8, t), I32), jax.ShapeDtypeStruct((t, LANES), F32),
                   jax.ShapeDtypeStruct((t, LANES), F32), jax.ShapeDtypeStruct((t // tm * 8, LANES), I32)],
        compiler_params=_params("parallel"),
        name="moe_dest",
    )(sel, gate, rank, tile_counts, pad_start)


def _copy_run(src, dst, length, n_bits, make_copy, wait):
    units = jnp.right_shift(length, RUN_ALIGN.bit_length() - 1)
    for b in range(n_bits):
        offset = jnp.left_shift(jnp.right_shift(units, b + 1), b + 1) * RUN_ALIGN

        @pl.when(jnp.bitwise_and(jnp.right_shift(units, b), 1) == 1)
        def _():
            copy = make_copy(pl.multiple_of(src + offset, RUN_ALIGN), pl.multiple_of(dst + offset, RUN_ALIGN),
                             RUN_ALIGN << b)
            if wait:
                copy.wait()
            else:
                copy.start()


def _dispatch_kernel(plan_ref, runs_ref, prev_runs_ref, lpos_ref, xn_ref, xs_hbm, sorted_s, sem):
    i = pl.program_id(0)
    tm = xn_ref.shape[0]
    n_local = sorted_s.shape[1]
    slot = i % 2

    def to_slots(buf):
        return lambda src, dst, size: pltpu.make_async_copy(
            sorted_s.at[buf, pl.ds(src, size)], xs_hbm.at[pl.ds(dst, size)], sem.at[buf])

    def start_runs(table, buf):
        def body(e, carry):
            _copy_run(table[RUN_START, e], table[RUN_SLOT, e], table[RUN_LEN, e], RUN_BITS, to_slots(buf),
                      False)
            return carry
        lax.fori_loop(0, N_EXPERTS, body, 0)

    def wait_runs(table, buf):
        _copy_run(0, 0, table[RUN_TOTAL, 0], TOTAL_BITS, lambda src, dst, size: to_slots(buf)(0, 0, size),
                  True)

    position = _iota((n_local, tm), 0)
    lpos = lpos_ref[...]
    place = jnp.zeros((n_local, tm), F32)
    for k in range(TOP_K):
        place = place + jnp.where(position == lpos[k:k + 1, :], 1.0, 0.0)
    sorted_s[slot] = _dot(place.astype(BF16), xn_ref[...].astype(BF16))
    start_runs(runs_ref, slot)

    @pl.when(i > 0)
    def _():
        wait_runs(prev_runs_ref, 1 - slot)

    @pl.when(i == pl.num_programs(0) - 1)
    def _():
        wait_runs(runs_ref, slot)

    @pl.when(i == 0)
    def _():
        def padding(wait):
            def fill(src, dst, size):
                return pltpu.make_async_copy(sorted_s.at[0, pl.ds(0, size)], xs_hbm.at[pl.ds(dst, size)],
                                             sem.at[1])

            def body(e, carry):
                cnt = plan_ref[PLAN_COUNT, e]
                _copy_run(0, plan_ref[PLAN_START, e] + cnt, plan_ref[PLAN_PADDED, e] - cnt, PAD_BITS, fill,
                          wait)
                return carry
            lax.fori_loop(0, N_EXPERTS, body, 0)

        padding(False)
        padding(True)


def _dispatch(plan, runs, lpos, xn, n_slots):
    t, d = xn.shape
    tm = ROW_TILE
    return pl.pallas_call(
        _dispatch_kernel,
        grid=(t // tm,),
        in_specs=[pl.BlockSpec(memory_space=pltpu.SMEM),
                  pl.BlockSpec((8, LANES), lambda i: (i, 0), memory_space=pltpu.SMEM),
                  pl.BlockSpec((8, LANES), lambda i: (jnp.maximum(i - 1, 0), 0), memory_space=pltpu.SMEM),
                  pl.BlockSpec((8, tm), lambda i: (0, i)),
                  pl.BlockSpec((tm, d), lambda i: (i, 0))],
        out_specs=pl.BlockSpec(memory_space=pl.ANY),
        out_shape=jax.ShapeDtypeStruct((n_slots, d), F32),
        scratch_shapes=[pltpu.VMEM((2, LOCAL_ROWS, d), F32), pltpu.SemaphoreType.DMA((2,))],
        compiler_params=_params("arbitrary"),
        name="moe_dispatch",
    )(plan, runs, runs, lpos, xn)


def _ffn_kernel(be_ref, na_ref, plan_ref, x_ref, wgu_hbm, bgu_ref, wd_hbm, bd_ref, o_ref, wgu_in_s, wd_in_s,
                wgu_s, wd32_s, wd_s, sem, *, layer):
    j = pl.program_id(0)
    n_active = na_ref[0]
    active = j < n_active
    expert = be_ref[j]
    new_expert = jnp.logical_or(j == 0, expert != be_ref[jnp.maximum(j - 1, 0)])
    rows_left = plan_ref[PLAN_START, expert] + plan_ref[PLAN_COUNT, expert] - j * MOE_BLOCK

    def fetch(e):
        return (pltpu.make_async_copy(wgu_hbm.at[layer, e], wgu_in_s, sem.at[0]),
                pltpu.make_async_copy(wd_hbm.at[layer, e], wd_in_s, sem.at[1]))

    @pl.when(j == 0)
    def _():
        for copy in fetch(expert):
            copy.start()

    @pl.when(jnp.logical_and(active, new_expert))
    def _():
        for copy in fetch(expert):
            copy.wait()
        wgu_s[...] = wgu_in_s[...].astype(BF16)
        half = LANES // 2
        for c in range(wd_s.shape[0]):
            cols = slice(c * LANES, (c + 1) * LANES)
            for k in range(wd_s.shape[1] // LANES):
                for hb in range(2):
                    src = k * LANES + hb * half
                    wd32_s[c, pl.ds(k * LANES + hb, half, stride=2), :] = wd_in_s[src:src + half, cols]
            wd_s[c] = wd32_s[c].astype(BF16)
        nxt = lax.while_loop(lambda b: jnp.logical_and(b < n_active, be_ref[jnp.minimum(b, n_active - 1)] == expert),
                             lambda b: b + 1, j + 1)

        @pl.when(nxt < n_active)
        def _():
            for copy in fetch(be_ref[jnp.minimum(nxt, n_active - 1)]):
                copy.start()

    def ffn_rows(n_rows):
        x = x_ref[0:n_rows, :].astype(BF16)
        hmid = _dot(x, wgu_s[...]) + bgu_ref[0]
        m = hmid.shape[0]
        even = (_iota((m, LANES), 1) & 1) == 0
        acts = []
        for k in range(hmid.shape[1] // (2 * LANES)):
            h0 = hmid[:, (2 * k) * LANES:(2 * k + 1) * LANES]
            h1 = hmid[:, (2 * k + 1) * LANES:(2 * k + 2) * LANES]
            glu = jnp.where(even, h0, pltpu.roll(h1, 1, 1))
            lin = jnp.where(even, pltpu.roll(h0, LANES - 1, 1), h1)
            glu = jnp.minimum(glu, SWIGLU_LIMIT)
            lin = jnp.clip(lin, -SWIGLU_LIMIT, SWIGLU_LIMIT)
            acts.append((glu * _sigmoid(SWIGLU_ALPHA * glu) * (lin + 1.0)).astype(BF16))
        act = jnp.concatenate(acts, axis=1)
        wd = jnp.concatenate([wd_s[c] for c in range(wd_s.shape[0])], axis=1)
        o_ref[0:n_rows, :] = _dot(act, wd) + bd_ref[0]

    half_block = MOE_BLOCK // 2

    @pl.when(jnp.logical_and(active, rows_left > half_block))
    def _():
        ffn_rows(MOE_BLOCK)

    @pl.when(jnp.logical_and(active, rows_left <= half_block))
    def _():
        ffn_rows(half_block)


def _ffn(layer, block_expert, n_active, plan, xs, wgu, bgu, wd, bd):
    n_slots, d = xs.shape
    dff2 = wgu.shape[3]
    nb = n_slots // MOE_BLOCK
    rows = lambda j, be, na, plan_: (jnp.maximum(jnp.minimum(j, na[0] - 1), 0), 0)
    per_expert = lambda j, be, na, plan_: (layer, be[j], 0, 0)
    grid_spec = pltpu.PrefetchScalarGridSpec(
        num_scalar_prefetch=3,
        grid=(nb,),
        in_specs=[pl.BlockSpec((MOE_BLOCK, d), rows),
                  pl.BlockSpec(memory_space=pl.ANY), pl.BlockSpec((None, 1, 1, dff2), per_expert),
                  pl.BlockSpec(memory_space=pl.ANY), pl.BlockSpec((None, 1, 1, d), per_expert)],
        out_specs=pl.BlockSpec((MOE_BLOCK, d), rows),
        scratch_shapes=[pltpu.VMEM((d, dff2), F32), pltpu.VMEM((dff2 // 2, d), F32),
                        pltpu.VMEM((d, dff2), BF16), pltpu.VMEM((d // LANES, dff2 // 2, LANES), F32),
                        pltpu.VMEM((d // LANES, dff2 // 2, LANES), BF16), pltpu.SemaphoreType.DMA((2,))],
    )
    return pl.pallas_call(
        functools.partial(_ffn_kernel, layer=layer),
        grid_spec=grid_spec,
        out_shape=jax.ShapeDtypeStruct((n_slots, d), F32),
        compiler_params=_params("arbitrary"),
        name="moe_ffn",
    )(block_expert, n_active, plan, xs, wgu, bgu, wd, bd)


def _combine_kernel(runs_ref, next_runs_ref, lk_ref, gk_ref, h_ref, g_ref, ys_hbm, o_ref, sorted_s, sem, *,
                    final_norm):
    i = pl.program_id(0)
    tm = h_ref.shape[0]
    n_local = sorted_s.shape[1]
    slot = i % 2

    def from_slots(buf):
        return lambda src, dst, size: pltpu.make_async_copy(
            ys_hbm.at[pl.ds(src, size)], sorted_s.at[buf, pl.ds(dst, size)], sem.at[buf])

    def start_runs(table, buf):
        def body(e, carry):
            _copy_run(table[RUN_SLOT, e], table[RUN_START, e], table[RUN_LEN, e], RUN_BITS, from_slots(buf),
                      False)
            return carry
        lax.fori_loop(0, N_EXPERTS, body, 0)

    @pl.when(i == 0)
    def _():
        sorted_s[...] = jnp.zeros_like(sorted_s)
        start_runs(runs_ref, slot)

    @pl.when(i + 1 < pl.num_programs(0))
    def _():
        start_runs(next_runs_ref, 1 - slot)

    _copy_run(0, 0, runs_ref[RUN_TOTAL, 0], TOTAL_BITS, lambda src, dst, size: from_slots(slot)(0, 0, size),
              True)
    position = _iota((tm, n_local), 1).astype(F32)
    lk = lk_ref[...]
    gk = gk_ref[...]
    weights = jnp.zeros((tm, n_local), F32)
    for k in range(TOP_K):
        weights = weights + jnp.where(position == lk[:, k:k + 1], gk[:, k:k + 1], 0.0)
    acc = h_ref[...] + _dot(weights.astype(BF16), sorted_s[slot].astype(BF16))
    if final_norm:
        ms = jnp.mean(acc * acc, axis=-1, keepdims=True)
        acc = acc * lax.rsqrt(ms + EPS) * g_ref[...]
    o_ref[...] = acc


def _combine(runs, lk, gk, h, ys, g, final_norm):
    t, d = h.shape
    tm = ROW_TILE
    return pl.pallas_call(
        functools.partial(_combine_kernel, final_norm=final_norm),
        grid=(t // tm,),
        in_specs=[pl.BlockSpec((8, LANES), lambda i: (i, 0), memory_space=pltpu.SMEM),
                  pl.BlockSpec((8, LANES), lambda i: (jnp.minimum(i + 1, t // tm - 1), 0),
                               memory_space=pltpu.SMEM),
                  pl.BlockSpec((tm, LANES), lambda i: (i, 0)),
                  pl.BlockSpec((tm, LANES), lambda i: (i, 0)),
                  pl.BlockSpec((tm, d), lambda i: (i, 0)),
                  pl.BlockSpec((1, d), lambda i: (0, 0)),
                  pl.BlockSpec(memory_space=pl.ANY)],
        out_specs=pl.BlockSpec((tm, d), lambda i: (i, 0)),
        out_shape=jax.ShapeDtypeStruct((t, d), F32),
        scratch_shapes=[pltpu.VMEM((2, LOCAL_ROWS, d), F32), pltpu.SemaphoreType.DMA((2,))],
        compiler_params=_params("arbitrary"),
        name="moe_combine",
    )(runs, runs, lk, gk, h, g.reshape(1, d), ys)


def _moe(h, xn, sel, gate, rank, counts, tile_counts, layer, w_gu, b_gu, w_down, b_down, g_final,
         final_norm):
    t, d = h.shape
    n_slots = t * TOP_K + (t // ROW_TILE) * N_EXPERTS * RUN_ALIGN + N_EXPERTS * MOE_BLOCK
    n_blocks = n_slots // MOE_BLOCK
    pad_start, plan = _plan(counts, n_blocks)
    lpos, lk, gk, runs = _dest(sel, gate, rank, tile_counts, pad_start)
    xs = _dispatch(plan, runs, lpos, xn, n_slots)
    block_expert = plan[PLAN_EXPERTS:].reshape(-1)[:n_blocks]
    n_active = plan[PLAN_ACTIVE, :1]
    depth = w_gu.shape[0]
    ys = _ffn(layer, block_expert, n_active, plan, xs, w_gu, b_gu.reshape(depth, N_EXPERTS, 1, -1), w_down,
              b_down.reshape(depth, N_EXPERTS, 1, d))
    return _combine(runs, lk, gk, h, ys, g_final, final_norm)


def kernel(x, norm_mix_g, w_in, sb_norm_g, rg_conv_w, rg_conv_b, rg_wa, rg_ba, rg_wx, rg_bx,
           rg_lambda, rg_norm_g, hg_lower_bounds, hg_norm_g, m2_conv_w, m2_conv_b, m2_dt_bias,
           m2_a_log, m2_d, m2_norm_g, w_out, norm_ffn_g, router_w, router_b, moe_w_gu, moe_b_gu,
           moe_w_down, moe_b_down, final_norm_g):
    b, l, d = x.shape
    depth = w_in.shape[0]
    t = b * l
    lbs = jnp.cumsum(jax.nn.softmax(hg_lower_bounds.astype(F32), axis=0), axis=0)
    lbs = lbs - lbs[0]
    d_in = w_in.shape[2]
    w_in_p = jnp.zeros((depth, d, PROJ_BLOCKS * GROUP_W), BF16).at[:, :, :d_in].set(w_in.astype(BF16))
    h = x.reshape(t, d)
    for layer in range(depth):
        proj = _norm_inproj(h, norm_mix_g[layer], w_in_p[layer]).reshape(b, l, PROJ_BLOCKS * GROUP_W)
        o_a = _sb_attn(proj, sb_norm_g[layer])
        o_b = _rglru(proj, rg_conv_w[layer], rg_conv_b[layer], rg_wa[layer], rg_ba[layer],
                     rg_wx[layer], rg_bx[layer], rg_lambda[layer], rg_norm_g[layer])
        o_c = _hgrn2(proj, lbs[layer], hg_norm_g[layer])
        o_d = _ssd(proj, m2_conv_w[layer], m2_conv_b[layer], m2_dt_bias[layer], m2_a_log[layer],
                   m2_d[layer], m2_norm_g[layer])
        h, xn, sel, gate, rank, counts, tile_counts = _outproj_router(
            h, (o_a, o_b, o_c, o_d), w_out[layer], norm_ffn_g[layer], router_w[layer],
            router_b[layer])
        h = _moe(h, xn, sel, gate, rank, counts, tile_counts, layer, moe_w_gu, moe_b_gu, moe_w_down,
                 moe_b_down, final_norm_g, layer == depth - 1)
    return h.reshape(b, l, d)
```

```python
import functools
import math

import jax
import jax.numpy as jnp
from jax import lax
from jax.experimental import pallas as pl
from jax.experimental.pallas import tpu as pltpu

F32 = jnp.float32
BF16 = jnp.bfloat16
I32 = jnp.int32
HIGHEST = lax.Precision.HIGHEST
EPS = 1e-6
EXP_UNDERFLOW = -104.0

LANES = 128
GROUP_W = 256
HEAD_DIM = 64
N_HEADS = 4
SSD_STATE = 128
RG_C = 8.0
N_EXPERTS = 32
TOP_K = 4
SWIGLU_LIMIT = 7.0
SWIGLU_ALPHA = 1.702
PROJ_BLOCKS = 14
VMEM_LIMIT = 56 * 1024 * 1024

ROW_TILE = 256
INPROJ_TILE = 512
ATTN_BLOCK = 256
SCAN_TILE = 256
HG_CHUNK = 16
HG_BLOCK = 256
SSD_CHUNK = LANES
FFN_SPLIT = 4
MOE_BLOCK = 512
HG_UNROLL = 32

PLAN_ACTIVE, PLAN_COUNT, PLAN_START, PLAN_PADDED, PLAN_EXPERTS = 0, 1, 2, 3, 4
PLAN_EXPERT_ROWS = 8 - PLAN_EXPERTS
RUN_LEN, RUN_START, RUN_SLOT, RUN_TOTAL = 0, 1, 2, 3
RUN_ALIGN = 8
RUN_BITS = (ROW_TILE // RUN_ALIGN).bit_length()
PAD_BITS = (MOE_BLOCK // RUN_ALIGN - 1).bit_length()
LOCAL_ROWS = ROW_TILE * TOP_K + N_EXPERTS * RUN_ALIGN
TOTAL_BITS = (LOCAL_ROWS // RUN_ALIGN).bit_length()


def _params(*sem):
    return pltpu.CompilerParams(dimension_semantics=sem, vmem_limit_bytes=VMEM_LIMIT)


def _dot(a, b):
    return jnp.dot(a, b, preferred_element_type=F32)


def _dot_hi(a, b):
    return jnp.dot(a, b, preferred_element_type=F32, precision=HIGHEST)


def _split3(x):
    hi = x.astype(BF16)
    rest = x - hi.astype(F32)
    mid = rest.astype(BF16)
    lo = (rest - mid.astype(F32)).astype(BF16)
    return hi, mid, lo


def _dot_exact_lhs(m, x):
    return sum(_dot(m, part) for part in _split3(x))


def _dot_exact_rhs(x, m):
    return sum(_dot(part, m) for part in _split3(x))


def _dot_nt(a, b, precision=None):
    return lax.dot_general(a, b, (((1,), (1,)), ((), ())), preferred_element_type=F32,
                           precision=precision)


def _dot_tn(a, b):
    return lax.dot_general(a, b, (((0,), (0,)), ((), ())), preferred_element_type=F32)


def _sigmoid(x):
    return 1.0 / (1.0 + jnp.exp(-x))


def _silu(x):
    return x * _sigmoid(x)


def _softplus(x):
    return jnp.maximum(x, 0.0) + jnp.log(1.0 + jnp.exp(-jnp.abs(x)))


def _iota(shape, dim):
    return lax.broadcasted_iota(I32, shape, dim)


def _div_pow2(x, n):
    assert n & (n - 1) == 0
    return jnp.right_shift(x, int(math.log2(n)))


def _shift_rows(x, d, prev8):
    r = pltpu.roll(x, d, 0)
    p = pltpu.roll(prev8, d, 0)
    head = jnp.where(_iota(prev8.shape, 0) < d, p, r[:8])
    return jnp.concatenate([head, r[8:]], axis=0)


def _causal_conv(x, prev8, w, b):
    k = w.shape[0]
    y = x * w[k - 1:k] + b
    for d in range(1, k):
        y = y + _shift_rows(x, d, prev8) * w[k - 1 - d:k - d]
    return y


def _norm_inproj_kernel(x_ref, g_ref, w_ref, o_ref):
    x = x_ref[...]
    ms = jnp.mean(x * x, axis=-1, keepdims=True)
    xn = (x * lax.rsqrt(ms + EPS) * g_ref[...]).astype(BF16)
    o_ref[...] = _dot(xn, w_ref[...])


def _norm_inproj(h, g, w):
    t, d = h.shape
    n = w.shape[1]
    return pl.pallas_call(
        _norm_inproj_kernel,
        grid=(t // INPROJ_TILE,),
        in_specs=[pl.BlockSpec((INPROJ_TILE, d), lambda i: (i, 0)),
                  pl.BlockSpec((1, d), lambda i: (0, 0)),
                  pl.BlockSpec((d, n), lambda i: (0, 0))],
        out_specs=pl.BlockSpec((INPROJ_TILE, n), lambda i: (i, 0)),
        out_shape=jax.ShapeDtypeStruct((t, n), F32),
        compiler_params=_params("parallel"),
        name="norm_inproj",
    )(h, g.reshape(1, d), w)


def _sb_attn_kernel(q_ref, k_ref, v_ref, g_ref, o_ref, acc_ref, run_ref):
    i = pl.program_id(1)
    blk = ATTN_BLOCK
    scale = HEAD_DIM ** -0.5
    rows = N_HEADS * blk
    below = _iota((rows, blk), 1) < jnp.bitwise_and(_iota((rows, blk), 0), blk - 1)
    r_w = _iota((blk, blk + LANES), 0)
    c_w = _iota((blk, blk + LANES), 1)
    later_sum = jnp.where(jnp.logical_or(c_w >= blk, r_w > c_w), 1.0, 0.0).astype(BF16)
    acc_ref[...] = jnp.zeros_like(acc_ref)
    run_ref[...] = jnp.zeros_like(run_ref)
    q_all = q_ref[0] * scale
    head_of_lane = _div_pow2(_iota((blk, GROUP_W), 1), HEAD_DIM)
    q_heads = jnp.concatenate([jnp.where(head_of_lane == h, q_all, 0.0) for h in range(N_HEADS)],
                              axis=0).astype(BF16)

    def key_block(j, diagonal):
        ks = pl.multiple_of(j * blk, blk)
        z = _dot_nt(q_heads, k_ref[0, pl.ds(ks, blk), :].astype(BF16))
        ls = jnp.minimum(z, 0.0) - jnp.log(1.0 + jnp.exp(-jnp.abs(z)))
        lk = ls - z
        if diagonal:
            lk = jnp.where(below, lk, 0.0)
        lk_hi = lk.astype(BF16)
        lk_lo = (lk - lk_hi.astype(F32)).astype(BF16)
        stacked = _dot(jnp.concatenate([lk_hi, lk_lo], axis=0), later_sum)
        sums = stacked[:rows] + stacked[rows:]
        run = run_ref[...]
        rem = sums[:, :blk] + jnp.concatenate([run] * (blk // LANES), axis=1)
        a = jnp.exp(ls + rem)
        if diagonal:
            a = jnp.where(below, a, 0.0)
        out = _dot(a.astype(BF16), v_ref[0, pl.ds(ks, blk), :].astype(BF16))
        for h in range(N_HEADS):
            sl = slice(h * HEAD_DIM, (h + 1) * HEAD_DIM)
            acc_ref[:, sl] += out[h * blk:(h + 1) * blk, sl]
        run_ref[...] = run + sums[:, blk:]

    key_block(i, True)

    def live():
        return jnp.max(run_ref[...]) > EXP_UNDERFLOW

    def more(carry):
        jj, alive = carry
        return jnp.logical_and(jj <= i, alive)

    def body(carry):
        jj, _ = carry
        key_block(i - jj, False)
        return jj + 1, live()

    lax.while_loop(more, body, (jnp.int32(1), live()))
    o = acc_ref[...]
    ms = jnp.mean(o * o, axis=-1, keepdims=True)
    o_ref[0] = o * lax.rsqrt(ms + EPS) * g_ref[...]


def _sb_attn(proj, g):
    b, l, _ = proj.shape
    blk = ATTN_BLOCK
    return pl.pallas_call(
        _sb_attn_kernel,
        grid=(b, l // blk),
        in_specs=[pl.BlockSpec((1, blk, GROUP_W), lambda bi, i: (bi, i, 0)),
                  pl.BlockSpec((1, l, GROUP_W), lambda bi, i: (bi, 0, 1)),
                  pl.BlockSpec((1, l, GROUP_W), lambda bi, i: (bi, 0, 2)),
                  pl.BlockSpec((1, GROUP_W), lambda bi, i: (0, 0))],
        out_specs=pl.BlockSpec((1, blk, GROUP_W), lambda bi, i: (bi, i, 0)),
        out_shape=jax.ShapeDtypeStruct((b, l, GROUP_W), F32),
        scratch_shapes=[pltpu.VMEM((blk, GROUP_W), F32), pltpu.VMEM((N_HEADS * blk, LANES), F32)],
        compiler_params=_params("parallel", "arbitrary"),
        name="sb_attn",
    )(proj, proj, proj, g.reshape(1, GROUP_W))


def _rglru_kernel(x_ref, gate_ref, cw_ref, cb_ref, wax_ref, ba_ref, bx_ref, lam_ref,
                  ng_ref, o_ref):
    l = x_ref.shape[1]
    w = x_ref.shape[2]
    tc = SCAN_TILE
    rows = _iota((tc, w), 0)
    cw = cw_ref[...]
    cb = cb_ref[...]
    neg_c_sp = -RG_C * _softplus(-lam_ref[...])

    def chunk(c, carry):
        h_prev, tail = carry
        s = pl.multiple_of(c * tc, tc)
        x = x_ref[0, pl.ds(s, tc), :]
        xc = _causal_conv(x, tail, cw, cb)
        xcb = xc.astype(BF16)
        gates = _dot(xcb, wax_ref[...])
        r = _sigmoid(gates[:, :w] + ba_ref[...])
        ig = _sigmoid(gates[:, w:] + bx_ref[...])
        log_a = r * neg_c_sp
        a = jnp.exp(log_a)
        th = jnp.tanh(log_a)
        mult = jnp.sqrt(-2.0 * th / (1.0 - th))
        mult = jnp.where(rows + s == 0, 1.0, mult)
        u = mult * ig * xc
        d = 1
        while d < tc:
            keep = rows >= d
            a_s = jnp.where(keep, pltpu.roll(a, d, 0), 1.0)
            u_s = jnp.where(keep, pltpu.roll(u, d, 0), 0.0)
            u = a * u_s + u
            a = a * a_s
            d *= 2
        hs = u + a * h_prev
        gate = gate_ref[0, pl.ds(s, tc), :]
        gelu = 0.5 * gate * (1.0 + jnp.tanh(math.sqrt(2.0 / math.pi) * (gate + 0.044715 * gate * gate * gate)))
        o = hs * gelu
        ms = jnp.mean(o * o, axis=-1, keepdims=True)
        o_ref[0, pl.ds(s, tc), :] = o * lax.rsqrt(ms + EPS) * ng_ref[...]
        return hs[tc - 1:tc], x[tc - 8:tc]

    lax.fori_loop(0, l // tc, chunk, (jnp.zeros((1, w), F32), jnp.zeros((8, w), F32)))


def _block_diag(wb):
    n, d, _ = wb.shape
    same = jnp.eye(n, dtype=bool)[:, None, :, None]
    return jnp.where(same, wb[:, :, None, :], 0).reshape(n * d, n * d)


def _rglru(proj, conv_w, conv_b, wa, ba, wx, bx, lam, ng):
    b, l, _ = proj.shape
    w = GROUP_W
    row = lambda v: v.reshape(1, w)
    full = lambda shape: pl.BlockSpec(shape, lambda bi: (0,) * len(shape))
    return pl.pallas_call(
        _rglru_kernel,
        grid=(b,),
        in_specs=[pl.BlockSpec((1, l, w), lambda bi: (bi, 0, 3)),
                  pl.BlockSpec((1, l, w), lambda bi: (bi, 0, 4)),
                  full((conv_w.shape[0], w)), full((1, w)), full((w, 2 * w)), full((1, w)),
                  full((1, w)), full((1, w)), full((1, w))],
        out_specs=pl.BlockSpec((1, l, w), lambda bi: (bi, 0, 0)),
        out_shape=jax.ShapeDtypeStruct((b, l, w), F32),
        compiler_params=_params("parallel"),
        name="rglru",
    )(proj, proj, conv_w, row(conv_b),
      jnp.concatenate([_block_diag(wa), _block_diag(wx)], axis=1).astype(BF16), row(ba), row(bx), row(lam),
      row(ng))


def _hgrn2_kernel(q_ref, f_ref, i_ref, g_ref, lb_ref, ng_ref, o_ref,
                  qe_s, ke_s, kl_s, dl_s, o_s, inc_s):
    l = q_ref.shape[1]
    w = q_ref.shape[2]
    tile = SCAN_TILE
    c = HG_CHUNK
    r_t = _iota((tile, tile), 0)
    c_t = _iota((tile, tile), 1)
    same = _div_pow2(r_t, c) == _div_pow2(c_t, c)
    cum_m = jnp.where(jnp.logical_and(same, c_t <= r_t), 1.0, 0.0).astype(BF16)
    tot_m = jnp.where(same, 1.0, 0.0).astype(BF16)
    lb = lb_ref[...]

    def prep(t, carry):
        s = pl.multiple_of(t * tile, tile)
        forget = lb + (1.0 - lb) * _sigmoid(f_ref[0, pl.ds(s, tile), :])
        log_f = jnp.log(forget)
        key = 1.0 - forget
        parts = _split3(log_f)
        bcum = sum(_dot(cum_m, part) for part in parts)
        blast = sum(_dot(tot_m, part) for part in parts)
        q = _silu(q_ref[0, pl.ds(s, tile), :])
        qe_s[pl.ds(s, tile), :] = (q * jnp.exp(bcum)).astype(BF16)
        ke_s[pl.ds(s, tile), :] = (key * jnp.exp(-bcum)).astype(BF16)
        kl_s[pl.ds(s, tile), :] = key * jnp.exp(blast - bcum)
        dl_s[pl.ds(s, tile), :] = jnp.exp(blast)
        return carry

    lax.fori_loop(0, l // tile, prep, 0)

    blk = HG_BLOCK
    per = blk // c
    q_row = jnp.bitwise_and(_iota((N_HEADS * blk, blk), 0), blk - 1)
    k_row = _iota((N_HEADS * blk, blk), 1)
    within = jnp.logical_and(_div_pow2(q_row, c) == _div_pow2(k_row, c), k_row <= q_row)
    own = (_div_pow2(_iota((blk, per * HEAD_DIM), 0), c)
           == _div_pow2(_iota((blk, per * HEAD_DIM), 1), HEAD_DIM))
    head_of_lane = _div_pow2(_iota((blk, w), 1), HEAD_DIM)

    def local(m, carry):
        s = pl.multiple_of(m * blk, blk)
        qe = qe_s[pl.ds(s, blk), :]
        ke = ke_s[pl.ds(s, blk), :]
        kl = kl_s[pl.ds(s, blk), :]
        v = i_ref[0, pl.ds(s, blk), :]
        q_heads = jnp.concatenate([jnp.where(head_of_lane == h, qe, jnp.zeros_like(qe))
                                   for h in range(N_HEADS)], axis=0)
        sc = jnp.where(within, _dot_nt(q_heads, ke), 0.0)
        out = _dot(sc.astype(BF16), v.astype(BF16))
        o_s[pl.ds(s, blk), :] = jnp.concatenate(
            [out[h * blk:(h + 1) * blk, h * HEAD_DIM:(h + 1) * HEAD_DIM] for h in range(N_HEADS)], axis=1)
        for h in range(N_HEADS):
            sl = slice(h * HEAD_DIM, (h + 1) * HEAD_DIM)
            wide = jnp.where(own, jnp.concatenate([kl[:, sl]] * per, axis=1), 0.0)
            inc = _dot_tn(v[:, sl], wide)
            for j in range(per):
                inc_s[m * per + j, :, sl] = inc[:, j * HEAD_DIM:(j + 1) * HEAD_DIM]
        return carry

    lax.fori_loop(0, l // blk, local, 0, unroll=2)

    head_of_chunk_lane = _div_pow2(_iota((c, w), 1), HEAD_DIM)

    def step(n, state):
        s = pl.multiple_of(n * c, c)
        qe = qe_s[pl.ds(s, c), :]
        dl = dl_s[pl.ds(s, 8), :][0:1]
        q_heads = jnp.concatenate([jnp.where(head_of_chunk_lane == h, qe, jnp.zeros_like(qe))
                                   for h in range(N_HEADS)], axis=0)
        carried = _dot_nt(q_heads, state.astype(BF16))
        o_s[pl.ds(s, c), :] += jnp.concatenate([carried[h * c:(h + 1) * c] for h in range(N_HEADS)], axis=1)
        return dl * state + inc_s[n]

    lax.fori_loop(0, l // c, step, jnp.zeros((HEAD_DIM, w), F32), unroll=HG_UNROLL)

    head_avg = jnp.where(_div_pow2(_iota((w, w), 0), HEAD_DIM) == _div_pow2(_iota((w, w), 1), HEAD_DIM),
                         1.0 / HEAD_DIM, 0.0).astype(BF16)

    def finish(t, carry):
        s = pl.multiple_of(t * tile, tile)
        o = o_s[pl.ds(s, tile), :]
        ms = _dot_exact_rhs(o * o, head_avg)
        o_ref[0, pl.ds(s, tile), :] = (o * lax.rsqrt(ms + EPS) * ng_ref[...]
                                       * _silu(g_ref[0, pl.ds(s, tile), :]))
        return carry

    lax.fori_loop(0, l // tile, finish, 0)


def _hgrn2(proj, lb, ng):
    b, l, _ = proj.shape
    w = GROUP_W
    full = lambda shape: pl.BlockSpec(shape, lambda bi: (0,) * len(shape))
    col = lambda j: pl.BlockSpec((1, l, w), lambda bi: (bi, 0, j))
    return pl.pallas_call(
        _hgrn2_kernel,
        grid=(b,),
        in_specs=[col(5), col(6), col(7), col(8), full((1, w)), full((1, w))],
        out_specs=pl.BlockSpec((1, l, w), lambda bi: (bi, 0, 0)),
        out_shape=jax.ShapeDtypeStruct((b, l, w), F32),
        scratch_shapes=[pltpu.VMEM((l, w), BF16), pltpu.VMEM((l, w), BF16), pltpu.VMEM((l, w), F32),
                        pltpu.VMEM((l, w), F32), pltpu.VMEM((l, w), F32),
                        pltpu.VMEM((l // HG_CHUNK, HEAD_DIM, w), F32)],
        compiler_params=_params("parallel"),
        name="hgrn2",
    )(proj, proj, proj, proj, lb.reshape(1, w), ng.reshape(1, w))


def _ssd_kernel(z_ref, x_ref, bm_ref, cm_ref, dt_ref, cw_ref, cb_ref, dtb_ref, aneg_ref,
                dskip_ref, ng_ref, o_ref):
    l = z_ref.shape[1]
    w = GROUP_W
    q = SSD_CHUNK
    p = HEAD_DIM
    r_q = _iota((q, q), 0)
    c_q = _iota((q, q), 1)
    tril = c_q <= r_q
    cum_m = jnp.where(tril, 1.0, 0.0).astype(BF16)
    cw = cw_ref[...]
    cb = cb_ref[...]

    def chunk(n, carry):
        state, tails = carry
        s = pl.multiple_of(n * q, q)
        parts = []
        new_tails = []
        for j, ref in enumerate((x_ref, bm_ref, cm_ref)):
            raw = ref[0, pl.ds(s, q), :]
            sl = slice(j * w, (j + 1) * w)
            parts.append(_silu(_causal_conv(raw, tails[j], cw[:, sl], cb[:, sl])))
            new_tails.append(raw[q - 8:q])
        xs, bm, cm = parts
        dt = _softplus(dt_ref[0, pl.ds(s, q), :] + dtb_ref[...])
        dt_rep = jnp.concatenate([jnp.broadcast_to(dt[:, h:h + 1], (q, p)) for h in range(N_HEADS)],
                                 axis=1)
        a_cs = _dot_exact_lhs(cum_m, dt_rep * aneg_ref[...])
        a_rows = [jnp.transpose(a_cs[:, g * LANES:(g + 1) * LANES]) for g in range(w // LANES)]
        xd = xs * dt_rep
        groups = w // SSD_STATE
        heads_per_group = N_HEADS // groups
        cgs = [cm[:, g * SSD_STATE:(g + 1) * SSD_STATE].astype(BF16) for g in range(groups)]
        cbs = [_dot_nt(cgs[g], bm[:, g * SSD_STATE:(g + 1) * SSD_STATE].astype(BF16)) for g in range(groups)]
        masked = []
        for h in range(N_HEADS):
            row = (h * p) % LANES
            a_row = a_rows[(h * p) // LANES][row:row + 1, :]
            a_col = a_cs[:, h * p:h * p + 1]
            decay = jnp.exp(jnp.where(tril, a_col - a_row, -jnp.inf))
            masked.append((cbs[h // heads_per_group] * decay).astype(BF16))
        within = _dot(jnp.concatenate(masked, axis=0), xd.astype(BF16))
        y = jnp.concatenate([within[h * q:(h + 1) * q, h * p:(h + 1) * p] for h in range(N_HEADS)], axis=1)
        to_end = xd * jnp.exp(a_cs[q - 1:q, :] - a_cs)
        carried = []
        new_state = []
        for g in range(groups):
            gs = slice(g * SSD_STATE, (g + 1) * SSD_STATE)
            st = state[g]
            carried.append(_dot_nt(cgs[g], st.astype(BF16)))
            chunk_decay = jnp.exp(a_rows[g][:, q - 1:q])
            new_state.append(chunk_decay * st + _dot_tn(to_end[:, gs], bm[:, gs]))
        y = y + jnp.exp(a_cs) * jnp.concatenate(carried, axis=1) + xs * dskip_ref[...]
        y = y * _silu(z_ref[0, pl.ds(s, q), :])
        halves = []
        for gi in range(2):
            yg = y[:, gi * LANES:(gi + 1) * LANES]
            ms = jnp.mean(yg * yg, axis=-1, keepdims=True)
            halves.append(yg * lax.rsqrt(ms + EPS))
        o_ref[0, pl.ds(s, q), :] = jnp.concatenate(halves, axis=1) * ng_ref[...]
        return tuple(new_state), tuple(new_tails)

    assert SSD_STATE == LANES and (N_HEADS // (w // SSD_STATE)) * p == LANES
    zero = jnp.zeros((LANES, SSD_STATE), F32)
    tail0 = jnp.zeros((8, w), F32)
    lax.fori_loop(0, l // q, chunk, ((zero,) * (w // SSD_STATE), (tail0,) * 3), unroll=4)


def _ssd(proj, conv_w, conv_b, dt_bias, a_log, d_skip, ng):
    b, l, _ = proj.shape
    w = GROUP_W
    full = lambda shape: pl.BlockSpec(shape, lambda bi: (0,) * len(shape))
    col = lambda j: pl.BlockSpec((1, l, w), lambda bi: (bi, 0, j))
    pad_heads = lambda v: jnp.zeros((1, w), F32).at[0, :N_HEADS].set(v)
    return pl.pallas_call(
        _ssd_kernel,
        grid=(b,),
        in_specs=[col(9), col(10), col(11), col(12), col(13),
                  full((conv_w.shape[0], 3 * w)), full((1, 3 * w)), full((1, w)), full((1, w)),
                  full((1, w)), full((1, w))],
        out_specs=pl.BlockSpec((1, l, w), lambda bi: (bi, 0, 0)),
        out_shape=jax.ShapeDtypeStruct((b, l, w), F32),
        compiler_params=_params("parallel"),
        name="ssd",
    )(proj, proj, proj, proj, proj, conv_w, conv_b.reshape(1, 3 * w), pad_heads(dt_bias),
      jnp.repeat(-jnp.exp(a_log), HEAD_DIM).reshape(1, w), jnp.repeat(d_skip, HEAD_DIM).reshape(1, w),
      ng.reshape(1, w))


def _outproj_router_kernel(h_ref, oa_ref, ob_ref, oc_ref, od_ref, wo_ref, g_ref, rw_ref, rb_ref,
                           h_out, xn_out, sel_out, gate_out, rank_out, cnt_out, tile_out, cnt_s):
    i = pl.program_id(0)
    tm = h_ref.shape[0]

    @pl.when(i == 0)
    def _():
        cnt_s[...] = jnp.zeros_like(cnt_s)

    mix = jnp.concatenate([ref[...].astype(BF16) for ref in (oa_ref, ob_ref, oc_ref, od_ref)], axis=1)
    acc = h_ref[...] + _dot(mix, wo_ref[...])
    h_out[...] = acc
    ms = jnp.mean(acc * acc, axis=-1, keepdims=True)
    xn = acc * lax.rsqrt(ms + EPS) * g_ref[...]
    xn_out[...] = xn
    lane = _iota((tm, LANES), 1)
    lane_f = lane.astype(F32)
    xn_hi = xn.astype(BF16)
    xn_lo = (xn - xn_hi.astype(F32)).astype(BF16)
    both = _dot(jnp.concatenate([xn_hi, xn_lo], axis=0), rw_ref[...])
    router = both[:tm, :LANES] + both[tm:, :LANES] + both[:tm, LANES:]
    logits = jnp.where(lane < N_EXPERTS, router + rb_ref[...], -jnp.inf)
    sel = jnp.zeros((tm, LANES), F32)
    tops = []
    for k in range(TOP_K):
        m = jnp.max(logits, axis=-1, keepdims=True)
        idx = jnp.min(jnp.where(logits == m, lane_f, float(LANES)), axis=-1, keepdims=True)
        hit = lane_f == idx
        sel = jnp.where(hit, float(k + 1), sel)
        logits = jnp.where(hit, -jnp.inf, logits)
        tops.append(m)
    exps = [jnp.exp(m - tops[0]) for m in tops]
    denom = exps[0] + exps[1] + exps[2] + exps[3]
    gate = jnp.zeros((tm, LANES), F32)
    for k in range(TOP_K):
        gate = jnp.where(sel == float(k + 1), exps[k] / denom, gate)
    sel_out[...] = sel
    gate_out[...] = gate
    member = jnp.where(sel > 0.0, 1.0, 0.0)
    before = jnp.where(_iota((tm, tm), 1) < _iota((tm, tm), 0), 1.0, 0.0).astype(BF16)
    carry = cnt_s[0:1, :]
    rank_out[...] = _dot(before, member.astype(BF16))
    in_tile = jnp.ceil(jnp.sum(member, axis=0, keepdims=True) * (1.0 / RUN_ALIGN)) * RUN_ALIGN
    total = carry + in_tile
    tile_out[...] = jnp.concatenate([carry, in_tile, jnp.zeros((6, LANES), F32)], axis=0)
    cnt_s[...] = jnp.broadcast_to(total, cnt_s.shape)
    cnt_out[...] = jnp.broadcast_to(total, cnt_out.shape)


def _outproj_router(h, outs, w_out, g, router_w, router_b):
    t, d = h.shape
    tm = ROW_TILE
    rw = jnp.zeros((d, LANES), F32).at[:, :N_EXPERTS].set(router_w)
    rb = jnp.zeros((1, LANES), F32).at[0, :N_EXPERTS].set(router_b)
    rw_hi = rw.astype(BF16)
    rw_lo = (rw - rw_hi.astype(F32)).astype(BF16)
    tile = lambda width: pl.BlockSpec((tm, width), lambda i: (i, 0))
    full = lambda shape: pl.BlockSpec(shape, lambda i: (0,) * len(shape))
    flat = [o.reshape(t, GROUP_W) for o in outs]
    return pl.pallas_call(
        _outproj_router_kernel,
        grid=(t // tm,),
        in_specs=[tile(d)] + [tile(GROUP_W)] * 4 + [full((d, d)), full((1, d)), full((d, 2 * LANES)),
                                                   full((1, LANES))],
        out_specs=[tile(d), tile(d), tile(LANES), tile(LANES), tile(LANES), full((8, LANES)),
                   pl.BlockSpec((8, LANES), lambda i: (i, 0))],
        out_shape=[jax.ShapeDtypeStruct((t, d), F32), jax.ShapeDtypeStruct((t, d), F32),
                   jax.ShapeDtypeStruct((t, LANES), F32), jax.ShapeDtypeStruct((t, LANES), F32),
                   jax.ShapeDtypeStruct((t, LANES), F32), jax.ShapeDtypeStruct((8, LANES), F32),
                   jax.ShapeDtypeStruct((t // tm * 8, LANES), F32)],
        scratch_shapes=[pltpu.VMEM((8, LANES), F32)],
        compiler_params=_params("arbitrary"),
        name="outproj_router",
    )(h, *flat, w_out.astype(BF16), g.reshape(1, d), jnp.concatenate([rw_hi, rw_lo], axis=1), rb)


def _plan_kernel(cnt_ref, start_out, plan_out):
    cnt = cnt_ref[...]
    padded = jnp.ceil(cnt * (1.0 / MOE_BLOCK)) * MOE_BLOCK
    upto = jnp.where(_iota((LANES, LANES), 0) <= _iota((LANES, LANES), 1), 1.0, 0.0)
    pad_end = _dot_hi(padded, upto)
    pad_start = pad_end - padded
    start_out[...] = pad_start
    ends = jnp.transpose(jnp.broadcast_to(pad_end[0:1, :], (LANES, LANES)))
    expert_ok = _iota((LANES, LANES), 0) < N_EXPERTS
    n_active = pad_end[0:1, N_EXPERTS - 1:N_EXPERTS] * (1.0 / MOE_BLOCK)
    rows = [jnp.broadcast_to(n_active, (1, LANES)), cnt[0:1, :], pad_start[0:1, :], padded[0:1, :]]
    for r in range(PLAN_EXPERT_ROWS):
        blk_start = ((_iota((LANES, LANES), 1) + r * LANES) * MOE_BLOCK).astype(F32)
        done = jnp.where(jnp.logical_and(expert_ok, ends <= blk_start), 1.0, 0.0)
        rows.append(jnp.minimum(jnp.sum(done, axis=0, keepdims=True), N_EXPERTS - 1.0))
    plan_out[...] = jnp.concatenate(rows, axis=0).astype(I32)


def _plan(counts, n_blocks):
    assert n_blocks <= PLAN_EXPERT_ROWS * LANES
    return pl.pallas_call(
        _plan_kernel,
        out_shape=[jax.ShapeDtypeStruct((8, LANES), F32), jax.ShapeDtypeStruct((8, LANES), I32)],
        name="moe_plan",
    )(counts)


def _dest_kernel(sel_ref, gate_ref, rank_ref, tile_ref, start_ref, lpos_out, lk_out, gk_out, runs_out):
    tm = sel_ref.shape[0]
    sel = sel_ref[...]
    gate = gate_ref[...]
    before_tile = tile_ref[0:1, :]
    in_tile = tile_ref[1:2, :]
    earlier = jnp.where(_iota((LANES, LANES), 0) < _iota((LANES, LANES), 1), 1.0, 0.0)
    run_start = _dot_hi(jnp.broadcast_to(in_tile, (8, LANES)), earlier)[0:1, :]
    lpos = rank_ref[...] + run_start
    lane = _iota((tm, LANES), 1)
    pieces = []
    gk = jnp.zeros((tm, LANES), F32)
    lk = jnp.zeros((tm, LANES), F32)
    for k in range(TOP_K):
        hit = sel == float(k + 1)
        mine = jnp.where(hit, lpos, 0.0)
        pieces.append(mine)
        gk = jnp.where(lane == k, jnp.sum(jnp.where(hit, gate, 0.0), axis=-1, keepdims=True), gk)
        lk = jnp.where(lane == k, jnp.sum(mine, axis=-1, keepdims=True), lk)
    stacked = jnp.concatenate(pieces, axis=1)
    picker = jnp.where(_div_pow2(_iota((8, TOP_K * LANES), 1), LANES) == _iota((8, TOP_K * LANES), 0),
                       1.0, 0.0).astype(BF16)
    lpos_out[...] = sum(_dot_nt(picker, part) for part in _split3(stacked)).astype(I32)
    lk_out[...] = lk
    gk_out[...] = gk
    first_slot = start_ref[0:1, :] + before_tile
    total = jnp.broadcast_to(jnp.sum(in_tile, axis=-1, keepdims=True), (1, LANES))
    runs_out[...] = jnp.concatenate([in_tile, run_start, first_slot, total, jnp.zeros((4, LANES), F32)],
                                    axis=0).astype(I32)


def _dest(sel, gate, rank, tile_counts, pad_start):
    t = sel.shape[0]
    tm = ROW_TILE
    tile = pl.BlockSpec((tm, LANES), lambda i: (i, 0))
    per_tile = pl.BlockSpec((8, LANES), lambda i: (i, 0))
    return pl.pallas_call(
        _dest_kernel,
        grid=(t // tm,),
        in_specs=[tile, tile, tile, per_tile, pl.BlockSpec((8, LANES), lambda i: (0, 0))],
        out_specs=[pl.BlockSpec((8, tm), lambda i: (0, i)), tile, tile, per_tile],
        out_shape=[jax.ShapeDtypeStruct((8, t), I32), jax.ShapeDtypeStruct((t, LANES), F32),
                   jax.ShapeDtypeStruct((t, LANES), F32), jax.ShapeDtypeStruct((t // tm * 8, LANES), I32)],
        compiler_params=_params("parallel"),
        name="moe_dest",
    )(sel, gate, rank, tile_counts, pad_start)


def _copy_run(src, dst, length, n_bits, make_copy, wait):
    units = jnp.right_shift(length, RUN_ALIGN.bit_length() - 1)
    for b in range(n_bits):
        offset = jnp.left_shift(jnp.right_shift(units, b + 1), b + 1) * RUN_ALIGN

        @pl.when(jnp.bitwise_and(jnp.right_shift(units, b), 1) == 1)
        def _():
            copy = make_copy(pl.multiple_of(src + offset, RUN_ALIGN), pl.multiple_of(dst + offset, RUN_ALIGN),
                             RUN_ALIGN << b)
            if wait:
                copy.wait()
            else:
                copy.start()


def _dispatch_kernel(plan_ref, runs_ref, prev_runs_ref, lpos_ref, xn_ref, xs_hbm, sorted_s, sem):
    i = pl.program_id(0)
    tm = xn_ref.shape[0]
    n_local = sorted_s.shape[1]
    slot = i % 2

    def to_slots(buf):
        return lambda src, dst, size: pltpu.make_async_copy(
            sorted_s.at[buf, pl.ds(src, size)], xs_hbm.at[pl.ds(dst, size)], sem.at[buf])

    def start_runs(table, buf):
        def body(e, carry):
            _copy_run(table[RUN_START, e], table[RUN_SLOT, e], table[RUN_LEN, e], RUN_BITS, to_slots(buf),
                      False)
            return carry
        lax.fori_loop(0, N_EXPERTS, body, 0)

    def wait_runs(table, buf):
        _copy_run(0, 0, table[RUN_TOTAL, 0], TOTAL_BITS, lambda src, dst, size: to_slots(buf)(0, 0, size),
                  True)

    position = _iota((n_local, tm), 0)
    lpos = lpos_ref[...]
    place = jnp.zeros((n_local, tm), F32)
    for k in range(TOP_K):
        place = place + jnp.where(position == lpos[k:k + 1, :], 1.0, 0.0)
    sorted_s[slot] = _dot(place.astype(BF16), xn_ref[...].astype(BF16))
    start_runs(runs_ref, slot)

    @pl.when(i > 0)
    def _():
        wait_runs(prev_runs_ref, 1 - slot)

    @pl.when(i == pl.num_programs(0) - 1)
    def _():
        wait_runs(runs_ref, slot)

    @pl.when(i == 0)
    def _():
        def padding(wait):
            def fill(src, dst, size):
                return pltpu.make_async_copy(sorted_s.at[0, pl.ds(0, size)], xs_hbm.at[pl.ds(dst, size)],
                                             sem.at[1])

            def body(e, carry):
                cnt = plan_ref[PLAN_COUNT, e]
                _copy_run(0, plan_ref[PLAN_START, e] + cnt, plan_ref[PLAN_PADDED, e] - cnt, PAD_BITS, fill,
                          wait)
                return carry
            lax.fori_loop(0, N_EXPERTS, body, 0)

        padding(False)
        padding(True)


def _dispatch(plan, runs, lpos, xn, n_slots):
    t, d = xn.shape
    tm = ROW_TILE
    return pl.pallas_call(
        _dispatch_kernel,
        grid=(t // tm,),
        in_specs=[pl.BlockSpec(memory_space=pltpu.SMEM),
                  pl.BlockSpec((8, LANES), lambda i: (i, 0), memory_space=pltpu.SMEM),
                  pl.BlockSpec((8, LANES), lambda i: (jnp.maximum(i - 1, 0), 0), memory_space=pltpu.SMEM),
                  pl.BlockSpec((8, tm), lambda i: (0, i)),
                  pl.BlockSpec((tm, d), lambda i: (i, 0))],
        out_specs=pl.BlockSpec(memory_space=pl.ANY),
        out_shape=jax.ShapeDtypeStruct((n_slots, d), F32),
        scratch_shapes=[pltpu.VMEM((2, LOCAL_ROWS, d), F32), pltpu.SemaphoreType.DMA((2,))],
        compiler_params=_params("arbitrary"),
        name="moe_dispatch",
    )(plan, runs, runs, lpos, xn)


def _ffn_kernel(be_ref, na_ref, plan_ref, x_ref, wgu_hbm, bgu_ref, wd_hbm, bd_ref, o_ref, wgu_in_s, wd_in_s,
                wgu_s, wd32_s, wd_s, sem, *, layer):
    j = pl.program_id(0)
    n_active = na_ref[0]
    active = j < n_active
    expert = be_ref[j]
    new_expert = jnp.logical_or(j == 0, expert != be_ref[jnp.maximum(j - 1, 0)])
    rows_left = plan_ref[PLAN_START, expert] + plan_ref[PLAN_COUNT, expert] - j * MOE_BLOCK

    def fetch(e):
        return (pltpu.make_async_copy(wgu_hbm.at[layer, e], wgu_in_s, sem.at[0]),
                pltpu.make_async_copy(wd_hbm.at[layer, e], wd_in_s, sem.at[1]))

    @pl.when(j == 0)
    def _():
        for copy in fetch(expert):
            copy.start()

    @pl.when(jnp.logical_and(active, new_expert))
    def _():
        for copy in fetch(expert):
            copy.wait()
        wgu_s[...] = wgu_in_s[...].astype(BF16)
        half = LANES // 2
        for c in range(wd_s.shape[0]):
            cols = slice(c * LANES, (c + 1) * LANES)
            for k in range(wd_s.shape[1] // LANES):
                for hb in range(2):
                    src = k * LANES + hb * half
                    wd32_s[c, pl.ds(k * LANES + hb, half, stride=2), :] = wd_in_s[src:src + half, cols]
            wd_s[c] = wd32_s[c].astype(BF16)
        nxt = lax.while_loop(lambda b: jnp.logical_and(b < n_active, be_ref[jnp.minimum(b, n_active - 1)] == expert),
                             lambda b: b + 1, j + 1)

        @pl.when(nxt < n_active)
        def _():
            for copy in fetch(be_ref[jnp.minimum(nxt, n_active - 1)]):
                copy.start()

    def ffn_rows(n_rows):
        x = x_ref[0:n_rows, :].astype(BF16)
        hmid = _dot(x, wgu_s[...]) + bgu_ref[0]
        m = hmid.shape[0]
        even = (_iota((m, LANES), 1) & 1) == 0
        acts = []
        for k in range(hmid.shape[1] // (2 * LANES)):
            h0 = hmid[:, (2 * k) * LANES:(2 * k + 1) * LANES]
            h1 = hmid[:, (2 * k + 1) * LANES:(2 * k + 2) * LANES]
            glu = jnp.where(even, h0, pltpu.roll(h1, 1, 1))
            lin = jnp.where(even, pltpu.roll(h0, LANES - 1, 1), h1)
            glu = jnp.minimum(glu, SWIGLU_LIMIT)
            lin = jnp.clip(lin, -SWIGLU_LIMIT, SWIGLU_LIMIT)
            acts.append((glu * _sigmoid(SWIGLU_ALPHA * glu) * (lin + 1.0)).astype(BF16))
        act = jnp.concatenate(acts, axis=1)
        wd = jnp.concatenate([wd_s[c] for c in range(wd_s.shape[0])], axis=1)
        o_ref[0:n_rows, :] = _dot(act, wd) + bd_ref[0]

    quarter = MOE_BLOCK // FFN_SPLIT
    for part in range(1, FFN_SPLIT + 1):
        fits = rows_left > (part - 1) * quarter
        if part < FFN_SPLIT:
            fits = jnp.logical_and(fits, rows_left <= part * quarter)
        pl.when(jnp.logical_and(active, fits))(functools.partial(ffn_rows, part * quarter))


def _ffn(layer, block_expert, n_active, plan, xs, wgu, bgu, wd, bd):
    n_slots, d = xs.shape
    dff2 = wgu.shape[3]
    nb = n_slots // MOE_BLOCK
    rows = lambda j, be, na, plan_: (jnp.maximum(jnp.minimum(j, na[0] - 1), 0), 0)
    per_expert = lambda j, be, na, plan_: (layer, be[j], 0, 0)
    grid_spec = pltpu.PrefetchScalarGridSpec(
        num_scalar_prefetch=3,
        grid=(nb,),
        in_specs=[pl.BlockSpec((MOE_BLOCK, d), rows),
                  pl.BlockSpec(memory_space=pl.ANY), pl.BlockSpec((None, 1, 1, dff2), per_expert),
                  pl.BlockSpec(memory_space=pl.ANY), pl.BlockSpec((None, 1, 1, d), per_expert)],
        out_specs=pl.BlockSpec((MOE_BLOCK, d), rows),
        scratch_shapes=[pltpu.VMEM((d, dff2), F32), pltpu.VMEM((dff2 // 2, d), F32),
                        pltpu.VMEM((d, dff2), BF16), pltpu.VMEM((d // LANES, dff2 // 2, LANES), F32),
                        pltpu.VMEM((d // LANES, dff2 // 2, LANES), BF16), pltpu.SemaphoreType.DMA((2,))],
    )
    return pl.pallas_call(
        functools.partial(_ffn_kernel, layer=layer),
        grid_spec=grid_spec,
        out_shape=jax.ShapeDtypeStruct((n_slots, d), F32),
        compiler_params=_params("arbitrary"),
        name="moe_ffn",
    )(block_expert, n_active, plan, xs, wgu, bgu, wd, bd)


def _combine_kernel(runs_ref, next_runs_ref, lk_ref, gk_ref, h_ref, g_ref, ys_hbm, o_ref, sorted_s, sem, *,
                    final_norm):
    i = pl.program_id(0)
    tm = h_ref.shape[0]
    n_local = sorted_s.shape[1]
    slot = i % 2

    def from_slots(buf):
        return lambda src, dst, size: pltpu.make_async_copy(
            ys_hbm.at[pl.ds(src, size)], sorted_s.at[buf, pl.ds(dst, size)], sem.at[buf])

    def start_runs(table, buf):
        def body(e, carry):
            _copy_run(table[RUN_SLOT, e], table[RUN_START, e], table[RUN_LEN, e], RUN_BITS, from_slots(buf),
                      False)
            return carry
        lax.fori_loop(0, N_EXPERTS, body, 0)

    @pl.when(i == 0)
    def _():
        sorted_s[...] = jnp.zeros_like(sorted_s)
        start_runs(runs_ref, slot)

    @pl.when(i + 1 < pl.num_programs(0))
    def _():
        start_runs(next_runs_ref, 1 - slot)

    _copy_run(0, 0, runs_ref[RUN_TOTAL, 0], TOTAL_BITS, lambda src, dst, size: from_slots(slot)(0, 0, size),
              True)
    position = _iota((tm, n_local), 1).astype(F32)
    lk = lk_ref[...]
    gk = gk_ref[...]
    weights = jnp.zeros((tm, n_local), F32)
    for k in range(TOP_K):
        weights = weights + jnp.where(position == lk[:, k:k + 1], gk[:, k:k + 1], 0.0)
    acc = h_ref[...] + _dot(weights.astype(BF16), sorted_s[slot].astype(BF16))
    if final_norm:
        ms = jnp.mean(acc * acc, axis=-1, keepdims=True)
        acc = acc * lax.rsqrt(ms + EPS) * g_ref[...]
    o_ref[...] = acc


def _combine(runs, lk, gk, h, ys, g, final_norm):
    t, d = h.shape
    tm = ROW_TILE
    return pl.pallas_call(
        functools.partial(_combine_kernel, final_norm=final_norm),
        grid=(t // tm,),
        in_specs=[pl.BlockSpec((8, LANES), lambda i: (i, 0), memory_space=pltpu.SMEM),
                  pl.BlockSpec((8, LANES), lambda i: (jnp.minimum(i + 1, t // tm - 1), 0),
                               memory_space=pltpu.SMEM),
                  pl.BlockSpec((tm, LANES), lambda i: (i, 0)),
                  pl.BlockSpec((tm, LANES), lambda i: (i, 0)),
                  pl.BlockSpec((tm, d), lambda i: (i, 0)),
                  pl.BlockSpec((1, d), lambda i: (0, 0)),
                  pl.BlockSpec(memory_space=pl.ANY)],
        out_specs=pl.BlockSpec((tm, d), lambda i: (i, 0)),
        out_shape=jax.ShapeDtypeStruct((t, d), F32),
        scratch_shapes=[pltpu.VMEM((2, LOCAL_ROWS, d), F32), pltpu.SemaphoreType.DMA((2,))],
        compiler_params=_params("arbitrary"),
        name="moe_combine",
    )(runs, runs, lk, gk, h, g.reshape(1, d), ys)


def _moe(h, xn, sel, gate, rank, counts, tile_counts, layer, w_gu, b_gu, w_down, b_down, g_final,
         final_norm):
    t, d = h.shape
    n_slots = t * TOP_K + (t // ROW_TILE) * N_EXPERTS * RUN_ALIGN + N_EXPERTS * MOE_BLOCK
    n_blocks = n_slots // MOE_BLOCK
    pad_start, plan = _plan(counts, n_blocks)
    lpos, lk, gk, runs = _dest(sel, gate, rank, tile_counts, pad_start)
    xs = _dispatch(plan, runs, lpos, xn, n_slots)
    block_expert = plan[PLAN_EXPERTS:].reshape(-1)[:n_blocks]
    n_active = plan[PLAN_ACTIVE, :1]
    depth = w_gu.shape[0]
    ys = _ffn(layer, block_expert, n_active, plan, xs, w_gu, b_gu.reshape(depth, N_EXPERTS, 1, -1), w_down,
              b_down.reshape(depth, N_EXPERTS, 1, d))
    return _combine(runs, lk, gk, h, ys, g_final, final_norm)


def kernel(x, norm_mix_g, w_in, sb_norm_g, rg_conv_w, rg_conv_b, rg_wa, rg_ba, rg_wx, rg_bx,
           rg_lambda, rg_norm_g, hg_lower_bounds, hg_norm_g, m2_conv_w, m2_conv_b, m2_dt_bias,
           m2_a_log, m2_d, m2_norm_g, w_out, norm_ffn_g, router_w, router_b, moe_w_gu, moe_b_gu,
           moe_w_down, moe_b_down, final_norm_g):
    b, l, d = x.shape
    depth = w_in.shape[0]
    t = b * l
    lbs = jnp.cumsum(jax.nn.softmax(hg_lower_bounds.astype(F32), axis=0), axis=0)
    lbs = lbs - lbs[0]
    d_in = w_in.shape[2]
    w_in_p = jnp.zeros((depth, d, PROJ_BLOCKS * GROUP_W), BF16).at[:, :, :d_in].set(w_in.astype(BF16))
    h = x.reshape(t, d)
    for layer in range(depth):
        proj = _norm_inproj(h, norm_mix_g[layer], w_in_p[layer]).reshape(b, l, PROJ_BLOCKS * GROUP_W)
        o_a = _sb_attn(proj, sb_norm_g[layer])
        o_b = _rglru(proj, rg_conv_w[layer], rg_conv_b[layer], rg_wa[layer], rg_ba[layer],
                     rg_wx[layer], rg_bx[layer], rg_lambda[layer], rg_norm_g[layer])
        o_c = _hgrn2(proj, lbs[layer], hg_norm_g[layer])
        o_d = _ssd(proj, m2_conv_w[layer], m2_conv_b[layer], m2_dt_bias[layer], m2_a_log[layer],
                   m2_d[layer], m2_norm_g[layer])
        h, xn, sel, gate, rank, counts, tile_counts = _outproj_router(
            h, (o_a, o_b, o_c, o_d), w_out[layer], norm_ffn_g[layer], router_w[layer],
            router_b[layer])
        h = _moe(h, xn, sel, gate, rank, counts, tile_counts, layer, moe_w_gu, moe_b_gu, moe_w_down,
                 moe_b_down, final_norm_g, layer == depth - 1)
    return h.reshape(b, l, d)
```

```python
import functools
import math

import jax
import jax.numpy as jnp
from jax import lax
from jax.experimental import pallas as pl
from jax.experimental.pallas import tpu as pltpu

F32 = jnp.float32
BF16 = jnp.bfloat16
I32 = jnp.int32
HIGHEST = lax.Precision.HIGHEST
EPS = 1e-6
EXP_UNDERFLOW = -104.0

LANES = 128
GROUP_W = 256
HEAD_DIM = 64
N_HEADS = 4
SSD_STATE = 128
RG_C = 8.0
N_EXPERTS = 32
TOP_K = 4
SWIGLU_LIMIT = 7.0
SWIGLU_ALPHA = 1.702
PROJ_BLOCKS = 14
VMEM_LIMIT = 56 * 1024 * 1024

ROW_TILE = 256
INPROJ_TILE = 512
ATTN_BLOCK = 256
SCAN_TILE = 256
HG_CHUNK = 16
HG_BLOCK = 256
SSD_CHUNK = LANES
MOE_BLOCK = 512
HG_UNROLL = 32

PLAN_ACTIVE, PLAN_COUNT, PLAN_START, PLAN_PADDED, PLAN_EXPERTS = 0, 1, 2, 3, 4
PLAN_EXPERT_ROWS = 8 - PLAN_EXPERTS
RUN_LEN, RUN_START, RUN_SLOT, RUN_TOTAL = 0, 1, 2, 3
RUN_ALIGN = 8
RUN_BITS = (ROW_TILE // RUN_ALIGN).bit_length()
PAD_BITS = (MOE_BLOCK // RUN_ALIGN - 1).bit_length()
LOCAL_ROWS = ROW_TILE * TOP_K + N_EXPERTS * RUN_ALIGN
TOTAL_BITS = (LOCAL_ROWS // RUN_ALIGN).bit_length()


def _params(*sem):
    return pltpu.CompilerParams(dimension_semantics=sem, vmem_limit_bytes=VMEM_LIMIT)


def _dot(a, b):
    return jnp.dot(a, b, preferred_element_type=F32)


def _dot_hi(a, b):
    return jnp.dot(a, b, preferred_element_type=F32, precision=HIGHEST)


def _split3(x):
    hi = x.astype(BF16)
    rest = x - hi.astype(F32)
    mid = rest.astype(BF16)
    lo = (rest - mid.astype(F32)).astype(BF16)
    return hi, mid, lo


def _dot_exact_lhs(m, x):
    return sum(_dot(m, part) for part in _split3(x))


def _dot_exact_rhs(x, m):
    return sum(_dot(part, m) for part in _split3(x))


def _dot_nt(a, b, precision=None):
    return lax.dot_general(a, b, (((1,), (1,)), ((), ())), preferred_element_type=F32,
                           precision=precision)


def _dot_tn(a, b):
    return lax.dot_general(a, b, (((0,), (0,)), ((), ())), preferred_element_type=F32)


def _sigmoid(x):
    return 1.0 / (1.0 + jnp.exp(-x))


def _silu(x):
    return x * _sigmoid(x)


def _softplus(x):
    return jnp.maximum(x, 0.0) + jnp.log(1.0 + jnp.exp(-jnp.abs(x)))


def _iota(shape, dim):
    return lax.broadcasted_iota(I32, shape, dim)


def _div_pow2(x, n):
    assert n & (n - 1) == 0
    return jnp.right_shift(x, int(math.log2(n)))


def _shift_rows(x, d, prev8):
    r = pltpu.roll(x, d, 0)
    p = pltpu.roll(prev8, d, 0)
    head = jnp.where(_iota(prev8.shape, 0) < d, p, r[:8])
    return jnp.concatenate([head, r[8:]], axis=0)


def _causal_conv(x, prev8, w, b):
    k = w.shape[0]
    y = x * w[k - 1:k] + b
    for d in range(1, k):
        y = y + _shift_rows(x, d, prev8) * w[k - 1 - d:k - d]
    return y


def _norm_inproj_kernel(x_ref, g_ref, w_ref, o_ref):
    x = x_ref[...]
    ms = jnp.mean(x * x, axis=-1, keepdims=True)
    xn = (x * lax.rsqrt(ms + EPS) * g_ref[...]).astype(BF16)
    o_ref[...] = _dot(xn, w_ref[...])


def _norm_inproj(h, g, w):
    t, d = h.shape
    n = w.shape[1]
    return pl.pallas_call(
        _norm_inproj_kernel,
        grid=(t // INPROJ_TILE,),
        in_specs=[pl.BlockSpec((INPROJ_TILE, d), lambda i: (i, 0)),
                  pl.BlockSpec((1, d), lambda i: (0, 0)),
                  pl.BlockSpec((d, n), lambda i: (0, 0))],
        out_specs=pl.BlockSpec((INPROJ_TILE, n), lambda i: (i, 0)),
        out_shape=jax.ShapeDtypeStruct((t, n), F32),
        compiler_params=_params("parallel"),
        name="norm_inproj",
    )(h, g.reshape(1, d), w)


def _sb_attn_kernel(q_ref, k_ref, v_ref, g_ref, o_ref, acc_ref, run_ref):
    i = pl.program_id(1)
    blk = ATTN_BLOCK
    scale = HEAD_DIM ** -0.5
    rows = N_HEADS * blk
    below = _iota((rows, blk), 1) < jnp.bitwise_and(_iota((rows, blk), 0), blk - 1)
    r_w = _iota((blk, blk + LANES), 0)
    c_w = _iota((blk, blk + LANES), 1)
    later_sum = jnp.where(jnp.logical_or(c_w >= blk, r_w > c_w), 1.0, 0.0).astype(BF16)
    acc_ref[...] = jnp.zeros_like(acc_ref)
    run_ref[...] = jnp.zeros_like(run_ref)
    q_all = q_ref[0] * scale
    head_of_lane = _div_pow2(_iota((blk, GROUP_W), 1), HEAD_DIM)
    q_heads = jnp.concatenate([jnp.where(head_of_lane == h, q_all, 0.0) for h in range(N_HEADS)],
                              axis=0).astype(BF16)

    def key_block(j, diagonal):
        ks = pl.multiple_of(j * blk, blk)
        z = _dot_nt(q_heads, k_ref[0, pl.ds(ks, blk), :].astype(BF16))
        ls = jnp.minimum(z, 0.0) - jnp.log(1.0 + jnp.exp(-jnp.abs(z)))
        lk = ls - z
        if diagonal:
            lk = jnp.where(below, lk, 0.0)
        lk_hi = lk.astype(BF16)
        lk_lo = (lk - lk_hi.astype(F32)).astype(BF16)
        stacked = _dot(jnp.concatenate([lk_hi, lk_lo], axis=0), later_sum)
        sums = stacked[:rows] + stacked[rows:]
        run = run_ref[...]
        rem = sums[:, :blk] + jnp.concatenate([run] * (blk // LANES), axis=1)
        a = jnp.exp(ls + rem)
        if diagonal:
            a = jnp.where(below, a, 0.0)
        out = _dot(a.astype(BF16), v_ref[0, pl.ds(ks, blk), :].astype(BF16))
        for h in range(N_HEADS):
            sl = slice(h * HEAD_DIM, (h + 1) * HEAD_DIM)
            acc_ref[:, sl] += out[h * blk:(h + 1) * blk, sl]
        run_ref[...] = run + sums[:, blk:]

    key_block(i, True)

    def live():
        return jnp.max(run_ref[...]) > EXP_UNDERFLOW

    def more(carry):
        jj, alive = carry
        return jnp.logical_and(jj <= i, alive)

    def body(carry):
        jj, _ = carry
        key_block(i - jj, False)
        return jj + 1, live()

    lax.while_loop(more, body, (jnp.int32(1), live()))
    o = acc_ref[...]
    ms = jnp.mean(o * o, axis=-1, keepdims=True)
    o_ref[0] = o * lax.rsqrt(ms + EPS) * g_ref[...]


def _sb_attn(proj, g):
    b, l, _ = proj.shape
    blk = ATTN_BLOCK
    return pl.pallas_call(
        _sb_attn_kernel,
        grid=(b, l // blk),
        in_specs=[pl.BlockSpec((1, blk, GROUP_W), lambda bi, i: (bi, i, 0)),
                  pl.BlockSpec((1, l, GROUP_W), lambda bi, i: (bi, 0, 1)),
                  pl.BlockSpec((1, l, GROUP_W), lambda bi, i: (bi, 0, 2)),
                  pl.BlockSpec((1, GROUP_W), lambda bi, i: (0, 0))],
        out_specs=pl.BlockSpec((1, blk, GROUP_W), lambda bi, i: (bi, i, 0)),
        out_shape=jax.ShapeDtypeStruct((b, l, GROUP_W), F32),
        scratch_shapes=[pltpu.VMEM((blk, GROUP_W), F32), pltpu.VMEM((N_HEADS * blk, LANES), F32)],
        compiler_params=_params("parallel", "arbitrary"),
        name="sb_attn",
    )(proj, proj, proj, g.reshape(1, GROUP_W))


def _rglru_kernel(x_ref, gate_ref, cw_ref, cb_ref, wax_ref, ba_ref, bx_ref, lam_ref,
                  ng_ref, o_ref):
    l = x_ref.shape[1]
    w = x_ref.shape[2]
    tc = SCAN_TILE
    rows = _iota((tc, w), 0)
    cw = cw_ref[...]
    cb = cb_ref[...]
    neg_c_sp = -RG_C * _softplus(-lam_ref[...])

    def chunk(c, carry):
        h_prev, tail = carry
        s = pl.multiple_of(c * tc, tc)
        x = x_ref[0, pl.ds(s, tc), :]
        xc = _causal_conv(x, tail, cw, cb)
        xcb = xc.astype(BF16)
        gates = _dot(xcb, wax_ref[...])
        r = _sigmoid(gates[:, :w] + ba_ref[...])
        ig = _sigmoid(gates[:, w:] + bx_ref[...])
        log_a = r * neg_c_sp
        a = jnp.exp(log_a)
        th = jnp.tanh(log_a)
        mult = jnp.sqrt(-2.0 * th / (1.0 - th))
        mult = jnp.where(rows + s == 0, 1.0, mult)
        u = mult * ig * xc
        d = 1
        while d < tc:
            keep = rows >= d
            a_s = jnp.where(keep, pltpu.roll(a, d, 0), 1.0)
            u_s = jnp.where(keep, pltpu.roll(u, d, 0), 0.0)
            u = a * u_s + u
            a = a * a_s
            d *= 2
        hs = u + a * h_prev
        gate = gate_ref[0, pl.ds(s, tc), :]
        gelu = 0.5 * gate * (1.0 + jnp.tanh(math.sqrt(2.0 / math.pi) * (gate + 0.044715 * gate * gate * gate)))
        o = hs * gelu
        ms = jnp.mean(o * o, axis=-1, keepdims=True)
        o_ref[0, pl.ds(s, tc), :] = o * lax.rsqrt(ms + EPS) * ng_ref[...]
        return hs[tc - 1:tc], x[tc - 8:tc]

    lax.fori_loop(0, l // tc, chunk, (jnp.zeros((1, w), F32), jnp.zeros((8, w), F32)))


def _block_diag(wb):
    n, d, _ = wb.shape
    same = jnp.eye(n, dtype=bool)[:, None, :, None]
    return jnp.where(same, wb[:, :, None, :], 0).reshape(n * d, n * d)


def _rglru(proj, conv_w, conv_b, wa, ba, wx, bx, lam, ng):
    b, l, _ = proj.shape
    w = GROUP_W
    row = lambda v: v.reshape(1, w)
    full = lambda shape: pl.BlockSpec(shape, lambda bi: (0,) * len(shape))
    return pl.pallas_call(
        _rglru_kernel,
        grid=(b,),
        in_specs=[pl.BlockSpec((1, l, w), lambda bi: (bi, 0, 3)),
                  pl.BlockSpec((1, l, w), lambda bi: (bi, 0, 4)),
                  full((conv_w.shape[0], w)), full((1, w)), full((w, 2 * w)), full((1, w)),
                  full((1, w)), full((1, w)), full((1, w))],
        out_specs=pl.BlockSpec((1, l, w), lambda bi: (bi, 0, 0)),
        out_shape=jax.ShapeDtypeStruct((b, l, w), F32),
        compiler_params=_params("parallel"),
        name="rglru",
    )(proj, proj, conv_w, row(conv_b),
      jnp.concatenate([_block_diag(wa), _block_diag(wx)], axis=1).astype(BF16), row(ba), row(bx), row(lam),
      row(ng))


def _hgrn2_kernel(q_ref, f_ref, i_ref, g_ref, lb_ref, ng_ref, o_ref,
                  qe_s, ke_s, kl_s, dl_s, o_s, inc_s):
    l = q_ref.shape[1]
    w = q_ref.shape[2]
    tile = SCAN_TILE
    c = HG_CHUNK
    r_t = _iota((tile, tile), 0)
    c_t = _iota((tile, tile), 1)
    same = _div_pow2(r_t, c) == _div_pow2(c_t, c)
    cum_m = jnp.where(jnp.logical_and(same, c_t <= r_t), 1.0, 0.0).astype(BF16)
    tot_m = jnp.where(same, 1.0, 0.0).astype(BF16)
    lb = lb_ref[...]

    def prep(t, carry):
        s = pl.multiple_of(t * tile, tile)
        forget = lb + (1.0 - lb) * _sigmoid(f_ref[0, pl.ds(s, tile), :])
        log_f = jnp.log(forget)
        key = 1.0 - forget
        parts = _split3(log_f)
        bcum = sum(_dot(cum_m, part) for part in parts)
        blast = sum(_dot(tot_m, part) for part in parts)
        q = _silu(q_ref[0, pl.ds(s, tile), :])
        qe_s[pl.ds(s, tile), :] = (q * jnp.exp(bcum)).astype(BF16)
        ke_s[pl.ds(s, tile), :] = (key * jnp.exp(-bcum)).astype(BF16)
        kl_s[pl.ds(s, tile), :] = key * jnp.exp(blast - bcum)
        dl_s[pl.ds(s, tile), :] = jnp.exp(blast)
        return carry

    lax.fori_loop(0, l // tile, prep, 0, unroll=2)

    blk = HG_BLOCK
    per = blk // c
    q_row = jnp.bitwise_and(_iota((N_HEADS * blk, blk), 0), blk - 1)
    k_row = _iota((N_HEADS * blk, blk), 1)
    within = jnp.logical_and(_div_pow2(q_row, c) == _div_pow2(k_row, c), k_row <= q_row)
    own = (_div_pow2(_iota((blk, per * HEAD_DIM), 0), c)
           == _div_pow2(_iota((blk, per * HEAD_DIM), 1), HEAD_DIM))
    head_of_lane = _div_pow2(_iota((blk, w), 1), HEAD_DIM)

    def local(m, carry):
        s = pl.multiple_of(m * blk, blk)
        qe = qe_s[pl.ds(s, blk), :]
        ke = ke_s[pl.ds(s, blk), :]
        kl = kl_s[pl.ds(s, blk), :]
        v = i_ref[0, pl.ds(s, blk), :]
        q_heads = jnp.concatenate([jnp.where(head_of_lane == h, qe, jnp.zeros_like(qe))
                                   for h in range(N_HEADS)], axis=0)
        sc = jnp.where(within, _dot_nt(q_heads, ke), 0.0)
        out = _dot(sc.astype(BF16), v.astype(BF16))
        o_s[pl.ds(s, blk), :] = jnp.concatenate(
            [out[h * blk:(h + 1) * blk, h * HEAD_DIM:(h + 1) * HEAD_DIM] for h in range(N_HEADS)], axis=1)
        for h in range(N_HEADS):
            sl = slice(h * HEAD_DIM, (h + 1) * HEAD_DIM)
            wide = jnp.where(own, jnp.concatenate([kl[:, sl]] * per, axis=1), 0.0)
            inc = _dot_tn(v[:, sl], wide)
            for j in range(per):
                inc_s[m * per + j, :, sl] = inc[:, j * HEAD_DIM:(j + 1) * HEAD_DIM]
        return carry

    lax.fori_loop(0, l // blk, local, 0, unroll=2)

    head_of_chunk_lane = _div_pow2(_iota((c, w), 1), HEAD_DIM)

    def step(n, state):
        s = pl.multiple_of(n * c, c)
        qe = qe_s[pl.ds(s, c), :]
        dl = dl_s[pl.ds(s, 8), :][0:1]
        q_heads = jnp.concatenate([jnp.where(head_of_chunk_lane == h, qe, jnp.zeros_like(qe))
                                   for h in range(N_HEADS)], axis=0)
        carried = _dot_nt(q_heads, state.astype(BF16))
        o_s[pl.ds(s, c), :] += jnp.concatenate([carried[h * c:(h + 1) * c] for h in range(N_HEADS)], axis=1)
        return dl * state + inc_s[n]

    lax.fori_loop(0, l // c, step, jnp.zeros((HEAD_DIM, w), F32), unroll=HG_UNROLL)

    head_avg = jnp.where(_div_pow2(_iota((w, w), 0), HEAD_DIM) == _div_pow2(_iota((w, w), 1), HEAD_DIM),
                         1.0 / HEAD_DIM, 0.0).astype(BF16)

    def finish(t, carry):
        s = pl.multiple_of(t * tile, tile)
        o = o_s[pl.ds(s, tile), :]
        ms = _dot_exact_rhs(o * o, head_avg)
        o_ref[0, pl.ds(s, tile), :] = (o * lax.rsqrt(ms + EPS) * ng_ref[...]
                                       * _silu(g_ref[0, pl.ds(s, tile), :]))
        return carry

    lax.fori_loop(0, l // tile, finish, 0, unroll=2)


def _hgrn2(proj, lb, ng):
    b, l, _ = proj.shape
    w = GROUP_W
    full = lambda shape: pl.BlockSpec(shape, lambda bi: (0,) * len(shape))
    col = lambda j: pl.BlockSpec((1, l, w), lambda bi: (bi, 0, j))
    return pl.pallas_call(
        _hgrn2_kernel,
        grid=(b,),
        in_specs=[col(5), col(6), col(7), col(8), full((1, w)), full((1, w))],
        out_specs=pl.BlockSpec((1, l, w), lambda bi: (bi, 0, 0)),
        out_shape=jax.ShapeDtypeStruct((b, l, w), F32),
        scratch_shapes=[pltpu.VMEM((l, w), BF16), pltpu.VMEM((l, w), BF16), pltpu.VMEM((l, w), F32),
                        pltpu.VMEM((l, w), F32), pltpu.VMEM((l, w), F32),
                        pltpu.VMEM((l // HG_CHUNK, HEAD_DIM, w), F32)],
        compiler_params=_params("parallel"),
        name="hgrn2",
    )(proj, proj, proj, proj, lb.reshape(1, w), ng.reshape(1, w))


def _ssd_kernel(z_ref, x_ref, bm_ref, cm_ref, dt_ref, cw_ref, cb_ref, dtb_ref, aneg_ref,
                dskip_ref, ng_ref, o_ref):
    l = z_ref.shape[1]
    w = GROUP_W
    q = SSD_CHUNK
    p = HEAD_DIM
    r_q = _iota((q, q), 0)
    c_q = _iota((q, q), 1)
    tril = c_q <= r_q
    cum_m = jnp.where(tril, 1.0, 0.0).astype(BF16)
    cw = cw_ref[...]
    cb = cb_ref[...]

    def chunk(n, carry):
        state, tails = carry
        s = pl.multiple_of(n * q, q)
        parts = []
        new_tails = []
        for j, ref in enumerate((x_ref, bm_ref, cm_ref)):
            raw = ref[0, pl.ds(s, q), :]
            sl = slice(j * w, (j + 1) * w)
            parts.append(_silu(_causal_conv(raw, tails[j], cw[:, sl], cb[:, sl])))
            new_tails.append(raw[q - 8:q])
        xs, bm, cm = parts
        dt = _softplus(dt_ref[0, pl.ds(s, q), :] + dtb_ref[...])
        dt_rep = jnp.concatenate([jnp.broadcast_to(dt[:, h:h + 1], (q, p)) for h in range(N_HEADS)],
                                 axis=1)
        a_cs = _dot_exact_lhs(cum_m, dt_rep * aneg_ref[...])
        a_rows = [jnp.transpose(a_cs[:, g * LANES:(g + 1) * LANES]) for g in range(w // LANES)]
        xd = xs * dt_rep
        groups = w // SSD_STATE
        heads_per_group = N_HEADS // groups
        cgs = [cm[:, g * SSD_STATE:(g + 1) * SSD_STATE].astype(BF16) for g in range(groups)]
        cbs = [_dot_nt(cgs[g], bm[:, g * SSD_STATE:(g + 1) * SSD_STATE].astype(BF16)) for g in range(groups)]
        masked = []
        for h in range(N_HEADS):
            row = (h * p) % LANES
            a_row = a_rows[(h * p) // LANES][row:row + 1, :]
            a_col = a_cs[:, h * p:h * p + 1]
            decay = jnp.exp(jnp.where(tril, a_col - a_row, -jnp.inf))
            masked.append((cbs[h // heads_per_group] * decay).astype(BF16))
        within = _dot(jnp.concatenate(masked, axis=0), xd.astype(BF16))
        y = jnp.concatenate([within[h * q:(h + 1) * q, h * p:(h + 1) * p] for h in range(N_HEADS)], axis=1)
        to_end = xd * jnp.exp(a_cs[q - 1:q, :] - a_cs)
        carried = []
        new_state = []
        for g in range(groups):
            gs = slice(g * SSD_STATE, (g + 1) * SSD_STATE)
            st = state[g]
            carried.append(_dot_nt(cgs[g], st.astype(BF16)))
            chunk_decay = jnp.exp(a_rows[g][:, q - 1:q])
            new_state.append(chunk_decay * st + _dot_tn(to_end[:, gs], bm[:, gs]))
        y = y + jnp.exp(a_cs) * jnp.concatenate(carried, axis=1) + xs * dskip_ref[...]
        y = y * _silu(z_ref[0, pl.ds(s, q), :])
        halves = []
        for gi in range(2):
            yg = y[:, gi * LANES:(gi + 1) * LANES]
            ms = jnp.mean(yg * yg, axis=-1, keepdims=True)
            halves.append(yg * lax.rsqrt(ms + EPS))
        o_ref[0, pl.ds(s, q), :] = jnp.concatenate(halves, axis=1) * ng_ref[...]
        return tuple(new_state), tuple(new_tails)

    assert SSD_STATE == LANES and (N_HEADS // (w // SSD_STATE)) * p == LANES
    zero = jnp.zeros((LANES, SSD_STATE), F32)
    tail0 = jnp.zeros((8, w), F32)
    lax.fori_loop(0, l // q, chunk, ((zero,) * (w // SSD_STATE), (tail0,) * 3), unroll=4)


def _ssd(proj, conv_w, conv_b, dt_bias, a_log, d_skip, ng):
    b, l, _ = proj.shape
    w = GROUP_W
    full = lambda shape: pl.BlockSpec(shape, lambda bi: (0,) * len(shape))
    col = lambda j: pl.BlockSpec((1, l, w), lambda bi: (bi, 0, j))
    pad_heads = lambda v: jnp.zeros((1, w), F32).at[0, :N_HEADS].set(v)
    return pl.pallas_call(
        _ssd_kernel,
        grid=(b,),
        in_specs=[col(9), col(10), col(11), col(12), col(13),
                  full((conv_w.shape[0], 3 * w)), full((1, 3 * w)), full((1, w)), full((1, w)),
                  full((1, w)), full((1, w))],
        out_specs=pl.BlockSpec((1, l, w), lambda bi: (bi, 0, 0)),
        out_shape=jax.ShapeDtypeStruct((b, l, w), F32),
        compiler_params=_params("parallel"),
        name="ssd",
    )(proj, proj, proj, proj, proj, conv_w, conv_b.reshape(1, 3 * w), pad_heads(dt_bias),
      jnp.repeat(-jnp.exp(a_log), HEAD_DIM).reshape(1, w), jnp.repeat(d_skip, HEAD_DIM).reshape(1, w),
      ng.reshape(1, w))


def _outproj_router_kernel(h_ref, oa_ref, ob_ref, oc_ref, od_ref, wo_ref, g_ref, rw_ref, rb_ref,
                           h_out, xn_out, sel_out, gate_out, rank_out, cnt_out, tile_out, cnt_s):
    i = pl.program_id(0)
    tm = h_ref.shape[0]

    @pl.when(i == 0)
    def _():
        cnt_s[...] = jnp.zeros_like(cnt_s)

    mix = jnp.concatenate([ref[...].astype(BF16) for ref in (oa_ref, ob_ref, oc_ref, od_ref)], axis=1)
    acc = h_ref[...] + _dot(mix, wo_ref[...])
    h_out[...] = acc
    ms = jnp.mean(acc * acc, axis=-1, keepdims=True)
    xn = acc * lax.rsqrt(ms + EPS) * g_ref[...]
    xn_out[...] = xn
    lane = _iota((tm, LANES), 1)
    lane_f = lane.astype(F32)
    xn_hi = xn.astype(BF16)
    xn_lo = (xn - xn_hi.astype(F32)).astype(BF16)
    both = _dot(jnp.concatenate([xn_hi, xn_lo], axis=0), rw_ref[...])
    router = both[:tm, :LANES] + both[tm:, :LANES] + both[:tm, LANES:]
    logits = jnp.where(lane < N_EXPERTS, router + rb_ref[...], -jnp.inf)
    sel = jnp.zeros((tm, LANES), F32)
    tops = []
    for k in range(TOP_K):
        m = jnp.max(logits, axis=-1, keepdims=True)
        idx = jnp.min(jnp.where(logits == m, lane_f, float(LANES)), axis=-1, keepdims=True)
        hit = lane_f == idx
        sel = jnp.where(hit, float(k + 1), sel)
        logits = jnp.where(hit, -jnp.inf, logits)
        tops.append(m)
    exps = [jnp.exp(m - tops[0]) for m in tops]
    denom = exps[0] + exps[1] + exps[2] + exps[3]
    gate = jnp.zeros((tm, LANES), F32)
    for k in range(TOP_K):
        gate = jnp.where(sel == float(k + 1), exps[k] / denom, gate)
    sel_out[...] = sel
    gate_out[...] = gate
    member = jnp.where(sel > 0.0, 1.0, 0.0)
    before = jnp.where(_iota((tm, tm), 1) < _iota((tm, tm), 0), 1.0, 0.0).astype(BF16)
    carry = cnt_s[0:1, :]
    rank_out[...] = _dot(before, member.astype(BF16))
    in_tile = jnp.ceil(jnp.sum(member, axis=0, keepdims=True) * (1.0 / RUN_ALIGN)) * RUN_ALIGN
    total = carry + in_tile
    tile_out[...] = jnp.concatenate([carry, in_tile, jnp.zeros((6, LANES), F32)], axis=0)
    cnt_s[...] = jnp.broadcast_to(total, cnt_s.shape)
    cnt_out[...] = jnp.broadcast_to(total, cnt_out.shape)


def _outproj_router(h, outs, w_out, g, router_w, router_b):
    t, d = h.shape
    tm = ROW_TILE
    rw = jnp.zeros((d, LANES), F32).at[:, :N_EXPERTS].set(router_w)
    rb = jnp.zeros((1, LANES), F32).at[0, :N_EXPERTS].set(router_b)
    rw_hi = rw.astype(BF16)
    rw_lo = (rw - rw_hi.astype(F32)).astype(BF16)
    tile = lambda width: pl.BlockSpec((tm, width), lambda i: (i, 0))
    full = lambda shape: pl.BlockSpec(shape, lambda i: (0,) * len(shape))
    flat = [o.reshape(t, GROUP_W) for o in outs]
    return pl.pallas_call(
        _outproj_router_kernel,
        grid=(t // tm,),
        in_specs=[tile(d)] + [tile(GROUP_W)] * 4 + [full((d, d)), full((1, d)), full((d, 2 * LANES)),
                                                   full((1, LANES))],
        out_specs=[tile(d), tile(d), tile(LANES), tile(LANES), tile(LANES), full((8, LANES)),
                   pl.BlockSpec((8, LANES), lambda i: (i, 0))],
        out_shape=[jax.ShapeDtypeStruct((t, d), F32), jax.ShapeDtypeStruct((t, d), F32),
                   jax.ShapeDtypeStruct((t, LANES), F32), jax.ShapeDtypeStruct((t, LANES), F32),
                   jax.ShapeDtypeStruct((t, LANES), F32), jax.ShapeDtypeStruct((8, LANES), F32),
                   jax.ShapeDtypeStruct((t // tm * 8, LANES), F32)],
        scratch_shapes=[pltpu.VMEM((8, LANES), F32)],
        compiler_params=_params("arbitrary"),
        name="outproj_router",
    )(h, *flat, w_out.astype(BF16), g.reshape(1, d), jnp.concatenate([rw_hi, rw_lo], axis=1), rb)


def _plan_kernel(cnt_ref, start_out, plan_out):
    cnt = cnt_ref[...]
    padded = jnp.ceil(cnt * (1.0 / MOE_BLOCK)) * MOE_BLOCK
    upto = jnp.where(_iota((LANES, LANES), 0) <= _iota((LANES, LANES), 1), 1.0, 0.0)
    pad_end = _dot_hi(padded, upto)
    pad_start = pad_end - padded
    start_out[...] = pad_start
    ends = jnp.transpose(jnp.broadcast_to(pad_end[0:1, :], (LANES, LANES)))
    expert_ok = _iota((LANES, LANES), 0) < N_EXPERTS
    n_active = pad_end[0:1, N_EXPERTS - 1:N_EXPERTS] * (1.0 / MOE_BLOCK)
    rows = [jnp.broadcast_to(n_active, (1, LANES)), cnt[0:1, :], pad_start[0:1, :], padded[0:1, :]]
    for r in range(PLAN_EXPERT_ROWS):
        blk_start = ((_iota((LANES, LANES), 1) + r * LANES) * MOE_BLOCK).astype(F32)
        done = jnp.where(jnp.logical_and(expert_ok, ends <= blk_start), 1.0, 0.0)
        rows.append(jnp.minimum(jnp.sum(done, axis=0, keepdims=True), N_EXPERTS - 1.0))
    plan_out[...] = jnp.concatenate(rows, axis=0).astype(I32)


def _plan(counts, n_blocks):
    assert n_blocks <= PLAN_EXPERT_ROWS * LANES
    return pl.pallas_call(
        _plan_kernel,
        out_shape=[jax.ShapeDtypeStruct((8, LANES), F32), jax.ShapeDtypeStruct((8, LANES), I32)],
        name="moe_plan",
    )(counts)


def _dest_kernel(sel_ref, gate_ref, rank_ref, tile_ref, start_ref, lpos_out, lk_out, gk_out, runs_out):
    tm = sel_ref.shape[0]
    sel = sel_ref[...]
    gate = gate_ref[...]
    before_tile = tile_ref[0:1, :]
    in_tile = tile_ref[1:2, :]
    earlier = jnp.where(_iota((LANES, LANES), 0) < _iota((LANES, LANES), 1), 1.0, 0.0)
    run_start = _dot_hi(jnp.broadcast_to(in_tile, (8, LANES)), earlier)[0:1, :]
    lpos = rank_ref[...] + run_start
    lane = _iota((tm, LANES), 1)
    pieces = []
    gk = jnp.zeros((tm, LANES), F32)
    lk = jnp.zeros((tm, LANES), F32)
    for k in range(TOP_K):
        hit = sel == float(k + 1)
        mine = jnp.where(hit, lpos, 0.0)
        pieces.append(mine)
        gk = jnp.where(lane == k, jnp.sum(jnp.where(hit, gate, 0.0), axis=-1, keepdims=True), gk)
        lk = jnp.where(lane == k, jnp.sum(mine, axis=-1, keepdims=True), lk)
    stacked = jnp.concatenate(pieces, axis=1)
    picker = jnp.where(_div_pow2(_iota((8, TOP_K * LANES), 1), LANES) == _iota((8, TOP_K * LANES), 0),
                       1.0, 0.0).astype(BF16)
    lpos_out[...] = sum(_dot_nt(picker, part) for part in _split3(stacked)).astype(I32)
    lk_out[...] = lk
    gk_out[...] = gk
    first_slot = start_ref[0:1, :] + before_tile
    total = jnp.broadcast_to(jnp.sum(in_tile, axis=-1, keepdims=True), (1, LANES))
    runs_out[...] = jnp.concatenate([in_tile, run_start, first_slot, total, jnp.zeros((4, LANES), F32)],
                                    axis=0).astype(I32)


def _dest(sel, gate, rank, tile_counts, pad_start):
    t = sel.shape[0]
    tm = ROW_TILE
    tile = pl.BlockSpec((tm, LANES), lambda i: (i, 0))
    per_tile = pl.BlockSpec((8, LANES), lambda i: (i, 0))
    return pl.pallas_call(
        _dest_kernel,
        grid=(t // tm,),
        in_specs=[tile, tile, tile, per_tile, pl.BlockSpec((8, LANES), lambda i: (0, 0))],
        out_specs=[pl.BlockSpec((8, tm), lambda i: (0, i)), tile, tile, per_tile],
        out_shape=[jax.ShapeDtypeStruct((8, t), I32), jax.ShapeDtypeStruct((t, LANES), F32),
                   jax.ShapeDtypeStruct((t, LANES), F32), jax.ShapeDtypeStruct((t // tm * 8, LANES), I32)],
        compiler_params=_params("parallel"),
        name="moe_dest",
    )(sel, gate, rank, tile_counts, pad_start)


def _copy_run(src, dst, length, n_bits, make_copy, wait):
    units = jnp.right_shift(length, RUN_ALIGN.bit_length() - 1)
    for b in range(n_bits):
        offset = jnp.left_shift(jnp.right_shift(units, b + 1), b + 1) * RUN_ALIGN

        @pl.when(jnp.bitwise_and(jnp.right_shift(units, b), 1) == 1)
        def _():
            copy = make_copy(pl.multiple_of(src + offset, RUN_ALIGN), pl.multiple_of(dst + offset, RUN_ALIGN),
                             RUN_ALIGN << b)
            if wait:
                copy.wait()
            else:
                copy.start()


def _dispatch_kernel(plan_ref, runs_ref, prev_runs_ref, lpos_ref, xn_ref, xs_hbm, sorted_s, sem):
    i = pl.program_id(0)
    tm = xn_ref.shape[0]
    n_local = sorted_s.shape[1]
    slot = i % 2

    def to_slots(buf):
        return lambda src, dst, size: pltpu.make_async_copy(
            sorted_s.at[buf, pl.ds(src, size)], xs_hbm.at[pl.ds(dst, size)], sem.at[buf])

    def start_runs(table, buf):
        def body(e, carry):
            _copy_run(table[RUN_START, e], table[RUN_SLOT, e], table[RUN_LEN, e], RUN_BITS, to_slots(buf),
                      False)
            return carry
        lax.fori_loop(0, N_EXPERTS, body, 0)

    def wait_runs(table, buf):
        _copy_run(0, 0, table[RUN_TOTAL, 0], TOTAL_BITS, lambda src, dst, size: to_slots(buf)(0, 0, size),
                  True)

    position = _iota((n_local, tm), 0)
    lpos = lpos_ref[...]
    place = jnp.zeros((n_local, tm), F32)
    for k in range(TOP_K):
        place = place + jnp.where(position == lpos[k:k + 1, :], 1.0, 0.0)
    sorted_s[slot] = _dot(place.astype(BF16), xn_ref[...].astype(BF16))
    start_runs(runs_ref, slot)

    @pl.when(i > 0)
    def _():
        wait_runs(prev_runs_ref, 1 - slot)

    @pl.when(i == pl.num_programs(0) - 1)
    def _():
        wait_runs(runs_ref, slot)

    @pl.when(i == 0)
    def _():
        def padding(wait):
            def fill(src, dst, size):
                return pltpu.make_async_copy(sorted_s.at[0, pl.ds(0, size)], xs_hbm.at[pl.ds(dst, size)],
                                             sem.at[1])

            def body(e, carry):
                cnt = plan_ref[PLAN_COUNT, e]
                _copy_run(0, plan_ref[PLAN_START, e] + cnt, plan_ref[PLAN_PADDED, e] - cnt, PAD_BITS, fill,
                          wait)
                return carry
            lax.fori_loop(0, N_EXPERTS, body, 0)

        padding(False)
        padding(True)


def _dispatch(plan, runs, lpos, xn, n_slots):
    t, d = xn.shape
    tm = ROW_TILE
    return pl.pallas_call(
        _dispatch_kernel,
        grid=(t // tm,),
        in_specs=[pl.BlockSpec(memory_space=pltpu.SMEM),
                  pl.BlockSpec((8, LANES), lambda i: (i, 0), memory_space=pltpu.SMEM),
                  pl.BlockSpec((8, LANES), lambda i: (jnp.maximum(i - 1, 0), 0), memory_space=pltpu.SMEM),
                  pl.BlockSpec((8, tm), lambda i: (0, i)),
                  pl.BlockSpec((tm, d), lambda i: (i, 0))],
        out_specs=pl.BlockSpec(memory_space=pl.ANY),
        out_shape=jax.ShapeDtypeStruct((n_slots, d), F32),
        scratch_shapes=[pltpu.VMEM((2, LOCAL_ROWS, d), F32), pltpu.SemaphoreType.DMA((2,))],
        compiler_params=_params("arbitrary"),
        name="moe_dispatch",
    )(plan, runs, runs, lpos, xn)


def _ffn_kernel(be_ref, na_ref, plan_ref, x_ref, wgu_hbm, bgu_ref, wd_hbm, bd_ref, o_ref, wgu_in_s, wd_in_s,
                wgu_s, wd32_s, wd_s, sem, *, layer):
    j = pl.program_id(0)
    n_active = na_ref[0]
    active = j < n_active
    expert = be_ref[j]
    new_expert = jnp.logical_or(j == 0, expert != be_ref[jnp.maximum(j - 1, 0)])
    rows_left = plan_ref[PLAN_START, expert] + plan_ref[PLAN_COUNT, expert] - j * MOE_BLOCK

    def fetch(e):
        return (pltpu.make_async_copy(wgu_hbm.at[layer, e], wgu_in_s, sem.at[0]),
                pltpu.make_async_copy(wd_hbm.at[layer, e], wd_in_s, sem.at[1]))

    @pl.when(j == 0)
    def _():
        for copy in fetch(expert):
            copy.start()

    @pl.when(jnp.logical_and(active, new_expert))
    def _():
        for copy in fetch(expert):
            copy.wait()
        wgu_s[...] = wgu_in_s[...].astype(BF16)
        half = LANES // 2
        for c in range(wd_s.shape[0]):
            cols = slice(c * LANES, (c + 1) * LANES)
            for k in range(wd_s.shape[1] // LANES):
                for hb in range(2):
                    src = k * LANES + hb * half
                    wd32_s[c, pl.ds(k * LANES + hb, half, stride=2), :] = wd_in_s[src:src + half, cols]
            wd_s[c] = wd32_s[c].astype(BF16)
        nxt = lax.while_loop(lambda b: jnp.logical_and(b < n_active, be_ref[jnp.minimum(b, n_active - 1)] == expert),
                             lambda b: b + 1, j + 1)

        @pl.when(nxt < n_active)
        def _():
            for copy in fetch(be_ref[jnp.minimum(nxt, n_active - 1)]):
                copy.start()

    def ffn_rows(n_rows):
        x = x_ref[0:n_rows, :].astype(BF16)
        hmid = _dot(x, wgu_s[...]) + bgu_ref[0]
        m = hmid.shape[0]
        even = (_iota((m, LANES), 1) & 1) == 0
        acts = []
        for k in range(hmid.shape[1] // (2 * LANES)):
            h0 = hmid[:, (2 * k) * LANES:(2 * k + 1) * LANES]
            h1 = hmid[:, (2 * k + 1) * LANES:(2 * k + 2) * LANES]
            glu = jnp.where(even, h0, pltpu.roll(h1, 1, 1))
            lin = jnp.where(even, pltpu.roll(h0, LANES - 1, 1), h1)
            glu = jnp.minimum(glu, SWIGLU_LIMIT)
            lin = jnp.clip(lin, -SWIGLU_LIMIT, SWIGLU_LIMIT)
            acts.append((glu * _sigmoid(SWIGLU_ALPHA * glu) * (lin + 1.0)).astype(BF16))
        act = jnp.concatenate(acts, axis=1)
        wd = jnp.concatenate([wd_s[c] for c in range(wd_s.shape[0])], axis=1)
        o_ref[0:n_rows, :] = _dot(act, wd) + bd_ref[0]

    half_block = MOE_BLOCK // 2

    @pl.when(jnp.logical_and(active, rows_left > half_block))
    def _():
        ffn_rows(MOE_BLOCK)

    @pl.when(jnp.logical_and(active, rows_left <= half_block))
    def _():
        ffn_rows(half_block)


def _ffn(layer, block_expert, n_active, plan, xs, wgu, bgu, wd, bd):
    n_slots, d = xs.shape
    dff2 = wgu.shape[3]
    nb = n_slots // MOE_BLOCK
    rows = lambda j, be, na, plan_: (jnp.maximum(jnp.minimum(j, na[0] - 1), 0), 0)
    per_expert = lambda j, be, na, plan_: (layer, be[j], 0, 0)
    grid_spec = pltpu.PrefetchScalarGridSpec(
        num_scalar_prefetch=3,
        grid=(nb,),
        in_specs=[pl.BlockSpec((MOE_BLOCK, d), rows),
                  pl.BlockSpec(memory_space=pl.ANY), pl.BlockSpec((None, 1, 1, dff2), per_expert),
                  pl.BlockSpec(memory_space=pl.ANY), pl.BlockSpec((None, 1, 1, d), per_expert)],
        out_specs=pl.BlockSpec((MOE_BLOCK, d), rows),
        scratch_shapes=[pltpu.VMEM((d, dff2), F32), pltpu.VMEM((dff2 // 2, d), F32),
                        pltpu.VMEM((d, dff2), BF16), pltpu.VMEM((d // LANES, dff2 // 2, LANES), F32),
                        pltpu.VMEM((d // LANES, dff2 // 2, LANES), BF16), pltpu.SemaphoreType.DMA((2,))],
    )
    return pl.pallas_call(
        functools.partial(_ffn_kernel, layer=layer),
        grid_spec=grid_spec,
        out_shape=jax.ShapeDtypeStruct((n_slots, d), F32),
        compiler_params=_params("arbitrary"),
        name="moe_ffn",
    )(block_expert, n_active, plan, xs, wgu, bgu, wd, bd)


def _combine_kernel(runs_ref, next_runs_ref, lk_ref, gk_ref, h_ref, g_ref, ys_hbm, o_ref, sorted_s, sem, *,
                    final_norm):
    i = pl.program_id(0)
    tm = h_ref.shape[0]
    n_local = sorted_s.shape[1]
    slot = i % 2

    def from_slots(buf):
        return lambda src, dst, size: pltpu.make_async_copy(
            ys_hbm.at[pl.ds(src, size)], sorted_s.at[buf, pl.ds(dst, size)], sem.at[buf])

    def start_runs(table, buf):
        def body(e, carry):
            _copy_run(table[RUN_SLOT, e], table[RUN_START, e], table[RUN_LEN, e], RUN_BITS, from_slots(buf),
                      False)
            return carry
        lax.fori_loop(0, N_EXPERTS, body, 0)

    @pl.when(i == 0)
    def _():
        sorted_s[...] = jnp.zeros_like(sorted_s)
        start_runs(runs_ref, slot)

    @pl.when(i + 1 < pl.num_programs(0))
    def _():
        start_runs(next_runs_ref, 1 - slot)

    _copy_run(0, 0, runs_ref[RUN_TOTAL, 0], TOTAL_BITS, lambda src, dst, size: from_slots(slot)(0, 0, size),
              True)
    position = _iota((tm, n_local), 1).astype(F32)
    lk = lk_ref[...]
    gk = gk_ref[...]
    weights = jnp.zeros((tm, n_local), F32)
    for k in range(TOP_K):
        weights = weights + jnp.where(position == lk[:, k:k + 1], gk[:, k:k + 1], 0.0)
    acc = h_ref[...] + _dot(weights.astype(BF16), sorted_s[slot].astype(BF16))
    if final_norm:
        ms = jnp.mean(acc * acc, axis=-1, keepdims=True)
        acc = acc * lax.rsqrt(ms + EPS) * g_ref[...]
    o_ref[...] = acc


def _combine(runs, lk, gk, h, ys, g, final_norm):
    t, d = h.shape
    tm = ROW_TILE
    return pl.pallas_call(
        functools.partial(_combine_kernel, final_norm=final_norm),
        grid=(t // tm,),
        in_specs=[pl.BlockSpec((8, LANES), lambda i: (i, 0), memory_space=pltpu.SMEM),
                  pl.BlockSpec((8, LANES), lambda i: (jnp.minimum(i + 1, t // tm - 1), 0),
                               memory_space=pltpu.SMEM),
                  pl.BlockSpec((tm, LANES), lambda i: (i, 0)),
                  pl.BlockSpec((tm, LANES), lambda i: (i, 0)),
                  pl.BlockSpec((tm, d), lambda i: (i, 0)),
                  pl.BlockSpec((1, d), lambda i: (0, 0)),
                  pl.BlockSpec(memory_space=pl.ANY)],
        out_specs=pl.BlockSpec((tm, d), lambda i: (i, 0)),
        out_shape=jax.ShapeDtypeStruct((t, d), F32),
        scratch_shapes=[pltpu.VMEM((2, LOCAL_ROWS, d), F32), pltpu.SemaphoreType.DMA((2,))],
        compiler_params=_params("arbitrary"),
        name="moe_combine",
    )(runs, runs, lk, gk, h, g.reshape(1, d), ys)


def _moe(h, xn, sel, gate, rank, counts, tile_counts, layer, w_gu, b_gu, w_down, b_down, g_final,
         final_norm):
    t, d = h.shape
    n_slots = t * TOP_K + (t // ROW_TILE) * N_EXPERTS * RUN_ALIGN + N_EXPERTS * MOE_BLOCK
    n_blocks = n_slots // MOE_BLOCK
    pad_start, plan = _plan(counts, n_blocks)
    lpos, lk, gk, runs = _dest(sel, gate, rank, tile_counts, pad_start)
    xs = _dispatch(plan, runs, lpos, xn, n_slots)
    block_expert = plan[PLAN_EXPERTS:].reshape(-1)[:n_blocks]
    n_active = plan[PLAN_ACTIVE, :1]
    depth = w_gu.shape[0]
    ys = _ffn(layer, block_expert, n_active, plan, xs, w_gu, b_gu.reshape(depth, N_EXPERTS, 1, -1), w_down,
              b_down.reshape(depth, N_EXPERTS, 1, d))
    return _combine(runs, lk, gk, h, ys, g_final, final_norm)


def kernel(x, norm_mix_g, w_in, sb_norm_g, rg_conv_w, rg_conv_b, rg_wa, rg_ba, rg_wx, rg_bx,
           rg_lambda, rg_norm_g, hg_lower_bounds, hg_norm_g, m2_conv_w, m2_conv_b, m2_dt_bias,
           m2_a_log, m2_d, m2_norm_g, w_out, norm_ffn_g, router_w, router_b, moe_w_gu, moe_b_gu,
           moe_w_down, moe_b_down, final_norm_g):
    b, l, d = x.shape
    depth = w_in.shape[0]
    t = b * l
    lbs = jnp.cumsum(jax.nn.softmax(hg_lower_bounds.astype(F32), axis=0), axis=0)
    lbs = lbs - lbs[0]
    d_in = w_in.shape[2]
    w_in_p = jnp.zeros((depth, d, PROJ_BLOCKS * GROUP_W), BF16).at[:, :, :d_in].set(w_in.astype(BF16))
    h = x.reshape(t, d)
    for layer in range(depth):
        proj = _norm_inproj(h, norm_mix_g[layer], w_in_p[layer]).reshape(b, l, PROJ_BLOCKS * GROUP_W)
        o_a = _sb_attn(proj, sb_norm_g[layer])
        o_b = _rglru(proj, rg_conv_w[layer], rg_conv_b[layer], rg_wa[layer], rg_ba[layer],
                     rg_wx[layer], rg_bx[layer], rg_lambda[layer], rg_norm_g[layer])
        o_c = _hgrn2(proj, lbs[layer], hg_norm_g[layer])
        o_d = _ssd(proj, m2_conv_w[layer], m2_conv_b[layer], m2_dt_bias[layer], m2_a_log[layer],
                   m2_d[layer], m2_norm_g[layer])
        h, xn, sel, gate, rank, counts, tile_counts = _outproj_router(
            h, (o_a, o_b, o_c, o_d), w_out[layer], norm_ffn_g[layer], router_w[layer],
            router_b[layer])
        h = _moe(h, xn, sel, gate, rank, counts, tile_counts, layer, moe_w_gu, moe_b_gu, moe_w_down,
                 moe_b_down, final_norm_g, layer == depth - 1)
    return h.reshape(b, l, d)
```
